```python
import jax, jax.numpy as jnp
from jax import lax
import numpy as np

D_MODEL = 1024
BATCH = 2
SEQ = 8192
DEPTH = 2
DEC_BATCH = 16
DEC_SEQ = 16
PAST_LEN = 4096

CHUNK = 64
Q_BLOCK = 128
DH = 64
H_A = 8
A_WIDTH = H_A * DH
B_WIDTH = D_MODEL - A_WIDTH
POOL_WINDOWS = (2, 4, 8, 16)
N_POOL_GROUPS = len(POOL_WINDOWS)
POOL_GC = B_WIDTH // N_POOL_GROUPS
POOL_HIST = max(POOL_WINDOWS) - 1
H_C = D_MODEL // DH
D_FF = 4 * D_MODEL
N_AB = (DEPTH + 1) // 2
N_C = DEPTH // 2
AB_IN = 3 * A_WIDTH + H_A + B_WIDTH
EPS = 1e-6

kernel_name = "fox_pool_stickbreak_stream_step"


def _rmsnorm(x, g):
    xf = x.astype(jnp.float32)
    y = xf * lax.rsqrt(jnp.mean(xf * xf, axis=-1, keepdims=True) + EPS)
    return (y * g.astype(jnp.float32)).astype(x.dtype)


def _sq_relu_mlp(h, w_up, w_down):
    return jnp.square(jax.nn.relu(h @ w_up)) @ w_down


def _fox_attend(q, fq, q_pos, k, v, fk, k_pos):
    s = jnp.einsum("bqhd,bkhd->bhqk", q, k).astype(jnp.float32) * (DH ** -0.5)
    s = s + jnp.swapaxes(fq, 1, 2)[..., :, None] - jnp.swapaxes(fk, 1, 2)[..., None, :]
    mask = k_pos[None, :] <= q_pos[:, None]
    p = jax.nn.softmax(jnp.where(mask, s, -jnp.inf), axis=-1)
    return jnp.einsum("bhqk,bkhd->bqhd", p.astype(v.dtype), v)


def _sb_attend(q, q_pos, k, v, k_pos):
    z = jnp.einsum("bqhd,bkhd->bhqk", q, k).astype(jnp.float32) * (DH ** -0.5)
    valid = k_pos[None, :] < q_pos[:, None]
    log1m = jnp.where(valid, jax.nn.log_sigmoid(-z), 0.0)
    later = lax.cumsum(log1m, axis=3, reverse=True) - log1m
    a = jnp.where(valid, jnp.exp(jax.nn.log_sigmoid(z) + later), 0.0)
    return jnp.einsum("bhqk,bkhd->bqhd", a.astype(v.dtype), v)


def _query_blocks(attend, q_side, q_pos, kv_side):
    b, l = q_side[0].shape[:2]
    nb = l // Q_BLOCK
    qs = tuple(jnp.moveaxis(a.reshape((b, nb, Q_BLOCK) + a.shape[2:]), 1, 0) for a in q_side)
    ps = q_pos.reshape(nb, Q_BLOCK)
    out = lax.map(lambda blk: attend(*blk[0], blk[1], *kv_side), (qs, ps))
    return jnp.moveaxis(out, 0, 1).reshape((b, l) + out.shape[3:])


def _multiscale_pool(u, u_hist, p0):
    b, l, _ = u.shape
    u_ext = jnp.concatenate([u_hist.astype(u.dtype), u], axis=1)
    csum = jnp.cumsum(u_ext.astype(jnp.float32), axis=1)
    csum = jnp.concatenate([jnp.zeros((b, 1, B_WIDTH), jnp.float32), csum], axis=1)
    end = csum[:, POOL_HIST + 1:POOL_HIST + 1 + l]
    pos = p0 + jnp.arange(l)
    means = []
    for g, w in enumerate(POOL_WINDOWS):
        sl = slice(g * POOL_GC, (g + 1) * POOL_GC)
        start = csum[:, POOL_HIST + 1 - w:POOL_HIST + 1 - w + l, sl]
        cnt = jnp.minimum(pos + 1, w).astype(jnp.float32)[None, :, None]
        means.append((end[..., sl] - start) / cnt)
    pooled = jnp.concatenate(means, axis=-1) - u.astype(jnp.float32)
    return pooled.astype(u.dtype), u_ext[:, -POOL_HIST:]


def _ab_mixer(h, w_in, b_f, w_pool, pool_scale, w_out, past):
    b, l, _ = h.shape
    proj = h @ w_in
    q, k, v, f_logit, u = jnp.split(proj, [A_WIDTH, 2 * A_WIDTH, 3 * A_WIDTH, 3 * A_WIDTH + H_A], axis=-1)
    q = q.reshape(b, l, H_A, DH)
    k = k.reshape(b, l, H_A, DH)
    v = v.reshape(b, l, H_A, DH)
    logf = jax.nn.log_sigmoid((f_logit + b_f).astype(jnp.float32))
    if past is None:
        p0 = 0
        fcum = jnp.cumsum(logf, axis=1)
        pos = jnp.arange(l)
        a_out = _query_blocks(_fox_attend, (q, fcum), pos, (k, v, fcum, pos))
        u_hist = jnp.zeros((b, POOL_HIST, B_WIDTH), u.dtype)
    else:
        pk, pv, plogf, u_hist = past
        p0 = pk.shape[1]
        k_all = jnp.concatenate([pk.astype(k.dtype), k], axis=1)
        v_all = jnp.concatenate([pv.astype(v.dtype), v], axis=1)
        fcum = jnp.cumsum(jnp.concatenate([plogf.astype(jnp.float32), logf], axis=1), axis=1)
        k_pos = jnp.arange(p0 + l)
        q_pos = p0 + jnp.arange(l)
        a_out = _fox_attend(q, fcum[:, p0:], q_pos, k_all, v_all, fcum, k_pos)
    pooled, new_rows = _multiscale_pool(u, u_hist, p0)
    pooled = jnp.einsum("blgc,gce->blge", pooled.reshape(b, l, N_POOL_GROUPS, POOL_GC), w_pool)
    pooled = pooled.reshape(b, l, B_WIDTH) * pool_scale
    y = jnp.concatenate([a_out.reshape(b, l, A_WIDTH), pooled], axis=-1) @ w_out
    return y, (k, v, logf, new_rows)


def _sb_mixer(h, w_in, w_out, past):
    b, l, _ = h.shape
    q, k, v = jnp.split(h @ w_in, 3, axis=-1)
    q = q.reshape(b, l, H_C, DH)
    k = k.reshape(b, l, H_C, DH)
    v = v.reshape(b, l, H_C, DH)
    if past is None:
        pos = jnp.arange(l)
        out = _query_blocks(_sb_attend, (q,), pos, (k, v, pos))
    else:
        pk, pv = past
        p0 = pk.shape[1]
        k_all = jnp.concatenate([pk.astype(k.dtype), k], axis=1)
        v_all = jnp.concatenate([pv.astype(v.dtype), v], axis=1)
        out = _sb_attend(q, p0 + jnp.arange(l), k_all, v_all, jnp.arange(p0 + l))
    return out.reshape(b, l, D_MODEL) @ w_out, (k, v)


def _trunk(x, c, past, w_ada, b_ada, norm_g, w_in_ab, b_forget, w_pool, pool_scale,
           w_out_ab, w_in_sb, w_out_sb, w_up, w_down, final_g):
    fk_l, fv_l, fl_l, pool_l, sk_l, sv_l = [], [], [], [], [], []
    cond = jax.nn.silu(c)
    for i in range(DEPTH):
        j = i // 2
        mod = cond @ w_ada[i] + b_ada[i]
        sh1, sc1, g1, sh2, sc2, g2 = (m[:, None, :] for m in jnp.split(mod, 6, axis=-1))
        h = _rmsnorm(x, norm_g[i, 0]) * (1 + sc1) + sh1
        if i % 2 == 0:
            lp = None if past is None else (past[0][j], past[1][j], past[2][j], past[3][j])
            y, (k, v, lf, rows) = _ab_mixer(h, w_in_ab[j], b_forget[j], w_pool[j], pool_scale[j], w_out_ab[j], lp)
            fk_l.append(k)
            fv_l.append(v)
            fl_l.append(lf)
            pool_l.append(rows)
        else:
            lp = None if past is None else (past[4][j], past[5][j])
            y, (k, v) = _sb_mixer(h, w_in_sb[j], w_out_sb[j], lp)
            sk_l.append(k)
            sv_l.append(v)
        x = x + g1 * y
        h = _rmsnorm(x, norm_g[i, 1]) * (1 + sc2) + sh2
        x = x + g2 * _sq_relu_mlp(h, w_up[i], w_down[i])
    states = (jnp.stack(fk_l), jnp.stack(fv_l), jnp.stack(fl_l), jnp.stack(pool_l), jnp.stack(sk_l), jnp.stack(sv_l))
    return _rmsnorm(x, final_g), states


def setup_inputs(seed: int = 0) -> dict:
    key = jax.random.key(seed)
    ks = jax.random.split(key, 24)

    def nrm(k, shape, scale):
        return jax.random.normal(k, shape, jnp.float32) * scale

    return {
        "x_prompt": nrm(ks[0], (BATCH, SEQ, D_MODEL), 1.0),
        "x_sample": nrm(ks[1], (DEC_BATCH, DEC_SEQ, D_MODEL), 1.0),
        "c_prompt": nrm(ks[2], (BATCH, D_MODEL), 1.0),
        "c_sample": nrm(ks[3], (DEC_BATCH, D_MODEL), 1.0),
        "cache_fox_k": nrm(ks[4], (N_AB, DEC_BATCH, PAST_LEN, H_A, DH), 1.0),
        "cache_fox_v": nrm(ks[5], (N_AB, DEC_BATCH, PAST_LEN, H_A, DH), 1.0),
        "cache_fox_logf": jax.nn.log_sigmoid(2.0 + nrm(ks[6], (N_AB, DEC_BATCH, PAST_LEN, H_A), 1.0)),
        "state_pool": nrm(ks[7], (N_AB, DEC_BATCH, POOL_HIST, B_WIDTH), 1.0),
        "cache_sb_k": nrm(ks[8], (N_C, DEC_BATCH, PAST_LEN, H_C, DH), 1.0),
        "cache_sb_v": nrm(ks[9], (N_C, DEC_BATCH, PAST_LEN, H_C, DH), 1.0),
        "w_ada": nrm(ks[10], (DEPTH, D_MODEL, 6 * D_MODEL), 0.5 * D_MODEL ** -0.5),
        "b_ada": nrm(ks[11], (DEPTH, 6 * D_MODEL), 0.02),
        "norm_g": 1.0 + nrm(ks[12], (DEPTH, 2, D_MODEL), 0.1),
        "w_in_ab": nrm(ks[13], (N_AB, D_MODEL, AB_IN), D_MODEL ** -0.5),
        "b_forget": 2.0 + nrm(ks[14], (N_AB, H_A), 0.5),
        "w_pool": nrm(ks[15], (N_AB, N_POOL_GROUPS, POOL_GC, POOL_GC), POOL_GC ** -0.5),
        "pool_scale": 1.0 + nrm(ks[16], (N_AB, B_WIDTH), 0.1),
        "w_out_ab": nrm(ks[17], (N_AB, D_MODEL, D_MODEL), D_MODEL ** -0.5),
        "w_in_sb": nrm(ks[18], (N_C, D_MODEL, 3 * D_MODEL), D_MODEL ** -0.5),
        "w_out_sb": nrm(ks[19], (N_C, D_MODEL, D_MODEL), D_MODEL ** -0.5),
        "w_up": nrm(ks[20], (DEPTH, D_MODEL, D_FF), D_MODEL ** -0.5),
        "w_down": nrm(ks[21], (DEPTH, D_FF, D_MODEL), D_FF ** -0.5),
        "final_g": 1.0 + nrm(ks[22], (D_MODEL,), 0.1),
    }


def reference(x_prompt, x_sample, c_prompt, c_sample, cache_fox_k, cache_fox_v, cache_fox_logf,
              state_pool, cache_sb_k, cache_sb_v, w_ada, b_ada, norm_g, w_in_ab, b_forget, w_pool,
              pool_scale, w_out_ab, w_in_sb, w_out_sb, w_up, w_down, final_g):
    y_prompt, (fk_p, fv_p, fl_p, pool_p, sk_p, sv_p) = _trunk(
        x_prompt, c_prompt, None, w_ada, b_ada, norm_g, w_in_ab, b_forget, w_pool, pool_scale,
        w_out_ab, w_in_sb, w_out_sb, w_up, w_down, final_g)
    past = (cache_fox_k, cache_fox_v, cache_fox_logf, state_pool, cache_sb_k, cache_sb_v)
    y_sample, (fk_s, fv_s, fl_s, pool_s, sk_s, sv_s) = _trunk(
        x_sample, c_sample, past, w_ada, b_ada, norm_g, w_in_ab, b_forget, w_pool, pool_scale,
        w_out_ab, w_in_sb, w_out_sb, w_up, w_down, final_g)
    return (y_prompt, y_sample, fk_p, fv_p, fl_p, pool_p, sk_p, sv_p, fk_s, fv_s, fl_s, pool_s, sk_s, sv_s)
```

```python
import functools

import jax
import jax.numpy as jnp
from jax import lax
from jax.experimental import pallas as pl
from jax.experimental.pallas import tpu as pltpu

F32 = jnp.float32
BF16 = jnp.bfloat16

DH = 64
EPS = 1e-6
POOL_WINDOWS = (2, 4, 8, 16)
POOL_GC = 128
HIST_ROWS = 16
NEG_BIG = -1e30
LANES = 128
VMEM_LIMIT = 56 * 1024 * 1024

T_ATT = 256
TK_DEC = 512
TM_ROWS = 512
ADA_ROWS = 32
ADA_TN = 1536


def _cparams(sem):
    return pltpu.CompilerParams(dimension_semantics=sem, vmem_limit_bytes=VMEM_LIMIT)


def _rms_mod(x, g, sc, sh):
    y = x * lax.rsqrt(jnp.mean(x * x, axis=-1, keepdims=True) + EPS)
    return (y * g) * (1.0 + sc) + sh


def _softplus(z):
    return jnp.maximum(z, 0.0) + jnp.log(1.0 + jnp.exp(-jnp.abs(z)))


def _split3(x):
    hi = x.astype(BF16)
    r = x - hi.astype(F32)
    mid = r.astype(BF16)
    lo = (r - mid.astype(F32)).astype(BF16)
    return hi, mid, lo


def _split2(x):
    hi = x.astype(BF16)
    lo = (x - hi.astype(F32)).astype(BF16)
    return hi, lo


def _dot(a, b):
    return jnp.dot(a, b, preferred_element_type=F32)


def _dot_nt(a, b):
    return lax.dot_general(a, b, (((1,), (1,)), ((), ())), preferred_element_type=F32)


def _iota(shape, dim):
    return lax.broadcasted_iota(jnp.int32, shape, dim)


def _ada_kernel(c_ref, w_ref, b_ref, o_ref):
    c = c_ref[...]
    cond = (c / (1.0 + jnp.exp(-c))).astype(BF16)
    o_ref[...] = _dot(cond, w_ref[...].astype(BF16)) + b_ref[...]


def _ada_call(c_all, w_ada, b_ada):
    depth, d, n = w_ada.shape
    return pl.pallas_call(
        _ada_kernel,
        out_shape=jax.ShapeDtypeStruct((depth, ADA_ROWS, n), F32),
        grid=(depth, n // ADA_TN),
        in_specs=[
            pl.BlockSpec((ADA_ROWS, d), lambda i, j: (0, 0)),
            pl.BlockSpec((None, d, ADA_TN), lambda i, j: (i, 0, j)),
            pl.BlockSpec((None, 1, ADA_TN), lambda i, j: (i, 0, j)),
        ],
        out_specs=pl.BlockSpec((None, ADA_ROWS, ADA_TN), lambda i, j: (i, 0, j)),
        compiler_params=_cparams(("arbitrary", "arbitrary")),
        name="ada_mod",
    )(c_all, w_ada, b_ada.reshape(depth, 1, n))


def _mod_spec(prompt, tm, d, chunk, tiles_per_batch):
    if prompt:
        return pl.BlockSpec((None, 1, d), lambda i: (i // tiles_per_batch, 0, chunk))
    return pl.BlockSpec((tm, d), lambda i: (i, chunk))


def _ab_in_kernel(prompt, tm, ng, rg, tiles_per_batch, p0_static, *refs):
    if prompt:
        (x_ref, sh_ref, sc_ref, g_ref, wm_ref, wf_ref, bf_ref, wp_ref, ps_ref,
         q_ref, kb_ref, vb_ref, k32_ref, v32_ref, lf_ref, fc_ref, pooled_ref, pst_ref,
         ext_ref, carry_ref) = refs
        hist_ref = None
    else:
        (x_ref, sh_ref, sc_ref, g_ref, wm_ref, wf_ref, bf_ref, wp_ref, ps_ref, hist_ref,
         q_ref, kb_ref, vb_ref, k32_ref, v32_ref, lf_ref, pooled_ref, pst_ref,
         ext_ref) = refs
    a_w = 8 * DH
    h = _rms_mod(x_ref[...], g_ref[...], sc_ref[...], sh_ref[...]).astype(BF16)
    proj = _dot(h, wm_ref[...])
    k = proj[:, a_w:2 * a_w]
    v = proj[:, 2 * a_w:3 * a_w]
    u = proj[:, 3 * a_w:4 * a_w]
    q_ref[...] = (proj[:, 0:a_w] * (DH ** -0.5)).astype(q_ref.dtype)
    k32_ref[...] = k
    v32_ref[...] = v
    kb_ref[...] = k.astype(BF16)
    vb_ref[...] = v.astype(BF16)

    fl = _dot(h, wf_ref[...]) + bf_ref[...]
    logf = jnp.minimum(fl, 0.0) - jnp.log(1.0 + jnp.exp(-jnp.abs(fl)))
    lf_ref[...] = logf[:, 0:8]

    if prompt:
        i = pl.program_id(0)
        tile_in_batch = i % tiles_per_batch
        first = tile_in_batch == 0
        tri = (_iota((tm, tm), 0) >= _iota((tm, tm), 1)).astype(BF16)
        hi, mid, lo = _split3(logf)
        cum = _dot(tri, hi) + _dot(tri, mid) + _dot(tri, lo)
        carry = jnp.where(first, 0.0, carry_ref[...])
        fcum = cum + carry
        carry_ref[...] = fcum[tm - 1:tm, :]
        fc_ref[...] = fcum[:, 0:8]

        @pl.when(first)
        def _():
            ext_ref[:, 0:HIST_ROWS, :] = jnp.zeros((ng, HIST_ROWS, a_w), F32)

        @pl.when(jnp.logical_not(first))
        def _():
            ext_ref[:, 0:HIST_ROWS, :] = ext_ref[:, rg:rg + HIST_ROWS, :]

        pos = tile_in_batch * tm + _iota((tm, 1), 0)
    else:
        ext_ref[:, 0:HIST_ROWS, :] = hist_ref[...]
        pos = p0_static + (_iota((tm, 1), 0) & (rg - 1))

    u3 = u.reshape(ng, rg, a_w)
    ext_ref[:, HIST_ROWS:HIST_ROWS + rg, :] = u3
    pst_ref[...] = u3[:, rg - HIST_ROWS:rg, :]
    for g, w in enumerate(POOL_WINDOWS):
        cs = slice(g * POOL_GC, (g + 1) * POOL_GC)
        acc = u3[:, :, cs]
        for kk in range(1, w):
            acc = acc + ext_ref[:, pl.ds(HIST_ROWS - kk, rg), cs]
        cnt = jnp.minimum(pos + 1, w).astype(F32)
        pooled = acc.reshape(tm, POOL_GC) / cnt - u[:, cs]
        pg = _dot(pooled.astype(BF16), wp_ref[g]) * ps_ref[:, cs]
        pooled_ref[:, cs] = pg.astype(pooled_ref.dtype)


def _ab_in_call(prompt, x2, mod_arr, g, wm, wf, bf, wp, ps, hist, rows_per_batch, p0):
    m, d = x2.shape
    a_w = 8 * DH
    if prompt:
        tm, ng, rg = TM_ROWS, 1, TM_ROWS
        tpb = rows_per_batch // tm
        nb = m // rows_per_batch
    else:
        tm, ng, rg = m, m // rows_per_batch, rows_per_batch
        tpb = 1
        nb = ng
    grid = (m // tm,)
    row = lambda i: (i, 0)
    const2 = lambda i: (0, 0)
    in_specs = [
        pl.BlockSpec((tm, d), row),
        _mod_spec(prompt, tm, d, 0, tpb),
        _mod_spec(prompt, tm, d, 1, tpb),
        pl.BlockSpec((1, d), const2),
        pl.BlockSpec(wm.shape, const2),
        pl.BlockSpec(wf.shape, const2),
        pl.BlockSpec(bf.shape, const2),
        pl.BlockSpec(wp.shape, lambda i: (0, 0, 0)),
        pl.BlockSpec(ps.shape, const2),
    ]
    args = [x2, mod_arr, mod_arr, g, wm, wf, bf, wp, ps]
    out_shape = [
        jax.ShapeDtypeStruct((m, a_w), BF16),
        jax.ShapeDtypeStruct((m, a_w), BF16),
        jax.ShapeDtypeStruct((m, a_w), BF16),
        jax.ShapeDtypeStruct((m, a_w), F32),
        jax.ShapeDtypeStruct((m, a_w), F32),
        jax.ShapeDtypeStruct((m, 8), F32),
    ]
    out_specs = [pl.BlockSpec((tm, a_w), row)] * 5 + [pl.BlockSpec((tm, 8), row)]
    scratch = [pltpu.VMEM((ng, HIST_ROWS + rg, a_w), F32)]
    if prompt:
        out_shape.append(jax.ShapeDtypeStruct((m, 8), F32))
        out_specs.append(pl.BlockSpec((tm, 8), row))
        scratch.append(pltpu.VMEM((1, LANES), F32))
    else:
        in_specs.append(pl.BlockSpec(hist.shape, lambda i: (0, 0, 0)))
        args.append(hist)
    out_shape.append(jax.ShapeDtypeStruct((m, a_w), BF16))
    out_specs.append(pl.BlockSpec((tm, a_w), row))
    out_shape.append(jax.ShapeDtypeStruct((nb, HIST_ROWS, a_w), F32))
    if prompt:
        out_specs.append(pl.BlockSpec((1, HIST_ROWS, a_w), lambda i: (i // tpb, 0, 0)))
    else:
        out_specs.append(pl.BlockSpec((nb, HIST_ROWS, a_w), lambda i: (0, 0, 0)))
    return pl.pallas_call(
        functools.partial(_ab_in_kernel, prompt, tm, ng, rg, tpb, p0),
        out_shape=out_shape, grid=grid, in_specs=in_specs, out_specs=out_specs,
        scratch_shapes=scratch,
        compiler_params=_cparams(("arbitrary",)),
        name="ab_in_prompt" if prompt else "ab_in_sample",
    )(*args)


def _fox_kernel(q_ref, k_ref, v_ref, fq_ref, fk_ref, o_ref, m_ref, l_ref, acc_ref):
    t = T_ATT
    qi = pl.program_id(2)
    q2 = q_ref[...]
    lane = _iota((1, LANES), 1)
    zero = jnp.zeros_like(q2)
    qh = (jnp.where(lane < DH, q2, zero), jnp.where(lane >= DH, q2, zero))
    fq = fq_ref[...]
    m_ref[...] = jnp.full(m_ref.shape, NEG_BIG, F32)
    l_ref[...] = jnp.zeros(l_ref.shape, F32)
    acc_ref[...] = jnp.zeros(acc_ref.shape, F32)

    def step(kb, diag):
        ks = pl.multiple_of(kb * t, t)
        k2 = k_ref[pl.ds(ks, t), :]
        v2 = v_ref[pl.ds(ks, t), :]
        fk = fk_ref[:, pl.ds(ks, t)]
        for j in range(2):
            s = _dot_nt(qh[j], k2) + (fq[:, j:j + 1] - fk[j:j + 1, :])
            if diag:
                s = jnp.where(_iota((t, t), 1) <= _iota((t, t), 0), s, NEG_BIG)
            m_prev = m_ref[j]
            m_new = jnp.maximum(m_prev, jnp.max(s, axis=1, keepdims=True))
            alpha = jnp.exp(m_prev - m_new)
            p = jnp.exp(s - m_new)
            l_ref[j] = alpha * l_ref[j] + jnp.sum(p, axis=1, keepdims=True)
            acc_ref[j] = alpha * acc_ref[j] + _dot(p.astype(BF16), v2)
            m_ref[j] = m_new

    def body(kb, c):
        step(kb, False)
        return c

    lax.fori_loop(0, qi, body, 0)
    step(qi, True)
    o = jnp.where(lane < DH, acc_ref[0] / l_ref[0], acc_ref[1] / l_ref[1])
    o_ref[...] = o.astype(o_ref.dtype)


def _fox_call(q, kb, vb, fq_arr, fk_arr):
    b, l, a_w = q.shape
    hp = a_w // LANES
    t = T_ATT
    return pl.pallas_call(
        _fox_kernel,
        out_shape=jax.ShapeDtypeStruct((b, l, a_w), BF16),
        grid=(b, hp, l // t),
        in_specs=[
            pl.BlockSpec((None, t, LANES), lambda bi, h, qi: (bi, qi, h)),
            pl.BlockSpec((None, l, LANES), lambda bi, h, qi: (bi, 0, h)),
            pl.BlockSpec((None, l, LANES), lambda bi, h, qi: (bi, 0, h)),
            pl.BlockSpec((None, None, t, 2), lambda bi, h, qi: (bi, h, qi, 0)),
            pl.BlockSpec((None, None, 2, l), lambda bi, h, qi: (bi, h, 0, 0)),
        ],
        out_specs=pl.BlockSpec((None, t, LANES), lambda bi, h, qi: (bi, qi, h)),
        scratch_shapes=[pltpu.VMEM((2, t, 1), F32), pltpu.VMEM((2, t, 1), F32),
                        pltpu.VMEM((2, t, LANES), F32)],
        compiler_params=_cparams(("arbitrary", "arbitrary", "arbitrary")),
        name="fox_prompt",
    )(q, kb, vb, fq_arr, fk_arr)


def _sb_kernel(q_ref, k_ref, v_ref, o_ref, carry_ref, acc_ref):
    t = T_ATT
    qi = pl.program_id(2)
    q2 = q_ref[...]
    lane = _iota((1, LANES), 1)
    zero = jnp.zeros_like(q2)
    qh = (jnp.where(lane < DH, q2, zero), jnp.where(lane >= DH, q2, zero))
    carry_ref[...] = jnp.zeros(carry_ref.shape, F32)
    acc_ref[...] = jnp.zeros(acc_ref.shape, F32)

    def step(kb, diag):
        ks = pl.multiple_of(kb * t, t)
        k2 = k_ref[pl.ds(ks, t), :]
        v2 = v_ref[pl.ds(ks, t), :]
        tri = (_iota((t, t), 0) > _iota((t, t), 1)).astype(BF16)
        for j in range(2):
            z = _dot_nt(qh[j], k2)
            sp = _softplus(z)
            if diag:
                valid = _iota((t, t), 1) < _iota((t, t), 0)
                spm = jnp.where(valid, sp, 0.0)
            else:
                spm = sp
            hi, lo = _split2(spm)
            later = _dot(hi, tri) + _dot(lo, tri)
            c = carry_ref[j]
            a = jnp.exp((z - sp) - later - c)
            if diag:
                a = jnp.where(valid, a, 0.0)
            acc_ref[j] = acc_ref[j] + _dot(a.astype(BF16), v2)
            carry_ref[j] = c + jnp.sum(spm, axis=1, keepdims=True)

    step(qi, True)

    def body(i, c):
        step(qi - 1 - i, False)
        return c

    lax.fori_loop(0, qi, body, 0)
    o_ref[...] = jnp.where(lane < DH, acc_ref[0], acc_ref[1]).astype(o_ref.dtype)


def _sb_call(q, kb, vb):
    b, l, d = q.shape
    hp = d // LANES
    t = T_ATT
    return pl.pallas_call(
        _sb_kernel,
        out_shape=jax.ShapeDtypeStruct((b, l, d), BF16),
        grid=(b, hp, l // t),
        in_specs=[
            pl.BlockSpec((None, t, LANES), lambda bi, h, qi: (bi, qi, h)),
            pl.BlockSpec((None, l, LANES), lambda bi, h, qi: (bi, 0, h)),
            pl.BlockSpec((None, l, LANES), lambda bi, h, qi: (bi, 0, h)),
        ],
        out_specs=pl.BlockSpec((None, t, LANES), lambda bi, h, qi: (bi, qi, h)),
        scratch_shapes=[pltpu.VMEM((2, t, 1), F32), pltpu.VMEM((2, t, LANES), F32)],
        compiler_params=_cparams(("arbitrary", "arbitrary", "arbitrary")),
        name="sb_prompt",
    )(q, kb, vb)


def _log2(n):
    assert n & (n - 1) == 0, n
    return n.bit_length() - 1


def _rep_matrix(rows, n_heads, ds):
    return ((_iota((rows, ds), 0) >> _log2(n_heads)) == _iota((rows, ds), 1)).astype(BF16)


def _head_mask(rows, n_heads, width):
    return (_iota((rows, width), 0) & (n_heads - 1)) == (_iota((rows, width), 1) >> _log2(DH))


def _fox_dec_kernel(n_heads, ds, q_ref, kn_ref, vn_ref, lfn_ref, kc_ref, vc_ref, lfc_ref,
                    o_ref, qbd_ref, m_ref, l_ref, acc_ref, cf_ref):
    rows = ds * n_heads
    width = n_heads * DH
    step_i = pl.program_id(1)
    n_steps = pl.num_programs(1)

    def update(s, vv):
        m_prev = m_ref[...]
        m_new = jnp.maximum(m_prev, jnp.max(s, axis=1, keepdims=True))
        alpha = jnp.exp(m_prev - m_new)
        p = jnp.exp(s - m_new)
        l_ref[...] = alpha * l_ref[...] + jnp.sum(p, axis=1, keepdims=True)
        acc_ref[...] = alpha * acc_ref[...] + _dot(p.astype(BF16), vv)
        m_ref[...] = m_new

    @pl.when(step_i == 0)
    def _():
        qrep = _dot(_rep_matrix(rows, n_heads, ds), q_ref[...])
        qbd_ref[...] = jnp.where(_head_mask(rows, n_heads, width), qrep, 0.0).astype(BF16)
        m_ref[...] = jnp.full(m_ref.shape, NEG_BIG, F32)
        l_ref[...] = jnp.zeros(l_ref.shape, F32)
        acc_ref[...] = jnp.zeros(acc_ref.shape, F32)
        a = jnp.concatenate([lfn_ref[...]] * ds, axis=0)
        ri = _iota((rows, ds), 0) >> _log2(n_heads)
        cj = _iota((rows, ds), 1)
        am = jnp.where(cj <= ri, a, 0.0)
        tri = (_iota((ds, ds), 0) > _iota((ds, ds), 1)).astype(BF16)
        hi, mid, lo = _split3(am)
        bias = _dot(hi, tri) + _dot(mid, tri) + _dot(lo, tri)
        s = _dot_nt(qbd_ref[...], kn_ref[...]) + bias
        s = jnp.where(cj <= ri, s, NEG_BIG)
        update(s, vn_ref[...])
        cf_ref[...] = jnp.sum(am, axis=1, keepdims=True)

    @pl.when(step_i > 0)
    def _():
        tk = kc_ref.shape[0]
        a = jnp.concatenate([lfc_ref[...]] * ds, axis=0)
        tri = (_iota((tk, tk), 0) > _iota((tk, tk), 1)).astype(BF16)
        hi, mid, lo = _split3(a)
        later = _dot(hi, tri) + _dot(mid, tri) + _dot(lo, tri)
        cf = cf_ref[...]
        s = _dot_nt(qbd_ref[...], kc_ref[...].astype(BF16)) + (later + cf)
        update(s, vc_ref[...].astype(BF16))
        cf_ref[...] = cf + jnp.sum(a, axis=1, keepdims=True)

    @pl.when(step_i == n_steps - 1)
    def _():
        o_full = acc_ref[...] / l_ref[...]
        om = jnp.where(_head_mask(rows, n_heads, width), o_full, 0.0).astype(BF16)
        rep_t = ((_iota((ds, rows), 1) >> _log2(n_heads)) == _iota((ds, rows), 0)).astype(BF16)
        o_ref[...] = _dot(rep_t, om).astype(o_ref.dtype)


def _sb_dec_kernel(n_heads, ds, q_ref, kn_ref, vn_ref, kc_ref, vc_ref,
                   o_ref, qbd_ref, acc_ref, carry_ref):
    rows = ds * n_heads
    width = n_heads * DH
    step_i = pl.program_id(1)
    n_steps = pl.num_programs(1)

    @pl.when(step_i == 0)
    def _():
        qrep = _dot(_rep_matrix(rows, n_heads, ds), q_ref[...])
        qbd_ref[...] = jnp.where(_head_mask(rows, n_heads, width), qrep, 0.0).astype(BF16)
        ri = _iota((rows, ds), 0) >> _log2(n_heads)
        cj = _iota((rows, ds), 1)
        valid = cj < ri
        z = _dot_nt(qbd_ref[...], kn_ref[...])
        sp = _softplus(z)
        spm = jnp.where(valid, sp, 0.0)
        tri = (_iota((ds, ds), 0) > _iota((ds, ds), 1)).astype(BF16)
        hi, lo = _split2(spm)
        later = _dot(hi, tri) + _dot(lo, tri)
        a = jnp.where(valid, jnp.exp((z - sp) - later), 0.0)
        acc_ref[...] = _dot(a.astype(BF16), vn_ref[...])
        carry_ref[...] = jnp.sum(spm, axis=1, keepdims=True)

    @pl.when(step_i > 0)
    def _():
        tk = kc_ref.shape[0]
        z = _dot_nt(qbd_ref[...], kc_ref[...].astype(BF16))
        sp = _softplus(z)
        tri = (_iota((tk, tk), 0) > _iota((tk, tk), 1)).astype(BF16)
        hi, lo = _split2(sp)
        later = _dot(hi, tri) + _dot(lo, tri)
        c = carry_ref[...]
        a = jnp.exp((z - sp) - later - c)
        acc_ref[...] = acc_ref[...] + _dot(a.astype(BF16), vc_ref[...].astype(BF16))
        carry_ref[...] = c + jnp.sum(sp, axis=1, keepdims=True)

    @pl.when(step_i == n_steps - 1)
    def _():
        om = jnp.where(_head_mask(rows, n_heads, width), acc_ref[...], 0.0).astype(BF16)
        rep_t = ((_iota((ds, rows), 1) >> _log2(n_heads)) == _iota((ds, rows), 0)).astype(BF16)
        o_ref[...] = _dot(rep_t, om).astype(o_ref.dtype)


def _dec_call(fox, q, kn, vn, kc, vc, lfn_t=None, lfc_t=None):
    db, ds, width = q.shape
    n_heads = width // DH
    rows = ds * n_heads
    p = kc.shape[1]
    nkb = p // TK_DEC
    cache_blk = lambda s: jnp.minimum(nkb - s, nkb - 1)
    cache_idx = lambda b, s: (b, cache_blk(s), 0)
    new_idx = lambda b, s: (b, 0, 0)
    in_specs = [pl.BlockSpec((None, ds, width), new_idx)] * 3
    args = [q, kn, vn]
    if fox:
        in_specs.append(pl.BlockSpec((None, n_heads, ds), new_idx))
        args.append(lfn_t)
    in_specs += [pl.BlockSpec((None, TK_DEC, width), cache_idx)] * 2
    args += [kc, vc]
    scratch = [pltpu.VMEM((rows, width), BF16)]
    if fox:
        in_specs.append(pl.BlockSpec((None, n_heads, TK_DEC),
                                     lambda b, s: (b, 0, cache_blk(s))))
        args.append(lfc_t)
        scratch += [pltpu.VMEM((rows, 1), F32), pltpu.VMEM((rows, 1), F32),
                    pltpu.VMEM((rows, width), F32), pltpu.VMEM((rows, 1), F32)]
        body = functools.partial(_fox_dec_kernel, n_heads, ds)
    else:
        scratch += [pltpu.VMEM((rows, width), F32), pltpu.VMEM((rows, 1), F32)]
        body = functools.partial(_sb_dec_kernel, n_heads, ds)
    return pl.pallas_call(
        body,
        out_shape=jax.ShapeDtypeStruct((db, ds, width), BF16),
        grid=(db, nkb + 1),
        in_specs=in_specs,
        out_specs=pl.BlockSpec((None, ds, width), new_idx),
        scratch_shapes=scratch,
        compiler_params=_cparams(("arbitrary", "arbitrary")),
        name="fox_decode" if fox else "sb_decode",
    )(*args)


def _out_res_kernel(n_act, *refs):
    acts = refs[:n_act]
    w_ref, x_ref, gate_ref, o_ref = refs[n_act:]
    y = None
    off = 0
    for a_ref in acts:
        kdim = a_ref.shape[1]
        part = _dot(a_ref[...], w_ref[off:off + kdim, :])
        y = part if y is None else y + part
        off += kdim
    o_ref[...] = x_ref[...] + gate_ref[...] * y


def _out_res_call(prompt, acts, w, x2, mod_arr, gate_chunk, rows_per_batch):
    m, d = x2.shape
    tm = TM_ROWS if prompt else m
    tpb = rows_per_batch // tm if prompt else 1
    row = lambda i: (i, 0)
    in_specs = [pl.BlockSpec((tm, a.shape[1]), row) for a in acts]
    in_specs += [pl.BlockSpec(w.shape, lambda i: (0, 0)), pl.BlockSpec((tm, d), row),
                 _mod_spec(prompt, tm, d, gate_chunk, tpb)]
    return pl.pallas_call(
        functools.partial(_out_res_kernel, len(acts)),
        out_shape=jax.ShapeDtypeStruct((m, d), F32),
        grid=(m // tm,), in_specs=in_specs, out_specs=pl.BlockSpec((tm, d), row),
        compiler_params=_cparams(("arbitrary",)),
        name="out_res",
    )(*acts, w, x2, mod_arr)


def _mlp_kernel(final, f_chunk, *refs):
    if final:
        x_ref, sh_ref, sc_ref, gate_ref, g_ref, wu_ref, wd_ref, fg_ref, o_ref = refs
    else:
        x_ref, sh_ref, sc_ref, gate_ref, g_ref, wu_ref, wd_ref, o_ref = refs
    x = x_ref[...]
    h = _rms_mod(x, g_ref[...], sc_ref[...], sh_ref[...]).astype(BF16)
    d_ff = wu_ref.shape[1]
    acc = None
    for c in range(d_ff // f_chunk):
        cs = slice(c * f_chunk, (c + 1) * f_chunk)
        a = jnp.maximum(_dot(h, wu_ref[:, cs]), 0.0)
        part = _dot((a * a).astype(BF16), wd_ref[cs, :])
        acc = part if acc is None else acc + part
    xo = x + gate_ref[...] * acc
    if final:
        y = xo * lax.rsqrt(jnp.mean(xo * xo, axis=-1, keepdims=True) + EPS)
        xo = y * fg_ref[...]
    o_ref[...] = xo


def _mlp_call(prompt, x2, mod_arr, g, wu, wd, final_g, rows_per_batch):
    m, d = x2.shape
    tm = TM_ROWS if prompt else m
    tpb = rows_per_batch // tm if prompt else 1
    row = lambda i: (i, 0)
    const2 = lambda i: (0, 0)
    final = final_g is not None
    in_specs = [pl.BlockSpec((tm, d), row),
                _mod_spec(prompt, tm, d, 3, tpb), _mod_spec(prompt, tm, d, 4, tpb),
                _mod_spec(prompt, tm, d, 5, tpb),
                pl.BlockSpec((1, d), const2),
                pl.BlockSpec(wu.shape, const2, pipeline_mode=pl.Buffered(1)),
                pl.BlockSpec(wd.shape, const2, pipeline_mode=pl.Buffered(1))]
    args = [x2, mod_arr, mod_arr, mod_arr, g, wu, wd]
    if final:
        in_specs.append(pl.BlockSpec((1, d), const2))
        args.append(final_g)
    return pl.pallas_call(
        functools.partial(_mlp_kernel, final, 1024),
        out_shape=jax.ShapeDtypeStruct((m, d), F32),
        grid=(m // tm,), in_specs=in_specs, out_specs=pl.BlockSpec((tm, d), row),
        compiler_params=_cparams(("arbitrary",)),
        name="mlp",
    )(*args)


def _sb_in_kernel(x_ref, sh_ref, sc_ref, g_ref, w_ref, q_ref, kb_ref, vb_ref, k32_ref, v32_ref):
    d = x_ref.shape[1]
    h = _rms_mod(x_ref[...], g_ref[...], sc_ref[...], sh_ref[...]).astype(BF16)
    q_ref[...] = (_dot(h, w_ref[:, 0:d]) * (DH ** -0.5)).astype(q_ref.dtype)
    k = _dot(h, w_ref[:, d:2 * d])
    k32_ref[...] = k
    kb_ref[...] = k.astype(BF16)
    v = _dot(h, w_ref[:, 2 * d:3 * d])
    v32_ref[...] = v
    vb_ref[...] = v.astype(BF16)


def _sb_in_call(prompt, x2, mod_arr, g, w, rows_per_batch):
    m, d = x2.shape
    tm = TM_ROWS if prompt else m
    tpb = rows_per_batch // tm if prompt else 1
    row = lambda i: (i, 0)
    in_specs = [pl.BlockSpec((tm, d), row),
                _mod_spec(prompt, tm, d, 0, tpb), _mod_spec(prompt, tm, d, 1, tpb),
                pl.BlockSpec((1, d), lambda i: (0, 0)),
                pl.BlockSpec(w.shape, lambda i: (0, 0))]
    out_shape = [jax.ShapeDtypeStruct((m, d), BF16),
                 jax.ShapeDtypeStruct((m, d), BF16), jax.ShapeDtypeStruct((m, d), BF16),
                 jax.ShapeDtypeStruct((m, d), F32), jax.ShapeDtypeStruct((m, d), F32)]
    return pl.pallas_call(
        _sb_in_kernel,
        out_shape=out_shape, grid=(m // tm,), in_specs=in_specs,
        out_specs=[pl.BlockSpec((tm, d), row)] * 5,
        compiler_params=_cparams(("arbitrary",)),
        name="sb_in",
    )(x2, mod_arr, mod_arr, g, w)


def _trunk(prompt, x, mods, past, wts):
    nb, rpb, d = x.shape
    m = nb * rpb
    a_w = 8 * DH
    x2 = x.reshape(m, d)
    (norm_g, wm, wf, bf, wp, ps, w_out_ab, w_in_sb, w_out_sb, w_up, w_down, final_g) = wts

    if prompt:
        hist, p0 = None, 0
    else:
        cfk, cfv, cflogf, spool, csk, csv = past
        p0 = cfk.shape[1]
        hist = jnp.pad(spool, ((0, 0), (HIST_ROWS - spool.shape[1], 0), (0, 0)))
    outs = _ab_in_call(prompt, x2, mods[0], norm_g[0, 0][None], wm, wf, bf, wp, ps, hist, rpb, p0)
    if prompt:
        q, kb, vb, k32, v32, logf, fcum, pooled, pstate = outs
        fc = fcum.reshape(nb, rpb, a_w // LANES, 2)
        a_out = _fox_call(q.reshape(nb, rpb, a_w), kb.reshape(nb, rpb, a_w),
                          vb.reshape(nb, rpb, a_w),
                          fc.transpose(0, 2, 1, 3), fc.transpose(0, 2, 3, 1))
    else:
        q, kb, vb, k32, v32, logf, pooled, pstate = outs
        lfn_t = logf.reshape(nb, rpb, 8).transpose(0, 2, 1)
        lfc_t = cflogf.transpose(0, 2, 1)
        a_out = _dec_call(True, q.reshape(nb, rpb, a_w), kb.reshape(nb, rpb, a_w),
                          vb.reshape(nb, rpb, a_w), cfk.reshape(nb, p0, a_w),
                          cfv.reshape(nb, p0, a_w), lfn_t, lfc_t)
    x2 = _out_res_call(prompt, [a_out.reshape(m, a_w), pooled], w_out_ab, x2, mods[0], 2, rpb)
    x2 = _mlp_call(prompt, x2, mods[0], norm_g[0, 1][None], w_up[0], w_down[0], None, rpb)

    q1, kb1, vb1, sk32, sv32 = _sb_in_call(prompt, x2, mods[1], norm_g[1, 0][None], w_in_sb, rpb)
    if prompt:
        s_out = _sb_call(q1.reshape(nb, rpb, d), kb1.reshape(nb, rpb, d), vb1.reshape(nb, rpb, d))
    else:
        s_out = _dec_call(False, q1.reshape(nb, rpb, d), kb1.reshape(nb, rpb, d),
                          vb1.reshape(nb, rpb, d), csk.reshape(nb, p0, d), csv.reshape(nb, p0, d))
    x2 = _out_res_call(prompt, [s_out.reshape(m, d)], w_out_sb, x2, mods[1], 2, rpb)
    y = _mlp_call(prompt, x2, mods[1], norm_g[1, 1][None], w_up[1], w_down[1], final_g[None], rpb)

    states = (k32.reshape(1, nb, rpb, 8, DH), v32.reshape(1, nb, rpb, 8, DH),
              logf.reshape(1, nb, rpb, 8), pstate[:, 1:, :][None],
              sk32.reshape(1, nb, rpb, d // DH, DH), sv32.reshape(1, nb, rpb, d // DH, DH))
    return y.reshape(nb, rpb, d), states


def kernel(x_prompt, x_sample, c_prompt, c_sample, cache_fox_k, cache_fox_v, cache_fox_logf, state_pool, cache_sb_k, cache_sb_v, w_ada, b_ada, norm_g, w_in_ab, b_forget, w_pool, pool_scale, w_out_ab, w_in_sb, w_out_sb, w_up, w_down, final_g):
    b, l, d = x_prompt.shape
    db, ds, _ = x_sample.shape
    a_w = 8 * DH

    c_all = jnp.concatenate([c_prompt, c_sample, jnp.zeros((ADA_ROWS - b - db, d), F32)], axis=0)
    mod = _ada_call(c_all, w_ada, b_ada)
    mods_p = [mod[i, :b][:, None, :] for i in range(mod.shape[0])]
    mods_s = [jnp.repeat(mod[i, b:b + db], ds, axis=0) for i in range(mod.shape[0])]

    w_in = w_in_ab[0]
    wm = jnp.concatenate([w_in[:, :3 * a_w], w_in[:, 3 * a_w + 8:]], axis=1).astype(BF16)
    wf = jnp.pad(w_in[:, 3 * a_w:3 * a_w + 8], ((0, 0), (0, LANES - 8))).astype(BF16)
    bf = jnp.pad(b_forget[0][None], ((0, 0), (0, LANES - 8)))
    wts = (norm_g, wm, wf, bf, w_pool[0].astype(BF16), pool_scale[0][None],
           w_out_ab[0].astype(BF16), w_in_sb[0].astype(BF16), w_out_sb[0].astype(BF16),
           w_up.astype(BF16), w_down.astype(BF16), final_g)

    y_p, st_p = _trunk(True, x_prompt, mods_p, None, wts)
    past = (cache_fox_k[0], cache_fox_v[0], cache_fox_logf[0], state_pool[0],
            cache_sb_k[0], cache_sb_v[0])
    y_s, st_s = _trunk(False, x_sample, mods_s, past, wts)
    return (y_p, y_s) + st_p + st_s
```

```python
import functools

import jax
import jax.numpy as jnp
from jax import lax
from jax.experimental import pallas as pl
from jax.experimental.pallas import tpu as pltpu

F32 = jnp.float32
BF16 = jnp.bfloat16

DH = 64
EPS = 1e-6
POOL_WINDOWS = (2, 4, 8, 16)
POOL_GC = 128
HIST_ROWS = 16
NEG_BIG = -1e30
LOG2E = 1.4426950408889634
LANES = 128
VMEM_LIMIT = 56 * 1024 * 1024

UNDERFLOW_EXP = 104.0
NORM_MARGIN = 1.01

T_ATT = 256
NQ_ATT = 2
TK_DEC = 512
TM_ROWS = 512
ADA_ROWS = 32
ADA_TN = 1536


def _cparams(sem):
    return pltpu.CompilerParams(dimension_semantics=sem, vmem_limit_bytes=VMEM_LIMIT)


def _rms_mod(x, g, sc, sh):
    y = x * lax.rsqrt(jnp.mean(x * x, axis=-1, keepdims=True) + EPS)
    return (y * g) * (1.0 + sc) + sh


def _softplus(z):
    return jnp.maximum(z, 0.0) + jnp.log(1.0 + jnp.exp2(jnp.abs(z) * (-LOG2E)))


def _split3(x):
    hi = x.astype(BF16)
    r = x - hi.astype(F32)
    mid = r.astype(BF16)
    lo = (r - mid.astype(F32)).astype(BF16)
    return hi, mid, lo


def _split2(x):
    hi = x.astype(BF16)
    lo = (x - hi.astype(F32)).astype(BF16)
    return hi, lo


def _dot(a, b):
    return jnp.dot(a, b, preferred_element_type=F32)


def _dot_nt(a, b):
    return lax.dot_general(a, b, (((1,), (1,)), ((), ())), preferred_element_type=F32)


def _iota(shape, dim):
    return lax.broadcasted_iota(jnp.int32, shape, dim)


def _ada_kernel(c_ref, w_ref, b_ref, o_ref):
    c = c_ref[...]
    cond = (c / (1.0 + jnp.exp(-c))).astype(BF16)
    o_ref[...] = _dot(cond, w_ref[...].astype(BF16)) + b_ref[...]


def _ada_call(c_all, w_ada, b_ada):
    depth, d, n = w_ada.shape
    return pl.pallas_call(
        _ada_kernel,
        out_shape=jax.ShapeDtypeStruct((depth, ADA_ROWS, n), F32),
        grid=(depth, n // ADA_TN),
        in_specs=[
            pl.BlockSpec((ADA_ROWS, d), lambda i, j: (0, 0)),
            pl.BlockSpec((None, d, ADA_TN), lambda i, j: (i, 0, j)),
            pl.BlockSpec((None, 1, ADA_TN), lambda i, j: (i, 0, j)),
        ],
        out_specs=pl.BlockSpec((None, ADA_ROWS, ADA_TN), lambda i, j: (i, 0, j)),
        compiler_params=_cparams(("arbitrary", "arbitrary")),
        name="ada_mod",
    )(c_all, w_ada, b_ada.reshape(depth, 1, n))


def _mod_spec(prompt, tm, d, chunk, tiles_per_batch):
    if prompt:
        return pl.BlockSpec((None, 1, d), lambda i: (i // tiles_per_batch, 0, chunk))
    return pl.BlockSpec((tm, d), lambda i: (i, chunk))


def _ab_in_kernel(prompt, tm, ng, rg, tiles_per_batch, p0_static, *refs):
    if prompt:
        (x_ref, sh_ref, sc_ref, g_ref, wm_ref, wf_ref, bf_ref, wp_ref, ps_ref,
         q_ref, kb_ref, vb_ref, k32_ref, v32_ref, lf_ref, fc_ref, pooled_ref, pst_ref,
         ext_ref, carry_ref) = refs
        hist_ref = None
    else:
        (x_ref, sh_ref, sc_ref, g_ref, wm_ref, wf_ref, bf_ref, wp_ref, ps_ref, hist_ref,
         q_ref, kb_ref, vb_ref, k32_ref, v32_ref, lf_ref, pooled_ref, pst_ref,
         ext_ref) = refs
    a_w = 8 * DH
    h = _rms_mod(x_ref[...], g_ref[...], sc_ref[...], sh_ref[...]).astype(BF16)
    proj = _dot(h, wm_ref[...])
    k = proj[:, a_w:2 * a_w]
    v = proj[:, 2 * a_w:3 * a_w]
    u = proj[:, 3 * a_w:4 * a_w]
    q_ref[...] = (proj[:, 0:a_w] * (DH ** -0.5)).astype(q_ref.dtype)
    k32_ref[...] = k
    v32_ref[...] = v
    kb_ref[...] = k.astype(BF16)
    vb_ref[...] = v.astype(BF16)

    fl = _dot(h, wf_ref[...]) + bf_ref[...]
    logf = jnp.minimum(fl, 0.0) - jnp.log(1.0 + jnp.exp(-jnp.abs(fl)))
    lf_ref[...] = logf[:, 0:8]

    if prompt:
        i = pl.program_id(0)
        tile_in_batch = i % tiles_per_batch
        first = tile_in_batch == 0
        tri = (_iota((tm, tm), 0) >= _iota((tm, tm), 1)).astype(BF16)
        hi, mid, lo = _split3(logf)
        cum = _dot(tri, hi) + _dot(tri, mid) + _dot(tri, lo)
        carry = jnp.where(first, 0.0, carry_ref[...])
        fcum = cum + carry
        carry_ref[...] = fcum[tm - 1:tm, :]
        fc_ref[...] = fcum[:, 0:8]

        @pl.when(first)
        def _():
            ext_ref[:, 0:HIST_ROWS, :] = jnp.zeros((ng, HIST_ROWS, a_w), F32)

        @pl.when(jnp.logical_not(first))
        def _():
            ext_ref[:, 0:HIST_ROWS, :] = ext_ref[:, rg:rg + HIST_ROWS, :]

        pos = tile_in_batch * tm + _iota((tm, 1), 0)
    else:
        ext_ref[:, 0:HIST_ROWS, :] = hist_ref[...]
        pos = p0_static + (_iota((tm, 1), 0) & (rg - 1))

    u3 = u.reshape(ng, rg, a_w)
    ext_ref[:, HIST_ROWS:HIST_ROWS + rg, :] = u3
    pst_ref[...] = u3[:, rg - HIST_ROWS:rg, :]
    for g, w in enumerate(POOL_WINDOWS):
        cs = slice(g * POOL_GC, (g + 1) * POOL_GC)
        acc = u3[:, :, cs]
        for kk in range(1, w):
            acc = acc + ext_ref[:, pl.ds(HIST_ROWS - kk, rg), cs]
        cnt = jnp.minimum(pos + 1, w).astype(F32)
        pooled = acc.reshape(tm, POOL_GC) / cnt - u[:, cs]
        pg = _dot(pooled.astype(BF16), wp_ref[g]) * ps_ref[:, cs]
        pooled_ref[:, cs] = pg.astype(pooled_ref.dtype)


def _ab_in_call(prompt, x2, mod_arr, g, wm, wf, bf, wp, ps, hist, rows_per_batch, p0):
    m, d = x2.shape
    a_w = 8 * DH
    if prompt:
        tm, ng, rg = TM_ROWS, 1, TM_ROWS
        tpb = rows_per_batch // tm
        nb = m // rows_per_batch
    else:
        tm, ng, rg = m, m // rows_per_batch, rows_per_batch
        tpb = 1
        nb = ng
    grid = (m // tm,)
    row = lambda i: (i, 0)
    const2 = lambda i: (0, 0)
    in_specs = [
        pl.BlockSpec((tm, d), row),
        _mod_spec(prompt, tm, d, 0, tpb),
        _mod_spec(prompt, tm, d, 1, tpb),
        pl.BlockSpec((1, d), const2),
        pl.BlockSpec(wm.shape, const2),
        pl.BlockSpec(wf.shape, const2),
        pl.BlockSpec(bf.shape, const2),
        pl.BlockSpec(wp.shape, lambda i: (0, 0, 0)),
        pl.BlockSpec(ps.shape, const2),
    ]
    args = [x2, mod_arr, mod_arr, g, wm, wf, bf, wp, ps]
    out_shape = [
        jax.ShapeDtypeStruct((m, a_w), BF16),
        jax.ShapeDtypeStruct((m, a_w), BF16),
        jax.ShapeDtypeStruct((m, a_w), BF16),
        jax.ShapeDtypeStruct((m, a_w), F32),
        jax.ShapeDtypeStruct((m, a_w), F32),
        jax.ShapeDtypeStruct((m, 8), F32),
    ]
    out_specs = [pl.BlockSpec((tm, a_w), row)] * 5 + [pl.BlockSpec((tm, 8), row)]
    scratch = [pltpu.VMEM((ng, HIST_ROWS + rg, a_w), F32)]
    if prompt:
        out_shape.append(jax.ShapeDtypeStruct((m, 8), F32))
        out_specs.append(pl.BlockSpec((tm, 8), row))
        scratch.append(pltpu.VMEM((1, LANES), F32))
    else:
        in_specs.append(pl.BlockSpec(hist.shape, lambda i: (0, 0, 0)))
        args.append(hist)
    out_shape.append(jax.ShapeDtypeStruct((m, a_w), BF16))
    out_specs.append(pl.BlockSpec((tm, a_w), row))
    out_shape.append(jax.ShapeDtypeStruct((nb, HIST_ROWS, a_w), F32))
    if prompt:
        out_specs.append(pl.BlockSpec((1, HIST_ROWS, a_w), lambda i: (i // tpb, 0, 0)))
    else:
        out_specs.append(pl.BlockSpec((nb, HIST_ROWS, a_w), lambda i: (0, 0, 0)))
    return pl.pallas_call(
        functools.partial(_ab_in_kernel, prompt, tm, ng, rg, tpb, p0),
        out_shape=out_shape, grid=grid, in_specs=in_specs, out_specs=out_specs,
        scratch_shapes=scratch,
        compiler_params=_cparams(("arbitrary",)),
        name="ab_in_prompt" if prompt else "ab_in_sample",
    )(*args)


def _head_sq_norms(x):
    sel = ((_iota((LANES, LANES), 0) >> _log2(DH)) == _iota((LANES, LANES), 1)).astype(BF16)
    xf = x.astype(F32)
    return _dot((xf * xf).astype(BF16), sel)


def _fox_kernel(q_ref, k_ref, v_ref, fq_ref, fk_ref, ft_ref, o_ref,
                m_ref, l_ref, acc_ref, kn_ref):
    t, nq = T_ATT, NQ_ATT
    qg = pl.program_id(2)
    lane = _iota((1, LANES), 1)
    lane2 = _iota((1, 2), 1)

    @pl.when(qg == 0)
    def _():
        chunk = 2 * t

        def body(i, mx):
            ks = pl.multiple_of(i * chunk, chunk)
            n2 = _head_sq_norms(k_ref[pl.ds(ks, chunk), :])
            return jnp.maximum(mx, jnp.max(n2, axis=0, keepdims=True))

        mx = lax.fori_loop(0, k_ref.shape[0] // chunk, body, jnp.zeros((1, LANES), F32))
        kn_ref[...] = jnp.sqrt(mx)

    qh, fq, bound0 = [], [], []
    for s in range(nq):
        q2 = q_ref[s * t:(s + 1) * t, :]
        zero = jnp.zeros_like(q2)
        qh.append((jnp.where(lane < DH, q2, zero), jnp.where(lane >= DH, q2, zero)))
        fq.append(fq_ref[s * t:(s + 1) * t, :])
        qn = jnp.sqrt(jnp.max(_head_sq_norms(q2), axis=0, keepdims=True))
        bound0.append((qn * kn_ref[...])[:, 0:2] * NORM_MARGIN + fq[s][0:1, :])
    m_ref[...] = jnp.full(m_ref.shape, NEG_BIG, F32)
    l_ref[...] = jnp.zeros(l_ref.shape, F32)
    acc_ref[...] = jnp.zeros(acc_ref.shape, F32)

    def step(n, diag):
        chains = [(s, j) for s in range(nq) for j in range(2)]
        kbs = [qg * nq + s - n for s in range(nq)]
        kss = [pl.multiple_of((kb if diag else jnp.maximum(kb, 0)) * t, t) for kb in kbs]
        causal = _iota((t, t), 1) <= _iota((t, t), 0)
        fks = [fk_ref[:, pl.ds(ks, t)] for ks in kss]
        sc = []
        for s, j in chains:
            col = fq[s][:, j:j + 1]
            if not diag:
                col = col + jnp.where(kbs[s] >= 0, 0.0, NEG_BIG)
            x = _dot_nt(qh[s][j], k_ref[pl.ds(kss[s], t), :]) + (col - fks[s][j:j + 1, :])
            sc.append(jnp.where(causal, x, NEG_BIG) if diag else x)
        m_prev = [m_ref[2 * s + j] for s, j in chains]
        m_new = [jnp.maximum(mp, jnp.max(x, axis=1, keepdims=True)) for mp, x in zip(m_prev, sc)]
        p = [jnp.exp(x - mn) for x, mn in zip(sc, m_new)]
        pv = [_dot(pp.astype(BF16), v_ref[pl.ds(kss[s], t), :]) for pp, (s, j) in zip(p, chains)]
        for i, (s, j) in enumerate(chains):
            idx = 2 * s + j
            alpha = jnp.exp(m_prev[i] - m_new[i])
            l_ref[idx] = alpha * l_ref[idx] + jnp.sum(p[i], axis=1, keepdims=True)
            acc_ref[idx] = alpha * acc_ref[idx] + pv[i]
            m_ref[idx] = m_new[i]

    def more(n):
        go = False
        for s in range(nq):
            kb = qg * nq + s - n
            ft = ft_ref[pl.ds(jnp.maximum(kb, 0), 1), :]
            m_min = jnp.where(lane2 == 0, jnp.min(m_ref[2 * s], axis=0, keepdims=True),
                              jnp.min(m_ref[2 * s + 1], axis=0, keepdims=True))
            top = jnp.max(bound0[s] - ft - m_min)
            go = jnp.logical_or(go, jnp.logical_and(kb >= 0, top > -UNDERFLOW_EXP))
        return go

    step(0, True)

    def body(c):
        n, _ = c
        step(n, False)
        return n + 1, more(n + 1)

    lax.while_loop(lambda c: c[1], body, (jnp.int32(1), more(1)))
    for s in range(nq):
        o = jnp.where(lane < DH, acc_ref[2 * s] / l_ref[2 * s], acc_ref[2 * s + 1] / l_ref[2 * s + 1])
        o_ref[s * t:(s + 1) * t, :] = o.astype(o_ref.dtype)


def _fox_call(q, kb, vb, fq_arr, fk_arr, ft_arr):
    b, l, a_w = q.shape
    hp = a_w // LANES
    t, nq = T_ATT, NQ_ATT
    tq = t * nq
    return pl.pallas_call(
        _fox_kernel,
        out_shape=jax.ShapeDtypeStruct((b, l, a_w), BF16),
        grid=(b, hp, l // tq),
        in_specs=[
            pl.BlockSpec((None, tq, LANES), lambda bi, h, qi: (bi, qi, h)),
            pl.BlockSpec((None, l, LANES), lambda bi, h, qi: (bi, 0, h)),
            pl.BlockSpec((None, l, LANES), lambda bi, h, qi: (bi, 0, h)),
            pl.BlockSpec((None, None, tq, 2), lambda bi, h, qi: (bi, h, qi, 0)),
            pl.BlockSpec((None, None, 2, l), lambda bi, h, qi: (bi, h, 0, 0)),
            pl.BlockSpec((None, None, l // t, 2), lambda bi, h, qi: (bi, h, 0, 0)),
        ],
        out_specs=pl.BlockSpec((None, tq, LANES), lambda bi, h, qi: (bi, qi, h)),
        scratch_shapes=[pltpu.VMEM((2 * nq, t, 1), F32), pltpu.VMEM((2 * nq, t, 1), F32),
                        pltpu.VMEM((2 * nq, t, LANES), F32), pltpu.VMEM((1, LANES), F32)],
        compiler_params=_cparams(("arbitrary", "arbitrary", "arbitrary")),
        name="fox_prompt",
    )(q, kb, vb, fq_arr, fk_arr, ft_arr)


def _sb_kernel(q_ref, k_ref, v_ref, o_ref, carry_ref, acc_ref):
    t, nq = T_ATT, NQ_ATT
    qg = pl.program_id(2)
    lane = _iota((1, LANES), 1)
    qh = []
    for s in range(nq):
        q2 = q_ref[s * t:(s + 1) * t, :]
        zero = jnp.zeros_like(q2)
        qh.append((jnp.where(lane < DH, q2, zero), jnp.where(lane >= DH, q2, zero)))
    carry_ref[...] = jnp.zeros(carry_ref.shape, F32)
    acc_ref[...] = jnp.zeros(acc_ref.shape, F32)

    def step(n, diag):
        tri = (_iota((t, t), 0) > _iota((t, t), 1)).astype(BF16)
        valid = _iota((t, t), 1) < _iota((t, t), 0)
        chains = [(s, j) for s in range(nq) for j in range(2)]
        kbs = [qg * nq + s - n for s in range(nq)]
        kss = [pl.multiple_of((kb if diag else jnp.maximum(kb, 0)) * t, t) for kb in kbs]
        z = [_dot_nt(qh[s][j], k_ref[pl.ds(kss[s], t), :]) for s, j in chains]
        sp = [_softplus(zz) for zz in z]
        spm = [jnp.where(valid, x, 0.0) for x in sp] if diag else sp
        later = [_dot(x.astype(BF16), tri) for x in spm]
        vs = []
        for s in range(nq):
            v2 = v_ref[pl.ds(kss[s], t), :]
            if not diag:
                v2 = jnp.where(kbs[s] >= 0, v2, jnp.zeros_like(v2))
            vs.append(v2)
        for i, (s, j) in enumerate(chains):
            idx = 2 * s + j
            c = carry_ref[idx]
            a = jnp.exp((z[i] - sp[i]) - later[i] - c)
            if diag:
                a = jnp.where(valid, a, 0.0)
            acc_ref[idx] = acc_ref[idx] + _dot(a.astype(BF16), vs[s])
            carry_ref[idx] = c + jnp.sum(spm[i], axis=1, keepdims=True)

    def more(n):
        in_range = n <= qg * nq + nq - 1
        return jnp.logical_and(in_range, jnp.min(carry_ref[...]) < UNDERFLOW_EXP)

    step(0, True)

    def body(c):
        n, _ = c
        step(n, False)
        return n + 1, more(n + 1)

    lax.while_loop(lambda c: c[1], body, (jnp.int32(1), more(1)))
    for s in range(nq):
        o = jnp.where(lane < DH, acc_ref[2 * s], acc_ref[2 * s + 1])
        o_ref[s * t:(s + 1) * t, :] = o.astype(o_ref.dtype)


def _sb_call(q, kb, vb):
    b, l, d = q.shape
    hp = d // LANES
    t, nq = T_ATT, NQ_ATT
    tq = t * nq
    return pl.pallas_call(
        _sb_kernel,
        out_shape=jax.ShapeDtypeStruct((b, l, d), BF16),
        grid=(b, hp, l // tq),
        in_specs=[
            pl.BlockSpec((None, tq, LANES), lambda bi, h, qi: (bi, qi, h)),
            pl.BlockSpec((None, l, LANES), lambda bi, h, qi: (bi, 0, h)),
            pl.BlockSpec((None, l, LANES), lambda bi, h, qi: (bi, 0, h)),
        ],
        out_specs=pl.BlockSpec((None, tq, LANES), lambda bi, h, qi: (bi, qi, h)),
        scratch_shapes=[pltpu.VMEM((2 * nq, t, 1), F32), pltpu.VMEM((2 * nq, t, LANES), F32)],
        compiler_params=_cparams(("arbitrary", "arbitrary", "arbitrary")),
        name="sb_prompt",
    )(q, kb, vb)


def _log2(n):
    assert n & (n - 1) == 0, n
    return n.bit_length() - 1


def _rep_matrix(rows, n_heads, ds):
    return ((_iota((rows, ds), 0) >> _log2(n_heads)) == _iota((rows, ds), 1)).astype(BF16)


def _head_mask(rows, n_heads, width):
    return (_iota((rows, width), 0) & (n_heads - 1)) == (_iota((rows, width), 1) >> _log2(DH))


def _fox_dec_kernel(n_heads, ds, q_ref, kn_ref, vn_ref, lfn_ref, kc_ref, vc_ref, lfc_ref,
                    o_ref, qbd_ref, m_ref, l_ref, acc_ref, cf_ref):
    rows = ds * n_heads
    width = n_heads * DH
    step_i = pl.program_id(1)
    n_steps = pl.num_programs(1)

    def update(s, vv):
        m_prev = m_ref[...]
        m_new = jnp.maximum(m_prev, jnp.max(s, axis=1, keepdims=True))
        alpha = jnp.exp(m_prev - m_new)
        p = jnp.exp(s - m_new)
        l_ref[...] = alpha * l_ref[...] + jnp.sum(p, axis=1, keepdims=True)
        acc_ref[...] = alpha * acc_ref[...] + _dot(p.astype(BF16), vv)
        m_ref[...] = m_new

    @pl.when(step_i == 0)
    def _():
        qrep = _dot(_rep_matrix(rows, n_heads, ds), q_ref[...])
        qbd_ref[...] = jnp.where(_head_mask(rows, n_heads, width), qrep, 0.0).astype(BF16)
        m_ref[...] = jnp.full(m_ref.shape, NEG_BIG, F32)
        l_ref[...] = jnp.zeros(l_ref.shape, F32)
        acc_ref[...] = jnp.zeros(acc_ref.shape, F32)
        a = jnp.concatenate([lfn_ref[...]] * ds, axis=0)
        ri = _iota((rows, ds), 0) >> _log2(n_heads)
        cj = _iota((rows, ds), 1)
        am = jnp.where(cj <= ri, a, 0.0)
        tri = (_iota((ds, ds), 0) > _iota((ds, ds), 1)).astype(BF16)
        hi, mid, lo = _split3(am)
        bias = _dot(hi, tri) + _dot(mid, tri) + _dot(lo, tri)
        s = _dot_nt(qbd_ref[...], kn_ref[...]) + bias
        s = jnp.where(cj <= ri, s, NEG_BIG)
        update(s, vn_ref[...])
        cf_ref[...] = jnp.sum(am, axis=1, keepdims=True)

    @pl.when(step_i > 0)
    def _():
        tk = kc_ref.shape[0]
        a = jnp.concatenate([lfc_ref[...]] * ds, axis=0)
        tri = (_iota((tk, tk), 0) > _iota((tk, tk), 1)).astype(BF16)
        hi, mid, lo = _split3(a)
        later = _dot(hi, tri) + _dot(mid, tri) + _dot(lo, tri)
        cf = cf_ref[...]
        s = _dot_nt(qbd_ref[...], kc_ref[...].astype(BF16)) + (later + cf)
        update(s, vc_ref[...].astype(BF16))
        cf_ref[...] = cf + jnp.sum(a, axis=1, keepdims=True)

    @pl.when(step_i == n_steps - 1)
    def _():
        o_full = acc_ref[...] / l_ref[...]
        om = jnp.where(_head_mask(rows, n_heads, width), o_full, 0.0).astype(BF16)
        rep_t = ((_iota((ds, rows), 1) >> _log2(n_heads)) == _iota((ds, rows), 0)).astype(BF16)
        o_ref[...] = _dot(rep_t, om).astype(o_ref.dtype)


def _sb_dec_kernel(n_heads, ds, q_ref, kn_ref, vn_ref, kc_ref, vc_ref,
                   o_ref, qbd_ref, acc_ref, carry_ref):
    rows = ds * n_heads
    width = n_heads * DH
    step_i = pl.program_id(1)
    n_steps = pl.num_programs(1)

    @pl.when(step_i == 0)
    def _():
        qrep = _dot(_rep_matrix(rows, n_heads, ds), q_ref[...])
        qbd_ref[...] = jnp.where(_head_mask(rows, n_heads, width), qrep, 0.0).astype(BF16)
        ri = _iota((rows, ds), 0) >> _log2(n_heads)
        cj = _iota((rows, ds), 1)
        valid = cj < ri
        z = _dot_nt(qbd_ref[...], kn_ref[...])
        sp = _softplus(z)
        spm = jnp.where(valid, sp, 0.0)
        tri = (_iota((ds, ds), 0) > _iota((ds, ds), 1)).astype(BF16)
        hi, lo = _split2(spm)
        later = _dot(hi, tri) + _dot(lo, tri)
        a = jnp.where(valid, jnp.exp((z - sp) - later), 0.0)
        acc_ref[...] = _dot(a.astype(BF16), vn_ref[...])
        carry_ref[...] = jnp.sum(spm, axis=1, keepdims=True)

    @pl.when(step_i > 0)
    def _():
        tk = kc_ref.shape[0]
        z = _dot_nt(qbd_ref[...], kc_ref[...].astype(BF16))
        sp = _softplus(z)
        tri = (_iota((tk, tk), 0) > _iota((tk, tk), 1)).astype(BF16)
        hi, lo = _split2(sp)
        later = _dot(hi, tri) + _dot(lo, tri)
        c = carry_ref[...]
        a = jnp.exp((z - sp) - later - c)
        acc_ref[...] = acc_ref[...] + _dot(a.astype(BF16), vc_ref[...].astype(BF16))
        carry_ref[...] = c + jnp.sum(sp, axis=1, keepdims=True)

    @pl.when(step_i == n_steps - 1)
    def _():
        om = jnp.where(_head_mask(rows, n_heads, width), acc_ref[...], 0.0).astype(BF16)
        rep_t = ((_iota((ds, rows), 1) >> _log2(n_heads)) == _iota((ds, rows), 0)).astype(BF16)
        o_ref[...] = _dot(rep_t, om).astype(o_ref.dtype)


def _dec_call(fox, q, kn, vn, kc, vc, lfn_t=None, lfc_t=None):
    db, ds, width = q.shape
    n_heads = width // DH
    rows = ds * n_heads
    p = kc.shape[1]
    nkb = p // TK_DEC
    cache_blk = lambda s: jnp.minimum(nkb - s, nkb - 1)
    cache_idx = lambda b, s: (b, cache_blk(s), 0)
    new_idx = lambda b, s: (b, 0, 0)
    in_specs = [pl.BlockSpec((None, ds, width), new_idx)] * 3
    args = [q, kn, vn]
    if fox:
        in_specs.append(pl.BlockSpec((None, n_heads, ds), new_idx))
        args.append(lfn_t)
    in_specs += [pl.BlockSpec((None, TK_DEC, width), cache_idx)] * 2
    args += [kc, vc]
    scratch = [pltpu.VMEM((rows, width), BF16)]
    if fox:
        in_specs.append(pl.BlockSpec((None, n_heads, TK_DEC),
                                     lambda b, s: (b, 0, cache_blk(s))))
        args.append(lfc_t)
        scratch += [pltpu.VMEM((rows, 1), F32), pltpu.VMEM((rows, 1), F32),
                    pltpu.VMEM((rows, width), F32), pltpu.VMEM((rows, 1), F32)]
        body = functools.partial(_fox_dec_kernel, n_heads, ds)
    else:
        scratch += [pltpu.VMEM((rows, width), F32), pltpu.VMEM((rows, 1), F32)]
        body = functools.partial(_sb_dec_kernel, n_heads, ds)
    return pl.pallas_call(
        body,
        out_shape=jax.ShapeDtypeStruct((db, ds, width), BF16),
        grid=(db, nkb + 1),
        in_specs=in_specs,
        out_specs=pl.BlockSpec((None, ds, width), new_idx),
        scratch_shapes=scratch,
        compiler_params=_cparams(("arbitrary", "arbitrary")),
        name="fox_decode" if fox else "sb_decode",
    )(*args)


def _out_res_kernel(n_act, *refs):
    acts = refs[:n_act]
    w_ref, x_ref, gate_ref, o_ref = refs[n_act:]
    y = None
    off = 0
    for a_ref in acts:
        kdim = a_ref.shape[1]
        part = _dot(a_ref[...], w_ref[off:off + kdim, :])
        y = part if y is None else y + part
        off += kdim
    o_ref[...] = x_ref[...] + gate_ref[...] * y


def _out_res_call(prompt, acts, w, x2, mod_arr, gate_chunk, rows_per_batch):
    m, d = x2.shape
    tm = TM_ROWS if prompt else m
    tpb = rows_per_batch // tm if prompt else 1
    row = lambda i: (i, 0)
    in_specs = [pl.BlockSpec((tm, a.shape[1]), row) for a in acts]
    in_specs += [pl.BlockSpec(w.shape, lambda i: (0, 0)), pl.BlockSpec((tm, d), row),
                 _mod_spec(prompt, tm, d, gate_chunk, tpb)]
    return pl.pallas_call(
        functools.partial(_out_res_kernel, len(acts)),
        out_shape=jax.ShapeDtypeStruct((m, d), F32),
        grid=(m // tm,), in_specs=in_specs, out_specs=pl.BlockSpec((tm, d), row),
        compiler_params=_cparams(("arbitrary",)),
        name="out_res",
    )(*acts, w, x2, mod_arr)


def _mlp_kernel(final, f_chunk, *refs):
    if final:
        x_ref, sh_ref, sc_ref, gate_ref, g_ref, wu_ref, wd_ref, fg_ref, o_ref = refs
    else:
        x_ref, sh_ref, sc_ref, gate_ref, g_ref, wu_ref, wd_ref, o_ref = refs
    x = x_ref[...]
    h = _rms_mod(x, g_ref[...], sc_ref[...], sh_ref[...]).astype(BF16)
    d_ff = wu_ref.shape[1]
    acc = None
    for c in range(d_ff // f_chunk):
        cs = slice(c * f_chunk, (c + 1) * f_chunk)
        a = jnp.maximum(_dot(h, wu_ref[:, cs]), 0.0)
        part = _dot((a * a).astype(BF16), wd_ref[cs, :])
        acc = part if acc is None else acc + part
    xo = x + gate_ref[...] * acc
    if final:
        y = xo * lax.rsqrt(jnp.mean(xo * xo, axis=-1, keepdims=True) + EPS)
        xo = y * fg_ref[...]
    o_ref[...] = xo


def _mlp_call(prompt, x2, mod_arr, g, wu, wd, final_g, rows_per_batch):
    m, d = x2.shape
    tm = TM_ROWS if prompt else m
    tpb = rows_per_batch // tm if prompt else 1
    row = lambda i: (i, 0)
    const2 = lambda i: (0, 0)
    final = final_g is not None
    in_specs = [pl.BlockSpec((tm, d), row),
                _mod_spec(prompt, tm, d, 3, tpb), _mod_spec(prompt, tm, d, 4, tpb),
                _mod_spec(prompt, tm, d, 5, tpb),
                pl.BlockSpec((1, d), const2),
                pl.BlockSpec(wu.shape, const2, pipeline_mode=pl.Buffered(1)),
                pl.BlockSpec(wd.shape, const2, pipeline_mode=pl.Buffered(1))]
    args = [x2, mod_arr, mod_arr, mod_arr, g, wu, wd]
    if final:
        in_specs.append(pl.BlockSpec((1, d), const2))
        args.append(final_g)
    return pl.pallas_call(
        functools.partial(_mlp_kernel, final, 1024),
        out_shape=jax.ShapeDtypeStruct((m, d), F32),
        grid=(m // tm,), in_specs=in_specs, out_specs=pl.BlockSpec((tm, d), row),
        compiler_params=_cparams(("arbitrary",)),
        name="mlp",
    )(*args)


def _sb_in_kernel(x_ref, sh_ref, sc_ref, g_ref, w_ref, q_ref, kb_ref, vb_ref, k32_ref, v32_ref):
    d = x_ref.shape[1]
    h = _rms_mod(x_ref[...], g_ref[...], sc_ref[...], sh_ref[...]).astype(BF16)
    q_ref[...] = (_dot(h, w_ref[:, 0:d]) * (DH ** -0.5)).astype(q_ref.dtype)
    k = _dot(h, w_ref[:, d:2 * d])
    k32_ref[...] = k
    kb_ref[...] = k.astype(BF16)
    v = _dot(h, w_ref[:, 2 * d:3 * d])
    v32_ref[...] = v
    vb_ref[...] = v.astype(BF16)


def _sb_in_call(prompt, x2, mod_arr, g, w, rows_per_batch):
    m, d = x2.shape
    tm = TM_ROWS if prompt else m
    tpb = rows_per_batch // tm if prompt else 1
    row = lambda i: (i, 0)
    in_specs = [pl.BlockSpec((tm, d), row),
                _mod_spec(prompt, tm, d, 0, tpb), _mod_spec(prompt, tm, d, 1, tpb),
                pl.BlockSpec((1, d), lambda i: (0, 0)),
                pl.BlockSpec(w.shape, lambda i: (0, 0))]
    out_shape = [jax.ShapeDtypeStruct((m, d), BF16),
                 jax.ShapeDtypeStruct((m, d), BF16), jax.ShapeDtypeStruct((m, d), BF16),
                 jax.ShapeDtypeStruct((m, d), F32), jax.ShapeDtypeStruct((m, d), F32)]
    return pl.pallas_call(
        _sb_in_kernel,
        out_shape=out_shape, grid=(m // tm,), in_specs=in_specs,
        out_specs=[pl.BlockSpec((tm, d), row)] * 5,
        compiler_params=_cparams(("arbitrary",)),
        name="sb_in",
    )(x2, mod_arr, mod_arr, g, w)


def _trunk(prompt, x, mods, past, wts):
    nb, rpb, d = x.shape
    m = nb * rpb
    a_w = 8 * DH
    x2 = x.reshape(m, d)
    (norm_g, wm, wf, bf, wp, ps, w_out_ab, w_in_sb, w_out_sb, w_up, w_down, final_g) = wts

    if prompt:
        hist, p0 = None, 0
    else:
        cfk, cfv, cflogf, spool, csk, csv = past
        p0 = cfk.shape[1]
        hist = jnp.pad(spool, ((0, 0), (HIST_ROWS - spool.shape[1], 0), (0, 0)))
    outs = _ab_in_call(prompt, x2, mods[0], norm_g[0, 0][None], wm, wf, bf, wp, ps, hist, rpb, p0)
    if prompt:
        q, kb, vb, k32, v32, logf, fcum, pooled, pstate = outs
        fc = fcum.reshape(nb, rpb, a_w // LANES, 2)
        a_out = _fox_call(q.reshape(nb, rpb, a_w), kb.reshape(nb, rpb, a_w),
                          vb.reshape(nb, rpb, a_w),
                          fc.transpose(0, 2, 1, 3), fc.transpose(0, 2, 3, 1),
                          fc[:, T_ATT - 1::T_ATT].transpose(0, 2, 1, 3))
    else:
        q, kb, vb, k32, v32, logf, pooled, pstate = outs
        lfn_t = logf.reshape(nb, rpb, 8).transpose(0, 2, 1)
        lfc_t = cflogf.transpose(0, 2, 1)
        a_out = _dec_call(True, q.reshape(nb, rpb, a_w), kb.reshape(nb, rpb, a_w),
                          vb.reshape(nb, rpb, a_w), cfk.reshape(nb, p0, a_w),
                          cfv.reshape(nb, p0, a_w), lfn_t, lfc_t)
    x2 = _out_res_call(prompt, [a_out.reshape(m, a_w), pooled], w_out_ab, x2, mods[0], 2, rpb)
    x2 = _mlp_call(prompt, x2, mods[0], norm_g[0, 1][None], w_up[0], w_down[0], None, rpb)

    q1, kb1, vb1, sk32, sv32 = _sb_in_call(prompt, x2, mods[1], norm_g[1, 0][None], w_in_sb, rpb)
    if prompt:
        s_out = _sb_call(q1.reshape(nb, rpb, d), kb1.reshape(nb, rpb, d), vb1.reshape(nb, rpb, d))
    else:
        s_out = _dec_call(False, q1.reshape(nb, rpb, d), kb1.reshape(nb, rpb, d),
                          vb1.reshape(nb, rpb, d), csk.reshape(nb, p0, d), csv.reshape(nb, p0, d))
    x2 = _out_res_call(prompt, [s_out.reshape(m, d)], w_out_sb, x2, mods[1], 2, rpb)
    y = _mlp_call(prompt, x2, mods[1], norm_g[1, 1][None], w_up[1], w_down[1], final_g[None], rpb)

    states = (k32.reshape(1, nb, rpb, 8, DH), v32.reshape(1, nb, rpb, 8, DH),
              logf.reshape(1, nb, rpb, 8), pstate[:, 1:, :][None],
              sk32.reshape(1, nb, rpb, d // DH, DH), sv32.reshape(1, nb, rpb, d // DH, DH))
    return y.reshape(nb, rpb, d), states


def kernel(x_prompt, x_sample, c_prompt, c_sample, cache_fox_k, cache_fox_v, cache_fox_logf, state_pool, cache_sb_k, cache_sb_v, w_ada, b_ada, norm_g, w_in_ab, b_forget, w_pool, pool_scale, w_out_ab, w_in_sb, w_out_sb, w_up, w_down, final_g):
    b, l, d = x_prompt.shape
    db, ds, _ = x_sample.shape
    a_w = 8 * DH

    c_all = jnp.concatenate([c_prompt, c_sample, jnp.zeros((ADA_ROWS - b - db, d), F32)], axis=0)
    mod = _ada_call(c_all, w_ada, b_ada)
    mods_p = [mod[i, :b][:, None, :] for i in range(mod.shape[0])]
    mods_s = [jnp.repeat(mod[i, b:b + db], ds, axis=0) for i in range(mod.shape[0])]

    w_in = w_in_ab[0]
    wm = jnp.concatenate([w_in[:, :3 * a_w], w_in[:, 3 * a_w + 8:]], axis=1).astype(BF16)
    wf = jnp.pad(w_in[:, 3 * a_w:3 * a_w + 8], ((0, 0), (0, LANES - 8))).astype(BF16)
    bf = jnp.pad(b_forget[0][None], ((0, 0), (0, LANES - 8)))
    wts = (norm_g, wm, wf, bf, w_pool[0].astype(BF16), pool_scale[0][None],
           w_out_ab[0].astype(BF16), w_in_sb[0].astype(BF16), w_out_sb[0].astype(BF16),
           w_up.astype(BF16), w_down.astype(BF16), final_g)

    y_p, st_p = _trunk(True, x_prompt, mods_p, None, wts)
    past = (cache_fox_k[0], cache_fox_v[0], cache_fox_logf[0], state_pool[0],
            cache_sb_k[0], cache_sb_v[0])
    y_s, st_s = _trunk(False, x_sample, mods_s, past, wts)
    return (y_p, y_s) + st_p + st_s
```

```python
import functools

import jax
import jax.numpy as jnp
from jax import lax
from jax.experimental import pallas as pl
from jax.experimental.pallas import tpu as pltpu

F32 = jnp.float32
BF16 = jnp.bfloat16

DH = 64
H_FOX = 8
EPS = 1e-6
POOL_WINDOWS = (2, 4, 8, 16)
POOL_GC = 128
HIST_ROWS = 16
NEG_BIG = -1e30
LOG2E = 1.4426950408889634
LANES = 128
GATE_ROWS = 16
VMEM_LIMIT = 56 * 1024 * 1024

UNDERFLOW_EXP = 104.0
NORM_MARGIN = 1.01

T_ATT = 256
NQ_ATT = 2
TK_DEC = 512
TM_ROWS = 512
ADA_ROWS = 32
ADA_TN = 1536


def _cparams(sem):
    return pltpu.CompilerParams(dimension_semantics=sem, vmem_limit_bytes=VMEM_LIMIT)


def _rms_mod(x, g, sc, sh):
    y = x * lax.rsqrt(jnp.mean(x * x, axis=-1, keepdims=True) + EPS)
    return (y * g) * (1.0 + sc) + sh


def _softplus(z):
    return jnp.maximum(z, 0.0) + jnp.log(1.0 + jnp.exp2(jnp.abs(z) * (-LOG2E)))


def _log_sigmoid(x):
    return jnp.minimum(x, 0.0) - jnp.log(1.0 + jnp.exp(-jnp.abs(x)))


def _split3(x):
    hi = x.astype(BF16)
    r = x - hi.astype(F32)
    mid = r.astype(BF16)
    lo = (r - mid.astype(F32)).astype(BF16)
    return hi, mid, lo


def _dot(a, b):
    return jnp.dot(a, b, preferred_element_type=F32)


def _dot_nt(a, b):
    return lax.dot_general(a, b, (((1,), (1,)), ((), ())), preferred_element_type=F32)


def _dot3(x, m):
    hi, mid, lo = _split3(x)
    return _dot(hi, m) + _dot(mid, m) + _dot(lo, m)


def _iota(shape, dim):
    return lax.broadcasted_iota(jnp.int32, shape, dim)


def _log2(n):
    assert n & (n - 1) == 0, n
    return n.bit_length() - 1


def _ada_kernel(c_ref, w_ref, b_ref, o_ref):
    c = c_ref[...]
    cond = (c / (1.0 + jnp.exp(-c))).astype(BF16)
    o_ref[...] = _dot(cond, w_ref[...].astype(BF16)) + b_ref[...]


def _ada_call(c_all, w_ada, b_ada):
    depth, d, n = w_ada.shape
    return pl.pallas_call(
        _ada_kernel,
        out_shape=jax.ShapeDtypeStruct((depth, ADA_ROWS, n), F32),
        grid=(depth, n // ADA_TN),
        in_specs=[
            pl.BlockSpec((ADA_ROWS, d), lambda i, j: (0, 0)),
            pl.BlockSpec((None, d, ADA_TN), lambda i, j: (i, 0, j)),
            pl.BlockSpec((None, 1, ADA_TN), lambda i, j: (i, 0, j)),
        ],
        out_specs=pl.BlockSpec((None, ADA_ROWS, ADA_TN), lambda i, j: (i, 0, j)),
        compiler_params=_cparams(("arbitrary", "arbitrary")),
        name="ada_mod",
    )(c_all, w_ada, b_ada.reshape(depth, 1, n))


def _mod_spec(prompt, tm, d, chunk, tiles_per_batch):
    if prompt:
        return pl.BlockSpec((None, 1, d), lambda i: (i // tiles_per_batch, 0, chunk))
    return pl.BlockSpec((tm, d), lambda i: (i, chunk))


def _pool_branch(u, ext_ref, pos, ng, rg, wp_ref, ps_ref, pooled_ref, pst_ref):
    tm, a_w = u.shape
    u3 = u.reshape(ng, rg, a_w)
    ext_ref[:, HIST_ROWS:HIST_ROWS + rg, :] = u3
    pst_ref[...] = u3[:, rg - HIST_ROWS:rg, :]
    for g, w in enumerate(POOL_WINDOWS):
        cs = slice(g * POOL_GC, (g + 1) * POOL_GC)
        acc = u3[:, :, cs]
        for kk in range(1, w):
            acc = acc + ext_ref[:, pl.ds(HIST_ROWS - kk, rg), cs]
        cnt = jnp.minimum(pos + 1, w).astype(F32)
        pooled = acc.reshape(tm, POOL_GC) / cnt - u[:, cs]
        pg = _dot(pooled.astype(BF16), wp_ref[g]) * ps_ref[:, cs]
        pooled_ref[:, cs] = pg.astype(pooled_ref.dtype)


def _ab_in_prompt_kernel(tm, tiles_per_batch, x_ref, sh_ref, sc_ref, g_ref, wq_ref, wkt_ref, wvt_ref,
                         wu_ref, wft_ref, bf_ref, wp_ref, ps_ref,
                         q_ref, ktb_ref, vtb_ref, kt_ref, vt_ref, lft_ref, fct_ref, pooled_ref, pst_ref,
                         ext_ref, carry_ref):
    h = _rms_mod(x_ref[...], g_ref[...], sc_ref[...], sh_ref[...]).astype(BF16)
    q_ref[...] = (_dot(h, wq_ref[...]) * (DH ** -0.5)).astype(q_ref.dtype)
    kt = _dot_nt(wkt_ref[...], h)
    kt_ref[...] = kt
    ktb_ref[...] = kt.astype(BF16)
    vt = _dot_nt(wvt_ref[...], h)
    vt_ref[...] = vt
    vtb_ref[...] = vt.astype(BF16)

    tile_in_batch = pl.program_id(0) % tiles_per_batch
    first = tile_in_batch == 0
    logf = _log_sigmoid(_dot_nt(wft_ref[...], h) + bf_ref[...])
    lft_ref[...] = logf[0:H_FOX, :]
    tri = (_iota((tm, tm), 0) <= _iota((tm, tm), 1)).astype(BF16)
    fcum = _dot3(logf, tri) + jnp.where(first, 0.0, carry_ref[...])
    carry_ref[...] = fcum[:, tm - 1:tm]
    fct_ref[...] = fcum[0:H_FOX, :]

    @pl.when(first)
    def _():
        ext_ref[:, 0:HIST_ROWS, :] = jnp.zeros((1, HIST_ROWS, ext_ref.shape[2]), F32)

    @pl.when(jnp.logical_not(first))
    def _():
        ext_ref[:, 0:HIST_ROWS, :] = ext_ref[:, tm:tm + HIST_ROWS, :]

    pos = tile_in_batch * tm + _iota((tm, 1), 0)
    _pool_branch(_dot(h, wu_ref[...]), ext_ref, pos, 1, tm, wp_ref, ps_ref, pooled_ref, pst_ref)


def _ab_in_sample_kernel(ng, rg, p0, x_ref, sh_ref, sc_ref, g_ref, wq_ref, wkt_ref, wvt_ref,
                         wu_ref, wft_ref, bf_ref, wp_ref, ps_ref, hist_ref,
                         q_ref, kb_ref, vb_ref, k32_ref, v32_ref, lft_ref, pooled_ref, pst_ref, ext_ref):
    tm = ng * rg
    h = _rms_mod(x_ref[...], g_ref[...], sc_ref[...], sh_ref[...]).astype(BF16)
    q_ref[...] = (_dot(h, wq_ref[...]) * (DH ** -0.5)).astype(q_ref.dtype)
    k = _dot_nt(h, wkt_ref[...])
    k32_ref[...] = k
    kb_ref[...] = k.astype(BF16)
    v = _dot_nt(h, wvt_ref[...])
    v32_ref[...] = v
    vb_ref[...] = v.astype(BF16)
    logf = _log_sigmoid(_dot_nt(wft_ref[...], h) + bf_ref[...])
    lft_ref[...] = logf[0:H_FOX, :]
    ext_ref[:, 0:HIST_ROWS, :] = hist_ref[...]
    pos = p0 + (_iota((tm, 1), 0) & (rg - 1))
    _pool_branch(_dot(h, wu_ref[...]), ext_ref, pos, ng, rg, wp_ref, ps_ref, pooled_ref, pst_ref)


def _ab_in_call(prompt, x2, mod_arr, g, wts, hist, rows_per_batch, p0):
    wq, wkt, wvt, wu, wft, bf, wp, ps = wts
    m, d = x2.shape
    a_w = H_FOX * DH
    nb = m // rows_per_batch
    tm = TM_ROWS if prompt else m
    tpb = rows_per_batch // tm if prompt else 1
    row = lambda i: (i, 0)
    const2 = lambda i: (0, 0)
    in_specs = [pl.BlockSpec((tm, d), row), _mod_spec(prompt, tm, d, 0, tpb), _mod_spec(prompt, tm, d, 1, tpb),
                pl.BlockSpec((1, d), const2)]
    in_specs += [pl.BlockSpec(w.shape, const2) for w in (wq, wkt, wvt, wu, wft, bf)]
    in_specs += [pl.BlockSpec(wp.shape, lambda i: (0, 0, 0)), pl.BlockSpec(ps.shape, const2)]
    args = [x2, mod_arr, mod_arr, g, wq, wkt, wvt, wu, wft, bf, wp, ps]
    if prompt:
        tcol = lambda i: (i // tpb, 0, i % tpb)
        out_shape = [jax.ShapeDtypeStruct((m, a_w), BF16),
                     jax.ShapeDtypeStruct((nb, a_w, rows_per_batch), BF16),
                     jax.ShapeDtypeStruct((nb, a_w, rows_per_batch), BF16),
                     jax.ShapeDtypeStruct((nb, a_w, rows_per_batch), F32),
                     jax.ShapeDtypeStruct((nb, a_w, rows_per_batch), F32),
                     jax.ShapeDtypeStruct((nb, H_FOX, rows_per_batch), F32),
                     jax.ShapeDtypeStruct((nb, H_FOX, rows_per_batch), F32),
                     jax.ShapeDtypeStruct((m, a_w), BF16),
                     jax.ShapeDtypeStruct((nb, HIST_ROWS, a_w), F32)]
        out_specs = ([pl.BlockSpec((tm, a_w), row)] + [pl.BlockSpec((None, a_w, tm), tcol)] * 4
                     + [pl.BlockSpec((None, H_FOX, tm), tcol)] * 2
                     + [pl.BlockSpec((tm, a_w), row),
                        pl.BlockSpec((1, HIST_ROWS, a_w), lambda i: (i // tpb, 0, 0))])
        scratch = [pltpu.VMEM((1, HIST_ROWS + tm, a_w), F32), pltpu.VMEM((GATE_ROWS, 1), F32)]
        body = functools.partial(_ab_in_prompt_kernel, tm, tpb)
    else:
        in_specs.append(pl.BlockSpec(hist.shape, lambda i: (0, 0, 0)))
        args.append(hist)
        out_shape = [jax.ShapeDtypeStruct((m, a_w), BF16), jax.ShapeDtypeStruct((m, a_w), BF16),
                     jax.ShapeDtypeStruct((m, a_w), BF16), jax.ShapeDtypeStruct((m, a_w), F32),
                     jax.ShapeDtypeStruct((m, a_w), F32), jax.ShapeDtypeStruct((H_FOX, m), F32),
                     jax.ShapeDtypeStruct((m, a_w), BF16), jax.ShapeDtypeStruct((nb, HIST_ROWS, a_w), F32)]
        out_specs = ([pl.BlockSpec((tm, a_w), row)] * 5 + [pl.BlockSpec((H_FOX, tm), const2)]
                     + [pl.BlockSpec((tm, a_w), row), pl.BlockSpec((nb, HIST_ROWS, a_w), lambda i: (0, 0, 0))])
        scratch = [pltpu.VMEM((nb, HIST_ROWS + rows_per_batch, a_w), F32)]
        body = functools.partial(_ab_in_sample_kernel, nb, rows_per_batch, p0)
    return pl.pallas_call(
        body, out_shape=out_shape, grid=(m // tm,), in_specs=in_specs, out_specs=out_specs,
        scratch_shapes=scratch,
        compiler_params=_cparams(("arbitrary",)),
        name="ab_in_prompt" if prompt else "ab_in_sample",
    )(*args)


def _head_split(q2):
    lane = _iota((1, LANES), 1)
    zero = jnp.zeros_like(q2)
    return jnp.where(lane < DH, q2, zero), jnp.where(lane >= DH, q2, zero)


def _pair_sel():
    return (_iota((8, LANES), 0) == (_iota((8, LANES), 1) >> _log2(DH))).astype(BF16)


def _fox_kernel(q_ref, kt_ref, vt_ref, fq_ref, fk_ref, ft_ref, o_ref,
                m_ref, l_ref, acc_ref, kn_ref):
    t, nq = T_ATT, NQ_ATT
    qg = pl.program_id(2)
    lane = _iota((1, LANES), 1)
    lane2 = _iota((1, 2), 1)

    @pl.when(qg == 0)
    def _():
        chunk = 4 * t

        def body(i, mx):
            ks = pl.multiple_of(i * chunk, chunk)
            kf = kt_ref[:, pl.ds(ks, chunk)].astype(F32)
            n2 = _dot(_pair_sel(), (kf * kf).astype(BF16))
            return jnp.maximum(mx, jnp.max(n2, axis=1, keepdims=True))

        mx = lax.fori_loop(0, kt_ref.shape[1] // chunk, body, jnp.zeros((8, 1), F32))
        kn_ref[...] = jnp.sqrt(mx)

    qh, fq, bound0 = [], [], []
    for s in range(nq):
        q2 = q_ref[s * t:(s + 1) * t, :]
        qh.append(_head_split(q2))
        fq.append(fq_ref[s * t:(s + 1) * t, :])
        qf = q2.astype(F32)
        qn2 = _dot_nt((qf * qf).astype(BF16), _pair_sel())
        qn = jnp.sqrt(jnp.max(qn2, axis=0, keepdims=True))
        qk = jnp.where(lane2 == 0, qn[:, 0:1] * kn_ref[0:1, :], qn[:, 1:2] * kn_ref[1:2, :])
        bound0.append(qk * NORM_MARGIN + fq[s][0:1, :])
    m_ref[...] = jnp.full(m_ref.shape, NEG_BIG, F32)
    l_ref[...] = jnp.zeros(l_ref.shape, F32)
    acc_ref[...] = jnp.zeros(acc_ref.shape, F32)

    def step(n, diag):
        chains = [(s, j) for s in range(nq) for j in range(2)]
        kbs = [qg * nq + s - n for s in range(nq)]
        kss = [pl.multiple_of((kb if diag else jnp.maximum(kb, 0)) * t, t) for kb in kbs]
        causal = _iota((t, t), 1) <= _iota((t, t), 0)
        fks = [fk_ref[:, pl.ds(ks, t)] for ks in kss]
        sc = []
        for s, j in chains:
            col = fq[s][:, j:j + 1]
            if not diag:
                col = col + jnp.where(kbs[s] >= 0, 0.0, NEG_BIG)
            x = _dot(qh[s][j], kt_ref[:, pl.ds(kss[s], t)]) + (col - fks[s][j:j + 1, :])
            sc.append(jnp.where(causal, x, NEG_BIG) if diag else x)
        m_prev = [m_ref[2 * s + j] for s, j in chains]
        m_new = [jnp.maximum(mp, jnp.max(x, axis=1, keepdims=True)) for mp, x in zip(m_prev, sc)]
        p = [jnp.exp(x - mn) for x, mn in zip(sc, m_new)]
        pv = [_dot_nt(pp.astype(BF16), vt_ref[:, pl.ds(kss[s], t)]) for pp, (s, j) in zip(p, chains)]
        for i, (s, j) in enumerate(chains):
            idx = 2 * s + j
            alpha = jnp.exp(m_prev[i] - m_new[i])
            l_ref[idx] = alpha * l_ref[idx] + jnp.sum(p[i], axis=1, keepdims=True)
            acc_ref[idx] = alpha * acc_ref[idx] + pv[i]
            m_ref[idx] = m_new[i]

    def more(n):
        go = False
        for s in range(nq):
            kb = qg * nq + s - n
            ft = ft_ref[pl.ds(jnp.maximum(kb, 0), 1), :]
            m_min = jnp.where(lane2 == 0, jnp.min(m_ref[2 * s], axis=0, keepdims=True),
                              jnp.min(m_ref[2 * s + 1], axis=0, keepdims=True))
            top = jnp.max(bound0[s] - ft - m_min)
            go = jnp.logical_or(go, jnp.logical_and(kb >= 0, top > -UNDERFLOW_EXP))
        return go

    step(0, True)

    def body(c):
        n, _ = c
        step(n, False)
        return n + 1, more(n + 1)

    lax.while_loop(lambda c: c[1], body, (jnp.int32(1), more(1)))
    for s in range(nq):
        o = jnp.where(lane < DH, acc_ref[2 * s] / l_ref[2 * s], acc_ref[2 * s + 1] / l_ref[2 * s + 1])
        o_ref[s * t:(s + 1) * t, :] = o.astype(o_ref.dtype)


def _fox_call(q, ktb, vtb, fq_arr, fk_arr, ft_arr):
    b, l, a_w = q.shape
    hp = a_w // LANES
    t, nq = T_ATT, NQ_ATT
    tq = t * nq
    return pl.pallas_call(
        _fox_kernel,
        out_shape=jax.ShapeDtypeStruct((b, l, a_w), BF16),
        grid=(b, hp, l // tq),
        in_specs=[
            pl.BlockSpec((None, tq, LANES), lambda bi, h, qi: (bi, qi, h)),
            pl.BlockSpec((None, LANES, l), lambda bi, h, qi: (bi, h, 0)),
            pl.BlockSpec((None, LANES, l), lambda bi, h, qi: (bi, h, 0)),
            pl.BlockSpec((None, None, tq, 2), lambda bi, h, qi: (bi, h, qi, 0)),
            pl.BlockSpec((None, None, 2, l), lambda bi, h, qi: (bi, h, 0, 0)),
            pl.BlockSpec((None, None, l // t, 2), lambda bi, h, qi: (bi, h, 0, 0)),
        ],
        out_specs=pl.BlockSpec((None, tq, LANES), lambda bi, h, qi: (bi, qi, h)),
        scratch_shapes=[pltpu.VMEM((2 * nq, t, 1), F32), pltpu.VMEM((2 * nq, t, 1), F32),
                        pltpu.VMEM((2 * nq, t, LANES), F32), pltpu.VMEM((8, 1), F32)],
        compiler_params=_cparams(("arbitrary", "arbitrary", "arbitrary")),
        name="fox_prompt",
    )(q, ktb, vtb, fq_arr, fk_arr, ft_arr)


def _sb_kernel(q_ref, kt_ref, vt_ref, o_ref, carry_ref, acc_ref):
    t, nq = T_ATT, NQ_ATT
    qg = pl.program_id(2)
    lane = _iota((1, LANES), 1)
    qh = [_head_split(q_ref[s * t:(s + 1) * t, :]) for s in range(nq)]
    carry_ref[...] = jnp.zeros(carry_ref.shape, F32)
    acc_ref[...] = jnp.zeros(acc_ref.shape, F32)

    def step(n, diag):
        tri = (_iota((t, t), 0) > _iota((t, t), 1)).astype(BF16)
        valid = _iota((t, t), 1) < _iota((t, t), 0)
        chains = [(s, j) for s in range(nq) for j in range(2)]
        kbs = [qg * nq + s - n for s in range(nq)]
        kss = [pl.multiple_of((kb if diag else jnp.maximum(kb, 0)) * t, t) for kb in kbs]
        z = [_dot(qh[s][j], kt_ref[:, pl.ds(kss[s], t)]) for s, j in chains]
        sp = [_softplus(zz) for zz in z]
        spm = [jnp.where(valid, x, 0.0) for x in sp] if diag else sp
        later = [_dot(x.astype(BF16), tri) for x in spm]
        vs = []
        for s in range(nq):
            v2 = vt_ref[:, pl.ds(kss[s], t)]
            if not diag:
                v2 = jnp.where(kbs[s] >= 0, v2, jnp.zeros_like(v2))
            vs.append(v2)
        for i, (s, j) in enumerate(chains):
            idx = 2 * s + j
            c = carry_ref[idx]
            a = jnp.exp((z[i] - sp[i]) - later[i] - c)
            if diag:
                a = jnp.where(valid, a, 0.0)
            acc_ref[idx] = acc_ref[idx] + _dot_nt(a.astype(BF16), vs[s])
            carry_ref[idx] = c + jnp.sum(spm[i], axis=1, keepdims=True)

    def more(n):
        in_range = n <= qg * nq + nq - 1
        return jnp.logical_and(in_range, jnp.min(carry_ref[...]) < UNDERFLOW_EXP)

    step(0, True)

    def body(c):
        n, _ = c
        step(n, False)
        return n + 1, more(n + 1)

    lax.while_loop(lambda c: c[1], body, (jnp.int32(1), more(1)))
    for s in range(nq):
        o = jnp.where(lane < DH, acc_ref[2 * s], acc_ref[2 * s + 1])
        o_ref[s * t:(s + 1) * t, :] = o.astype(o_ref.dtype)


def _sb_call(q, ktb, vtb):
    b, l, d = q.shape
    hp = d // LANES
    t, nq = T_ATT, NQ_ATT
    tq = t * nq
    return pl.pallas_call(
        _sb_kernel,
        out_shape=jax.ShapeDtypeStruct((b, l, d), BF16),
        grid=(b, hp, l // tq),
        in_specs=[
            pl.BlockSpec((None, tq, LANES), lambda bi, h, qi: (bi, qi, h)),
            pl.BlockSpec((None, LANES, l), lambda bi, h, qi: (bi, h, 0)),
            pl.BlockSpec((None, LANES, l), lambda bi, h, qi: (bi, h, 0)),
        ],
        out_specs=pl.BlockSpec((None, tq, LANES), lambda bi, h, qi: (bi, qi, h)),
        scratch_shapes=[pltpu.VMEM((2 * nq, t, 1), F32), pltpu.VMEM((2 * nq, t, LANES), F32)],
        compiler_params=_cparams(("arbitrary", "arbitrary", "arbitrary")),
        name="sb_prompt",
    )(q, ktb, vtb)


def _rep_matrix(rows, n_heads, ds):
    return ((_iota((rows, ds), 0) >> _log2(n_heads)) == _iota((rows, ds), 1)).astype(BF16)


def _head_mask(rows, n_heads, width):
    return (_iota((rows, width), 0) & (n_heads - 1)) == (_iota((rows, width), 1) >> _log2(DH))


def _block_diag_q(q, n_heads):
    ds, width = q.shape
    rows = ds * n_heads
    qrep = _dot(_rep_matrix(rows, n_heads, ds), q)
    return jnp.where(_head_mask(rows, n_heads, width), qrep, 0.0).astype(BF16)


def _collect_heads(acc, n_heads, ds):
    rows, width = acc.shape
    om = jnp.where(_head_mask(rows, n_heads, width), acc, 0.0).astype(BF16)
    rep_t = ((_iota((ds, rows), 1) >> _log2(n_heads)) == _iota((ds, rows), 0)).astype(BF16)
    return _dot(rep_t, om)


def _fox_dec_kernel(n_heads, ds, q_ref, kn_ref, vn_ref, lfn_ref, kc_ref, vc_ref, lfc_ref,
                    o_ref, qbd_ref, m_ref, l_ref, acc_ref, cf_ref):
    rows = ds * n_heads
    step_i = pl.program_id(1)
    n_steps = pl.num_programs(1)

    def update(s, pv_fn):
        m_prev = m_ref[...]
        m_new = jnp.maximum(m_prev, jnp.max(s, axis=1, keepdims=True))
        alpha = jnp.exp(m_prev - m_new)
        p = jnp.exp(s - m_new)
        l_ref[...] = alpha * l_ref[...] + jnp.sum(p, axis=1, keepdims=True)
        acc_ref[...] = alpha * acc_ref[...] + pv_fn(p.astype(BF16))
        m_ref[...] = m_new

    @pl.when(step_i == 0)
    def _():
        qbd_ref[...] = _block_diag_q(q_ref[...], n_heads)
        m_ref[...] = jnp.full(m_ref.shape, NEG_BIG, F32)
        l_ref[...] = jnp.zeros(l_ref.shape, F32)
        acc_ref[...] = jnp.zeros(acc_ref.shape, F32)
        a = jnp.concatenate([lfn_ref[...]] * ds, axis=0)
        ri = _iota((rows, ds), 0) >> _log2(n_heads)
        cj = _iota((rows, ds), 1)
        am = jnp.where(cj <= ri, a, 0.0)
        tri = (_iota((ds, ds), 0) > _iota((ds, ds), 1)).astype(BF16)
        s = _dot_nt(qbd_ref[...], kn_ref[...]) + _dot3(am, tri)
        s = jnp.where(cj <= ri, s, NEG_BIG)
        update(s, lambda p: _dot(p, vn_ref[...]))
        cf_ref[...] = jnp.sum(am, axis=1, keepdims=True)

    @pl.when(step_i > 0)
    def _():
        tk = kc_ref.shape[1]
        a = jnp.concatenate([lfc_ref[...]] * ds, axis=0)
        tri = (_iota((tk, tk), 0) > _iota((tk, tk), 1)).astype(BF16)
        cf = cf_ref[...]
        s = _dot(qbd_ref[...], kc_ref[...].astype(BF16)) + (_dot3(a, tri) + cf)
        update(s, lambda p: _dot_nt(p, vc_ref[...].astype(BF16)))
        cf_ref[...] = cf + jnp.sum(a, axis=1, keepdims=True)

    @pl.when(step_i == n_steps - 1)
    def _():
        o_ref[...] = _collect_heads(acc_ref[...] / l_ref[...], n_heads, ds).astype(o_ref.dtype)


def _fox_dec_call(q, kn, vn, kct, vct, lfn_t, lfc_t):
    db, ds, width = q.shape
    n_heads = width // DH
    rows = ds * n_heads
    nkb = kct.shape[2] // TK_DEC
    cache_blk = lambda s: jnp.minimum(nkb - s, nkb - 1)
    new_idx = lambda b, s: (b, 0, 0)
    cache_idx = lambda b, s: (b, 0, cache_blk(s))
    in_specs = ([pl.BlockSpec((None, ds, width), new_idx)] * 3
                + [pl.BlockSpec((None, n_heads, ds), new_idx)]
                + [pl.BlockSpec((None, width, TK_DEC), cache_idx)] * 2
                + [pl.BlockSpec((None, n_heads, TK_DEC), cache_idx)])
    scratch = [pltpu.VMEM((rows, width), BF16), pltpu.VMEM((rows, 1), F32), pltpu.VMEM((rows, 1), F32),
               pltpu.VMEM((rows, width), F32), pltpu.VMEM((rows, 1), F32)]
    return pl.pallas_call(
        functools.partial(_fox_dec_kernel, n_heads, ds),
        out_shape=jax.ShapeDtypeStruct((db, ds, width), BF16),
        grid=(db, nkb + 1),
        in_specs=in_specs,
        out_specs=pl.BlockSpec((None, ds, width), new_idx),
        scratch_shapes=scratch,
        compiler_params=_cparams(("arbitrary", "arbitrary")),
        name="fox_decode",
    )(q, kn, vn, lfn_t, kct, vct, lfc_t)


def _sb_dec_tile(qbd, kc_ref, vc_ref, carry):
    tk = kc_ref.shape[1]
    z = _dot(qbd, kc_ref[...].astype(BF16))
    sp = _softplus(z)
    tri = (_iota((tk, tk), 0) > _iota((tk, tk), 1)).astype(BF16)
    later = _dot(sp.astype(BF16), tri)
    a = jnp.exp((z - sp) - later - carry)
    return _dot_nt(a.astype(BF16), vc_ref[...].astype(BF16)), carry + jnp.sum(sp, axis=1, keepdims=True)


def _sb_dec_first_kernel(n_heads, ds, q_ref, kn_ref, vn_ref, kc_ref, vc_ref,
                         o_ref, carry_out_ref, cmin_ref):
    rows = ds * n_heads
    qbd = _block_diag_q(q_ref[...], n_heads)
    ri = _iota((rows, ds), 0) >> _log2(n_heads)
    cj = _iota((rows, ds), 1)
    valid = cj < ri
    z = _dot_nt(qbd, kn_ref[...])
    sp = _softplus(z)
    spm = jnp.where(valid, sp, 0.0)
    tri = (_iota((ds, ds), 0) > _iota((ds, ds), 1)).astype(BF16)
    later = _dot(spm.astype(BF16), tri)
    a = jnp.where(valid, jnp.exp((z - sp) - later), 0.0)
    acc = _dot(a.astype(BF16), vn_ref[...])
    carry = jnp.sum(spm, axis=1, keepdims=True)
    part, carry = _sb_dec_tile(qbd, kc_ref, vc_ref, carry)
    o_ref[...] = _collect_heads(acc + part, n_heads, ds).astype(o_ref.dtype)
    carry_out_ref[...] = carry
    cmin_ref[...] = jnp.broadcast_to(jnp.min(carry, axis=0, keepdims=True), cmin_ref.shape)


def _sb_dec_rest_kernel(n_heads, ds, need_ref, q_ref, carry_in_ref, o1_ref, kc_ref, vc_ref,
                        o_ref, qbd_ref, acc_ref, carry_ref):
    step_i = pl.program_id(1)
    last = pl.num_programs(1) - 1
    needed = need_ref[pl.program_id(0)] != 0

    @pl.when(needed)
    def _():
        @pl.when(step_i == 0)
        def _():
            qbd_ref[...] = _block_diag_q(q_ref[...], n_heads)
            acc_ref[...] = jnp.zeros(acc_ref.shape, F32)
            carry_ref[...] = carry_in_ref[...]

        @pl.when(jnp.min(carry_ref[...]) < UNDERFLOW_EXP)
        def _():
            part, carry = _sb_dec_tile(qbd_ref[...], kc_ref, vc_ref, carry_ref[...])
            acc_ref[...] = acc_ref[...] + part
            carry_ref[...] = carry

        @pl.when(step_i == last)
        def _():
            o = o1_ref[...].astype(F32) + _collect_heads(acc_ref[...], n_heads, ds)
            o_ref[...] = o.astype(o_ref.dtype)

    @pl.when(jnp.logical_and(jnp.logical_not(needed), step_i == last))
    def _():
        o_ref[...] = o1_ref[...]


def _sb_dec_call(q, kn, vn, kct, vct):
    db, ds, width = q.shape
    n_heads = width // DH
    rows = ds * n_heads
    nkb = kct.shape[2] // TK_DEC
    bidx = lambda b: (b, 0, 0)
    tile = (None, width, TK_DEC)
    o1, carry, cmin = pl.pallas_call(
        functools.partial(_sb_dec_first_kernel, n_heads, ds),
        out_shape=[jax.ShapeDtypeStruct((db, ds, width), BF16),
                   jax.ShapeDtypeStruct((db, rows, 1), F32),
                   jax.ShapeDtypeStruct((db, 1, LANES), F32)],
        grid=(db,),
        in_specs=[pl.BlockSpec((None, ds, width), bidx)] * 3
        + [pl.BlockSpec(tile, lambda b: (b, 0, nkb - 1))] * 2,
        out_specs=[pl.BlockSpec((None, ds, width), bidx), pl.BlockSpec((None, rows, 1), bidx),
                   pl.BlockSpec((None, 1, LANES), bidx)],
        compiler_params=_cparams(("arbitrary",)),
        name="sb_decode_first",
    )(q, kn, vn, kct, vct)
    if nkb == 1:
        return o1
    need = (cmin[:, 0, 0] < UNDERFLOW_EXP).astype(jnp.int32)
    bidx2 = lambda b, s, need_ref: (b, 0, 0)

    def cache_idx(b, s, need_ref):
        on = need_ref[b] != 0
        return (jnp.where(on, b, 0), 0, jnp.where(on, nkb - 2 - s, 0))

    grid_spec = pltpu.PrefetchScalarGridSpec(
        num_scalar_prefetch=1,
        grid=(db, nkb - 1),
        in_specs=[pl.BlockSpec((None, ds, width), bidx2), pl.BlockSpec((None, rows, 1), bidx2),
                  pl.BlockSpec((None, ds, width), bidx2),
                  pl.BlockSpec(tile, cache_idx), pl.BlockSpec(tile, cache_idx)],
        out_specs=pl.BlockSpec((None, ds, width), bidx2),
        scratch_shapes=[pltpu.VMEM((rows, width), BF16), pltpu.VMEM((rows, width), F32),
                        pltpu.VMEM((rows, 1), F32)],
    )
    return pl.pallas_call(
        functools.partial(_sb_dec_rest_kernel, n_heads, ds),
        out_shape=jax.ShapeDtypeStruct((db, ds, width), BF16),
        grid_spec=grid_spec,
        compiler_params=_cparams(("arbitrary", "arbitrary")),
        name="sb_decode_rest",
    )(need, q, carry, o1, kct, vct)


def _out_res_kernel(n_act, *refs):
    acts = refs[:n_act]
    w_ref, x_ref, gate_ref, o_ref = refs[n_act:]
    y = None
    off = 0
    for a_ref in acts:
        kdim = a_ref.shape[1]
        part = _dot(a_ref[...], w_ref[off:off + kdim, :])
        y = part if y is None else y + part
        off += kdim
    o_ref[...] = x_ref[...] + gate_ref[...] * y


def _out_res_call(prompt, acts, w, x2, mod_arr, gate_chunk, rows_per_batch):
    m, d = x2.shape
    tm = TM_ROWS if prompt else m
    tpb = rows_per_batch // tm if prompt else 1
    row = lambda i: (i, 0)
    in_specs = [pl.BlockSpec((tm, a.shape[1]), row) for a in acts]
    in_specs += [pl.BlockSpec(w.shape, lambda i: (0, 0)), pl.BlockSpec((tm, d), row),
                 _mod_spec(prompt, tm, d, gate_chunk, tpb)]
    return pl.pallas_call(
        functools.partial(_out_res_kernel, len(acts)),
        out_shape=jax.ShapeDtypeStruct((m, d), F32),
        grid=(m // tm,), in_specs=in_specs, out_specs=pl.BlockSpec((tm, d), row),
        compiler_params=_cparams(("arbitrary",)),
        name="out_res",
    )(*acts, w, x2, mod_arr)


def _mlp_kernel(final, f_chunk, *refs):
    if final:
        x_ref, sh_ref, sc_ref, gate_ref, g_ref, wu_ref, wd_ref, fg_ref, o_ref = refs
    else:
        x_ref, sh_ref, sc_ref, gate_ref, g_ref, wu_ref, wd_ref, o_ref = refs
    x = x_ref[...]
    h = _rms_mod(x, g_ref[...], sc_ref[...], sh_ref[...]).astype(BF16)
    d_ff = wu_ref.shape[1]
    acc = None
    for c in range(d_ff // f_chunk):
        cs = slice(c * f_chunk, (c + 1) * f_chunk)
        a = jnp.maximum(_dot(h, wu_ref[:, cs]), 0.0)
        part = _dot((a * a).astype(BF16), wd_ref[cs, :])
        acc = part if acc is None else acc + part
    xo = x + gate_ref[...] * acc
    if final:
        y = xo * lax.rsqrt(jnp.mean(xo * xo, axis=-1, keepdims=True) + EPS)
        xo = y * fg_ref[...]
    o_ref[...] = xo


def _mlp_call(prompt, x2, mod_arr, g, wu, wd, final_g, rows_per_batch):
    m, d = x2.shape
    tm = TM_ROWS if prompt else m
    tpb = rows_per_batch // tm if prompt else 1
    row = lambda i: (i, 0)
    const2 = lambda i: (0, 0)
    final = final_g is not None
    in_specs = [pl.BlockSpec((tm, d), row),
                _mod_spec(prompt, tm, d, 3, tpb), _mod_spec(prompt, tm, d, 4, tpb),
                _mod_spec(prompt, tm, d, 5, tpb),
                pl.BlockSpec((1, d), const2),
                pl.BlockSpec(wu.shape, const2, pipeline_mode=pl.Buffered(1)),
                pl.BlockSpec(wd.shape, const2, pipeline_mode=pl.Buffered(1))]
    args = [x2, mod_arr, mod_arr, mod_arr, g, wu, wd]
    if final:
        in_specs.append(pl.BlockSpec((1, d), const2))
        args.append(final_g)
    return pl.pallas_call(
        functools.partial(_mlp_kernel, final, 1024),
        out_shape=jax.ShapeDtypeStruct((m, d), F32),
        grid=(m // tm,), in_specs=in_specs, out_specs=pl.BlockSpec((tm, d), row),
        compiler_params=_cparams(("arbitrary",)),
        name="mlp",
    )(*args)


def _sb_in_prompt_kernel(x_ref, sh_ref, sc_ref, g_ref, wq_ref, wkt_ref, wvt_ref,
                         q_ref, ktb_ref, vtb_ref, kt_ref, vt_ref):
    h = _rms_mod(x_ref[...], g_ref[...], sc_ref[...], sh_ref[...]).astype(BF16)
    q_ref[...] = (_dot(h, wq_ref[...]) * (DH ** -0.5)).astype(q_ref.dtype)
    kt = _dot_nt(wkt_ref[...], h)
    kt_ref[...] = kt
    ktb_ref[...] = kt.astype(BF16)
    vt = _dot_nt(wvt_ref[...], h)
    vt_ref[...] = vt
    vtb_ref[...] = vt.astype(BF16)


def _sb_in_sample_kernel(x_ref, sh_ref, sc_ref, g_ref, wq_ref, wkt_ref, wvt_ref,
                         q_ref, kb_ref, vb_ref, k32_ref, v32_ref):
    h = _rms_mod(x_ref[...], g_ref[...], sc_ref[...], sh_ref[...]).astype(BF16)
    q_ref[...] = (_dot(h, wq_ref[...]) * (DH ** -0.5)).astype(q_ref.dtype)
    k = _dot_nt(h, wkt_ref[...])
    k32_ref[...] = k
    kb_ref[...] = k.astype(BF16)
    v = _dot_nt(h, wvt_ref[...])
    v32_ref[...] = v
    vb_ref[...] = v.astype(BF16)


def _sb_in_call(prompt, x2, mod_arr, g, wts, rows_per_batch):
    wq, wkt, wvt = wts
    m, d = x2.shape
    nb = m // rows_per_batch
    tm = TM_ROWS if prompt else m
    tpb = rows_per_batch // tm if prompt else 1
    row = lambda i: (i, 0)
    const2 = lambda i: (0, 0)
    in_specs = [pl.BlockSpec((tm, d), row),
                _mod_spec(prompt, tm, d, 0, tpb), _mod_spec(prompt, tm, d, 1, tpb),
                pl.BlockSpec((1, d), const2)] + [pl.BlockSpec(w.shape, const2) for w in wts]
    if prompt:
        tcol = lambda i: (i // tpb, 0, i % tpb)
        out_shape = ([jax.ShapeDtypeStruct((m, d), BF16)]
                     + [jax.ShapeDtypeStruct((nb, d, rows_per_batch), BF16)] * 2
                     + [jax.ShapeDtypeStruct((nb, d, rows_per_batch), F32)] * 2)
        out_specs = [pl.BlockSpec((tm, d), row)] + [pl.BlockSpec((None, d, tm), tcol)] * 4
        body = _sb_in_prompt_kernel
    else:
        out_shape = [jax.ShapeDtypeStruct((m, d), BF16)] * 3 + [jax.ShapeDtypeStruct((m, d), F32)] * 2
        out_specs = [pl.BlockSpec((tm, d), row)] * 5
        body = _sb_in_sample_kernel
    return pl.pallas_call(
        body, out_shape=out_shape, grid=(m // tm,), in_specs=in_specs, out_specs=out_specs,
        compiler_params=_cparams(("arbitrary",)),
        name="sb_in_prompt" if prompt else "sb_in_sample",
    )(x2, mod_arr, mod_arr, g, wq, wkt, wvt)


def _to_state(xt, n_heads):
    nb, _, l = xt.shape
    return xt.reshape(nb, n_heads, DH, l).transpose(0, 3, 1, 2)[None]


def _from_cache(c):
    nb, p, n_heads, dh = c.shape
    return c.transpose(0, 2, 3, 1).reshape(nb, n_heads * dh, p)


def _trunk(prompt, x, mods, past, wts):
    nb, rpb, d = x.shape
    m = nb * rpb
    a_w = H_FOX * DH
    x2 = x.reshape(m, d)
    norm_g, ab_wts, w_out_ab, sb_wts, w_out_sb, w_up, w_down, final_g = wts

    if prompt:
        outs = _ab_in_call(True, x2, mods[0], norm_g[0, 0][None], ab_wts, None, rpb, 0)
        q, ktb, vtb, kt32, vt32, lft, fct, pooled, pstate = outs
        fk_arr = fct.reshape(nb, a_w // LANES, 2, rpb)
        fq_arr = fk_arr.transpose(0, 1, 3, 2)
        a_out = _fox_call(q.reshape(nb, rpb, a_w), ktb, vtb, fq_arr, fk_arr, fq_arr[:, :, T_ATT - 1::T_ATT, :])
        st_fox = (_to_state(kt32, H_FOX), _to_state(vt32, H_FOX), lft.transpose(0, 2, 1)[None])
    else:
        cfk, cfv, cflogf, spool, csk, csv = past
        p0 = cfk.shape[1]
        hist = jnp.pad(spool, ((0, 0), (HIST_ROWS - spool.shape[1], 0), (0, 0)))
        outs = _ab_in_call(False, x2, mods[0], norm_g[0, 0][None], ab_wts, hist, rpb, p0)
        q, kb, vb, k32, v32, lft, pooled, pstate = outs
        lfn_t = lft.reshape(H_FOX, nb, rpb).transpose(1, 0, 2)
        a_out = _fox_dec_call(q.reshape(nb, rpb, a_w), kb.reshape(nb, rpb, a_w), vb.reshape(nb, rpb, a_w),
                              _from_cache(cfk), _from_cache(cfv), lfn_t, cflogf.transpose(0, 2, 1))
        st_fox = (k32.reshape(1, nb, rpb, H_FOX, DH), v32.reshape(1, nb, rpb, H_FOX, DH),
                  lfn_t.transpose(0, 2, 1)[None])
    x2 = _out_res_call(prompt, [a_out.reshape(m, a_w), pooled], w_out_ab, x2, mods[0], 2, rpb)
    x2 = _mlp_call(prompt, x2, mods[0], norm_g[0, 1][None], w_up[0], w_down[0], None, rpb)

    n_heads = d // DH
    outs = _sb_in_call(prompt, x2, mods[1], norm_g[1, 0][None], sb_wts, rpb)
    if prompt:
        q1, ktb1, vtb1, skt32, svt32 = outs
        s_out = _sb_call(q1.reshape(nb, rpb, d), ktb1, vtb1)
        st_sb = (_to_state(skt32, n_heads), _to_state(svt32, n_heads))
    else:
        q1, kb1, vb1, sk32, sv32 = outs
        s_out = _sb_dec_call(q1.reshape(nb, rpb, d), kb1.reshape(nb, rpb, d), vb1.reshape(nb, rpb, d),
                             _from_cache(csk), _from_cache(csv))
        st_sb = (sk32.reshape(1, nb, rpb, n_heads, DH), sv32.reshape(1, nb, rpb, n_heads, DH))
    x2 = _out_res_call(prompt, [s_out.reshape(m, d)], w_out_sb, x2, mods[1], 2, rpb)
    y = _mlp_call(prompt, x2, mods[1], norm_g[1, 1][None], w_up[1], w_down[1], final_g[None], rpb)

    states = st_fox + (pstate[:, 1:, :][None],) + st_sb
    return y.reshape(nb, rpb, d), states


def kernel(x_prompt, x_sample, c_prompt, c_sample, cache_fox_k, cache_fox_v, cache_fox_logf, state_pool, cache_sb_k, cache_sb_v, w_ada, b_ada, norm_g, w_in_ab, b_forget, w_pool, pool_scale, w_out_ab, w_in_sb, w_out_sb, w_up, w_down, final_g):
    b, l, d = x_prompt.shape
    db, ds, _ = x_sample.shape
    a_w = H_FOX * DH

    c_all = jnp.concatenate([c_prompt, c_sample, jnp.zeros((ADA_ROWS - b - db, d), F32)], axis=0)
    mod = _ada_call(c_all, w_ada, b_ada)
    mods_p = [mod[i, :b][:, None, :] for i in range(mod.shape[0])]
    mods_s = [jnp.repeat(mod[i, b:b + db], ds, axis=0) for i in range(mod.shape[0])]

    w_in = w_in_ab[0]
    gate_pad = ((0, GATE_ROWS - H_FOX), (0, 0))
    ab_wts = (w_in[:, :a_w].astype(BF16),
              w_in[:, a_w:2 * a_w].T.astype(BF16),
              w_in[:, 2 * a_w:3 * a_w].T.astype(BF16),
              w_in[:, 3 * a_w + H_FOX:].astype(BF16),
              jnp.pad(w_in[:, 3 * a_w:3 * a_w + H_FOX].T, gate_pad).astype(BF16),
              jnp.pad(b_forget[0][:, None], gate_pad),
              w_pool[0].astype(BF16), pool_scale[0][None])
    w_sb = w_in_sb[0]
    sb_wts = (w_sb[:, :d].astype(BF16), w_sb[:, d:2 * d].T.astype(BF16), w_sb[:, 2 * d:].T.astype(BF16))
    wts = (norm_g, ab_wts, w_out_ab[0].astype(BF16), sb_wts, w_out_sb[0].astype(BF16),
           w_up.astype(BF16), w_down.astype(BF16), final_g)

    y_p, st_p = _trunk(True, x_prompt, mods_p, None, wts)
    past = (cache_fox_k[0], cache_fox_v[0], cache_fox_logf[0], state_pool[0],
            cache_sb_k[0], cache_sb_v[0])
    y_s, st_s = _trunk(False, x_sample, mods_s, past, wts)
    fk, fv, fl, pool, sk, sv = st_p
    fk2, fv2, fl2, pool2, sk2, sv2 = st_s
    return (y_p, y_s, fk, fv, fl, pool, sk, sv, fk2, fv2, fl2, pool2, sk2, sv2)
```

```python
import functools

import jax
import jax.numpy as jnp
from jax import lax
from jax.experimental import pallas as pl
from jax.experimental.pallas import tpu as pltpu

F32 = jnp.float32
BF16 = jnp.bfloat16

DH = 64
H_FOX = 8
EPS = 1e-6
POOL_WINDOWS = (2, 4, 8, 16)
POOL_GC = 128
HIST_ROWS = 16
NEG_BIG = -1e30
LOG2E = 1.4426950408889634
LANES = 128
GATE_ROWS = 16
AUG_ROWS = 16
OFF_SLOT = 12
VMEM_LIMIT = 56 * 1024 * 1024

UNDERFLOW_EXP = 88.0
NORM_MARGIN = 1.01

T_ATT = 256
NQ_ATT = 4
TK_DEC = 512
TM_ROWS = 512
ADA_ROWS = 32
ADA_TN = 1536


def _cparams(sem):
    return pltpu.CompilerParams(dimension_semantics=sem, vmem_limit_bytes=VMEM_LIMIT)


def _rms_mod(x, g, sc, sh):
    y = x * lax.rsqrt(jnp.mean(x * x, axis=-1, keepdims=True) + EPS)
    return (y * g) * (1.0 + sc) + sh


def _softplus(z):
    return jnp.maximum(z, 0.0) + jnp.log(1.0 + jnp.exp2(jnp.abs(z) * (-LOG2E)))


def _log_sigmoid(x):
    return jnp.minimum(x, 0.0) - jnp.log(1.0 + jnp.exp(-jnp.abs(x)))


def _split3(x):
    hi = x.astype(BF16)
    r = x - hi.astype(F32)
    mid = r.astype(BF16)
    lo = (r - mid.astype(F32)).astype(BF16)
    return hi, mid, lo


def _dot(a, b):
    return jnp.dot(a, b, preferred_element_type=F32)


def _dot_nt(a, b):
    return lax.dot_general(a, b, (((1,), (1,)), ((), ())), preferred_element_type=F32)


def _dot3(x, m):
    hi, mid, lo = _split3(x)
    return _dot(hi, m) + _dot(mid, m) + _dot(lo, m)


def _iota(shape, dim):
    return lax.broadcasted_iota(jnp.int32, shape, dim)


def _log2(n):
    assert n & (n - 1) == 0, n
    return n.bit_length() - 1


def _ada_kernel(c_ref, w_ref, b_ref, o_ref):
    c = c_ref[...]
    cond = (c / (1.0 + jnp.exp(-c))).astype(BF16)
    o_ref[...] = _dot(cond, w_ref[...].astype(BF16)) + b_ref[...]


def _ada_call(c_all, w_ada, b_ada):
    depth, d, n = w_ada.shape
    return pl.pallas_call(
        _ada_kernel,
        out_shape=jax.ShapeDtypeStruct((depth, ADA_ROWS, n), F32),
        grid=(depth, n // ADA_TN),
        in_specs=[
            pl.BlockSpec((ADA_ROWS, d), lambda i, j: (0, 0)),
            pl.BlockSpec((None, d, ADA_TN), lambda i, j: (i, 0, j)),
            pl.BlockSpec((None, 1, ADA_TN), lambda i, j: (i, 0, j)),
        ],
        out_specs=pl.BlockSpec((None, ADA_ROWS, ADA_TN), lambda i, j: (i, 0, j)),
        compiler_params=_cparams(("arbitrary", "arbitrary")),
        name="ada_mod",
    )(c_all, w_ada, b_ada.reshape(depth, 1, n))


def _mod_spec(prompt, tm, d, chunk, tiles_per_batch):
    if prompt:
        return pl.BlockSpec((None, 1, d), lambda i: (i // tiles_per_batch, 0, chunk))
    return pl.BlockSpec((tm, d), lambda i: (i, chunk))


def _pool_branch(u, ext_ref, pos, ng, rg, wp_ref, ps_ref, pooled_ref, pst_ref):
    tm, a_w = u.shape
    u3 = u.reshape(ng, rg, a_w)
    ext_ref[:, HIST_ROWS:HIST_ROWS + rg, :] = u3
    pst_ref[...] = u3[:, rg - HIST_ROWS:rg, :]
    for g, w in enumerate(POOL_WINDOWS):
        cs = slice(g * POOL_GC, (g + 1) * POOL_GC)
        acc = u3[:, :, cs]
        for kk in range(1, w):
            acc = acc + ext_ref[:, pl.ds(HIST_ROWS - kk, rg), cs]
        cnt = jnp.minimum(pos + 1, w).astype(F32)
        pooled = acc.reshape(tm, POOL_GC) / cnt - u[:, cs]
        pg = _dot(pooled.astype(BF16), wp_ref[g]) * ps_ref[:, cs]
        pooled_ref[:, cs] = pg.astype(pooled_ref.dtype)


def _ab_in_prompt_kernel(tm, tiles_per_batch, x_ref, sh_ref, sc_ref, g_ref, wq_ref, wkt_ref, wvt_ref,
                         wu_ref, wft_ref, bf_ref, wp_ref, ps_ref,
                         q_ref, ktb_ref, vtb_ref, kt_ref, vt_ref, lft_ref, fct_ref, kaug_ref, pooled_ref, pst_ref,
                         ext_ref, carry_ref):
    h = _rms_mod(x_ref[...], g_ref[...], sc_ref[...], sh_ref[...]).astype(BF16)
    q_ref[...] = (_dot(h, wq_ref[...]) * (DH ** -0.5)).astype(q_ref.dtype)
    kt = _dot_nt(wkt_ref[...], h)
    kt_ref[...] = kt
    ktb_ref[...] = kt.astype(BF16)
    vt = _dot_nt(wvt_ref[...], h)
    vt_ref[...] = vt
    vtb_ref[...] = vt.astype(BF16)

    tile_in_batch = pl.program_id(0) % tiles_per_batch
    first = tile_in_batch == 0
    logf = _log_sigmoid(_dot_nt(wft_ref[...], h) + bf_ref[...])
    lft_ref[...] = logf[0:H_FOX, :]
    tri = (_iota((tm, tm), 0) <= _iota((tm, tm), 1)).astype(BF16)
    fcum = _dot3(logf, tri) + jnp.where(first, 0.0, carry_ref[...])
    carry_ref[...] = fcum[:, tm - 1:tm]
    fct_ref[...] = fcum[0:H_FOX, :]

    n_aug = (H_FOX // 2) * AUG_ROWS
    slot = _iota((n_aug, GATE_ROWS), 0) & (AUG_ROWS - 1)
    head0 = (_iota((n_aug, GATE_ROWS), 0) >> _log2(AUG_ROWS)) * 2
    col = _iota((n_aug, GATE_ROWS), 1)
    kaug = None
    for part, piece in enumerate(_split3(fcum)):
        sel = jnp.logical_or(jnp.logical_and(slot == 3 + part, col == head0),
                             jnp.logical_and(slot == 9 + part, col == head0 + 1))
        term = _dot(jnp.where(sel, -1.0, 0.0).astype(BF16), piece)
        kaug = term if kaug is None else kaug + term
    slot1 = _iota((n_aug, 1), 0) & (AUG_ROWS - 1)
    ones = jnp.logical_or(slot1 < 3, jnp.logical_and(slot1 >= 6, slot1 < 9))
    kaug_ref[...] = (kaug + jnp.where(ones, 1.0, 0.0)).astype(BF16)

    @pl.when(first)
    def _():
        ext_ref[:, 0:HIST_ROWS, :] = jnp.zeros((1, HIST_ROWS, ext_ref.shape[2]), F32)

    @pl.when(jnp.logical_not(first))
    def _():
        ext_ref[:, 0:HIST_ROWS, :] = ext_ref[:, tm:tm + HIST_ROWS, :]

    pos = tile_in_batch * tm + _iota((tm, 1), 0)
    _pool_branch(_dot(h, wu_ref[...]), ext_ref, pos, 1, tm, wp_ref, ps_ref, pooled_ref, pst_ref)


def _ab_in_sample_kernel(ng, rg, p0, x_ref, sh_ref, sc_ref, g_ref, wq_ref, wkt_ref, wvt_ref,
                         wu_ref, wft_ref, bf_ref, wp_ref, ps_ref, hist_ref,
                         q_ref, kb_ref, vb_ref, k32_ref, v32_ref, lft_ref, pooled_ref, pst_ref, ext_ref):
    tm = ng * rg
    h = _rms_mod(x_ref[...], g_ref[...], sc_ref[...], sh_ref[...]).astype(BF16)
    q_ref[...] = (_dot(h, wq_ref[...]) * (DH ** -0.5)).astype(q_ref.dtype)
    k = _dot_nt(h, wkt_ref[...])
    k32_ref[...] = k
    kb_ref[...] = k.astype(BF16)
    v = _dot_nt(h, wvt_ref[...])
    v32_ref[...] = v
    vb_ref[...] = v.astype(BF16)
    logf = _log_sigmoid(_dot_nt(wft_ref[...], h) + bf_ref[...])
    lft_ref[...] = logf[0:H_FOX, :]
    ext_ref[:, 0:HIST_ROWS, :] = hist_ref[...]
    pos = p0 + (_iota((tm, 1), 0) & (rg - 1))
    _pool_branch(_dot(h, wu_ref[...]), ext_ref, pos, ng, rg, wp_ref, ps_ref, pooled_ref, pst_ref)


def _ab_in_call(prompt, x2, mod_arr, g, wts, hist, rows_per_batch, p0):
    wq, wkt, wvt, wu, wft, bf, wp, ps = wts
    m, d = x2.shape
    a_w = H_FOX * DH
    nb = m // rows_per_batch
    tm = TM_ROWS if prompt else m
    tpb = rows_per_batch // tm if prompt else 1
    row = lambda i: (i, 0)
    const2 = lambda i: (0, 0)
    in_specs = [pl.BlockSpec((tm, d), row), _mod_spec(prompt, tm, d, 0, tpb), _mod_spec(prompt, tm, d, 1, tpb),
                pl.BlockSpec((1, d), const2)]
    in_specs += [pl.BlockSpec(w.shape, const2) for w in (wq, wkt, wvt, wu, wft, bf)]
    in_specs += [pl.BlockSpec(wp.shape, lambda i: (0, 0, 0)), pl.BlockSpec(ps.shape, const2)]
    args = [x2, mod_arr, mod_arr, g, wq, wkt, wvt, wu, wft, bf, wp, ps]
    if prompt:
        tcol = lambda i: (i // tpb, 0, i % tpb)
        out_shape = [jax.ShapeDtypeStruct((m, a_w), BF16),
                     jax.ShapeDtypeStruct((nb, a_w, rows_per_batch), BF16),
                     jax.ShapeDtypeStruct((nb, a_w, rows_per_batch), BF16),
                     jax.ShapeDtypeStruct((nb, a_w, rows_per_batch), F32),
                     jax.ShapeDtypeStruct((nb, a_w, rows_per_batch), F32),
                     jax.ShapeDtypeStruct((nb, H_FOX, rows_per_batch), F32),
                     jax.ShapeDtypeStruct((nb, H_FOX, rows_per_batch), F32),
                     jax.ShapeDtypeStruct((nb, (H_FOX // 2) * AUG_ROWS, rows_per_batch), BF16),
                     jax.ShapeDtypeStruct((m, a_w), BF16),
                     jax.ShapeDtypeStruct((nb, HIST_ROWS, a_w), F32)]
        out_specs = ([pl.BlockSpec((tm, a_w), row)] + [pl.BlockSpec((None, a_w, tm), tcol)] * 4
                     + [pl.BlockSpec((None, H_FOX, tm), tcol)] * 2
                     + [pl.BlockSpec((None, (H_FOX // 2) * AUG_ROWS, tm), tcol)]
                     + [pl.BlockSpec((tm, a_w), row),
                        pl.BlockSpec((1, HIST_ROWS, a_w), lambda i: (i // tpb, 0, 0))])
        scratch = [pltpu.VMEM((1, HIST_ROWS + tm, a_w), F32), pltpu.VMEM((GATE_ROWS, 1), F32)]
        body = functools.partial(_ab_in_prompt_kernel, tm, tpb)
    else:
        in_specs.append(pl.BlockSpec(hist.shape, lambda i: (0, 0, 0)))
        args.append(hist)
        out_shape = [jax.ShapeDtypeStruct((m, a_w), BF16), jax.ShapeDtypeStruct((m, a_w), BF16),
                     jax.ShapeDtypeStruct((m, a_w), BF16), jax.ShapeDtypeStruct((m, a_w), F32),
                     jax.ShapeDtypeStruct((m, a_w), F32), jax.ShapeDtypeStruct((H_FOX, m), F32),
                     jax.ShapeDtypeStruct((m, a_w), BF16), jax.ShapeDtypeStruct((nb, HIST_ROWS, a_w), F32)]
        out_specs = ([pl.BlockSpec((tm, a_w), row)] * 5 + [pl.BlockSpec((H_FOX, tm), const2)]
                     + [pl.BlockSpec((tm, a_w), row), pl.BlockSpec((nb, HIST_ROWS, a_w), lambda i: (0, 0, 0))])
        scratch = [pltpu.VMEM((nb, HIST_ROWS + rows_per_batch, a_w), F32)]
        body = functools.partial(_ab_in_sample_kernel, nb, rows_per_batch, p0)
    return pl.pallas_call(
        body, out_shape=out_shape, grid=(m // tm,), in_specs=in_specs, out_specs=out_specs,
        scratch_shapes=scratch,
        compiler_params=_cparams(("arbitrary",)),
        name="ab_in_prompt" if prompt else "ab_in_sample",
    )(*args)


def _head_split(q2):
    lane = _iota((1, LANES), 1)
    zero = jnp.zeros_like(q2)
    return jnp.where(lane < DH, q2, zero), jnp.where(lane >= DH, q2, zero)


def _pair_sel():
    return (_iota((8, LANES), 0) == (_iota((8, LANES), 1) >> _log2(DH))).astype(BF16)


def _fox_kernel(q_ref, kt_ref, vt_ref, ka_ref, fq_ref, ft_ref, o_ref,
                m_ref, acc_ref, kn_ref):
    t, nq = T_ATT, NQ_ATT
    qg = pl.program_id(2)
    lane = _iota((1, LANES), 1)
    lane2 = _iota((1, 2), 1)

    @pl.when(qg == 0)
    def _():
        chunk = 4 * t

        def body(i, mx):
            ks = pl.multiple_of(i * chunk, chunk)
            kf = kt_ref[:, pl.ds(ks, chunk)].astype(F32)
            n2 = _dot(_pair_sel(), (kf * kf).astype(BF16))
            return jnp.maximum(mx, jnp.max(n2, axis=1, keepdims=True))

        mx = lax.fori_loop(0, kt_ref.shape[1] // chunk, body, jnp.zeros((8, 1), F32))
        kn_ref[...] = jnp.sqrt(mx)

    lane_a = _iota((1, AUG_ROWS), 1)
    row_v = _iota((LANES, 1), 0)
    qcat, bound0 = [], []
    for s in range(nq):
        q2 = q_ref[s * t:(s + 1) * t, :]
        fq = fq_ref[s * t:(s + 1) * t, :]
        per_head = []
        for j, qh in enumerate(_head_split(q2)):
            hi, mid, lo = (x.astype(F32) for x in _split3(fq[:, j:j + 1]))
            base = 6 * j
            qa = jnp.where(lane_a == base, hi, jnp.where(lane_a == base + 1, mid, jnp.where(
                lane_a == base + 2, lo,
                jnp.where(jnp.logical_and(lane_a >= base + 3, lane_a < base + 6), 1.0, 0.0))))
            qa = jnp.where(lane_a == OFF_SLOT, NEG_BIG, qa)
            per_head.append(jnp.concatenate([qh, qa.astype(BF16)], axis=1))
        qcat.append(per_head)
        qf = q2.astype(F32)
        qn2 = _dot_nt((qf * qf).astype(BF16), _pair_sel())
        qn = jnp.sqrt(jnp.max(qn2, axis=0, keepdims=True))
        qk = jnp.where(lane2 == 0, qn[:, 0:1] * kn_ref[0:1, :], qn[:, 1:2] * kn_ref[1:2, :])
        bound0.append(qk * NORM_MARGIN + fq[0:1, :])
    m_ref[...] = jnp.full(m_ref.shape, NEG_BIG, F32)
    acc_ref[...] = jnp.zeros(acc_ref.shape, F32)

    def step(n, diag):
        chains = [(s, j) for s in range(nq) for j in range(2)]
        kbs = [qg * nq + s - n for s in range(nq)]
        kss = [pl.multiple_of((kb if diag else jnp.maximum(kb, 0)) * t, t) for kb in kbs]
        causal = _iota((t, t), 1) <= _iota((t, t), 0)
        kcat = []
        for s in range(nq):
            ka = ka_ref[:, pl.ds(kss[s], t)]
            if not diag:
                off = jnp.where(kbs[s] >= 0, 0.0, 1.0).astype(BF16)
                ka = jnp.where(_iota((AUG_ROWS, 1), 0) == OFF_SLOT, off, ka)
            kcat.append(jnp.concatenate([kt_ref[:, pl.ds(kss[s], t)], ka], axis=0))
        sc = []
        for s, j in chains:
            x = _dot(qcat[s][j], kcat[s])
            sc.append(jnp.where(causal, x, NEG_BIG) if diag else x)
        vcat = []
        for s, j in chains:
            v2 = vt_ref[:, pl.ds(kss[s], t)]
            mine = (row_v < DH) if j == 0 else (row_v >= DH)
            vcat.append(jnp.where(mine, v2, jnp.ones_like(v2)))
        m_prev = [m_ref[2 * s + j] for s, j in chains]
        m_new = [jnp.maximum(mp, jnp.max(x, axis=1, keepdims=True)) for mp, x in zip(m_prev, sc)]
        p = [jnp.exp(x - jnp.concatenate([mn] * (t // LANES), axis=1)) for x, mn in zip(sc, m_new)]
        pv = [_dot_nt(pp.astype(BF16), vv) for pp, vv in zip(p, vcat)]
        for i, (s, j) in enumerate(chains):
            idx = 2 * s + j
            acc_ref[idx] = jnp.exp(m_prev[i] - m_new[i]) * acc_ref[idx] + pv[i]
            m_ref[idx] = m_new[i]

    def more(n):
        go = False
        for s in range(nq):
            kb = qg * nq + s - n
            ft = ft_ref[pl.ds(jnp.maximum(kb, 0), 1), :]
            m_min = jnp.where(lane2 == 0, jnp.min(m_ref[2 * s], axis=(0, 1), keepdims=True),
                              jnp.min(m_ref[2 * s + 1], axis=(0, 1), keepdims=True))
            top = jnp.max(bound0[s] - ft - m_min)
            go = jnp.logical_or(go, jnp.logical_and(kb >= 0, top > -UNDERFLOW_EXP))
        return go

    step(0, True)

    def body(c):
        n, _ = c
        step(n, False)
        return n + 1, more(n + 1)

    lax.while_loop(lambda c: c[1], body, (jnp.int32(1), more(1)))
    for s in range(nq):
        a0, a1 = acc_ref[2 * s], acc_ref[2 * s + 1]
        o = jnp.where(lane < DH, a0 / pltpu.roll(a0, DH, 1), a1 / pltpu.roll(a1, DH, 1))
        o_ref[s * t:(s + 1) * t, :] = o.astype(o_ref.dtype)


def _fox_call(q, ktb, vtb, kaug, fq_arr, ft_arr):
    b, l, a_w = q.shape
    hp = a_w // LANES
    t, nq = T_ATT, NQ_ATT
    tq = t * nq
    return pl.pallas_call(
        _fox_kernel,
        out_shape=jax.ShapeDtypeStruct((b, l, a_w), BF16),
        grid=(b, hp, l // tq),
        in_specs=[
            pl.BlockSpec((None, tq, LANES), lambda bi, h, qi: (bi, qi, h)),
            pl.BlockSpec((None, LANES, l), lambda bi, h, qi: (bi, h, 0)),
            pl.BlockSpec((None, LANES, l), lambda bi, h, qi: (bi, h, 0)),
            pl.BlockSpec((None, AUG_ROWS, l), lambda bi, h, qi: (bi, h, 0)),
            pl.BlockSpec((None, None, tq, 2), lambda bi, h, qi: (bi, h, qi, 0)),
            pl.BlockSpec((None, None, l // t, 2), lambda bi, h, qi: (bi, h, 0, 0)),
        ],
        out_specs=pl.BlockSpec((None, tq, LANES), lambda bi, h, qi: (bi, qi, h)),
        scratch_shapes=[pltpu.VMEM((2 * nq, t, LANES), F32), pltpu.VMEM((2 * nq, t, LANES), F32),
                        pltpu.VMEM((8, 1), F32)],
        compiler_params=_cparams(("arbitrary", "arbitrary", "arbitrary")),
        name="fox_prompt",
    )(q, ktb, vtb, kaug, fq_arr, ft_arr)


def _sb_kernel(q_ref, kt_ref, vt_ref, o_ref, carry_ref, acc_ref):
    t, nq = T_ATT, NQ_ATT
    qg = pl.program_id(2)
    lane = _iota((1, LANES), 1)
    qh = [_head_split(q_ref[s * t:(s + 1) * t, :]) for s in range(nq)]
    carry_ref[...] = jnp.zeros(carry_ref.shape, F32)
    acc_ref[...] = jnp.zeros(acc_ref.shape, F32)

    def step(n, diag):
        tri = (_iota((t, t), 0) > _iota((t, t), 1)).astype(BF16)
        valid = _iota((t, t), 1) < _iota((t, t), 0)
        chains = [(s, j) for s in range(nq) for j in range(2)]
        kbs = [qg * nq + s - n for s in range(nq)]
        kss = [pl.multiple_of((kb if diag else jnp.maximum(kb, 0)) * t, t) for kb in kbs]
        z = [_dot(qh[s][j], kt_ref[:, pl.ds(kss[s], t)]) for s, j in chains]
        sp = [_softplus(zz) for zz in z]
        spm = [jnp.where(valid, x, 0.0) for x in sp] if diag else sp
        later = [_dot(x.astype(BF16), tri) for x in spm]
        vs = []
        for s in range(nq):
            v2 = vt_ref[:, pl.ds(kss[s], t)]
            if not diag:
                v2 = jnp.where(kbs[s] >= 0, v2, jnp.zeros_like(v2))
            vs.append(v2)
        for i, (s, j) in enumerate(chains):
            idx = 2 * s + j
            c = carry_ref[idx]
            a = jnp.exp((z[i] - sp[i]) - later[i] - c)
            if diag:
                a = jnp.where(valid, a, 0.0)
            acc_ref[idx] = acc_ref[idx] + _dot_nt(a.astype(BF16), vs[s])
            carry_ref[idx] = c + jnp.sum(spm[i], axis=1, keepdims=True)

    def more(n):
        in_range = n <= qg * nq + nq - 1
        return jnp.logical_and(in_range, jnp.min(carry_ref[...]) < UNDERFLOW_EXP)

    step(0, True)

    def body(c):
        n, _ = c
        step(n, False)
        return n + 1, more(n + 1)

    lax.while_loop(lambda c: c[1], body, (jnp.int32(1), more(1)))
    for s in range(nq):
        o = jnp.where(lane < DH, acc_ref[2 * s], acc_ref[2 * s + 1])
        o_ref[s * t:(s + 1) * t, :] = o.astype(o_ref.dtype)


def _sb_call(q, ktb, vtb):
    b, l, d = q.shape
    hp = d // LANES
    t, nq = T_ATT, NQ_ATT
    tq = t * nq
    return pl.pallas_call(
        _sb_kernel,
        out_shape=jax.ShapeDtypeStruct((b, l, d), BF16),
        grid=(b, hp, l // tq),
        in_specs=[
            pl.BlockSpec((None, tq, LANES), lambda bi, h, qi: (bi, qi, h)),
            pl.BlockSpec((None, LANES, l), lambda bi, h, qi: (bi, h, 0)),
            pl.BlockSpec((None, LANES, l), lambda bi, h, qi: (bi, h, 0)),
        ],
        out_specs=pl.BlockSpec((None, tq, LANES), lambda bi, h, qi: (bi, qi, h)),
        scratch_shapes=[pltpu.VMEM((2 * nq, t, 1), F32), pltpu.VMEM((2 * nq, t, LANES), F32)],
        compiler_params=_cparams(("arbitrary", "arbitrary", "arbitrary")),
        name="sb_prompt",
    )(q, ktb, vtb)


def _rep_matrix(rows, n_heads, ds):
    return ((_iota((rows, ds), 0) >> _log2(n_heads)) == _iota((rows, ds), 1)).astype(BF16)


def _head_mask(rows, n_heads, width):
    return (_iota((rows, width), 0) & (n_heads - 1)) == (_iota((rows, width), 1) >> _log2(DH))


def _block_diag_q(q, n_heads):
    ds, width = q.shape
    rows = ds * n_heads
    qrep = _dot(_rep_matrix(rows, n_heads, ds), q)
    return jnp.where(_head_mask(rows, n_heads, width), qrep, 0.0).astype(BF16)


def _collect_heads(acc, n_heads, ds):
    rows, width = acc.shape
    om = jnp.where(_head_mask(rows, n_heads, width), acc, 0.0).astype(BF16)
    rep_t = ((_iota((ds, rows), 1) >> _log2(n_heads)) == _iota((ds, rows), 0)).astype(BF16)
    return _dot(rep_t, om)


def _fox_dec_kernel(n_heads, ds, q_ref, kn_ref, vn_ref, lfn_ref, kc_ref, vc_ref, lfc_ref,
                    o_ref, qbd_ref, m_ref, l_ref, acc_ref, cf_ref):
    rows = ds * n_heads
    step_i = pl.program_id(1)
    n_steps = pl.num_programs(1)

    def update(s, pv_fn):
        m_prev = m_ref[...]
        m_new = jnp.maximum(m_prev, jnp.max(s, axis=1, keepdims=True))
        alpha = jnp.exp(m_prev - m_new)
        p = jnp.exp(s - m_new)
        l_ref[...] = alpha * l_ref[...] + jnp.sum(p, axis=1, keepdims=True)
        acc_ref[...] = alpha * acc_ref[...] + pv_fn(p.astype(BF16))
        m_ref[...] = m_new

    @pl.when(step_i == 0)
    def _():
        qbd_ref[...] = _block_diag_q(q_ref[...], n_heads)
        m_ref[...] = jnp.full(m_ref.shape, NEG_BIG, F32)
        l_ref[...] = jnp.zeros(l_ref.shape, F32)
        acc_ref[...] = jnp.zeros(acc_ref.shape, F32)
        a = jnp.concatenate([lfn_ref[...]] * ds, axis=0)
        ri = _iota((rows, ds), 0) >> _log2(n_heads)
        cj = _iota((rows, ds), 1)
        am = jnp.where(cj <= ri, a, 0.0)
        tri = (_iota((ds, ds), 0) > _iota((ds, ds), 1)).astype(BF16)
        s = _dot_nt(qbd_ref[...], kn_ref[...]) + _dot3(am, tri)
        s = jnp.where(cj <= ri, s, NEG_BIG)
        update(s, lambda p: _dot(p, vn_ref[...]))
        cf_ref[...] = jnp.sum(am, axis=1, keepdims=True)

    @pl.when(step_i > 0)
    def _():
        tk = kc_ref.shape[1]
        a = jnp.concatenate([lfc_ref[...]] * ds, axis=0)
        tri = (_iota((tk, tk), 0) > _iota((tk, tk), 1)).astype(BF16)
        cf = cf_ref[...]
        s = _dot(qbd_ref[...], kc_ref[...].astype(BF16)) + (_dot3(a, tri) + cf)
        update(s, lambda p: _dot_nt(p, vc_ref[...].astype(BF16)))
        cf_ref[...] = cf + jnp.sum(a, axis=1, keepdims=True)

    @pl.when(step_i == n_steps - 1)
    def _():
        o_ref[...] = _collect_heads(acc_ref[...] / l_ref[...], n_heads, ds).astype(o_ref.dtype)


def _fox_dec_call(q, kn, vn, kct, vct, lfn_t, lfc_t):
    db, ds, width = q.shape
    n_heads = width // DH
    rows = ds * n_heads
    nkb = kct.shape[2] // TK_DEC
    cache_blk = lambda s: jnp.minimum(nkb - s, nkb - 1)
    new_idx = lambda b, s: (b, 0, 0)
    cache_idx = lambda b, s: (b, 0, cache_blk(s))
    in_specs = ([pl.BlockSpec((None, ds, width), new_idx)] * 3
                + [pl.BlockSpec((None, n_heads, ds), new_idx)]
                + [pl.BlockSpec((None, width, TK_DEC), cache_idx)] * 2
                + [pl.BlockSpec((None, n_heads, TK_DEC), cache_idx)])
    scratch = [pltpu.VMEM((rows, width), BF16), pltpu.VMEM((rows, 1), F32), pltpu.VMEM((rows, 1), F32),
               pltpu.VMEM((rows, width), F32), pltpu.VMEM((rows, 1), F32)]
    return pl.pallas_call(
        functools.partial(_fox_dec_kernel, n_heads, ds),
        out_shape=jax.ShapeDtypeStruct((db, ds, width), BF16),
        grid=(db, nkb + 1),
        in_specs=in_specs,
        out_specs=pl.BlockSpec((None, ds, width), new_idx),
        scratch_shapes=scratch,
        compiler_params=_cparams(("arbitrary", "arbitrary")),
        name="fox_decode",
    )(q, kn, vn, lfn_t, kct, vct, lfc_t)


def _sb_dec_tile(qbd, kc_ref, vc_ref, carry):
    tk = kc_ref.shape[1]
    z = _dot(qbd, kc_ref[...].astype(BF16))
    sp = _softplus(z)
    tri = (_iota((tk, tk), 0) > _iota((tk, tk), 1)).astype(BF16)
    later = _dot(sp.astype(BF16), tri)
    a = jnp.exp((z - sp) - later - carry)
    return _dot_nt(a.astype(BF16), vc_ref[...].astype(BF16)), carry + jnp.sum(sp, axis=1, keepdims=True)


def _sb_dec_first_kernel(n_heads, ds, q_ref, kn_ref, vn_ref, kc_ref, vc_ref,
                         o_ref, carry_out_ref, cmin_ref):
    rows = ds * n_heads
    qbd = _block_diag_q(q_ref[...], n_heads)
    ri = _iota((rows, ds), 0) >> _log2(n_heads)
    cj = _iota((rows, ds), 1)
    valid = cj < ri
    z = _dot_nt(qbd, kn_ref[...])
    sp = _softplus(z)
    spm = jnp.where(valid, sp, 0.0)
    tri = (_iota((ds, ds), 0) > _iota((ds, ds), 1)).astype(BF16)
    later = _dot(spm.astype(BF16), tri)
    a = jnp.where(valid, jnp.exp((z - sp) - later), 0.0)
    acc = _dot(a.astype(BF16), vn_ref[...])
    carry = jnp.sum(spm, axis=1, keepdims=True)
    part, carry = _sb_dec_tile(qbd, kc_ref, vc_ref, carry)
    o_ref[...] = _collect_heads(acc + part, n_heads, ds).astype(o_ref.dtype)
    carry_out_ref[...] = carry
    cmin_ref[...] = jnp.broadcast_to(jnp.min(carry, axis=0, keepdims=True), cmin_ref.shape)


def _sb_dec_rest_kernel(n_heads, ds, need_ref, q_ref, carry_in_ref, o1_ref, kc_ref, vc_ref,
                        o_ref, qbd_ref, acc_ref, carry_ref):
    step_i = pl.program_id(1)
    last = pl.num_programs(1) - 1
    needed = need_ref[pl.program_id(0)] != 0

    @pl.when(needed)
    def _():
        @pl.when(step_i == 0)
        def _():
            qbd_ref[...] = _block_diag_q(q_ref[...], n_heads)
            acc_ref[...] = jnp.zeros(acc_ref.shape, F32)
            carry_ref[...] = carry_in_ref[...]

        @pl.when(jnp.min(carry_ref[...]) < UNDERFLOW_EXP)
        def _():
            part, carry = _sb_dec_tile(qbd_ref[...], kc_ref, vc_ref, carry_ref[...])
            acc_ref[...] = acc_ref[...] + part
            carry_ref[...] = carry

        @pl.when(step_i == last)
        def _():
            o = o1_ref[...].astype(F32) + _collect_heads(acc_ref[...], n_heads, ds)
            o_ref[...] = o.astype(o_ref.dtype)

    @pl.when(jnp.logical_and(jnp.logical_not(needed), step_i == last))
    def _():
        o_ref[...] = o1_ref[...]


def _sb_dec_call(q, kn, vn, kct, vct):
    db, ds, width = q.shape
    n_heads = width // DH
    rows = ds * n_heads
    nkb = kct.shape[2] // TK_DEC
    bidx = lambda b: (b, 0, 0)
    tile = (None, width, TK_DEC)
    o1, carry, cmin = pl.pallas_call(
        functools.partial(_sb_dec_first_kernel, n_heads, ds),
        out_shape=[jax.ShapeDtypeStruct((db, ds, width), BF16),
                   jax.ShapeDtypeStruct((db, rows, 1), F32),
                   jax.ShapeDtypeStruct((db, 1, LANES), F32)],
        grid=(db,),
        in_specs=[pl.BlockSpec((None, ds, width), bidx)] * 3
        + [pl.BlockSpec(tile, lambda b: (b, 0, nkb - 1))] * 2,
        out_specs=[pl.BlockSpec((None, ds, width), bidx), pl.BlockSpec((None, rows, 1), bidx),
                   pl.BlockSpec((None, 1, LANES), bidx)],
        compiler_params=_cparams(("arbitrary",)),
        name="sb_decode_first",
    )(q, kn, vn, kct, vct)
    if nkb == 1:
        return o1
    need = (cmin[:, 0, 0] < UNDERFLOW_EXP).astype(jnp.int32)
    bidx2 = lambda b, s, need_ref: (b, 0, 0)

    def cache_idx(b, s, need_ref):
        on = need_ref[b] != 0
        return (jnp.where(on, b, 0), 0, jnp.where(on, nkb - 2 - s, 0))

    grid_spec = pltpu.PrefetchScalarGridSpec(
        num_scalar_prefetch=1,
        grid=(db, nkb - 1),
        in_specs=[pl.BlockSpec((None, ds, width), bidx2), pl.BlockSpec((None, rows, 1), bidx2),
                  pl.BlockSpec((None, ds, width), bidx2),
                  pl.BlockSpec(tile, cache_idx), pl.BlockSpec(tile, cache_idx)],
        out_specs=pl.BlockSpec((None, ds, width), bidx2),
        scratch_shapes=[pltpu.VMEM((rows, width), BF16), pltpu.VMEM((rows, width), F32),
                        pltpu.VMEM((rows, 1), F32)],
    )
    return pl.pallas_call(
        functools.partial(_sb_dec_rest_kernel, n_heads, ds),
        out_shape=jax.ShapeDtypeStruct((db, ds, width), BF16),
        grid_spec=grid_spec,
        compiler_params=_cparams(("arbitrary", "arbitrary")),
        name="sb_decode_rest",
    )(need, q, carry, o1, kct, vct)


def _post_kernel(n_act, final, f_chunk, *refs):
    acts = refs[:n_act]
    wo_ref, x_ref, gate1_ref, sh_ref, sc_ref, gate_ref, g_ref, wu_ref, wd_ref = refs[n_act:n_act + 9]
    fg_ref = refs[n_act + 9] if final else None
    o_ref = refs[-1]
    y = None
    off = 0
    for a_ref in acts:
        kdim = a_ref.shape[1]
        part = _dot(a_ref[...], wo_ref[off:off + kdim, :])
        y = part if y is None else y + part
        off += kdim
    x = x_ref[...] + gate1_ref[...] * y
    h = _rms_mod(x, g_ref[...], sc_ref[...], sh_ref[...]).astype(BF16)
    d_ff = wu_ref.shape[1]
    acc = None
    for c in range(d_ff // f_chunk):
        cs = slice(c * f_chunk, (c + 1) * f_chunk)
        a = jnp.maximum(_dot(h, wu_ref[:, cs]), 0.0)
        part = _dot((a * a).astype(BF16), wd_ref[cs, :])
        acc = part if acc is None else acc + part
    xo = x + gate_ref[...] * acc
    if final:
        y = xo * lax.rsqrt(jnp.mean(xo * xo, axis=-1, keepdims=True) + EPS)
        xo = y * fg_ref[...]
    o_ref[...] = xo


def _post_call(prompt, acts, wo, x2, mod_arr, g, wu, wd, final_g, rows_per_batch):
    m, d = x2.shape
    tm = TM_ROWS if prompt else m
    tpb = rows_per_batch // tm if prompt else 1
    row = lambda i: (i, 0)
    const2 = lambda i: (0, 0)
    final = final_g is not None
    resident = lambda w: pl.BlockSpec(w.shape, const2, pipeline_mode=pl.Buffered(1))
    in_specs = [pl.BlockSpec((tm, a.shape[1]), row) for a in acts]
    in_specs += [resident(wo), pl.BlockSpec((tm, d), row)]
    in_specs += [_mod_spec(prompt, tm, d, c, tpb) for c in (2, 3, 4, 5)]
    in_specs += [pl.BlockSpec((1, d), const2), resident(wu), resident(wd)]
    args = list(acts) + [wo, x2] + [mod_arr] * 4 + [g, wu, wd]
    if final:
        in_specs.append(pl.BlockSpec((1, d), const2))
        args.append(final_g)
    return pl.pallas_call(
        functools.partial(_post_kernel, len(acts), final, 1024),
        out_shape=jax.ShapeDtypeStruct((m, d), F32),
        grid=(m // tm,), in_specs=in_specs, out_specs=pl.BlockSpec((tm, d), row),
        compiler_params=_cparams(("arbitrary",)),
        name="post",
    )(*args)


def _sb_in_prompt_kernel(x_ref, sh_ref, sc_ref, g_ref, wq_ref, wkt_ref, wvt_ref,
                         q_ref, ktb_ref, vtb_ref, kt_ref, vt_ref):
    h = _rms_mod(x_ref[...], g_ref[...], sc_ref[...], sh_ref[...]).astype(BF16)
    q_ref[...] = (_dot(h, wq_ref[...]) * (DH ** -0.5)).astype(q_ref.dtype)
    kt = _dot_nt(wkt_ref[...], h)
    kt_ref[...] = kt
    ktb_ref[...] = kt.astype(BF16)
    vt = _dot_nt(wvt_ref[...], h)
    vt_ref[...] = vt
    vtb_ref[...] = vt.astype(BF16)


def _sb_in_sample_kernel(x_ref, sh_ref, sc_ref, g_ref, wq_ref, wkt_ref, wvt_ref,
                         q_ref, kb_ref, vb_ref, k32_ref, v32_ref):
    h = _rms_mod(x_ref[...], g_ref[...], sc_ref[...], sh_ref[...]).astype(BF16)
    q_ref[...] = (_dot(h, wq_ref[...]) * (DH ** -0.5)).astype(q_ref.dtype)
    k = _dot_nt(h, wkt_ref[...])
    k32_ref[...] = k
    kb_ref[...] = k.astype(BF16)
    v = _dot_nt(h, wvt_ref[...])
    v32_ref[...] = v
    vb_ref[...] = v.astype(BF16)


def _sb_in_call(prompt, x2, mod_arr, g, wts, rows_per_batch):
    wq, wkt, wvt = wts
    m, d = x2.shape
    nb = m // rows_per_batch
    tm = TM_ROWS if prompt else m
    tpb = rows_per_batch // tm if prompt else 1
    row = lambda i: (i, 0)
    const2 = lambda i: (0, 0)
    in_specs = [pl.BlockSpec((tm, d), row),
                _mod_spec(prompt, tm, d, 0, tpb), _mod_spec(prompt, tm, d, 1, tpb),
                pl.BlockSpec((1, d), const2)] + [pl.BlockSpec(w.shape, const2) for w in wts]
    if prompt:
        tcol = lambda i: (i // tpb, 0, i % tpb)
        out_shape = ([jax.ShapeDtypeStruct((m, d), BF16)]
                     + [jax.ShapeDtypeStruct((nb, d, rows_per_batch), BF16)] * 2
                     + [jax.ShapeDtypeStruct((nb, d, rows_per_batch), F32)] * 2)
        out_specs = [pl.BlockSpec((tm, d), row)] + [pl.BlockSpec((None, d, tm), tcol)] * 4
        body = _sb_in_prompt_kernel
    else:
        out_shape = [jax.ShapeDtypeStruct((m, d), BF16)] * 3 + [jax.ShapeDtypeStruct((m, d), F32)] * 2
        out_specs = [pl.BlockSpec((tm, d), row)] * 5
        body = _sb_in_sample_kernel
    return pl.pallas_call(
        body, out_shape=out_shape, grid=(m // tm,), in_specs=in_specs, out_specs=out_specs,
        compiler_params=_cparams(("arbitrary",)),
        name="sb_in_prompt" if prompt else "sb_in_sample",
    )(x2, mod_arr, mod_arr, g, wq, wkt, wvt)


def _to_state(xt, n_heads):
    nb, _, l = xt.shape
    return xt.reshape(nb, n_heads, DH, l).transpose(0, 3, 1, 2)[None]


def _from_cache(c):
    nb, p, n_heads, dh = c.shape
    return c.transpose(0, 2, 3, 1).reshape(nb, n_heads * dh, p)


def _trunk(prompt, x, mods, past, wts):
    nb, rpb, d = x.shape
    m = nb * rpb
    a_w = H_FOX * DH
    x2 = x.reshape(m, d)
    norm_g, ab_wts, w_out_ab, sb_wts, w_out_sb, w_up, w_down, final_g = wts

    if prompt:
        outs = _ab_in_call(True, x2, mods[0], norm_g[0, 0][None], ab_wts, None, rpb, 0)
        q, ktb, vtb, kt32, vt32, lft, fct, kaug, pooled, pstate = outs
        fq_arr = fct.reshape(nb, a_w // LANES, 2, rpb).transpose(0, 1, 3, 2)
        a_out = _fox_call(q.reshape(nb, rpb, a_w), ktb, vtb, kaug, fq_arr, fq_arr[:, :, T_ATT - 1::T_ATT, :])
        st_fox = (_to_state(kt32, H_FOX), _to_state(vt32, H_FOX), lft.transpose(0, 2, 1)[None])
    else:
        cfk, cfv, cflogf, spool, csk, csv = past
        p0 = cfk.shape[1]
        hist = jnp.pad(spool, ((0, 0), (HIST_ROWS - spool.shape[1], 0), (0, 0)))
        outs = _ab_in_call(False, x2, mods[0], norm_g[0, 0][None], ab_wts, hist, rpb, p0)
        q, kb, vb, k32, v32, lft, pooled, pstate = outs
        lfn_t = lft.reshape(H_FOX, nb, rpb).transpose(1, 0, 2)
        a_out = _fox_dec_call(q.reshape(nb, rpb, a_w), kb.reshape(nb, rpb, a_w), vb.reshape(nb, rpb, a_w),
                              _from_cache(cfk), _from_cache(cfv), lfn_t, cflogf.transpose(0, 2, 1))
        st_fox = (k32.reshape(1, nb, rpb, H_FOX, DH), v32.reshape(1, nb, rpb, H_FOX, DH),
                  lfn_t.transpose(0, 2, 1)[None])
    x2 = _post_call(prompt, [a_out.reshape(m, a_w), pooled], w_out_ab, x2, mods[0], norm_g[0, 1][None],
                    w_up[0], w_down[0], None, rpb)

    n_heads = d // DH
    outs = _sb_in_call(prompt, x2, mods[1], norm_g[1, 0][None], sb_wts, rpb)
    if prompt:
        q1, ktb1, vtb1, skt32, svt32 = outs
        s_out = _sb_call(q1.reshape(nb, rpb, d), ktb1, vtb1)
        st_sb = (_to_state(skt32, n_heads), _to_state(svt32, n_heads))
    else:
        q1, kb1, vb1, sk32, sv32 = outs
        s_out = _sb_dec_call(q1.reshape(nb, rpb, d), kb1.reshape(nb, rpb, d), vb1.reshape(nb, rpb, d),
                             _from_cache(csk), _from_cache(csv))
        st_sb = (sk32.reshape(1, nb, rpb, n_heads, DH), sv32.reshape(1, nb, rpb, n_heads, DH))
    y = _post_call(prompt, [s_out.reshape(m, d)], w_out_sb, x2, mods[1], norm_g[1, 1][None],
                   w_up[1], w_down[1], final_g[None], rpb)

    states = st_fox + (pstate[:, 1:, :][None],) + st_sb
    return y.reshape(nb, rpb, d), states


def kernel(x_prompt, x_sample, c_prompt, c_sample, cache_fox_k, cache_fox_v, cache_fox_logf, state_pool, cache_sb_k, cache_sb_v, w_ada, b_ada, norm_g, w_in_ab, b_forget, w_pool, pool_scale, w_out_ab, w_in_sb, w_out_sb, w_up, w_down, final_g):
    b, l, d = x_prompt.shape
    db, ds, _ = x_sample.shape
    a_w = H_FOX * DH

    c_all = jnp.concatenate([c_prompt, c_sample, jnp.zeros((ADA_ROWS - b - db, d), F32)], axis=0)
    mod = _ada_call(c_all, w_ada, b_ada)
    mods_p = [mod[i, :b][:, None, :] for i in range(mod.shape[0])]
    mods_s = [jnp.repeat(mod[i, b:b + db], ds, axis=0) for i in range(mod.shape[0])]

    w_in = w_in_ab[0]
    gate_pad = ((0, GATE_ROWS - H_FOX), (0, 0))
    ab_wts = (w_in[:, :a_w].astype(BF16),
              w_in[:, a_w:2 * a_w].T.astype(BF16),
              w_in[:, 2 * a_w:3 * a_w].T.astype(BF16),
              w_in[:, 3 * a_w + H_FOX:].astype(BF16),
              jnp.pad(w_in[:, 3 * a_w:3 * a_w + H_FOX].T, gate_pad).astype(BF16),
              jnp.pad(b_forget[0][:, None], gate_pad),
              w_pool[0].astype(BF16), pool_scale[0][None])
    w_sb = w_in_sb[0]
    sb_wts = (w_sb[:, :d].astype(BF16), w_sb[:, d:2 * d].T.astype(BF16), w_sb[:, 2 * d:].T.astype(BF16))
    wts = (norm_g, ab_wts, w_out_ab[0].astype(BF16), sb_wts, w_out_sb[0].astype(BF16),
           w_up.astype(BF16), w_down.astype(BF16), final_g)

    y_p, st_p = _trunk(True, x_prompt, mods_p, None, wts)
    past = (cache_fox_k[0], cache_fox_v[0], cache_fox_logf[0], state_pool[0],
            cache_sb_k[0], cache_sb_v[0])
    y_s, st_s = _trunk(False, x_sample, mods_s, past, wts)
    fk, fv, fl, pool, sk, sv = st_p
    fk2, fv2, fl2, pool2, sk2, sv2 = st_s
    return (y_p, y_s, fk, fv, fl, pool, sk, sv, fk2, fv2, fl2, pool2, sk2, sv2)
```

```python
import functools

import jax
import jax.numpy as jnp
from jax import lax
from jax.experimental import pallas as pl
from jax.experimental.pallas import tpu as pltpu

F32 = jnp.float32
BF16 = jnp.bfloat16

DH = 64
H_FOX = 8
EPS = 1e-6
POOL_WINDOWS = (2, 4, 8, 16)
POOL_GC = 128
HIST_ROWS = 16
NEG_BIG = -1e30
LOG2E = 1.4426950408889634
LANES = 128
GATE_ROWS = 16
AUG_ROWS = 16
OFF_SLOT = 12
VMEM_LIMIT = 56 * 1024 * 1024

UNDERFLOW_EXP = 88.0
NORM_MARGIN = 1.01

T_ATT = 256
NQ_ATT = 4
TK_DEC = 512
TK_DEC_FOX = 1024
TM_ROWS = 512
ADA_ROWS = 32
ADA_TN = 1536


def _cparams(sem):
    return pltpu.CompilerParams(dimension_semantics=sem, vmem_limit_bytes=VMEM_LIMIT)


def _rms_mod(x, g, sc, sh):
    y = x * lax.rsqrt(jnp.mean(x * x, axis=-1, keepdims=True) + EPS)
    return (y * g) * (1.0 + sc) + sh


def _softplus(z):
    return jnp.maximum(z, 0.0) + jnp.log(1.0 + jnp.exp2(jnp.abs(z) * (-LOG2E)))


def _log_sigmoid(x):
    return jnp.minimum(x, 0.0) - jnp.log(1.0 + jnp.exp(-jnp.abs(x)))


def _split3(x):
    hi = x.astype(BF16)
    r = x - hi.astype(F32)
    mid = r.astype(BF16)
    lo = (r - mid.astype(F32)).astype(BF16)
    return hi, mid, lo


def _dot(a, b):
    return jnp.dot(a, b, preferred_element_type=F32)


def _dot_nt(a, b):
    return lax.dot_general(a, b, (((1,), (1,)), ((), ())), preferred_element_type=F32)


def _dot3(x, m):
    hi, mid, lo = _split3(x)
    return _dot(hi, m) + _dot(mid, m) + _dot(lo, m)


def _iota(shape, dim):
    return lax.broadcasted_iota(jnp.int32, shape, dim)


def _log2(n):
    assert n & (n - 1) == 0, n
    return n.bit_length() - 1


def _ada_kernel(c_ref, w_ref, b_ref, o_ref):
    c = c_ref[...]
    cond = (c / (1.0 + jnp.exp(-c))).astype(BF16)
    o_ref[...] = _dot(cond, w_ref[...].astype(BF16)) + b_ref[...]


def _ada_call(c_all, w_ada, b_ada):
    depth, d, n = w_ada.shape
    return pl.pallas_call(
        _ada_kernel,
        out_shape=jax.ShapeDtypeStruct((depth, ADA_ROWS, n), F32),
        grid=(depth, n // ADA_TN),
        in_specs=[
            pl.BlockSpec((ADA_ROWS, d), lambda i, j: (0, 0)),
            pl.BlockSpec((None, d, ADA_TN), lambda i, j: (i, 0, j)),
            pl.BlockSpec((None, 1, ADA_TN), lambda i, j: (i, 0, j)),
        ],
        out_specs=pl.BlockSpec((None, ADA_ROWS, ADA_TN), lambda i, j: (i, 0, j)),
        compiler_params=_cparams(("arbitrary", "arbitrary")),
        name="ada_mod",
    )(c_all, w_ada, b_ada.reshape(depth, 1, n))


def _mod_spec(prompt, tm, d, chunk, tiles_per_batch):
    if prompt:
        return pl.BlockSpec((None, 1, d), lambda i: (i // tiles_per_batch, 0, chunk))
    return pl.BlockSpec((tm, d), lambda i: (i, chunk))


def _pool_branch(u, ext_ref, pos, ng, rg, wp_ref, ps_ref, pooled_ref, pst_ref):
    tm, a_w = u.shape
    u3 = u.reshape(ng, rg, a_w)
    ext_ref[:, HIST_ROWS:HIST_ROWS + rg, :] = u3
    pst_ref[...] = u3[:, rg - HIST_ROWS:rg, :]
    for g, w in enumerate(POOL_WINDOWS):
        cs = slice(g * POOL_GC, (g + 1) * POOL_GC)
        acc = u3[:, :, cs]
        for kk in range(1, w):
            acc = acc + ext_ref[:, pl.ds(HIST_ROWS - kk, rg), cs]
        cnt = jnp.minimum(pos + 1, w).astype(F32)
        pooled = acc.reshape(tm, POOL_GC) / cnt - u[:, cs]
        pg = _dot(pooled.astype(BF16), wp_ref[g]) * ps_ref[:, cs]
        pooled_ref[:, cs] = pg.astype(pooled_ref.dtype)


def _ab_in_prompt_kernel(tm, tiles_per_batch, x_ref, sh_ref, sc_ref, g_ref, wq_ref, wkt_ref, wvt_ref,
                         wu_ref, wft_ref, bf_ref, wp_ref, ps_ref,
                         q_ref, ktb_ref, vtb_ref, kt_ref, vt_ref, lft_ref, fct_ref, kaug_ref, pooled_ref, pst_ref,
                         ext_ref, carry_ref):
    h = _rms_mod(x_ref[...], g_ref[...], sc_ref[...], sh_ref[...]).astype(BF16)
    q_ref[...] = (_dot(h, wq_ref[...]) * (DH ** -0.5)).astype(q_ref.dtype)
    kt = _dot_nt(wkt_ref[...], h)
    kt_ref[...] = kt
    ktb_ref[...] = kt.astype(BF16)
    vt = _dot_nt(wvt_ref[...], h)
    vt_ref[...] = vt
    vtb_ref[...] = vt.astype(BF16)

    tile_in_batch = pl.program_id(0) % tiles_per_batch
    first = tile_in_batch == 0
    logf = _log_sigmoid(_dot_nt(wft_ref[...], h) + bf_ref[...])
    lft_ref[...] = logf[0:H_FOX, :]
    tri = (_iota((tm, tm), 0) <= _iota((tm, tm), 1)).astype(BF16)
    fcum = _dot3(logf, tri) + jnp.where(first, 0.0, carry_ref[...])
    carry_ref[...] = fcum[:, tm - 1:tm]
    fct_ref[...] = fcum[0:H_FOX, :]

    n_aug = (H_FOX // 2) * AUG_ROWS
    slot = _iota((n_aug, GATE_ROWS), 0) & (AUG_ROWS - 1)
    head0 = (_iota((n_aug, GATE_ROWS), 0) >> _log2(AUG_ROWS)) * 2
    col = _iota((n_aug, GATE_ROWS), 1)
    kaug = None
    for part, piece in enumerate(_split3(fcum)):
        sel = jnp.logical_or(jnp.logical_and(slot == 3 + part, col == head0),
                             jnp.logical_and(slot == 9 + part, col == head0 + 1))
        term = _dot(jnp.where(sel, -1.0, 0.0).astype(BF16), piece)
        kaug = term if kaug is None else kaug + term
    slot1 = _iota((n_aug, 1), 0) & (AUG_ROWS - 1)
    ones = jnp.logical_or(slot1 < 3, jnp.logical_and(slot1 >= 6, slot1 < 9))
    kaug_ref[...] = (kaug + jnp.where(ones, 1.0, 0.0)).astype(BF16)

    @pl.when(first)
    def _():
        ext_ref[:, 0:HIST_ROWS, :] = jnp.zeros((1, HIST_ROWS, ext_ref.shape[2]), F32)

    @pl.when(jnp.logical_not(first))
    def _():
        ext_ref[:, 0:HIST_ROWS, :] = ext_ref[:, tm:tm + HIST_ROWS, :]

    pos = tile_in_batch * tm + _iota((tm, 1), 0)
    _pool_branch(_dot(h, wu_ref[...]), ext_ref, pos, 1, tm, wp_ref, ps_ref, pooled_ref, pst_ref)


def _ab_in_sample_kernel(ng, rg, p0, x_ref, sh_ref, sc_ref, g_ref, wq_ref, wkt_ref, wvt_ref,
                         wu_ref, wft_ref, bf_ref, wp_ref, ps_ref, hist_ref,
                         q_ref, kb_ref, vb_ref, k32_ref, v32_ref, lft_ref, pooled_ref, pst_ref, ext_ref):
    tm = ng * rg
    h = _rms_mod(x_ref[...], g_ref[...], sc_ref[...], sh_ref[...]).astype(BF16)
    q_ref[...] = (_dot(h, wq_ref[...]) * (DH ** -0.5)).astype(q_ref.dtype)
    k = _dot_nt(h, wkt_ref[...])
    k32_ref[...] = k
    kb_ref[...] = k.astype(BF16)
    v = _dot_nt(h, wvt_ref[...])
    v32_ref[...] = v
    vb_ref[...] = v.astype(BF16)
    logf = _log_sigmoid(_dot_nt(wft_ref[...], h) + bf_ref[...])
    lft_ref[...] = logf[0:H_FOX, :]
    ext_ref[:, 0:HIST_ROWS, :] = hist_ref[...]
    pos = p0 + (_iota((tm, 1), 0) & (rg - 1))
    _pool_branch(_dot(h, wu_ref[...]), ext_ref, pos, ng, rg, wp_ref, ps_ref, pooled_ref, pst_ref)


def _ab_in_call(prompt, x2, mod_arr, g, wts, hist, rows_per_batch, p0):
    wq, wkt, wvt, wu, wft, bf, wp, ps = wts
    m, d = x2.shape
    a_w = H_FOX * DH
    nb = m // rows_per_batch
    tm = TM_ROWS if prompt else m
    tpb = rows_per_batch // tm if prompt else 1
    row = lambda i: (i, 0)
    const2 = lambda i: (0, 0)
    in_specs = [pl.BlockSpec((tm, d), row), _mod_spec(prompt, tm, d, 0, tpb), _mod_spec(prompt, tm, d, 1, tpb),
                pl.BlockSpec((1, d), const2)]
    in_specs += [pl.BlockSpec(w.shape, const2) for w in (wq, wkt, wvt, wu, wft, bf)]
    in_specs += [pl.BlockSpec(wp.shape, lambda i: (0, 0, 0)), pl.BlockSpec(ps.shape, const2)]
    args = [x2, mod_arr, mod_arr, g, wq, wkt, wvt, wu, wft, bf, wp, ps]
    if prompt:
        tcol = lambda i: (i // tpb, 0, i % tpb)
        out_shape = [jax.ShapeDtypeStruct((m, a_w), BF16),
                     jax.ShapeDtypeStruct((nb, a_w, rows_per_batch), BF16),
                     jax.ShapeDtypeStruct((nb, a_w, rows_per_batch), BF16),
                     jax.ShapeDtypeStruct((nb, a_w, rows_per_batch), F32),
                     jax.ShapeDtypeStruct((nb, a_w, rows_per_batch), F32),
                     jax.ShapeDtypeStruct((nb, H_FOX, rows_per_batch), F32),
                     jax.ShapeDtypeStruct((nb, H_FOX, rows_per_batch), F32),
                     jax.ShapeDtypeStruct((nb, (H_FOX // 2) * AUG_ROWS, rows_per_batch), BF16),
                     jax.ShapeDtypeStruct((m, a_w), BF16),
                     jax.ShapeDtypeStruct((nb, HIST_ROWS, a_w), F32)]
        out_specs = ([pl.BlockSpec((tm, a_w), row)] + [pl.BlockSpec((None, a_w, tm), tcol)] * 4
                     + [pl.BlockSpec((None, H_FOX, tm), tcol)] * 2
                     + [pl.BlockSpec((None, (H_FOX // 2) * AUG_ROWS, tm), tcol)]
                     + [pl.BlockSpec((tm, a_w), row),
                        pl.BlockSpec((1, HIST_ROWS, a_w), lambda i: (i // tpb, 0, 0))])
        scratch = [pltpu.VMEM((1, HIST_ROWS + tm, a_w), F32), pltpu.VMEM((GATE_ROWS, 1), F32)]
        body = functools.partial(_ab_in_prompt_kernel, tm, tpb)
    else:
        in_specs.append(pl.BlockSpec(hist.shape, lambda i: (0, 0, 0)))
        args.append(hist)
        out_shape = [jax.ShapeDtypeStruct((m, a_w), BF16), jax.ShapeDtypeStruct((m, a_w), BF16),
                     jax.ShapeDtypeStruct((m, a_w), BF16), jax.ShapeDtypeStruct((m, a_w), F32),
                     jax.ShapeDtypeStruct((m, a_w), F32), jax.ShapeDtypeStruct((H_FOX, m), F32),
                     jax.ShapeDtypeStruct((m, a_w), BF16), jax.ShapeDtypeStruct((nb, HIST_ROWS, a_w), F32)]
        out_specs = ([pl.BlockSpec((tm, a_w), row)] * 5 + [pl.BlockSpec((H_FOX, tm), const2)]
                     + [pl.BlockSpec((tm, a_w), row), pl.BlockSpec((nb, HIST_ROWS, a_w), lambda i: (0, 0, 0))])
        scratch = [pltpu.VMEM((nb, HIST_ROWS + rows_per_batch, a_w), F32)]
        body = functools.partial(_ab_in_sample_kernel, nb, rows_per_batch, p0)
    return pl.pallas_call(
        body, out_shape=out_shape, grid=(m // tm,), in_specs=in_specs, out_specs=out_specs,
        scratch_shapes=scratch,
        compiler_params=_cparams(("arbitrary",)),
        name="ab_in_prompt" if prompt else "ab_in_sample",
    )(*args)


def _head_split(q2):
    lane = _iota((1, LANES), 1)
    zero = jnp.zeros_like(q2)
    return jnp.where(lane < DH, q2, zero), jnp.where(lane >= DH, q2, zero)


def _pair_sel():
    return (_iota((8, LANES), 0) == (_iota((8, LANES), 1) >> _log2(DH))).astype(BF16)


def _fox_kernel(q_ref, kt_ref, vt_ref, ka_ref, fq_ref, ft_ref, o_ref,
                m_ref, acc_ref, kn_ref):
    t, nq = T_ATT, NQ_ATT
    qg = pl.program_id(2)
    lane = _iota((1, LANES), 1)
    lane2 = _iota((1, 2), 1)

    @pl.when(qg == 0)
    def _():
        chunk = 4 * t

        def body(i, mx):
            ks = pl.multiple_of(i * chunk, chunk)
            kf = kt_ref[:, pl.ds(ks, chunk)].astype(F32)
            n2 = _dot(_pair_sel(), (kf * kf).astype(BF16))
            return jnp.maximum(mx, jnp.max(n2, axis=1, keepdims=True))

        mx = lax.fori_loop(0, kt_ref.shape[1] // chunk, body, jnp.zeros((8, 1), F32))
        kn_ref[...] = jnp.sqrt(mx)

    lane_a = _iota((1, AUG_ROWS), 1)
    row_v = _iota((LANES, 1), 0)
    qcat, bound0 = [], []
    for s in range(nq):
        q2 = q_ref[s * t:(s + 1) * t, :]
        fq = fq_ref[s * t:(s + 1) * t, :]
        per_head = []
        for j, qh in enumerate(_head_split(q2)):
            hi, mid, lo = (x.astype(F32) for x in _split3(fq[:, j:j + 1]))
            base = 6 * j
            qa = jnp.where(lane_a == base, hi, jnp.where(lane_a == base + 1, mid, jnp.where(
                lane_a == base + 2, lo,
                jnp.where(jnp.logical_and(lane_a >= base + 3, lane_a < base + 6), 1.0, 0.0))))
            qa = jnp.where(lane_a == OFF_SLOT, NEG_BIG, qa)
            per_head.append(jnp.concatenate([qh, qa.astype(BF16)], axis=1))
        qcat.append(per_head)
        qf = q2.astype(F32)
        qn2 = _dot_nt((qf * qf).astype(BF16), _pair_sel())
        qn = jnp.sqrt(jnp.max(qn2, axis=0, keepdims=True))
        qk = jnp.where(lane2 == 0, qn[:, 0:1] * kn_ref[0:1, :], qn[:, 1:2] * kn_ref[1:2, :])
        bound0.append(qk * NORM_MARGIN + fq[0:1, :])
    chains = [(s, j) for s in range(nq) for j in range(2)]
    rep = lambda m: jnp.concatenate([m] * (t // LANES), axis=1)

    def front(n, diag):
        kbs = [qg * nq + s - n for s in range(nq)]
        kss = [pl.multiple_of((kb if diag else jnp.maximum(kb, 0)) * t, t) for kb in kbs]
        causal = _iota((t, t), 1) <= _iota((t, t), 0)
        kcat = []
        for s in range(nq):
            ka = ka_ref[:, pl.ds(kss[s], t)]
            if not diag:
                off = jnp.where(kbs[s] >= 0, 0.0, 1.0).astype(BF16)
                ka = jnp.where(_iota((AUG_ROWS, 1), 0) == OFF_SLOT, off, ka)
            kcat.append(jnp.concatenate([kt_ref[:, pl.ds(kss[s], t)], ka], axis=0))
        sc = []
        for s, j in chains:
            x = _dot(qcat[s][j], kcat[s])
            sc.append(jnp.where(causal, x, NEG_BIG) if diag else x)
        vcat = []
        for s, j in chains:
            v2 = vt_ref[:, pl.ds(kss[s], t)]
            mine = (row_v < DH) if j == 0 else (row_v >= DH)
            vcat.append(jnp.where(mine, v2, jnp.ones_like(v2)))
        return sc, vcat

    (sc0, v0), (sc1, v1) = front(0, True), front(1, False)
    m01 = [jnp.broadcast_to(jnp.maximum(jnp.max(x0, axis=1, keepdims=True), jnp.max(x1, axis=1, keepdims=True)),
                            (t, LANES)) for x0, x1 in zip(sc0, sc1)]
    pv0 = [_dot_nt(jnp.exp(x - rep(m)).astype(BF16), vv) for x, m, vv in zip(sc0, m01, v0)]
    pv1 = [_dot_nt(jnp.exp(x - rep(m)).astype(BF16), vv) for x, m, vv in zip(sc1, m01, v1)]
    for i, (s, j) in enumerate(chains):
        acc_ref[2 * s + j] = pv0[i] + pv1[i]
        m_ref[2 * s + j] = m01[i]

    def step(n):
        sc, vcat = front(n, False)
        m_prev = [m_ref[2 * s + j] for s, j in chains]
        m_new = [jnp.maximum(mp, jnp.max(x, axis=1, keepdims=True)) for mp, x in zip(m_prev, sc)]
        pv = [_dot_nt(jnp.exp(x - rep(mn)).astype(BF16), vv) for x, mn, vv in zip(sc, m_new, vcat)]
        for i, (s, j) in enumerate(chains):
            idx = 2 * s + j
            acc_ref[idx] = jnp.exp(m_prev[i] - m_new[i]) * acc_ref[idx] + pv[i]
            m_ref[idx] = m_new[i]

    def more(n):
        go = False
        for s in range(nq):
            kb = qg * nq + s - n
            ft = ft_ref[pl.ds(jnp.maximum(kb, 0), 1), :]
            m_min = jnp.where(lane2 == 0, jnp.min(m_ref[2 * s], axis=(0, 1), keepdims=True),
                              jnp.min(m_ref[2 * s + 1], axis=(0, 1), keepdims=True))
            top = jnp.max(bound0[s] - ft - m_min)
            go = jnp.logical_or(go, jnp.logical_and(kb >= 0, top > -UNDERFLOW_EXP))
        return go

    def body(c):
        n, _ = c
        step(n)
        return n + 1, more(n + 1)

    lax.while_loop(lambda c: c[1], body, (jnp.int32(2), more(2)))
    for s in range(nq):
        a0, a1 = acc_ref[2 * s], acc_ref[2 * s + 1]
        o = jnp.where(lane < DH, a0 / pltpu.roll(a0, DH, 1), a1 / pltpu.roll(a1, DH, 1))
        o_ref[s * t:(s + 1) * t, :] = o.astype(o_ref.dtype)


def _fox_call(q, ktb, vtb, kaug, fq_arr, ft_arr):
    b, l, a_w = q.shape
    hp = a_w // LANES
    t, nq = T_ATT, NQ_ATT
    tq = t * nq
    return pl.pallas_call(
        _fox_kernel,
        out_shape=jax.ShapeDtypeStruct((b, l, a_w), BF16),
        grid=(b, hp, l // tq),
        in_specs=[
            pl.BlockSpec((None, tq, LANES), lambda bi, h, qi: (bi, qi, h)),
            pl.BlockSpec((None, LANES, l), lambda bi, h, qi: (bi, h, 0)),
            pl.BlockSpec((None, LANES, l), lambda bi, h, qi: (bi, h, 0)),
            pl.BlockSpec((None, AUG_ROWS, l), lambda bi, h, qi: (bi, h, 0)),
            pl.BlockSpec((None, None, tq, 2), lambda bi, h, qi: (bi, h, qi, 0)),
            pl.BlockSpec((None, None, l // t, 2), lambda bi, h, qi: (bi, h, 0, 0)),
        ],
        out_specs=pl.BlockSpec((None, tq, LANES), lambda bi, h, qi: (bi, qi, h)),
        scratch_shapes=[pltpu.VMEM((2 * nq, t, LANES), F32), pltpu.VMEM((2 * nq, t, LANES), F32),
                        pltpu.VMEM((8, 1), F32)],
        compiler_params=_cparams(("arbitrary", "arbitrary", "arbitrary")),
        name="fox_prompt",
    )(q, ktb, vtb, kaug, fq_arr, ft_arr)


def _sb_kernel(q_ref, kt_ref, vt_ref, o_ref, carry_ref, acc_ref):
    t, nq = T_ATT, NQ_ATT
    qg = pl.program_id(2)
    lane = _iota((1, LANES), 1)
    qh = [_head_split(q_ref[s * t:(s + 1) * t, :]) for s in range(nq)]
    chains = [(s, j) for s in range(nq) for j in range(2)]
    valid = _iota((t, t), 1) < _iota((t, t), 0)

    def front(n, diag):
        tri = (_iota((t, t), 0) > _iota((t, t), 1)).astype(BF16)
        kbs = [qg * nq + s - n for s in range(nq)]
        kss = [pl.multiple_of((kb if diag else jnp.maximum(kb, 0)) * t, t) for kb in kbs]
        z = [_dot(qh[s][j], kt_ref[:, pl.ds(kss[s], t)]) for s, j in chains]
        sp = [_softplus(zz) for zz in z]
        spm = [jnp.where(valid, x, 0.0) for x in sp] if diag else sp
        later = [_dot(x.astype(BF16), tri) for x in spm]
        vs = []
        for s in range(nq):
            v2 = vt_ref[:, pl.ds(kss[s], t)]
            if not diag:
                v2 = jnp.where(kbs[s] >= 0, v2, jnp.zeros_like(v2))
            vs.append(v2)
        ls = [zz - x for zz, x in zip(z, sp)]
        return ls, spm, later, vs

    def back(fr, diag, carry):
        ls, spm, later, vs = fr
        pv, new_carry = [], []
        for i, (s, j) in enumerate(chains):
            e = ls[i] - later[i]
            a = jnp.exp(e if carry is None else e - carry[i])
            if diag:
                a = jnp.where(valid, a, 0.0)
            pv.append(_dot_nt(a.astype(BF16), vs[s]))
            rs = jnp.sum(spm[i], axis=1, keepdims=True)
            new_carry.append(rs if carry is None else carry[i] + rs)
        return pv, new_carry

    f0, f1 = front(0, True), front(1, False)
    pv0, c0 = back(f0, True, None)
    pv1, c1 = back(f1, False, c0)
    for i, (s, j) in enumerate(chains):
        acc_ref[2 * s + j] = pv0[i] + pv1[i]
        carry_ref[2 * s + j] = c1[i]

    def more(n):
        in_range = n <= qg * nq + nq - 1
        return jnp.logical_and(in_range, jnp.min(carry_ref[...]) < UNDERFLOW_EXP)

    def body(c):
        n, _ = c
        pv, carry = back(front(n, False), False, [carry_ref[2 * s + j] for s, j in chains])
        for i, (s, j) in enumerate(chains):
            acc_ref[2 * s + j] = acc_ref[2 * s + j] + pv[i]
            carry_ref[2 * s + j] = carry[i]
        return n + 1, more(n + 1)

    lax.while_loop(lambda c: c[1], body, (jnp.int32(2), more(2)))
    for s in range(nq):
        o = jnp.where(lane < DH, acc_ref[2 * s], acc_ref[2 * s + 1])
        o_ref[s * t:(s + 1) * t, :] = o.astype(o_ref.dtype)


def _sb_call(q, ktb, vtb):
    b, l, d = q.shape
    hp = d // LANES
    t, nq = T_ATT, NQ_ATT
    tq = t * nq
    return pl.pallas_call(
        _sb_kernel,
        out_shape=jax.ShapeDtypeStruct((b, l, d), BF16),
        grid=(b, hp, l // tq),
        in_specs=[
            pl.BlockSpec((None, tq, LANES), lambda bi, h, qi: (bi, qi, h)),
            pl.BlockSpec((None, LANES, l), lambda bi, h, qi: (bi, h, 0)),
            pl.BlockSpec((None, LANES, l), lambda bi, h, qi: (bi, h, 0)),
        ],
        out_specs=pl.BlockSpec((None, tq, LANES), lambda bi, h, qi: (bi, qi, h)),
        scratch_shapes=[pltpu.VMEM((2 * nq, t, 1), F32), pltpu.VMEM((2 * nq, t, LANES), F32)],
        compiler_params=_cparams(("arbitrary", "arbitrary", "arbitrary")),
        name="sb_prompt",
    )(q, ktb, vtb)


def _rep_matrix(rows, n_heads, ds):
    return ((_iota((rows, ds), 0) >> _log2(n_heads)) == _iota((rows, ds), 1)).astype(BF16)


def _head_mask(rows, n_heads, width):
    return (_iota((rows, width), 0) & (n_heads - 1)) == (_iota((rows, width), 1) >> _log2(DH))


def _block_diag_q(q, n_heads):
    ds, width = q.shape
    rows = ds * n_heads
    qrep = _dot(_rep_matrix(rows, n_heads, ds), q)
    return jnp.where(_head_mask(rows, n_heads, width), qrep, 0.0).astype(BF16)


def _collect_heads(acc, n_heads, ds):
    rows, width = acc.shape
    om = jnp.where(_head_mask(rows, n_heads, width), acc, 0.0).astype(BF16)
    rep_t = ((_iota((ds, rows), 1) >> _log2(n_heads)) == _iota((ds, rows), 0)).astype(BF16)
    return _dot(rep_t, om)


def _fox_dec_kernel(n_heads, ds, q_ref, kn_ref, vn_ref, lfn_ref, kc_ref, vc_ref, lfc_ref,
                    o_ref, qbd_ref, m_ref, l_ref, acc_ref, cf_ref, tri_ref):
    rows = ds * n_heads
    step_i = pl.program_id(1)
    n_steps = pl.num_programs(1)

    def update(s, pv_fn):
        m_prev = m_ref[...]
        m_new = jnp.maximum(m_prev, jnp.max(s, axis=1, keepdims=True))
        alpha = jnp.exp(m_prev - m_new)
        p = jnp.exp(s - m_new)
        l_ref[...] = alpha * l_ref[...] + jnp.sum(p, axis=1, keepdims=True)
        acc_ref[...] = alpha * acc_ref[...] + pv_fn(p.astype(BF16))
        m_ref[...] = m_new

    @pl.when(step_i == 0)
    def _():
        qbd_ref[...] = _block_diag_q(q_ref[...], n_heads)
        m_ref[...] = jnp.full(m_ref.shape, NEG_BIG, F32)
        l_ref[...] = jnp.zeros(l_ref.shape, F32)
        acc_ref[...] = jnp.zeros(acc_ref.shape, F32)
        a = jnp.concatenate([lfn_ref[...]] * ds, axis=0)
        ri = _iota((rows, ds), 0) >> _log2(n_heads)
        cj = _iota((rows, ds), 1)
        am = jnp.where(cj <= ri, a, 0.0)
        tri = (_iota((ds, ds), 0) > _iota((ds, ds), 1)).astype(BF16)
        s = _dot_nt(qbd_ref[...], kn_ref[...]) + _dot3(am, tri)
        s = jnp.where(cj <= ri, s, NEG_BIG)
        update(s, lambda p: _dot(p, vn_ref[...]))
        cf_ref[...] = jnp.sum(am, axis=1, keepdims=True)

    @pl.when(step_i > 0)
    def _():
        tk = kc_ref.shape[1]

        @pl.when(jnp.logical_and(pl.program_id(0) == 0, step_i == 1))
        def _():
            tri_ref[...] = (_iota((tk, tk), 0) > _iota((tk, tk), 1)).astype(BF16)

        lf2 = jnp.concatenate([lfc_ref[...]] * 2, axis=0)
        reps = rows // (2 * n_heads)
        later = jnp.concatenate([_dot3(lf2, tri_ref[...])] * reps, axis=0)
        cf = cf_ref[...]
        s = _dot(qbd_ref[...], kc_ref[...].astype(BF16)) + (later + cf)
        update(s, lambda p: _dot_nt(p, vc_ref[...].astype(BF16)))
        cf_ref[...] = cf + jnp.concatenate([jnp.sum(lf2, axis=1, keepdims=True)] * reps, axis=0)

    @pl.when(step_i == n_steps - 1)
    def _():
        o_ref[...] = _collect_heads(acc_ref[...] / l_ref[...], n_heads, ds).astype(o_ref.dtype)


def _fox_dec_call(q, kn, vn, kct, vct, lfn_t, lfc_t):
    db, ds, width = q.shape
    n_heads = width // DH
    rows = ds * n_heads
    tk = min(TK_DEC_FOX, kct.shape[2])
    nkb = kct.shape[2] // tk
    cache_blk = lambda s: jnp.minimum(nkb - s, nkb - 1)
    new_idx = lambda b, s: (b, 0, 0)
    cache_idx = lambda b, s: (b, 0, cache_blk(s))
    in_specs = ([pl.BlockSpec((None, ds, width), new_idx)] * 3
                + [pl.BlockSpec((None, n_heads, ds), new_idx)]
                + [pl.BlockSpec((None, width, tk), cache_idx)] * 2
                + [pl.BlockSpec((None, n_heads, tk), cache_idx)])
    scratch = [pltpu.VMEM((rows, width), BF16), pltpu.VMEM((rows, 1), F32), pltpu.VMEM((rows, 1), F32),
               pltpu.VMEM((rows, width), F32), pltpu.VMEM((rows, 1), F32), pltpu.VMEM((tk, tk), BF16)]
    return pl.pallas_call(
        functools.partial(_fox_dec_kernel, n_heads, ds),
        out_shape=jax.ShapeDtypeStruct((db, ds, width), BF16),
        grid=(db, nkb + 1),
        in_specs=in_specs,
        out_specs=pl.BlockSpec((None, ds, width), new_idx),
        scratch_shapes=scratch,
        compiler_params=_cparams(("arbitrary", "arbitrary")),
        name="fox_decode",
    )(q, kn, vn, lfn_t, kct, vct, lfc_t)


def _sb_dec_tile(qbd, kc_ref, vc_ref, carry):
    tk = kc_ref.shape[1]
    z = _dot(qbd, kc_ref[...].astype(BF16))
    sp = _softplus(z)
    tri = (_iota((tk, tk), 0) > _iota((tk, tk), 1)).astype(BF16)
    later = _dot(sp.astype(BF16), tri)
    a = jnp.exp((z - sp) - later - carry)
    return _dot_nt(a.astype(BF16), vc_ref[...].astype(BF16)), carry + jnp.sum(sp, axis=1, keepdims=True)


def _sb_dec_first_kernel(n_heads, ds, q_ref, kn_ref, vn_ref, kc_ref, vc_ref,
                         o_ref, carry_out_ref, cmin_ref):
    rows = ds * n_heads
    qbd = _block_diag_q(q_ref[...], n_heads)
    ri = _iota((rows, ds), 0) >> _log2(n_heads)
    cj = _iota((rows, ds), 1)
    valid = cj < ri
    z = _dot_nt(qbd, kn_ref[...])
    sp = _softplus(z)
    spm = jnp.where(valid, sp, 0.0)
    tri = (_iota((ds, ds), 0) > _iota((ds, ds), 1)).astype(BF16)
    later = _dot(spm.astype(BF16), tri)
    a = jnp.where(valid, jnp.exp((z - sp) - later), 0.0)
    acc = _dot(a.astype(BF16), vn_ref[...])
    carry = jnp.sum(spm, axis=1, keepdims=True)
    part, carry = _sb_dec_tile(qbd, kc_ref, vc_ref, carry)
    o_ref[...] = _collect_heads(acc + part, n_heads, ds).astype(o_ref.dtype)
    carry_out_ref[...] = carry
    cmin_ref[...] = jnp.broadcast_to(jnp.min(carry, axis=0, keepdims=True), cmin_ref.shape)


def _sb_dec_rest_kernel(n_heads, ds, need_ref, q_ref, carry_in_ref, o1_ref, kc_ref, vc_ref,
                        o_ref, qbd_ref, acc_ref, carry_ref):
    step_i = pl.program_id(1)
    last = pl.num_programs(1) - 1
    needed = need_ref[pl.program_id(0)] != 0

    @pl.when(needed)
    def _():
        @pl.when(step_i == 0)
        def _():
            qbd_ref[...] = _block_diag_q(q_ref[...], n_heads)
            acc_ref[...] = jnp.zeros(acc_ref.shape, F32)
            carry_ref[...] = carry_in_ref[...]

        @pl.when(jnp.min(carry_ref[...]) < UNDERFLOW_EXP)
        def _():
            part, carry = _sb_dec_tile(qbd_ref[...], kc_ref, vc_ref, carry_ref[...])
            acc_ref[...] = acc_ref[...] + part
            carry_ref[...] = carry

        @pl.when(step_i == last)
        def _():
            o = o1_ref[...].astype(F32) + _collect_heads(acc_ref[...], n_heads, ds)
            o_ref[...] = o.astype(o_ref.dtype)

    @pl.when(jnp.logical_and(jnp.logical_not(needed), step_i == last))
    def _():
        o_ref[...] = o1_ref[...]


def _sb_dec_call(q, kn, vn, kct, vct):
    db, ds, width = q.shape
    n_heads = width // DH
    rows = ds * n_heads
    nkb = kct.shape[2] // TK_DEC
    bidx = lambda b: (b, 0, 0)
    tile = (None, width, TK_DEC)
    o1, carry, cmin = pl.pallas_call(
        functools.partial(_sb_dec_first_kernel, n_heads, ds),
        out_shape=[jax.ShapeDtypeStruct((db, ds, width), BF16),
                   jax.ShapeDtypeStruct((db, rows, 1), F32),
                   jax.ShapeDtypeStruct((db, 1, LANES), F32)],
        grid=(db,),
        in_specs=[pl.BlockSpec((None, ds, width), bidx)] * 3
        + [pl.BlockSpec(tile, lambda b: (b, 0, nkb - 1))] * 2,
        out_specs=[pl.BlockSpec((None, ds, width), bidx), pl.BlockSpec((None, rows, 1), bidx),
                   pl.BlockSpec((None, 1, LANES), bidx)],
        compiler_params=_cparams(("arbitrary",)),
        name="sb_decode_first",
    )(q, kn, vn, kct, vct)
    if nkb == 1:
        return o1
    need = (cmin[:, 0, 0] < UNDERFLOW_EXP).astype(jnp.int32)
    bidx2 = lambda b, s, need_ref: (b, 0, 0)

    def cache_idx(b, s, need_ref):
        on = need_ref[b] != 0
        return (jnp.where(on, b, 0), 0, jnp.where(on, nkb - 2 - s, 0))

    grid_spec = pltpu.PrefetchScalarGridSpec(
        num_scalar_prefetch=1,
        grid=(db, nkb - 1),
        in_specs=[pl.BlockSpec((None, ds, width), bidx2), pl.BlockSpec((None, rows, 1), bidx2),
                  pl.BlockSpec((None, ds, width), bidx2),
                  pl.BlockSpec(tile, cache_idx), pl.BlockSpec(tile, cache_idx)],
        out_specs=pl.BlockSpec((None, ds, width), bidx2),
        scratch_shapes=[pltpu.VMEM((rows, width), BF16), pltpu.VMEM((rows, width), F32),
                        pltpu.VMEM((rows, 1), F32)],
    )
    return pl.pallas_call(
        functools.partial(_sb_dec_rest_kernel, n_heads, ds),
        out_shape=jax.ShapeDtypeStruct((db, ds, width), BF16),
        grid_spec=grid_spec,
        compiler_params=_cparams(("arbitrary", "arbitrary")),
        name="sb_decode_rest",
    )(need, q, carry, o1, kct, vct)


def _post_kernel(n_act, final, f_chunk, *refs):
    acts = refs[:n_act]
    wo_ref, x_ref, gate1_ref, sh_ref, sc_ref, gate_ref, g_ref, wu_ref, wd_ref = refs[n_act:n_act + 9]
    fg_ref = refs[n_act + 9] if final else None
    o_ref = refs[-1]
    y = None
    off = 0
    for a_ref in acts:
        kdim = a_ref.shape[1]
        part = _dot(a_ref[...], wo_ref[off:off + kdim, :])
        y = part if y is None else y + part
        off += kdim
    x = x_ref[...] + gate1_ref[...] * y
    h = _rms_mod(x, g_ref[...], sc_ref[...], sh_ref[...]).astype(BF16)
    d_ff = wu_ref.shape[1]
    acc = None
    for c in range(d_ff // f_chunk):
        cs = slice(c * f_chunk, (c + 1) * f_chunk)
        a = jnp.maximum(_dot(h, wu_ref[:, cs]), 0.0)
        part = _dot((a * a).astype(BF16), wd_ref[cs, :])
        acc = part if acc is None else acc + part
    xo = x + gate_ref[...] * acc
    if final:
        y = xo * lax.rsqrt(jnp.mean(xo * xo, axis=-1, keepdims=True) + EPS)
        xo = y * fg_ref[...]
    o_ref[...] = xo


def _post_call(prompt, acts, wo, x2, mod_arr, g, wu, wd, final_g, rows_per_batch):
    m, d = x2.shape
    tm = TM_ROWS if prompt else m
    tpb = rows_per_batch // tm if prompt else 1
    row = lambda i: (i, 0)
    const2 = lambda i: (0, 0)
    final = final_g is not None
    resident = lambda w: pl.BlockSpec(w.shape, const2, pipeline_mode=pl.Buffered(1))
    in_specs = [pl.BlockSpec((tm, a.shape[1]), row) for a in acts]
    in_specs += [resident(wo), pl.BlockSpec((tm, d), row)]
    in_specs += [_mod_spec(prompt, tm, d, c, tpb) for c in (2, 3, 4, 5)]
    in_specs += [pl.BlockSpec((1, d), const2), resident(wu), resident(wd)]
    args = list(acts) + [wo, x2] + [mod_arr] * 4 + [g, wu, wd]
    if final:
        in_specs.append(pl.BlockSpec((1, d), const2))
        args.append(final_g)
    return pl.pallas_call(
        functools.partial(_post_kernel, len(acts), final, 1024),
        out_shape=jax.ShapeDtypeStruct((m, d), F32),
        grid=(m // tm,), in_specs=in_specs, out_specs=pl.BlockSpec((tm, d), row),
        compiler_params=_cparams(("arbitrary",)),
        name="post",
    )(*args)


def _sb_in_prompt_kernel(x_ref, sh_ref, sc_ref, g_ref, wq_ref, wkt_ref, wvt_ref,
                         q_ref, ktb_ref, vtb_ref, kt_ref, vt_ref):
    h = _rms_mod(x_ref[...], g_ref[...], sc_ref[...], sh_ref[...]).astype(BF16)
    q_ref[...] = (_dot(h, wq_ref[...]) * (DH ** -0.5)).astype(q_ref.dtype)
    kt = _dot_nt(wkt_ref[...], h)
    kt_ref[...] = kt
    ktb_ref[...] = kt.astype(BF16)
    vt = _dot_nt(wvt_ref[...], h)
    vt_ref[...] = vt
    vtb_ref[...] = vt.astype(BF16)


def _sb_in_sample_kernel(x_ref, sh_ref, sc_ref, g_ref, wq_ref, wkt_ref, wvt_ref,
                         q_ref, kb_ref, vb_ref, k32_ref, v32_ref):
    h = _rms_mod(x_ref[...], g_ref[...], sc_ref[...], sh_ref[...]).astype(BF16)
    q_ref[...] = (_dot(h, wq_ref[...]) * (DH ** -0.5)).astype(q_ref.dtype)
    k = _dot_nt(h, wkt_ref[...])
    k32_ref[...] = k
    kb_ref[...] = k.astype(BF16)
    v = _dot_nt(h, wvt_ref[...])
    v32_ref[...] = v
    vb_ref[...] = v.astype(BF16)


def _sb_in_call(prompt, x2, mod_arr, g, wts, rows_per_batch):
    wq, wkt, wvt = wts
    m, d = x2.shape
    nb = m // rows_per_batch
    tm = TM_ROWS if prompt else m
    tpb = rows_per_batch // tm if prompt else 1
    row = lambda i: (i, 0)
    const2 = lambda i: (0, 0)
    in_specs = [pl.BlockSpec((tm, d), row),
                _mod_spec(prompt, tm, d, 0, tpb), _mod_spec(prompt, tm, d, 1, tpb),
                pl.BlockSpec((1, d), const2)] + [pl.BlockSpec(w.shape, const2) for w in wts]
    if prompt:
        tcol = lambda i: (i // tpb, 0, i % tpb)
        out_shape = ([jax.ShapeDtypeStruct((m, d), BF16)]
                     + [jax.ShapeDtypeStruct((nb, d, rows_per_batch), BF16)] * 2
                     + [jax.ShapeDtypeStruct((nb, d, rows_per_batch), F32)] * 2)
        out_specs = [pl.BlockSpec((tm, d), row)] + [pl.BlockSpec((None, d, tm), tcol)] * 4
        body = _sb_in_prompt_kernel
    else:
        out_shape = [jax.ShapeDtypeStruct((m, d), BF16)] * 3 + [jax.ShapeDtypeStruct((m, d), F32)] * 2
        out_specs = [pl.BlockSpec((tm, d), row)] * 5
        body = _sb_in_sample_kernel
    return pl.pallas_call(
        body, out_shape=out_shape, grid=(m // tm,), in_specs=in_specs, out_specs=out_specs,
        compiler_params=_cparams(("arbitrary",)),
        name="sb_in_prompt" if prompt else "sb_in_sample",
    )(x2, mod_arr, mod_arr, g, wq, wkt, wvt)


def _to_state(xt, n_heads):
    nb, _, l = xt.shape
    return xt.reshape(nb, n_heads, DH, l).transpose(0, 3, 1, 2)[None]


def _from_cache(c):
    nb, p, n_heads, dh = c.shape
    return c.transpose(0, 2, 3, 1).reshape(nb, n_heads * dh, p)


def _trunk(prompt, x, mods, past, wts):
    nb, rpb, d = x.shape
    m = nb * rpb
    a_w = H_FOX * DH
    x2 = x.reshape(m, d)
    norm_g, ab_wts, w_out_ab, sb_wts, w_out_sb, w_up, w_down, final_g = wts

    if prompt:
        outs = _ab_in_call(True, x2, mods[0], norm_g[0, 0][None], ab_wts, None, rpb, 0)
        q, ktb, vtb, kt32, vt32, lft, fct, kaug, pooled, pstate = outs
        fq_arr = fct.reshape(nb, a_w // LANES, 2, rpb).transpose(0, 1, 3, 2)
        a_out = _fox_call(q.reshape(nb, rpb, a_w), ktb, vtb, kaug, fq_arr, fq_arr[:, :, T_ATT - 1::T_ATT, :])
        st_fox = (_to_state(kt32, H_FOX), _to_state(vt32, H_FOX), lft.transpose(0, 2, 1)[None])
    else:
        cfk, cfv, cflogf, spool, csk, csv = past
        p0 = cfk.shape[1]
        hist = jnp.pad(spool, ((0, 0), (HIST_ROWS - spool.shape[1], 0), (0, 0)))
        outs = _ab_in_call(False, x2, mods[0], norm_g[0, 0][None], ab_wts, hist, rpb, p0)
        q, kb, vb, k32, v32, lft, pooled, pstate = outs
        lfn_t = lft.reshape(H_FOX, nb, rpb).transpose(1, 0, 2)
        a_out = _fox_dec_call(q.reshape(nb, rpb, a_w), kb.reshape(nb, rpb, a_w), vb.reshape(nb, rpb, a_w),
                              _from_cache(cfk), _from_cache(cfv), lfn_t, cflogf.transpose(0, 2, 1))
        st_fox = (k32.reshape(1, nb, rpb, H_FOX, DH), v32.reshape(1, nb, rpb, H_FOX, DH),
                  lfn_t.transpose(0, 2, 1)[None])
    x2 = _post_call(prompt, [a_out.reshape(m, a_w), pooled], w_out_ab, x2, mods[0], norm_g[0, 1][None],
                    w_up[0], w_down[0], None, rpb)

    n_heads = d // DH
    outs = _sb_in_call(prompt, x2, mods[1], norm_g[1, 0][None], sb_wts, rpb)
    if prompt:
        q1, ktb1, vtb1, skt32, svt32 = outs
        s_out = _sb_call(q1.reshape(nb, rpb, d), ktb1, vtb1)
        st_sb = (_to_state(skt32, n_heads), _to_state(svt32, n_heads))
    else:
        q1, kb1, vb1, sk32, sv32 = outs
        s_out = _sb_dec_call(q1.reshape(nb, rpb, d), kb1.reshape(nb, rpb, d), vb1.reshape(nb, rpb, d),
                             _from_cache(csk), _from_cache(csv))
        st_sb = (sk32.reshape(1, nb, rpb, n_heads, DH), sv32.reshape(1, nb, rpb, n_heads, DH))
    y = _post_call(prompt, [s_out.reshape(m, d)], w_out_sb, x2, mods[1], norm_g[1, 1][None],
                   w_up[1], w_down[1], final_g[None], rpb)

    states = st_fox + (pstate[:, 1:, :][None],) + st_sb
    return y.reshape(nb, rpb, d), states


def kernel(x_prompt, x_sample, c_prompt, c_sample, cache_fox_k, cache_fox_v, cache_fox_logf, state_pool, cache_sb_k, cache_sb_v, w_ada, b_ada, norm_g, w_in_ab, b_forget, w_pool, pool_scale, w_out_ab, w_in_sb, w_out_sb, w_up, w_down, final_g):
    b, l, d = x_prompt.shape
    db, ds, _ = x_sample.shape
    a_w = H_FOX * DH

    c_all = jnp.concatenate([c_prompt, c_sample, jnp.zeros((ADA_ROWS - b - db, d), F32)], axis=0)
    mod = _ada_call(c_all, w_ada, b_ada)
    mods_p = [mod[i, :b][:, None, :] for i in range(mod.shape[0])]
    mods_s = [jnp.repeat(mod[i, b:b + db], ds, axis=0) for i in range(mod.shape[0])]

    w_in = w_in_ab[0]
    gate_pad = ((0, GATE_ROWS - H_FOX), (0, 0))
    ab_wts = (w_in[:, :a_w].astype(BF16),
              w_in[:, a_w:2 * a_w].T.astype(BF16),
              w_in[:, 2 * a_w:3 * a_w].T.astype(BF16),
              w_in[:, 3 * a_w + H_FOX:].astype(BF16),
              jnp.pad(w_in[:, 3 * a_w:3 * a_w + H_FOX].T, gate_pad).astype(BF16),
              jnp.pad(b_forget[0][:, None], gate_pad),
              w_pool[0].astype(BF16), pool_scale[0][None])
    w_sb = w_in_sb[0]
    sb_wts = (w_sb[:, :d].astype(BF16), w_sb[:, d:2 * d].T.astype(BF16), w_sb[:, 2 * d:].T.astype(BF16))
    wts = (norm_g, ab_wts, w_out_ab[0].astype(BF16), sb_wts, w_out_sb[0].astype(BF16),
           w_up.astype(BF16), w_down.astype(BF16), final_g)

    y_p, st_p = _trunk(True, x_prompt, mods_p, None, wts)
    past = (cache_fox_k[0], cache_fox_v[0], cache_fox_logf[0], state_pool[0],
            cache_sb_k[0], cache_sb_v[0])
    y_s, st_s = _trunk(False, x_sample, mods_s, past, wts)
    fk, fv, fl, pool, sk, sv = st_p
    fk2, fv2, fl2, pool2, sk2, sv2 = st_s
    return (y_p, y_s, fk, fv, fl, pool, sk, sv, fk2, fv2, fl2, pool2, sk2, sv2)
```

```python
import functools

import jax
import jax.numpy as jnp
from jax import lax
from jax.experimental import pallas as pl
from jax.experimental.pallas import tpu as pltpu

F32 = jnp.float32
BF16 = jnp.bfloat16

DH = 64
H_FOX = 8
EPS = 1e-6
POOL_WINDOWS = (2, 4, 8, 16)
POOL_GC = 128
HIST_ROWS = 16
NEG_BIG = -1e30
LOG2E = 1.4426950408889634
LANES = 128
GATE_ROWS = 16
AUG_ROWS = 16
OFF_SLOT = 12
VMEM_LIMIT = 56 * 1024 * 1024

UNDERFLOW_EXP = 88.0
NORM_MARGIN = 1.01

T_ATT = 256
NQ_ATT = 4
TK_DEC = 512
TK_DEC_FOX = 1024
TM_ROWS = 512
ADA_ROWS = 32
ADA_TN = 1536


def _cparams(sem):
    return pltpu.CompilerParams(dimension_semantics=sem, vmem_limit_bytes=VMEM_LIMIT)


def _rms_mod(x, g, sc, sh):
    y = x * lax.rsqrt(jnp.mean(x * x, axis=-1, keepdims=True) + EPS)
    return (y * g) * (1.0 + sc) + sh


def _softplus(z):
    return jnp.maximum(z, 0.0) + jnp.log(1.0 + jnp.exp2(jnp.abs(z) * (-LOG2E)))


def _log_sigmoid(x):
    return jnp.minimum(x, 0.0) - jnp.log(1.0 + jnp.exp(-jnp.abs(x)))


def _split3(x):
    hi = x.astype(BF16)
    r = x - hi.astype(F32)
    mid = r.astype(BF16)
    lo = (r - mid.astype(F32)).astype(BF16)
    return hi, mid, lo


def _dot(a, b):
    return jnp.dot(a, b, preferred_element_type=F32)


def _dot_nt(a, b):
    return lax.dot_general(a, b, (((1,), (1,)), ((), ())), preferred_element_type=F32)


def _dot3(x, m):
    hi, mid, lo = _split3(x)
    return _dot(hi, m) + _dot(mid, m) + _dot(lo, m)


def _iota(shape, dim):
    return lax.broadcasted_iota(jnp.int32, shape, dim)


def _log2(n):
    assert n & (n - 1) == 0, n
    return n.bit_length() - 1


def _ada_kernel(c_ref, w_ref, b_ref, o_ref):
    c = c_ref[...]
    cond = (c / (1.0 + jnp.exp(-c))).astype(BF16)
    o_ref[...] = _dot(cond, w_ref[...].astype(BF16)) + b_ref[...]


def _ada_call(c_all, w_ada, b_ada):
    depth, d, n = w_ada.shape
    return pl.pallas_call(
        _ada_kernel,
        out_shape=jax.ShapeDtypeStruct((depth, ADA_ROWS, n), F32),
        grid=(depth, n // ADA_TN),
        in_specs=[
            pl.BlockSpec((ADA_ROWS, d), lambda i, j: (0, 0)),
            pl.BlockSpec((None, d, ADA_TN), lambda i, j: (i, 0, j)),
            pl.BlockSpec((None, 1, ADA_TN), lambda i, j: (i, 0, j)),
        ],
        out_specs=pl.BlockSpec((None, ADA_ROWS, ADA_TN), lambda i, j: (i, 0, j)),
        compiler_params=_cparams(("arbitrary", "arbitrary")),
        name="ada_mod",
    )(c_all, w_ada, b_ada.reshape(depth, 1, n))


def _mod_spec(prompt, tm, d, chunk, tiles_per_batch):
    if prompt:
        return pl.BlockSpec((None, 1, d), lambda i: (i // tiles_per_batch, 0, chunk))
    return pl.BlockSpec((tm, d), lambda i: (i, chunk))


def _pool_branch(u, ext_ref, pos, ng, rg, wp_ref, ps_ref, pooled_ref, pst_ref):
    tm, a_w = u.shape
    u3 = u.reshape(ng, rg, a_w)
    ext_ref[:, HIST_ROWS:HIST_ROWS + rg, :] = u3
    pst_ref[...] = u3[:, rg - HIST_ROWS:rg, :]
    for g, w in enumerate(POOL_WINDOWS):
        cs = slice(g * POOL_GC, (g + 1) * POOL_GC)
        acc = u3[:, :, cs]
        for kk in range(1, w):
            acc = acc + ext_ref[:, pl.ds(HIST_ROWS - kk, rg), cs]
        cnt = jnp.minimum(pos + 1, w).astype(F32)
        pooled = acc.reshape(tm, POOL_GC) / cnt - u[:, cs]
        pg = _dot(pooled.astype(BF16), wp_ref[g]) * ps_ref[:, cs]
        pooled_ref[:, cs] = pg.astype(pooled_ref.dtype)


def _ab_in_prompt_kernel(tm, tiles_per_batch, x_ref, sh_ref, sc_ref, g_ref, wq_ref, wkt_ref, wvt_ref,
                         wu_ref, wft_ref, bf_ref, wp_ref, ps_ref,
                         q_ref, ktb_ref, vtb_ref, kt_ref, vt_ref, lft_ref, fct_ref, kaug_ref, pooled_ref, pst_ref,
                         ext_ref, carry_ref):
    h = _rms_mod(x_ref[...], g_ref[...], sc_ref[...], sh_ref[...]).astype(BF16)
    q_ref[...] = (_dot(h, wq_ref[...]) * (DH ** -0.5)).astype(q_ref.dtype)
    kt = _dot_nt(wkt_ref[...], h)
    kt_ref[...] = kt
    ktb_ref[...] = kt.astype(BF16)
    vt = _dot_nt(wvt_ref[...], h)
    vt_ref[...] = vt
    vtb_ref[...] = vt.astype(BF16)

    tile_in_batch = pl.program_id(0) % tiles_per_batch
    first = tile_in_batch == 0
    logf = _log_sigmoid(_dot_nt(wft_ref[...], h) + bf_ref[...])
    lft_ref[...] = logf[0:H_FOX, :]
    tri = (_iota((tm, tm), 0) <= _iota((tm, tm), 1)).astype(BF16)
    fcum = _dot3(logf, tri) + jnp.where(first, 0.0, carry_ref[...])
    carry_ref[...] = fcum[:, tm - 1:tm]
    fct_ref[...] = fcum[0:H_FOX, :]

    n_aug = (H_FOX // 2) * AUG_ROWS
    slot = _iota((n_aug, GATE_ROWS), 0) & (AUG_ROWS - 1)
    head0 = (_iota((n_aug, GATE_ROWS), 0) >> _log2(AUG_ROWS)) * 2
    col = _iota((n_aug, GATE_ROWS), 1)
    kaug = None
    for part, piece in enumerate(_split3(fcum)):
        sel = jnp.logical_or(jnp.logical_and(slot == 3 + part, col == head0),
                             jnp.logical_and(slot == 9 + part, col == head0 + 1))
        term = _dot(jnp.where(sel, -1.0, 0.0).astype(BF16), piece)
        kaug = term if kaug is None else kaug + term
    slot1 = _iota((n_aug, 1), 0) & (AUG_ROWS - 1)
    ones = jnp.logical_or(slot1 < 3, jnp.logical_and(slot1 >= 6, slot1 < 9))
    kaug_ref[...] = (kaug + jnp.where(ones, 1.0, 0.0)).astype(BF16)

    @pl.when(first)
    def _():
        ext_ref[:, 0:HIST_ROWS, :] = jnp.zeros((1, HIST_ROWS, ext_ref.shape[2]), F32)

    @pl.when(jnp.logical_not(first))
    def _():
        ext_ref[:, 0:HIST_ROWS, :] = ext_ref[:, tm:tm + HIST_ROWS, :]

    pos = tile_in_batch * tm + _iota((tm, 1), 0)
    _pool_branch(_dot(h, wu_ref[...]), ext_ref, pos, 1, tm, wp_ref, ps_ref, pooled_ref, pst_ref)


def _ab_in_sample_kernel(ng, rg, p0, x_ref, sh_ref, sc_ref, g_ref, wq_ref, wkt_ref, wvt_ref,
                         wu_ref, wft_ref, bf_ref, wp_ref, ps_ref, hist_ref,
                         q_ref, kb_ref, vb_ref, k32_ref, v32_ref, lft_ref, pooled_ref, pst_ref, ext_ref):
    tm = ng * rg
    h = _rms_mod(x_ref[...], g_ref[...], sc_ref[...], sh_ref[...]).astype(BF16)
    q_ref[...] = (_dot(h, wq_ref[...]) * (DH ** -0.5)).astype(q_ref.dtype)
    k = _dot_nt(h, wkt_ref[...])
    k32_ref[...] = k
    kb_ref[...] = k.astype(BF16)
    v = _dot_nt(h, wvt_ref[...])
    v32_ref[...] = v
    vb_ref[...] = v.astype(BF16)
    logf = _log_sigmoid(_dot_nt(wft_ref[...], h) + bf_ref[...])
    lft_ref[...] = logf[0:H_FOX, :]
    ext_ref[:, 0:HIST_ROWS, :] = hist_ref[...]
    pos = p0 + (_iota((tm, 1), 0) & (rg - 1))
    _pool_branch(_dot(h, wu_ref[...]), ext_ref, pos, ng, rg, wp_ref, ps_ref, pooled_ref, pst_ref)


def _ab_in_call(prompt, x2, mod_arr, g, wts, hist, rows_per_batch, p0):
    wq, wkt, wvt, wu, wft, bf, wp, ps = wts
    m, d = x2.shape
    a_w = H_FOX * DH
    nb = m // rows_per_batch
    tm = TM_ROWS if prompt else m
    tpb = rows_per_batch // tm if prompt else 1
    row = lambda i: (i, 0)
    const2 = lambda i: (0, 0)
    in_specs = [pl.BlockSpec((tm, d), row), _mod_spec(prompt, tm, d, 0, tpb), _mod_spec(prompt, tm, d, 1, tpb),
                pl.BlockSpec((1, d), const2)]
    in_specs += [pl.BlockSpec(w.shape, const2) for w in (wq, wkt, wvt, wu, wft, bf)]
    in_specs += [pl.BlockSpec(wp.shape, lambda i: (0, 0, 0)), pl.BlockSpec(ps.shape, const2)]
    args = [x2, mod_arr, mod_arr, g, wq, wkt, wvt, wu, wft, bf, wp, ps]
    if prompt:
        tcol = lambda i: (i // tpb, 0, i % tpb)
        out_shape = [jax.ShapeDtypeStruct((m, a_w), BF16),
                     jax.ShapeDtypeStruct((nb, a_w, rows_per_batch), BF16),
                     jax.ShapeDtypeStruct((nb, a_w, rows_per_batch), BF16),
                     jax.ShapeDtypeStruct((nb, a_w, rows_per_batch), F32),
                     jax.ShapeDtypeStruct((nb, a_w, rows_per_batch), F32),
                     jax.ShapeDtypeStruct((nb, H_FOX, rows_per_batch), F32),
                     jax.ShapeDtypeStruct((nb, H_FOX, rows_per_batch), F32),
                     jax.ShapeDtypeStruct((nb, (H_FOX // 2) * AUG_ROWS, rows_per_batch), BF16),
                     jax.ShapeDtypeStruct((m, a_w), BF16),
                     jax.ShapeDtypeStruct((nb, HIST_ROWS, a_w), F32)]
        out_specs = ([pl.BlockSpec((tm, a_w), row)] + [pl.BlockSpec((None, a_w, tm), tcol)] * 4
                     + [pl.BlockSpec((None, H_FOX, tm), tcol)] * 2
                     + [pl.BlockSpec((None, (H_FOX // 2) * AUG_ROWS, tm), tcol)]
                     + [pl.BlockSpec((tm, a_w), row),
                        pl.BlockSpec((1, HIST_ROWS, a_w), lambda i: (i // tpb, 0, 0))])
        scratch = [pltpu.VMEM((1, HIST_ROWS + tm, a_w), F32), pltpu.VMEM((GATE_ROWS, 1), F32)]
        body = functools.partial(_ab_in_prompt_kernel, tm, tpb)
    else:
        in_specs.append(pl.BlockSpec(hist.shape, lambda i: (0, 0, 0)))
        args.append(hist)
        out_shape = [jax.ShapeDtypeStruct((m, a_w), BF16), jax.ShapeDtypeStruct((m, a_w), BF16),
                     jax.ShapeDtypeStruct((m, a_w), BF16), jax.ShapeDtypeStruct((m, a_w), F32),
                     jax.ShapeDtypeStruct((m, a_w), F32), jax.ShapeDtypeStruct((H_FOX, m), F32),
                     jax.ShapeDtypeStruct((m, a_w), BF16), jax.ShapeDtypeStruct((nb, HIST_ROWS, a_w), F32)]
        out_specs = ([pl.BlockSpec((tm, a_w), row)] * 5 + [pl.BlockSpec((H_FOX, tm), const2)]
                     + [pl.BlockSpec((tm, a_w), row), pl.BlockSpec((nb, HIST_ROWS, a_w), lambda i: (0, 0, 0))])
        scratch = [pltpu.VMEM((nb, HIST_ROWS + rows_per_batch, a_w), F32)]
        body = functools.partial(_ab_in_sample_kernel, nb, rows_per_batch, p0)
    return pl.pallas_call(
        body, out_shape=out_shape, grid=(m // tm,), in_specs=in_specs, out_specs=out_specs,
        scratch_shapes=scratch,
        compiler_params=_cparams(("arbitrary",)),
        name="ab_in_prompt" if prompt else "ab_in_sample",
    )(*args)


def _head_split(q2):
    lane = _iota((1, LANES), 1)
    zero = jnp.zeros_like(q2)
    return jnp.where(lane < DH, q2, zero), jnp.where(lane >= DH, q2, zero)


def _pair_sel():
    return (_iota((8, LANES), 0) == (_iota((8, LANES), 1) >> _log2(DH))).astype(BF16)


def _fox_kernel(q_ref, kt_ref, vt_ref, ka_ref, fq_ref, ft_ref, o_ref,
                m_ref, acc_ref, kn_ref):
    t, nq = T_ATT, NQ_ATT
    qg = pl.program_id(2)
    lane = _iota((1, LANES), 1)
    lane2 = _iota((1, 2), 1)

    @pl.when(qg == 0)
    def _():
        chunk = 4 * t

        def body(i, mx):
            ks = pl.multiple_of(i * chunk, chunk)
            kf = kt_ref[:, pl.ds(ks, chunk)].astype(F32)
            n2 = _dot(_pair_sel(), (kf * kf).astype(BF16))
            return jnp.maximum(mx, jnp.max(n2, axis=1, keepdims=True))

        mx = lax.fori_loop(0, kt_ref.shape[1] // chunk, body, jnp.zeros((8, 1), F32))
        kn_ref[...] = jnp.sqrt(mx)

    lane_a = _iota((1, AUG_ROWS), 1)
    row_v = _iota((LANES, 1), 0)
    qcat, bound0 = [], []
    for s in range(nq):
        q2 = q_ref[s * t:(s + 1) * t, :]
        fq = fq_ref[s * t:(s + 1) * t, :]
        per_head = []
        for j, qh in enumerate(_head_split(q2)):
            hi, mid, lo = (x.astype(F32) for x in _split3(fq[:, j:j + 1]))
            base = 6 * j
            qa = jnp.where(lane_a == base, hi, jnp.where(lane_a == base + 1, mid, jnp.where(
                lane_a == base + 2, lo,
                jnp.where(jnp.logical_and(lane_a >= base + 3, lane_a < base + 6), 1.0, 0.0))))
            qa = jnp.where(lane_a == OFF_SLOT, NEG_BIG, qa)
            per_head.append(jnp.concatenate([qh, qa.astype(BF16)], axis=1))
        qcat.append(per_head)
        qf = q2.astype(F32)
        qn2 = _dot_nt((qf * qf).astype(BF16), _pair_sel())
        qn = jnp.sqrt(jnp.max(qn2, axis=0, keepdims=True))
        qk = jnp.where(lane2 == 0, qn[:, 0:1] * kn_ref[0:1, :], qn[:, 1:2] * kn_ref[1:2, :])
        bound0.append(qk * NORM_MARGIN + fq[0:1, :])
    chains = [(s, j) for s in range(nq) for j in range(2)]
    rep = lambda m: jnp.concatenate([m] * (t // LANES), axis=1)

    def front(n, diag, chains=chains):
        kbs = [qg * nq + s - n for s in range(nq)]
        kss = [pl.multiple_of((kb if diag else jnp.maximum(kb, 0)) * t, t) for kb in kbs]
        causal = _iota((t, t), 1) <= _iota((t, t), 0)
        kcat = []
        for s in range(nq):
            ka = ka_ref[:, pl.ds(kss[s], t)]
            if not diag:
                off = jnp.where(kbs[s] >= 0, 0.0, 1.0).astype(BF16)
                ka = jnp.where(_iota((AUG_ROWS, 1), 0) == OFF_SLOT, off, ka)
            kcat.append(jnp.concatenate([kt_ref[:, pl.ds(kss[s], t)], ka], axis=0))
        sc = []
        for s, j in chains:
            x = _dot(qcat[s][j], kcat[s])
            sc.append(jnp.where(causal, x, NEG_BIG) if diag else x)
        vcat = []
        for s, j in chains:
            v2 = vt_ref[:, pl.ds(kss[s], t)]
            mine = (row_v < DH) if j == 0 else (row_v >= DH)
            vcat.append(jnp.where(mine, v2, jnp.ones_like(v2)))
        return sc, vcat

    (sc0, v0), (sc1, v1) = front(0, True), front(1, False)
    m01 = [jnp.broadcast_to(jnp.maximum(jnp.max(x0, axis=1, keepdims=True), jnp.max(x1, axis=1, keepdims=True)),
                            (t, LANES)) for x0, x1 in zip(sc0, sc1)]
    pv0 = [_dot_nt(jnp.exp(x - rep(m)).astype(BF16), vv) for x, m, vv in zip(sc0, m01, v0)]
    pv1 = [_dot_nt(jnp.exp(x - rep(m)).astype(BF16), vv) for x, m, vv in zip(sc1, m01, v1)]
    for i, (s, j) in enumerate(chains):
        acc_ref[2 * s + j] = pv0[i] + pv1[i]
        m_ref[2 * s + j] = m01[i]

    def step(n, chains):
        sc, vcat = front(n, False, chains)
        m_prev = [m_ref[2 * s + j] for s, j in chains]
        m_new = [jnp.maximum(mp, jnp.max(x, axis=1, keepdims=True)) for mp, x in zip(m_prev, sc)]
        pv = [_dot_nt(jnp.exp(x - rep(mn)).astype(BF16), vv) for x, mn, vv in zip(sc, m_new, vcat)]
        for i, (s, j) in enumerate(chains):
            idx = 2 * s + j
            acc_ref[idx] = jnp.exp(m_prev[i] - m_new[i]) * acc_ref[idx] + pv[i]
            m_ref[idx] = m_new[i]

    def more(n):
        go = [False, False]
        for s in range(nq):
            kb = qg * nq + s - n
            ft = ft_ref[pl.ds(jnp.maximum(kb, 0), 1), :]
            for j in range(2):
                top = jnp.max(bound0[s][:, j:j + 1] - ft[:, j:j + 1]) - jnp.min(m_ref[2 * s + j])
                go[j] = jnp.logical_or(go[j], jnp.logical_and(kb >= 0, top > -UNDERFLOW_EXP))
        return go

    def body(chains, c):
        n = c[0]
        step(n, chains)
        return (n + 1, *more(n + 1))

    c = lax.while_loop(lambda c: jnp.logical_and(c[1], c[2]), functools.partial(body, chains),
                       (jnp.int32(2), *more(2)))
    c = lax.while_loop(lambda c: c[1], functools.partial(body, [(s, 0) for s in range(nq)]), c)
    lax.while_loop(lambda c: c[2], functools.partial(body, [(s, 1) for s in range(nq)]), c)
    for s in range(nq):
        a0, a1 = acc_ref[2 * s], acc_ref[2 * s + 1]
        o = jnp.where(lane < DH, a0 / pltpu.roll(a0, DH, 1), a1 / pltpu.roll(a1, DH, 1))
        o_ref[s * t:(s + 1) * t, :] = o.astype(o_ref.dtype)


def _fox_call(q, ktb, vtb, kaug, fq_arr, ft_arr):
    b, l, a_w = q.shape
    hp = a_w // LANES
    t, nq = T_ATT, NQ_ATT
    tq = t * nq
    return pl.pallas_call(
        _fox_kernel,
        out_shape=jax.ShapeDtypeStruct((b, l, a_w), BF16),
        grid=(b, hp, l // tq),
        in_specs=[
            pl.BlockSpec((None, tq, LANES), lambda bi, h, qi: (bi, qi, h)),
            pl.BlockSpec((None, LANES, l), lambda bi, h, qi: (bi, h, 0)),
            pl.BlockSpec((None, LANES, l), lambda bi, h, qi: (bi, h, 0)),
            pl.BlockSpec((None, AUG_ROWS, l), lambda bi, h, qi: (bi, h, 0)),
            pl.BlockSpec((None, None, tq, 2), lambda bi, h, qi: (bi, h, qi, 0)),
            pl.BlockSpec((None, None, l // t, 2), lambda bi, h, qi: (bi, h, 0, 0)),
        ],
        out_specs=pl.BlockSpec((None, tq, LANES), lambda bi, h, qi: (bi, qi, h)),
        scratch_shapes=[pltpu.VMEM((2 * nq, t, LANES), F32), pltpu.VMEM((2 * nq, t, LANES), F32),
                        pltpu.VMEM((8, 1), F32)],
        compiler_params=_cparams(("arbitrary", "arbitrary", "arbitrary")),
        name="fox_prompt",
    )(q, ktb, vtb, kaug, fq_arr, ft_arr)


def _sb_kernel(q_ref, kt_ref, vt_ref, o_ref, carry_ref, acc_ref):
    t, nq = T_ATT, NQ_ATT
    qg = pl.program_id(2)
    lane = _iota((1, LANES), 1)
    qh = [_head_split(q_ref[s * t:(s + 1) * t, :]) for s in range(nq)]
    chains = [(s, j) for s in range(nq) for j in range(2)]
    valid = _iota((t, t), 1) < _iota((t, t), 0)

    def front(n, diag):
        tri = (_iota((t, t), 0) > _iota((t, t), 1)).astype(BF16)
        kbs = [qg * nq + s - n for s in range(nq)]
        kss = [pl.multiple_of((kb if diag else jnp.maximum(kb, 0)) * t, t) for kb in kbs]
        z = [_dot(qh[s][j], kt_ref[:, pl.ds(kss[s], t)]) for s, j in chains]
        sp = [_softplus(zz) for zz in z]
        spm = [jnp.where(valid, x, 0.0) for x in sp] if diag else sp
        later = [_dot(x.astype(BF16), tri) for x in spm]
        vs = []
        for s in range(nq):
            v2 = vt_ref[:, pl.ds(kss[s], t)]
            if not diag:
                v2 = jnp.where(kbs[s] >= 0, v2, jnp.zeros_like(v2))
            vs.append(v2)
        ls = [zz - x for zz, x in zip(z, sp)]
        return ls, spm, later, vs

    def back(fr, diag, carry):
        ls, spm, later, vs = fr
        pv, new_carry = [], []
        for i, (s, j) in enumerate(chains):
            e = ls[i] - later[i]
            a = jnp.exp(e if carry is None else e - carry[i])
            if diag:
                a = jnp.where(valid, a, 0.0)
            pv.append(_dot_nt(a.astype(BF16), vs[s]))
            rs = jnp.sum(spm[i], axis=1, keepdims=True)
            new_carry.append(rs if carry is None else carry[i] + rs)
        return pv, new_carry

    f0, f1 = front(0, True), front(1, False)
    pv0, c0 = back(f0, True, None)
    pv1, c1 = back(f1, False, c0)
    for i, (s, j) in enumerate(chains):
        acc_ref[2 * s + j] = pv0[i] + pv1[i]
        carry_ref[2 * s + j] = c1[i]

    def more(n):
        in_range = n <= qg * nq + nq - 1
        return jnp.logical_and(in_range, jnp.min(carry_ref[...]) < UNDERFLOW_EXP)

    def body(c):
        n, _ = c
        pv, carry = back(front(n, False), False, [carry_ref[2 * s + j] for s, j in chains])
        for i, (s, j) in enumerate(chains):
            acc_ref[2 * s + j] = acc_ref[2 * s + j] + pv[i]
            carry_ref[2 * s + j] = carry[i]
        return n + 1, more(n + 1)

    lax.while_loop(lambda c: c[1], body, (jnp.int32(2), more(2)))
    for s in range(nq):
        o = jnp.where(lane < DH, acc_ref[2 * s], acc_ref[2 * s + 1])
        o_ref[s * t:(s + 1) * t, :] = o.astype(o_ref.dtype)


def _sb_call(q, ktb, vtb):
    b, l, d = q.shape
    hp = d // LANES
    t, nq = T_ATT, NQ_ATT
    tq = t * nq
    return pl.pallas_call(
        _sb_kernel,
        out_shape=jax.ShapeDtypeStruct((b, l, d), BF16),
        grid=(b, hp, l // tq),
        in_specs=[
            pl.BlockSpec((None, tq, LANES), lambda bi, h, qi: (bi, qi, h)),
            pl.BlockSpec((None, LANES, l), lambda bi, h, qi: (bi, h, 0)),
            pl.BlockSpec((None, LANES, l), lambda bi, h, qi: (bi, h, 0)),
        ],
        out_specs=pl.BlockSpec((None, tq, LANES), lambda bi, h, qi: (bi, qi, h)),
        scratch_shapes=[pltpu.VMEM((2 * nq, t, 1), F32), pltpu.VMEM((2 * nq, t, LANES), F32)],
        compiler_params=_cparams(("arbitrary", "arbitrary", "arbitrary")),
        name="sb_prompt",
    )(q, ktb, vtb)


def _rep_matrix(rows, n_heads, ds):
    return ((_iota((rows, ds), 0) >> _log2(n_heads)) == _iota((rows, ds), 1)).astype(BF16)


def _head_mask(rows, n_heads, width):
    return (_iota((rows, width), 0) & (n_heads - 1)) == (_iota((rows, width), 1) >> _log2(DH))


def _block_diag_q(q, n_heads):
    ds, width = q.shape
    rows = ds * n_heads
    qrep = _dot(_rep_matrix(rows, n_heads, ds), q)
    return jnp.where(_head_mask(rows, n_heads, width), qrep, 0.0).astype(BF16)


def _collect_heads(acc, n_heads, ds):
    rows, width = acc.shape
    om = jnp.where(_head_mask(rows, n_heads, width), acc, 0.0).astype(BF16)
    rep_t = ((_iota((ds, rows), 1) >> _log2(n_heads)) == _iota((ds, rows), 0)).astype(BF16)
    return _dot(rep_t, om)


def _suffix_sums(x, blk=2 * LANES):
    r, n = x.shape
    parts = [x[:, b * blk:(b + 1) * blk] for b in range(n // blk)]
    tri = (_iota((blk, blk), 0) > _iota((blk, blk), 1)).astype(BF16)
    w = _dot3(jnp.concatenate(parts, axis=0), tri)
    out, tail = [], None
    for b in reversed(range(len(parts))):
        cur = w[b * r:(b + 1) * r]
        out.append(cur if tail is None else cur + tail)
        tot = jnp.sum(parts[b], axis=1, keepdims=True)
        tail = tot if tail is None else tail + tot
    return jnp.concatenate(out[::-1], axis=1), tail


def _fox_dec_kernel(n_heads, ds, q_ref, kn_ref, vn_ref, lfn_ref, kc_ref, vc_ref, lfc_ref,
                    o_ref, qbd_ref, m_ref, l_ref, acc_ref, cf_ref):
    rows = ds * n_heads
    step_i = pl.program_id(1)
    n_steps = pl.num_programs(1)

    def update(s, pv_fn):
        m_prev = m_ref[...]
        m_new = jnp.maximum(m_prev, jnp.max(s, axis=1, keepdims=True))
        alpha = jnp.exp(m_prev - m_new)
        p = jnp.exp(s - m_new)
        l_ref[...] = alpha * l_ref[...] + jnp.sum(p, axis=1, keepdims=True)
        acc_ref[...] = alpha * acc_ref[...] + pv_fn(p.astype(BF16))
        m_ref[...] = m_new

    @pl.when(step_i == 0)
    def _():
        qbd_ref[...] = _block_diag_q(q_ref[...], n_heads)
        m_ref[...] = jnp.full(m_ref.shape, NEG_BIG, F32)
        l_ref[...] = jnp.zeros(l_ref.shape, F32)
        acc_ref[...] = jnp.zeros(acc_ref.shape, F32)
        a = jnp.concatenate([lfn_ref[...]] * ds, axis=0)
        ri = _iota((rows, ds), 0) >> _log2(n_heads)
        cj = _iota((rows, ds), 1)
        am = jnp.where(cj <= ri, a, 0.0)
        tri = (_iota((ds, ds), 0) > _iota((ds, ds), 1)).astype(BF16)
        s = _dot_nt(qbd_ref[...], kn_ref[...]) + _dot3(am, tri)
        s = jnp.where(cj <= ri, s, NEG_BIG)
        update(s, lambda p: _dot(p, vn_ref[...]))
        cf_ref[...] = jnp.sum(am, axis=1, keepdims=True)

    @pl.when(step_i > 0)
    def _():
        reps = rows // (2 * n_heads)
        later, total = _suffix_sums(jnp.concatenate([lfc_ref[...]] * 2, axis=0))
        cf = cf_ref[...]
        s = _dot(qbd_ref[...], kc_ref[...].astype(BF16)) + (jnp.concatenate([later] * reps, axis=0) + cf)
        update(s, lambda p: _dot_nt(p, vc_ref[...].astype(BF16)))
        cf_ref[...] = cf + jnp.concatenate([total] * reps, axis=0)

    @pl.when(step_i == n_steps - 1)
    def _():
        o_ref[...] = _collect_heads(acc_ref[...] / l_ref[...], n_heads, ds).astype(o_ref.dtype)


def _fox_dec_call(q, kn, vn, kct, vct, lfn_t, lfc_t):
    db, ds, width = q.shape
    n_heads = width // DH
    rows = ds * n_heads
    tk = min(TK_DEC_FOX, kct.shape[2])
    nkb = kct.shape[2] // tk
    cache_blk = lambda s: jnp.minimum(nkb - s, nkb - 1)
    new_idx = lambda b, s: (b, 0, 0)
    cache_idx = lambda b, s: (b, 0, cache_blk(s))
    in_specs = ([pl.BlockSpec((None, ds, width), new_idx)] * 3
                + [pl.BlockSpec((None, n_heads, ds), new_idx)]
                + [pl.BlockSpec((None, width, tk), cache_idx)] * 2
                + [pl.BlockSpec((None, n_heads, tk), cache_idx)])
    scratch = [pltpu.VMEM((rows, width), BF16), pltpu.VMEM((rows, 1), F32), pltpu.VMEM((rows, 1), F32),
               pltpu.VMEM((rows, width), F32), pltpu.VMEM((rows, 1), F32)]
    return pl.pallas_call(
        functools.partial(_fox_dec_kernel, n_heads, ds),
        out_shape=jax.ShapeDtypeStruct((db, ds, width), BF16),
        grid=(db, nkb + 1),
        in_specs=in_specs,
        out_specs=pl.BlockSpec((None, ds, width), new_idx),
        scratch_shapes=scratch,
        compiler_params=_cparams(("arbitrary", "arbitrary")),
        name="fox_decode",
    )(q, kn, vn, lfn_t, kct, vct, lfc_t)


def _sb_dec_tile(qbd, kc_ref, vc_ref, carry):
    tk = kc_ref.shape[1]
    z = _dot(qbd, kc_ref[...].astype(BF16))
    sp = _softplus(z)
    tri = (_iota((tk, tk), 0) > _iota((tk, tk), 1)).astype(BF16)
    later = _dot(sp.astype(BF16), tri)
    a = jnp.exp((z - sp) - later - carry)
    return _dot_nt(a.astype(BF16), vc_ref[...].astype(BF16)), carry + jnp.sum(sp, axis=1, keepdims=True)


def _sb_dec_first_kernel(n_heads, ds, q_ref, kn_ref, vn_ref, kc_ref, vc_ref,
                         o_ref, carry_out_ref, cmin_ref):
    rows = ds * n_heads
    qbd = _block_diag_q(q_ref[...], n_heads)
    ri = _iota((rows, ds), 0) >> _log2(n_heads)
    cj = _iota((rows, ds), 1)
    valid = cj < ri
    z = _dot_nt(qbd, kn_ref[...])
    sp = _softplus(z)
    spm = jnp.where(valid, sp, 0.0)
    tri = (_iota((ds, ds), 0) > _iota((ds, ds), 1)).astype(BF16)
    later = _dot(spm.astype(BF16), tri)
    a = jnp.where(valid, jnp.exp((z - sp) - later), 0.0)
    acc = _dot(a.astype(BF16), vn_ref[...])
    carry = jnp.sum(spm, axis=1, keepdims=True)
    part, carry = _sb_dec_tile(qbd, kc_ref, vc_ref, carry)
    o_ref[...] = _collect_heads(acc + part, n_heads, ds).astype(o_ref.dtype)
    carry_out_ref[...] = carry
    cmin_ref[...] = jnp.broadcast_to(jnp.min(carry, axis=0, keepdims=True), cmin_ref.shape)


def _sb_dec_rest_kernel(n_heads, ds, need_ref, q_ref, carry_in_ref, o1_ref, kc_ref, vc_ref,
                        o_ref, qbd_ref, acc_ref, carry_ref):
    step_i = pl.program_id(1)
    last = pl.num_programs(1) - 1
    needed = need_ref[pl.program_id(0)] != 0

    @pl.when(needed)
    def _():
        @pl.when(step_i == 0)
        def _():
            qbd_ref[...] = _block_diag_q(q_ref[...], n_heads)
            acc_ref[...] = jnp.zeros(acc_ref.shape, F32)
            carry_ref[...] = carry_in_ref[...]

        @pl.when(jnp.min(carry_ref[...]) < UNDERFLOW_EXP)
        def _():
            part, carry = _sb_dec_tile(qbd_ref[...], kc_ref, vc_ref, carry_ref[...])
            acc_ref[...] = acc_ref[...] + part
            carry_ref[...] = carry

        @pl.when(step_i == last)
        def _():
            o = o1_ref[...].astype(F32) + _collect_heads(acc_ref[...], n_heads, ds)
            o_ref[...] = o.astype(o_ref.dtype)

    @pl.when(jnp.logical_and(jnp.logical_not(needed), step_i == last))
    def _():
        o_ref[...] = o1_ref[...]


def _sb_dec_call(q, kn, vn, kct, vct):
    db, ds, width = q.shape
    n_heads = width // DH
    rows = ds * n_heads
    nkb = kct.shape[2] // TK_DEC
    bidx = lambda b: (b, 0, 0)
    tile = (None, width, TK_DEC)
    o1, carry, cmin = pl.pallas_call(
        functools.partial(_sb_dec_first_kernel, n_heads, ds),
        out_shape=[jax.ShapeDtypeStruct((db, ds, width), BF16),
                   jax.ShapeDtypeStruct((db, rows, 1), F32),
                   jax.ShapeDtypeStruct((db, 1, LANES), F32)],
        grid=(db,),
        in_specs=[pl.BlockSpec((None, ds, width), bidx)] * 3
        + [pl.BlockSpec(tile, lambda b: (b, 0, nkb - 1))] * 2,
        out_specs=[pl.BlockSpec((None, ds, width), bidx), pl.BlockSpec((None, rows, 1), bidx),
                   pl.BlockSpec((None, 1, LANES), bidx)],
        compiler_params=_cparams(("arbitrary",)),
        name="sb_decode_first",
    )(q, kn, vn, kct, vct)
    if nkb == 1:
        return o1
    need = (cmin[:, 0, 0] < UNDERFLOW_EXP).astype(jnp.int32)
    bidx2 = lambda b, s, need_ref: (b, 0, 0)

    def cache_idx(b, s, need_ref):
        on = need_ref[b] != 0
        return (jnp.where(on, b, 0), 0, jnp.where(on, nkb - 2 - s, 0))

    grid_spec = pltpu.PrefetchScalarGridSpec(
        num_scalar_prefetch=1,
        grid=(db, nkb - 1),
        in_specs=[pl.BlockSpec((None, ds, width), bidx2), pl.BlockSpec((None, rows, 1), bidx2),
                  pl.BlockSpec((None, ds, width), bidx2),
                  pl.BlockSpec(tile, cache_idx), pl.BlockSpec(tile, cache_idx)],
        out_specs=pl.BlockSpec((None, ds, width), bidx2),
        scratch_shapes=[pltpu.VMEM((rows, width), BF16), pltpu.VMEM((rows, width), F32),
                        pltpu.VMEM((rows, 1), F32)],
    )
    return pl.pallas_call(
        functools.partial(_sb_dec_rest_kernel, n_heads, ds),
        out_shape=jax.ShapeDtypeStruct((db, ds, width), BF16),
        grid_spec=grid_spec,
        compiler_params=_cparams(("arbitrary", "arbitrary")),
        name="sb_decode_rest",
    )(need, q, carry, o1, kct, vct)


def _post_kernel(n_act, final, f_chunk, *refs):
    acts = refs[:n_act]
    wo_ref, x_ref, gate1_ref, sh_ref, sc_ref, gate_ref, g_ref, wu_ref, wd_ref = refs[n_act:n_act + 9]
    fg_ref = refs[n_act + 9] if final else None
    o_ref = refs[-1]
    y = None
    off = 0
    for a_ref in acts:
        kdim = a_ref.shape[1]
        part = _dot(a_ref[...], wo_ref[off:off + kdim, :])
        y = part if y is None else y + part
        off += kdim
    x = x_ref[...] + gate1_ref[...] * y
    h = _rms_mod(x, g_ref[...], sc_ref[...], sh_ref[...]).astype(BF16)
    d_ff = wu_ref.shape[1]
    acc = None
    for c in range(d_ff // f_chunk):
        cs = slice(c * f_chunk, (c + 1) * f_chunk)
        a = jnp.maximum(_dot(h, wu_ref[:, cs]), 0.0)
        part = _dot((a * a).astype(BF16), wd_ref[cs, :])
        acc = part if acc is None else acc + part
    xo = x + gate_ref[...] * acc
    if final:
        y = xo * lax.rsqrt(jnp.mean(xo * xo, axis=-1, keepdims=True) + EPS)
        xo = y * fg_ref[...]
    o_ref[...] = xo


def _post_call(prompt, acts, wo, x2, mod_arr, g, wu, wd, final_g, rows_per_batch):
    m, d = x2.shape
    tm = TM_ROWS if prompt else m
    tpb = rows_per_batch // tm if prompt else 1
    row = lambda i: (i, 0)
    const2 = lambda i: (0, 0)
    final = final_g is not None
    resident = lambda w: pl.BlockSpec(w.shape, const2, pipeline_mode=pl.Buffered(1))
    in_specs = [pl.BlockSpec((tm, a.shape[1]), row) for a in acts]
    in_specs += [resident(wo), pl.BlockSpec((tm, d), row)]
    in_specs += [_mod_spec(prompt, tm, d, c, tpb) for c in (2, 3, 4, 5)]
    in_specs += [pl.BlockSpec((1, d), const2), resident(wu), resident(wd)]
    args = list(acts) + [wo, x2] + [mod_arr] * 4 + [g, wu, wd]
    if final:
        in_specs.append(pl.BlockSpec((1, d), const2))
        args.append(final_g)
    return pl.pallas_call(
        functools.partial(_post_kernel, len(acts), final, 1024),
        out_shape=jax.ShapeDtypeStruct((m, d), F32),
        grid=(m // tm,), in_specs=in_specs, out_specs=pl.BlockSpec((tm, d), row),
        compiler_params=_cparams(("arbitrary",)),
        name="post",
    )(*args)


def _sb_in_prompt_kernel(x_ref, sh_ref, sc_ref, g_ref, wq_ref, wkt_ref, wvt_ref,
                         q_ref, ktb_ref, vtb_ref, kt_ref, vt_ref):
    h = _rms_mod(x_ref[...], g_ref[...], sc_ref[...], sh_ref[...]).astype(BF16)
    q_ref[...] = (_dot(h, wq_ref[...]) * (DH ** -0.5)).astype(q_ref.dtype)
    kt = _dot_nt(wkt_ref[...], h)
    kt_ref[...] = kt
    ktb_ref[...] = kt.astype(BF16)
    vt = _dot_nt(wvt_ref[...], h)
    vt_ref[...] = vt
    vtb_ref[...] = vt.astype(BF16)


def _sb_in_sample_kernel(x_ref, sh_ref, sc_ref, g_ref, wq_ref, wkt_ref, wvt_ref,
                         q_ref, kb_ref, vb_ref, k32_ref, v32_ref):
    h = _rms_mod(x_ref[...], g_ref[...], sc_ref[...], sh_ref[...]).astype(BF16)
    q_ref[...] = (_dot(h, wq_ref[...]) * (DH ** -0.5)).astype(q_ref.dtype)
    k = _dot_nt(h, wkt_ref[...])
    k32_ref[...] = k
    kb_ref[...] = k.astype(BF16)
    v = _dot_nt(h, wvt_ref[...])
    v32_ref[...] = v
    vb_ref[...] = v.astype(BF16)


def _sb_in_call(prompt, x2, mod_arr, g, wts, rows_per_batch):
    wq, wkt, wvt = wts
    m, d = x2.shape
    nb = m // rows_per_batch
    tm = TM_ROWS if prompt else m
    tpb = rows_per_batch // tm if prompt else 1
    row = lambda i: (i, 0)
    const2 = lambda i: (0, 0)
    in_specs = [pl.BlockSpec((tm, d), row),
                _mod_spec(prompt, tm, d, 0, tpb), _mod_spec(prompt, tm, d, 1, tpb),
                pl.BlockSpec((1, d), const2)] + [pl.BlockSpec(w.shape, const2) for w in wts]
    if prompt:
        tcol = lambda i: (i // tpb, 0, i % tpb)
        out_shape = ([jax.ShapeDtypeStruct((m, d), BF16)]
                     + [jax.ShapeDtypeStruct((nb, d, rows_per_batch), BF16)] * 2
                     + [jax.ShapeDtypeStruct((nb, d, rows_per_batch), F32)] * 2)
        out_specs = [pl.BlockSpec((tm, d), row)] + [pl.BlockSpec((None, d, tm), tcol)] * 4
        body = _sb_in_prompt_kernel
    else:
        out_shape = [jax.ShapeDtypeStruct((m, d), BF16)] * 3 + [jax.ShapeDtypeStruct((m, d), F32)] * 2
        out_specs = [pl.BlockSpec((tm, d), row)] * 5
        body = _sb_in_sample_kernel
    return pl.pallas_call(
        body, out_shape=out_shape, grid=(m // tm,), in_specs=in_specs, out_specs=out_specs,
        compiler_params=_cparams(("arbitrary",)),
        name="sb_in_prompt" if prompt else "sb_in_sample",
    )(x2, mod_arr, mod_arr, g, wq, wkt, wvt)


def _to_state(xt, n_heads):
    nb, _, l = xt.shape
    return xt.reshape(nb, n_heads, DH, l).transpose(0, 3, 1, 2)[None]


def _from_cache(c):
    nb, p, n_heads, dh = c.shape
    return c.transpose(0, 2, 3, 1).reshape(nb, n_heads * dh, p)


def _trunk(prompt, x, mods, past, wts):
    nb, rpb, d = x.shape
    m = nb * rpb
    a_w = H_FOX * DH
    x2 = x.reshape(m, d)
    norm_g, ab_wts, w_out_ab, sb_wts, w_out_sb, w_up, w_down, final_g = wts

    if prompt:
        outs = _ab_in_call(True, x2, mods[0], norm_g[0, 0][None], ab_wts, None, rpb, 0)
        q, ktb, vtb, kt32, vt32, lft, fct, kaug, pooled, pstate = outs
        fq_arr = fct.reshape(nb, a_w // LANES, 2, rpb).transpose(0, 1, 3, 2)
        a_out = _fox_call(q.reshape(nb, rpb, a_w), ktb, vtb, kaug, fq_arr, fq_arr[:, :, T_ATT - 1::T_ATT, :])
        st_fox = (_to_state(kt32, H_FOX), _to_state(vt32, H_FOX), lft.transpose(0, 2, 1)[None])
    else:
        cfk, cfv, cflogf, spool, csk, csv = past
        p0 = cfk.shape[1]
        hist = jnp.pad(spool, ((0, 0), (HIST_ROWS - spool.shape[1], 0), (0, 0)))
        outs = _ab_in_call(False, x2, mods[0], norm_g[0, 0][None], ab_wts, hist, rpb, p0)
        q, kb, vb, k32, v32, lft, pooled, pstate = outs
        lfn_t = lft.reshape(H_FOX, nb, rpb).transpose(1, 0, 2)
        a_out = _fox_dec_call(q.reshape(nb, rpb, a_w), kb.reshape(nb, rpb, a_w), vb.reshape(nb, rpb, a_w),
                              _from_cache(cfk), _from_cache(cfv), lfn_t, cflogf.transpose(0, 2, 1))
        st_fox = (k32.reshape(1, nb, rpb, H_FOX, DH), v32.reshape(1, nb, rpb, H_FOX, DH),
                  lfn_t.transpose(0, 2, 1)[None])
    x2 = _post_call(prompt, [a_out.reshape(m, a_w), pooled], w_out_ab, x2, mods[0], norm_g[0, 1][None],
                    w_up[0], w_down[0], None, rpb)

    n_heads = d // DH
    outs = _sb_in_call(prompt, x2, mods[1], norm_g[1, 0][None], sb_wts, rpb)
    if prompt:
        q1, ktb1, vtb1, skt32, svt32 = outs
        s_out = _sb_call(q1.reshape(nb, rpb, d), ktb1, vtb1)
        st_sb = (_to_state(skt32, n_heads), _to_state(svt32, n_heads))
    else:
        q1, kb1, vb1, sk32, sv32 = outs
        s_out = _sb_dec_call(q1.reshape(nb, rpb, d), kb1.reshape(nb, rpb, d), vb1.reshape(nb, rpb, d),
                             _from_cache(csk), _from_cache(csv))
        st_sb = (sk32.reshape(1, nb, rpb, n_heads, DH), sv32.reshape(1, nb, rpb, n_heads, DH))
    y = _post_call(prompt, [s_out.reshape(m, d)], w_out_sb, x2, mods[1], norm_g[1, 1][None],
                   w_up[1], w_down[1], final_g[None], rpb)

    states = st_fox + (pstate[:, 1:, :][None],) + st_sb
    return y.reshape(nb, rpb, d), states


def kernel(x_prompt, x_sample, c_prompt, c_sample, cache_fox_k, cache_fox_v, cache_fox_logf, state_pool, cache_sb_k, cache_sb_v, w_ada, b_ada, norm_g, w_in_ab, b_forget, w_pool, pool_scale, w_out_ab, w_in_sb, w_out_sb, w_up, w_down, final_g):
    b, l, d = x_prompt.shape
    db, ds, _ = x_sample.shape
    a_w = H_FOX * DH

    c_all = jnp.concatenate([c_prompt, c_sample, jnp.zeros((ADA_ROWS - b - db, d), F32)], axis=0)
    mod = _ada_call(c_all, w_ada, b_ada)
    mods_p = [mod[i, :b][:, None, :] for i in range(mod.shape[0])]
    mods_s = [jnp.repeat(mod[i, b:b + db], ds, axis=0) for i in range(mod.shape[0])]

    w_in = w_in_ab[0]
    gate_pad = ((0, GATE_ROWS - H_FOX), (0, 0))
    ab_wts = (w_in[:, :a_w].astype(BF16),
              w_in[:, a_w:2 * a_w].T.astype(BF16),
              w_in[:, 2 * a_w:3 * a_w].T.astype(BF16),
              w_in[:, 3 * a_w + H_FOX:].astype(BF16),
              jnp.pad(w_in[:, 3 * a_w:3 * a_w + H_FOX].T, gate_pad).astype(BF16),
              jnp.pad(b_forget[0][:, None], gate_pad),
              w_pool[0].astype(BF16), pool_scale[0][None])
    w_sb = w_in_sb[0]
    sb_wts = (w_sb[:, :d].astype(BF16), w_sb[:, d:2 * d].T.astype(BF16), w_sb[:, 2 * d:].T.astype(BF16))
    wts = (norm_g, ab_wts, w_out_ab[0].astype(BF16), sb_wts, w_out_sb[0].astype(BF16),
           [w.astype(BF16) for w in w_up], [w.astype(BF16) for w in w_down], final_g)

    y_p, st_p = _trunk(True, x_prompt, mods_p, None, wts)
    past = (cache_fox_k[0], cache_fox_v[0], cache_fox_logf[0], state_pool[0],
            cache_sb_k[0], cache_sb_v[0])
    y_s, st_s = _trunk(False, x_sample, mods_s, past, wts)
    fk, fv, fl, pool, sk, sv = st_p
    fk2, fv2, fl2, pool2, sk2, sv2 = st_s
    return (y_p, y_s, fk, fv, fl, pool, sk, sv, fk2, fv2, fl2, pool2, sk2, sv2)
```

```python
import functools

import jax
import jax.numpy as jnp
from jax import lax
from jax.experimental import pallas as pl
from jax.experimental.pallas import tpu as pltpu

F32 = jnp.float32
BF16 = jnp.bfloat16

DH = 64
H_FOX = 8
EPS = 1e-6
POOL_WINDOWS = (2, 4, 8, 16)
POOL_GC = 128
HIST_ROWS = 16
NEG_BIG = -1e30
LOG2E = 1.4426950408889634
LANES = 128
GATE_ROWS = 16
AUG_ROWS = 16
OFF_SLOT = 12
VMEM_LIMIT = 56 * 1024 * 1024

UNDERFLOW_EXP = 88.0
NORM_MARGIN = 1.01

T_ATT = 256
NQ_ATT = 4
T_SB = 256
NQ_SB = 4
FIRST_SB = 2
TK_DEC_FIRST = 256
TK_DEC_REST = 1280
TK_DEC_FOX = 1024
TM_ROWS = 512
ADA_ROWS = 32
ADA_TN = 1536


def _cparams(sem):
    return pltpu.CompilerParams(dimension_semantics=sem, vmem_limit_bytes=VMEM_LIMIT)


def _rms_mod(x, g, sc, sh):
    y = x * lax.rsqrt(jnp.mean(x * x, axis=-1, keepdims=True) + EPS)
    return (y * g) * (1.0 + sc) + sh


def _softplus(z):
    return jnp.maximum(z, 0.0) + jnp.log(1.0 + jnp.exp2(jnp.abs(z) * (-LOG2E)))


def _log_sigmoid(x):
    return jnp.minimum(x, 0.0) - jnp.log(1.0 + jnp.exp(-jnp.abs(x)))


def _split3(x):
    hi = x.astype(BF16)
    r = x - hi.astype(F32)
    mid = r.astype(BF16)
    lo = (r - mid.astype(F32)).astype(BF16)
    return hi, mid, lo


def _dot(a, b):
    return jnp.dot(a, b, preferred_element_type=F32)


def _dot_nt(a, b):
    return lax.dot_general(a, b, (((1,), (1,)), ((), ())), preferred_element_type=F32)


def _dot3(x, m):
    hi, mid, lo = _split3(x)
    return _dot(hi, m) + _dot(mid, m) + _dot(lo, m)


def _iota(shape, dim):
    return lax.broadcasted_iota(jnp.int32, shape, dim)


def _log2(n):
    assert n & (n - 1) == 0, n
    return n.bit_length() - 1


def _ada_kernel(c_ref, w_ref, b_ref, o_ref):
    c = c_ref[...]
    cond = (c / (1.0 + jnp.exp(-c))).astype(BF16)
    o_ref[...] = _dot(cond, w_ref[...].astype(BF16)) + b_ref[...]


def _ada_call(c_all, w_ada, b_ada):
    depth, d, n = w_ada.shape
    return pl.pallas_call(
        _ada_kernel,
        out_shape=jax.ShapeDtypeStruct((depth, ADA_ROWS, n), F32),
        grid=(depth, n // ADA_TN),
        in_specs=[
            pl.BlockSpec((ADA_ROWS, d), lambda i, j: (0, 0)),
            pl.BlockSpec((None, d, ADA_TN), lambda i, j: (i, 0, j)),
            pl.BlockSpec((None, 1, ADA_TN), lambda i, j: (i, 0, j)),
        ],
        out_specs=pl.BlockSpec((None, ADA_ROWS, ADA_TN), lambda i, j: (i, 0, j)),
        compiler_params=_cparams(("arbitrary", "arbitrary")),
        name="ada_mod",
    )(c_all, w_ada, b_ada.reshape(depth, 1, n))


def _mod_spec(prompt, tm, d, chunk, tiles_per_batch):
    if prompt:
        return pl.BlockSpec((None, 1, d), lambda i: (i // tiles_per_batch, 0, chunk))
    return pl.BlockSpec((tm, d), lambda i: (i, chunk))


def _pool_branch(u, ext_ref, pos, ng, rg, wp_ref, ps_ref, pooled_ref, pst_ref):
    tm, a_w = u.shape
    u3 = u.reshape(ng, rg, a_w)
    ext_ref[:, HIST_ROWS:HIST_ROWS + rg, :] = u3
    pst_ref[...] = u3[:, rg - HIST_ROWS:rg, :]
    for g, w in enumerate(POOL_WINDOWS):
        cs = slice(g * POOL_GC, (g + 1) * POOL_GC)
        acc = u3[:, :, cs]
        for kk in range(1, w):
            acc = acc + ext_ref[:, pl.ds(HIST_ROWS - kk, rg), cs]
        cnt = jnp.minimum(pos + 1, w).astype(F32)
        pooled = acc.reshape(tm, POOL_GC) / cnt - u[:, cs]
        pg = _dot(pooled.astype(BF16), wp_ref[g]) * ps_ref[:, cs]
        pooled_ref[:, cs] = pg.astype(pooled_ref.dtype)


def _ab_in_prompt_kernel(tm, tiles_per_batch, x_ref, sh_ref, sc_ref, g_ref, wq_ref, wkt_ref, wvt_ref,
                         wu_ref, wft_ref, bf_ref, wp_ref, ps_ref,
                         q_ref, ktb_ref, vtb_ref, kt_ref, vt_ref, lft_ref, fcp_ref, kaug_ref, pooled_ref, pst_ref,
                         ext_ref, carry_ref):
    h = _rms_mod(x_ref[...], g_ref[...], sc_ref[...], sh_ref[...]).astype(BF16)
    q_ref[...] = (_dot(h, wq_ref[...]) * (DH ** -0.5)).astype(q_ref.dtype)
    kt = _dot_nt(wkt_ref[...], h)
    kt_ref[...] = kt
    ktb_ref[...] = kt.astype(BF16)
    vt = _dot_nt(wvt_ref[...], h)
    vt_ref[...] = vt
    vtb_ref[...] = vt.astype(BF16)

    tile_in_batch = pl.program_id(0) % tiles_per_batch
    first = tile_in_batch == 0
    logf = _log_sigmoid(_dot_nt(wft_ref[...], h) + bf_ref[...])
    lft_ref[...] = logf[0:H_FOX, :]
    tri = (_iota((tm, tm), 0) <= _iota((tm, tm), 1)).astype(BF16)
    fcum = _dot3(logf, tri) + jnp.where(first, 0.0, carry_ref[...])
    carry_ref[...] = fcum[:, tm - 1:tm]
    parts = _split3(fcum)
    prow = _iota((4 * H_FOX, GATE_ROWS), 0)
    pick = jnp.logical_and((prow & 7) < 2, _iota((4 * H_FOX, GATE_ROWS), 1) == (prow >> 3) * 2 + (prow & 7))
    pick = jnp.where(pick, 1.0, 0.0).astype(BF16)
    fcp_ref[...] = (_dot(pick, parts[0]) + _dot(pick, parts[1])) + _dot(pick, parts[2])

    n_aug = (H_FOX // 2) * AUG_ROWS
    slot = _iota((n_aug, GATE_ROWS), 0) & (AUG_ROWS - 1)
    head0 = (_iota((n_aug, GATE_ROWS), 0) >> _log2(AUG_ROWS)) * 2
    col = _iota((n_aug, GATE_ROWS), 1)
    kaug = None
    for part, piece in enumerate(parts):
        sel = jnp.logical_or(jnp.logical_and(slot == 3 + part, col == head0),
                             jnp.logical_and(slot == 9 + part, col == head0 + 1))
        term = _dot(jnp.where(sel, -1.0, 0.0).astype(BF16), piece)
        kaug = term if kaug is None else kaug + term
    slot1 = _iota((n_aug, 1), 0) & (AUG_ROWS - 1)
    ones = jnp.logical_or(slot1 < 3, jnp.logical_and(slot1 >= 6, slot1 < 9))
    kaug_ref[...] = (kaug + jnp.where(ones, 1.0, 0.0)).astype(BF16)

    @pl.when(first)
    def _():
        ext_ref[:, 0:HIST_ROWS, :] = jnp.zeros((1, HIST_ROWS, ext_ref.shape[2]), F32)

    @pl.when(jnp.logical_not(first))
    def _():
        ext_ref[:, 0:HIST_ROWS, :] = ext_ref[:, tm:tm + HIST_ROWS, :]

    pos = tile_in_batch * tm + _iota((tm, 1), 0)
    _pool_branch(_dot(h, wu_ref[...]), ext_ref, pos, 1, tm, wp_ref, ps_ref, pooled_ref, pst_ref)


def _ab_in_sample_kernel(ng, rg, p0, x_ref, sh_ref, sc_ref, g_ref, wq_ref, wkt_ref, wvt_ref,
                         wu_ref, wft_ref, bf_ref, wp_ref, ps_ref, hist_ref,
                         q_ref, kb_ref, vb_ref, k32_ref, v32_ref, lft_ref, pooled_ref, pst_ref, ext_ref):
    tm = ng * rg
    h = _rms_mod(x_ref[...], g_ref[...], sc_ref[...], sh_ref[...]).astype(BF16)
    q_ref[...] = (_dot(h, wq_ref[...]) * (DH ** -0.5)).astype(q_ref.dtype)
    k = _dot_nt(h, wkt_ref[...])
    k32_ref[...] = k
    kb_ref[...] = k.astype(BF16)
    v = _dot_nt(h, wvt_ref[...])
    v32_ref[...] = v
    vb_ref[...] = v.astype(BF16)
    logf = _log_sigmoid(_dot_nt(wft_ref[...], h) + bf_ref[...])
    lft_ref[...] = logf[0:H_FOX, :]
    ext_ref[:, 0:HIST_ROWS, :] = hist_ref[...]
    pos = p0 + (_iota((tm, 1), 0) & (rg - 1))
    _pool_branch(_dot(h, wu_ref[...]), ext_ref, pos, ng, rg, wp_ref, ps_ref, pooled_ref, pst_ref)


def _ab_in_call(prompt, x2, mod_arr, g, wts, hist, rows_per_batch, p0):
    wq, wkt, wvt, wu, wft, bf, wp, ps = wts
    m, d = x2.shape
    a_w = H_FOX * DH
    nb = m // rows_per_batch
    tm = TM_ROWS if prompt else m
    tpb = rows_per_batch // tm if prompt else 1
    row = lambda i: (i, 0)
    const2 = lambda i: (0, 0)
    in_specs = [pl.BlockSpec((tm, d), row), _mod_spec(prompt, tm, d, 0, tpb), _mod_spec(prompt, tm, d, 1, tpb),
                pl.BlockSpec((1, d), const2)]
    in_specs += [pl.BlockSpec(w.shape, const2) for w in (wq, wkt, wvt, wu, wft, bf)]
    in_specs += [pl.BlockSpec(wp.shape, lambda i: (0, 0, 0)), pl.BlockSpec(ps.shape, const2)]
    args = [x2, mod_arr, mod_arr, g, wq, wkt, wvt, wu, wft, bf, wp, ps]
    if prompt:
        tcol = lambda i: (i // tpb, 0, i % tpb)
        out_shape = [jax.ShapeDtypeStruct((m, a_w), BF16),
                     jax.ShapeDtypeStruct((nb, a_w, rows_per_batch), BF16),
                     jax.ShapeDtypeStruct((nb, a_w, rows_per_batch), BF16),
                     jax.ShapeDtypeStruct((nb, a_w, rows_per_batch), F32),
                     jax.ShapeDtypeStruct((nb, a_w, rows_per_batch), F32),
                     jax.ShapeDtypeStruct((nb, H_FOX, rows_per_batch), F32),
                     jax.ShapeDtypeStruct((nb, 4 * H_FOX, rows_per_batch), F32),
                     jax.ShapeDtypeStruct((nb, (H_FOX // 2) * AUG_ROWS, rows_per_batch), BF16),
                     jax.ShapeDtypeStruct((m, a_w), BF16),
                     jax.ShapeDtypeStruct((nb, HIST_ROWS, a_w), F32)]
        out_specs = ([pl.BlockSpec((tm, a_w), row)] + [pl.BlockSpec((None, a_w, tm), tcol)] * 4
                     + [pl.BlockSpec((None, H_FOX, tm), tcol), pl.BlockSpec((None, 4 * H_FOX, tm), tcol)]
                     + [pl.BlockSpec((None, (H_FOX // 2) * AUG_ROWS, tm), tcol)]
                     + [pl.BlockSpec((tm, a_w), row),
                        pl.BlockSpec((1, HIST_ROWS, a_w), lambda i: (i // tpb, 0, 0))])
        scratch = [pltpu.VMEM((1, HIST_ROWS + tm, a_w), F32), pltpu.VMEM((GATE_ROWS, 1), F32)]
        body = functools.partial(_ab_in_prompt_kernel, tm, tpb)
    else:
        in_specs.append(pl.BlockSpec(hist.shape, lambda i: (0, 0, 0)))
        args.append(hist)
        out_shape = [jax.ShapeDtypeStruct((m, a_w), BF16), jax.ShapeDtypeStruct((m, a_w), BF16),
                     jax.ShapeDtypeStruct((m, a_w), BF16), jax.ShapeDtypeStruct((m, a_w), F32),
                     jax.ShapeDtypeStruct((m, a_w), F32), jax.ShapeDtypeStruct((H_FOX, m), F32),
                     jax.ShapeDtypeStruct((m, a_w), BF16), jax.ShapeDtypeStruct((nb, HIST_ROWS, a_w), F32)]
        out_specs = ([pl.BlockSpec((tm, a_w), row)] * 5 + [pl.BlockSpec((H_FOX, tm), const2)]
                     + [pl.BlockSpec((tm, a_w), row), pl.BlockSpec((nb, HIST_ROWS, a_w), lambda i: (0, 0, 0))])
        scratch = [pltpu.VMEM((nb, HIST_ROWS + rows_per_batch, a_w), F32)]
        body = functools.partial(_ab_in_sample_kernel, nb, rows_per_batch, p0)
    return pl.pallas_call(
        body, out_shape=out_shape, grid=(m // tm,), in_specs=in_specs, out_specs=out_specs,
        scratch_shapes=scratch,
        compiler_params=_cparams(("arbitrary",)),
        name="ab_in_prompt" if prompt else "ab_in_sample",
    )(*args)


def _head_split(q2):
    lane = _iota((1, LANES), 1)
    zero = jnp.zeros_like(q2)
    return jnp.where(lane < DH, q2, zero), jnp.where(lane >= DH, q2, zero)


def _pair_sel():
    return (_iota((8, LANES), 0) == (_iota((8, LANES), 1) >> _log2(DH))).astype(BF16)


def _fox_kernel(q_ref, kt_ref, vt_ref, ka_ref, fr_ref, fall_ref, o_ref,
                m_ref, acc_ref, kn_ref):
    t, nq = T_ATT, NQ_ATT
    qg = pl.program_id(2)
    lane = _iota((1, LANES), 1)
    lane2 = _iota((1, 2), 1)

    @pl.when(qg == 0)
    def _():
        chunk = 4 * t

        def body(i, mx):
            ks = pl.multiple_of(i * chunk, chunk)
            kf = kt_ref[:, pl.ds(ks, chunk)].astype(F32)
            n2 = _dot(_pair_sel(), (kf * kf).astype(BF16))
            return jnp.maximum(mx, jnp.max(n2, axis=1, keepdims=True))

        mx = lax.fori_loop(0, kt_ref.shape[1] // chunk, body, jnp.zeros((8, 1), F32))
        kn_ref[...] = jnp.sqrt(mx)

    lane_a = _iota((1, AUG_ROWS), 1)
    row_v = _iota((LANES, 1), 0)
    qcat, bound0 = [], []
    for s in range(nq):
        q2 = q_ref[s * t:(s + 1) * t, :]
        eye = (_iota((t, t), 0) == _iota((t, t), 1)).astype(BF16)
        fq = sum(_dot_nt(eye, part) for part in _split3(fr_ref[:, s * t:(s + 1) * t]))[:, 0:2]
        per_head = []
        for j, qh in enumerate(_head_split(q2)):
            hi, mid, lo = (x.astype(F32) for x in _split3(fq[:, j:j + 1]))
            base = 6 * j
            qa = jnp.where(lane_a == base, hi, jnp.where(lane_a == base + 1, mid, jnp.where(
                lane_a == base + 2, lo,
                jnp.where(jnp.logical_and(lane_a >= base + 3, lane_a < base + 6), 1.0, 0.0))))
            qa = jnp.where(lane_a == OFF_SLOT, NEG_BIG, qa)
            per_head.append(jnp.concatenate([qh, qa.astype(BF16)], axis=1))
        qcat.append(per_head)
        qf = q2.astype(F32)
        qn2 = _dot_nt((qf * qf).astype(BF16), _pair_sel())
        qn = jnp.sqrt(jnp.max(qn2, axis=0, keepdims=True))
        qk = jnp.where(lane2 == 0, qn[:, 0:1] * kn_ref[0:1, :], qn[:, 1:2] * kn_ref[1:2, :])
        bound0.append(qk * NORM_MARGIN + fq[0:1, :])
    chains = [(s, j) for s in range(nq) for j in range(2)]
    rep = lambda m: jnp.concatenate([m] * (t // LANES), axis=1)

    def front(n, diag, chains=chains):
        kbs = [qg * nq + s - n for s in range(nq)]
        kss = [pl.multiple_of((kb if diag else jnp.maximum(kb, 0)) * t, t) for kb in kbs]
        causal = _iota((t, t), 1) <= _iota((t, t), 0)
        kcat = []
        for s in range(nq):
            ka = ka_ref[:, pl.ds(kss[s], t)]
            if not diag:
                off = jnp.where(kbs[s] >= 0, 0.0, 1.0).astype(BF16)
                ka = jnp.where(_iota((AUG_ROWS, 1), 0) == OFF_SLOT, off, ka)
            kcat.append(jnp.concatenate([kt_ref[:, pl.ds(kss[s], t)], ka], axis=0))
        sc = []
        for s, j in chains:
            x = _dot(qcat[s][j], kcat[s])
            sc.append(jnp.where(causal, x, NEG_BIG) if diag else x)
        vcat = []
        for s, j in chains:
            v2 = vt_ref[:, pl.ds(kss[s], t)]
            mine = (row_v < DH) if j == 0 else (row_v >= DH)
            vcat.append(jnp.where(mine, v2, jnp.ones_like(v2)))
        return sc, vcat

    (sc0, v0), (sc1, v1) = front(0, True), front(1, False)
    m01 = [jnp.broadcast_to(jnp.maximum(jnp.max(x0, axis=1, keepdims=True), jnp.max(x1, axis=1, keepdims=True)),
                            (t, LANES)) for x0, x1 in zip(sc0, sc1)]
    pv0 = [_dot_nt(jnp.exp(x - rep(m)).astype(BF16), vv) for x, m, vv in zip(sc0, m01, v0)]
    pv1 = [_dot_nt(jnp.exp(x - rep(m)).astype(BF16), vv) for x, m, vv in zip(sc1, m01, v1)]
    for i, (s, j) in enumerate(chains):
        acc_ref[2 * s + j] = pv0[i] + pv1[i]
        m_ref[2 * s + j] = m01[i]

    def step(n, chains):
        sc, vcat = front(n, False, chains)
        m_prev = [m_ref[2 * s + j] for s, j in chains]
        m_new = [jnp.maximum(mp, jnp.max(x, axis=1, keepdims=True)) for mp, x in zip(m_prev, sc)]
        pv = [_dot_nt(jnp.exp(x - rep(mn)).astype(BF16), vv) for x, mn, vv in zip(sc, m_new, vcat)]
        for i, (s, j) in enumerate(chains):
            idx = 2 * s + j
            acc_ref[idx] = jnp.exp(m_prev[i] - m_new[i]) * acc_ref[idx] + pv[i]
            m_ref[idx] = m_new[i]

    def more(n):
        go = [False, False]
        for s in range(nq):
            kb = qg * nq + s - n
            last = pl.multiple_of(jnp.maximum(kb, 0) * t + (t - LANES), LANES)
            ft = fall_ref[:, pl.ds(last, LANES)][:, LANES - 1:LANES]
            for j in range(2):
                top = jnp.max(bound0[s][:, j:j + 1] - ft[j:j + 1, :]) - jnp.min(m_ref[2 * s + j])
                go[j] = jnp.logical_or(go[j], jnp.logical_and(kb >= 0, top > -UNDERFLOW_EXP))
        return go

    def body(chains, c):
        n = c[0]
        step(n, chains)
        return (n + 1, *more(n + 1))

    c = lax.while_loop(lambda c: jnp.logical_and(c[1], c[2]), functools.partial(body, chains),
                       (jnp.int32(2), *more(2)))
    c = lax.while_loop(lambda c: c[1], functools.partial(body, [(s, 0) for s in range(nq)]), c)
    lax.while_loop(lambda c: c[2], functools.partial(body, [(s, 1) for s in range(nq)]), c)
    for s in range(nq):
        a0, a1 = acc_ref[2 * s], acc_ref[2 * s + 1]
        o = jnp.where(lane < DH, a0 / pltpu.roll(a0, DH, 1), a1 / pltpu.roll(a1, DH, 1))
        o_ref[s * t:(s + 1) * t, :] = o.astype(o_ref.dtype)


def _fox_call(q, ktb, vtb, kaug, fcp):
    b, l, a_w = q.shape
    hp = a_w // LANES
    t, nq = T_ATT, NQ_ATT
    tq = t * nq
    return pl.pallas_call(
        _fox_kernel,
        out_shape=jax.ShapeDtypeStruct((b, l, a_w), BF16),
        grid=(b, hp, l // tq),
        in_specs=[
            pl.BlockSpec((None, tq, LANES), lambda bi, h, qi: (bi, qi, h)),
            pl.BlockSpec((None, LANES, l), lambda bi, h, qi: (bi, h, 0)),
            pl.BlockSpec((None, LANES, l), lambda bi, h, qi: (bi, h, 0)),
            pl.BlockSpec((None, AUG_ROWS, l), lambda bi, h, qi: (bi, h, 0)),
            pl.BlockSpec((None, 8, tq), lambda bi, h, qi: (bi, h, qi)),
            pl.BlockSpec((None, 8, l), lambda bi, h, qi: (bi, h, 0)),
        ],
        out_specs=pl.BlockSpec((None, tq, LANES), lambda bi, h, qi: (bi, qi, h)),
        scratch_shapes=[pltpu.VMEM((2 * nq, t, LANES), F32), pltpu.VMEM((2 * nq, t, LANES), F32),
                        pltpu.VMEM((8, 1), F32)],
        compiler_params=_cparams(("arbitrary", "arbitrary", "arbitrary")),
        name="fox_prompt",
    )(q, ktb, vtb, kaug, fcp, fcp)


def _sb_kernel(q_ref, kt_ref, vt_ref, o_ref, carry_ref, acc_ref):
    t, nq = T_SB, NQ_SB
    qg = pl.program_id(2)
    lane = _iota((1, LANES), 1)
    qh = [_head_split(q_ref[s * t:(s + 1) * t, :]) for s in range(nq)]
    chains = [(s, j) for s in range(nq) for j in range(2)]
    valid = _iota((t, t), 1) < _iota((t, t), 0)

    def front(n, diag):
        tri = (_iota((t, t), 0) > _iota((t, t), 1)).astype(BF16)
        kbs = [qg * nq + s - n for s in range(nq)]
        kss = [pl.multiple_of((kb if diag else jnp.maximum(kb, 0)) * t, t) for kb in kbs]
        z = [_dot(qh[s][j], kt_ref[:, pl.ds(kss[s], t)]) for s, j in chains]
        sp = [_softplus(zz) for zz in z]
        spm = [jnp.where(valid, x, 0.0) for x in sp] if diag else sp
        later = [_dot(x.astype(BF16), tri) for x in spm]
        vs = []
        for s in range(nq):
            v2 = vt_ref[:, pl.ds(kss[s], t)]
            if not diag:
                v2 = jnp.where(kbs[s] >= 0, v2, jnp.zeros_like(v2))
            vs.append(v2)
        ls = [zz - x for zz, x in zip(z, sp)]
        return ls, spm, later, vs

    def back(fr, diag, carry):
        ls, spm, later, vs = fr
        pv, new_carry = [], []
        for i, (s, j) in enumerate(chains):
            e = ls[i] - later[i]
            a = jnp.exp(e if carry is None else e - carry[i])
            if diag:
                a = jnp.where(valid, a, 0.0)
            pv.append(_dot_nt(a.astype(BF16), vs[s]))
            rs = jnp.sum(spm[i], axis=1, keepdims=True)
            new_carry.append(rs if carry is None else carry[i] + rs)
        return pv, new_carry

    fronts = [front(n, n == 0) for n in range(FIRST_SB)]
    total, carry = None, None
    for n, fr in enumerate(fronts):
        pv, carry = back(fr, n == 0, carry)
        total = pv if total is None else [a + b for a, b in zip(total, pv)]
    for i, (s, j) in enumerate(chains):
        acc_ref[2 * s + j] = total[i]
        carry_ref[2 * s + j] = carry[i]

    def more(n):
        in_range = n <= qg * nq + nq - 1
        return jnp.logical_and(in_range, jnp.min(carry_ref[...]) < UNDERFLOW_EXP)

    def body(c):
        n, _ = c
        pv, carry = back(front(n, False), False, [carry_ref[2 * s + j] for s, j in chains])
        for i, (s, j) in enumerate(chains):
            acc_ref[2 * s + j] = acc_ref[2 * s + j] + pv[i]
            carry_ref[2 * s + j] = carry[i]
        return n + 1, more(n + 1)

    lax.while_loop(lambda c: c[1], body, (jnp.int32(FIRST_SB), more(FIRST_SB)))
    for s in range(nq):
        o = jnp.where(lane < DH, acc_ref[2 * s], acc_ref[2 * s + 1])
        o_ref[s * t:(s + 1) * t, :] = o.astype(o_ref.dtype)


def _sb_call(q, ktb, vtb):
    b, l, d = q.shape
    hp = d // LANES
    t, nq = T_SB, NQ_SB
    tq = t * nq
    return pl.pallas_call(
        _sb_kernel,
        out_shape=jax.ShapeDtypeStruct((b, l, d), BF16),
        grid=(b, hp, l // tq),
        in_specs=[
            pl.BlockSpec((None, tq, LANES), lambda bi, h, qi: (bi, qi, h)),
            pl.BlockSpec((None, LANES, l), lambda bi, h, qi: (bi, h, 0)),
            pl.BlockSpec((None, LANES, l), lambda bi, h, qi: (bi, h, 0)),
        ],
        out_specs=pl.BlockSpec((None, tq, LANES), lambda bi, h, qi: (bi, qi, h)),
        scratch_shapes=[pltpu.VMEM((2 * nq, t, 1), F32), pltpu.VMEM((2 * nq, t, LANES), F32)],
        compiler_params=_cparams(("arbitrary", "arbitrary", "arbitrary")),
        name="sb_prompt",
    )(q, ktb, vtb)


def _rep_matrix(rows, n_heads, ds):
    return ((_iota((rows, ds), 0) >> _log2(n_heads)) == _iota((rows, ds), 1)).astype(BF16)


def _head_mask(rows, n_heads, width):
    return (_iota((rows, width), 0) & (n_heads - 1)) == (_iota((rows, width), 1) >> _log2(DH))


def _block_diag_q(q, n_heads):
    ds, width = q.shape
    rows = ds * n_heads
    qrep = _dot(_rep_matrix(rows, n_heads, ds), q)
    return jnp.where(_head_mask(rows, n_heads, width), qrep, 0.0).astype(BF16)


def _collect_heads(acc, n_heads, ds):
    rows, width = acc.shape
    om = jnp.where(_head_mask(rows, n_heads, width), acc, 0.0).astype(BF16)
    rep_t = ((_iota((ds, rows), 1) >> _log2(n_heads)) == _iota((ds, rows), 0)).astype(BF16)
    return _dot(rep_t, om)


def _suffix_sums(x, exact=True, blk=2 * LANES):
    r, n = x.shape
    parts = [x[:, b * blk:(b + 1) * blk] for b in range(n // blk)]
    tri = (_iota((blk, blk), 0) > _iota((blk, blk), 1)).astype(BF16)
    stacked = jnp.concatenate(parts, axis=0)
    w = _dot3(stacked, tri) if exact else _dot(stacked.astype(BF16), tri)
    out, tail = [], None
    for b in reversed(range(len(parts))):
        cur = w[b * r:(b + 1) * r]
        out.append(cur if tail is None else cur + tail)
        tot = jnp.sum(parts[b], axis=1, keepdims=True)
        tail = tot if tail is None else tail + tot
    return jnp.concatenate(out[::-1], axis=1), tail


def _fox_dec_kernel(n_heads, ds, q_ref, kn_ref, vn_ref, lfn_ref, kc_ref, vc_ref, lfc_ref,
                    o_ref, qbd_ref, m_ref, l_ref, acc_ref, cf_ref):
    rows = ds * n_heads
    step_i = pl.program_id(1)
    n_steps = pl.num_programs(1)

    def update(s, pv_fn):
        m_prev = m_ref[...]
        m_new = jnp.maximum(m_prev, jnp.max(s, axis=1, keepdims=True))
        alpha = jnp.exp(m_prev - m_new)
        p = jnp.exp(s - m_new)
        l_ref[...] = alpha * l_ref[...] + jnp.sum(p, axis=1, keepdims=True)
        acc_ref[...] = alpha * acc_ref[...] + pv_fn(p.astype(BF16))
        m_ref[...] = m_new

    @pl.when(step_i == 0)
    def _():
        qbd_ref[...] = _block_diag_q(q_ref[...], n_heads)
        m_ref[...] = jnp.full(m_ref.shape, NEG_BIG, F32)
        l_ref[...] = jnp.zeros(l_ref.shape, F32)
        acc_ref[...] = jnp.zeros(acc_ref.shape, F32)
        a = jnp.concatenate([lfn_ref[...]] * ds, axis=0)
        ri = _iota((rows, ds), 0) >> _log2(n_heads)
        cj = _iota((rows, ds), 1)
        am = jnp.where(cj <= ri, a, 0.0)
        tri = (_iota((ds, ds), 0) > _iota((ds, ds), 1)).astype(BF16)
        s = _dot_nt(qbd_ref[...], kn_ref[...]) + _dot3(am, tri)
        s = jnp.where(cj <= ri, s, NEG_BIG)
        update(s, lambda p: _dot(p, vn_ref[...]))
        cf_ref[...] = jnp.sum(am, axis=1, keepdims=True)

    @pl.when(step_i > 0)
    def _():
        reps = rows // (2 * n_heads)
        later, total = _suffix_sums(jnp.concatenate([lfc_ref[...]] * 2, axis=0))
        cf = cf_ref[...]
        s = _dot(qbd_ref[...], kc_ref[...].astype(BF16)) + (jnp.concatenate([later] * reps, axis=0) + cf)
        update(s, lambda p: _dot_nt(p, vc_ref[...].astype(BF16)))
        cf_ref[...] = cf + jnp.concatenate([total] * reps, axis=0)

    @pl.when(step_i == n_steps - 1)
    def _():
        o_ref[...] = _collect_heads(acc_ref[...] / l_ref[...], n_heads, ds).astype(o_ref.dtype)


def _fox_dec_call(q, kn, vn, kct, vct, lfn_t, lfc_t):
    db, ds, width = q.shape
    n_heads = width // DH
    rows = ds * n_heads
    tk = min(TK_DEC_FOX, kct.shape[2])
    nkb = kct.shape[2] // tk
    cache_blk = lambda s: jnp.minimum(nkb - s, nkb - 1)
    new_idx = lambda b, s: (b, 0, 0)
    cache_idx = lambda b, s: (b, 0, cache_blk(s))
    in_specs = ([pl.BlockSpec((None, ds, width), new_idx)] * 3
                + [pl.BlockSpec((None, n_heads, ds), new_idx)]
                + [pl.BlockSpec((None, width, tk), cache_idx)] * 2
                + [pl.BlockSpec((None, n_heads, tk), cache_idx)])
    scratch = [pltpu.VMEM((rows, width), BF16), pltpu.VMEM((rows, 1), F32), pltpu.VMEM((rows, 1), F32),
               pltpu.VMEM((rows, width), F32), pltpu.VMEM((rows, 1), F32)]
    return pl.pallas_call(
        functools.partial(_fox_dec_kernel, n_heads, ds),
        out_shape=jax.ShapeDtypeStruct((db, ds, width), BF16),
        grid=(db, nkb + 1),
        in_specs=in_specs,
        out_specs=pl.BlockSpec((None, ds, width), new_idx),
        scratch_shapes=scratch,
        compiler_params=_cparams(("arbitrary", "arbitrary")),
        name="fox_decode",
    )(q, kn, vn, lfn_t, kct, vct, lfc_t)


def _sb_dec_tile(qbd, kc_ref, vc_ref, carry):
    z = _dot(qbd, kc_ref[...].astype(BF16))
    sp = _softplus(z)
    later, total = _suffix_sums(sp, exact=False)
    a = jnp.exp((z - sp) - later - carry)
    return _dot_nt(a.astype(BF16), vc_ref[...].astype(BF16)), carry + total


def _sb_dec_first_kernel(n_heads, ds, q_ref, kn_ref, vn_ref, kc_ref, vc_ref,
                         o_ref, carry_out_ref, cmin_ref):
    rows = ds * n_heads
    qbd = _block_diag_q(q_ref[...], n_heads)
    ri = _iota((rows, ds), 0) >> _log2(n_heads)
    cj = _iota((rows, ds), 1)
    valid = cj < ri
    z = _dot_nt(qbd, kn_ref[...])
    sp = _softplus(z)
    spm = jnp.where(valid, sp, 0.0)
    tri = (_iota((ds, ds), 0) > _iota((ds, ds), 1)).astype(BF16)
    later = _dot(spm.astype(BF16), tri)
    a = jnp.where(valid, jnp.exp((z - sp) - later), 0.0)
    acc = _dot(a.astype(BF16), vn_ref[...])
    carry = jnp.sum(spm, axis=1, keepdims=True)
    part, carry = _sb_dec_tile(qbd, kc_ref, vc_ref, carry)
    o_ref[...] = _collect_heads(acc + part, n_heads, ds).astype(o_ref.dtype)
    carry_out_ref[...] = carry
    cmin_ref[...] = jnp.broadcast_to(jnp.min(carry, axis=0, keepdims=True), cmin_ref.shape)


def _sb_dec_rest_kernel(n_heads, ds, need_ref, q_ref, carry_in_ref, o1_ref, kc_ref, vc_ref,
                        o_ref, qbd_ref, acc_ref, carry_ref):
    step_i = pl.program_id(1)
    last = pl.num_programs(1) - 1
    needed = need_ref[pl.program_id(0)] != 0

    @pl.when(needed)
    def _():
        @pl.when(step_i == 0)
        def _():
            qbd_ref[...] = _block_diag_q(q_ref[...], n_heads)
            acc_ref[...] = jnp.zeros(acc_ref.shape, F32)
            carry_ref[...] = carry_in_ref[...]

        @pl.when(jnp.min(carry_ref[...]) < UNDERFLOW_EXP)
        def _():
            part, carry = _sb_dec_tile(qbd_ref[...], kc_ref, vc_ref, carry_ref[...])
            acc_ref[...] = acc_ref[...] + part
            carry_ref[...] = carry

        @pl.when(step_i == last)
        def _():
            o = o1_ref[...].astype(F32) + _collect_heads(acc_ref[...], n_heads, ds)
            o_ref[...] = o.astype(o_ref.dtype)

    @pl.when(jnp.logical_and(jnp.logical_not(needed), step_i == last))
    def _():
        o_ref[...] = o1_ref[...]


def _sb_dec_call(q, kn, vn, kct, vct):
    db, ds, width = q.shape
    n_heads = width // DH
    rows = ds * n_heads
    past = kct.shape[2]
    first = min(TK_DEC_FIRST, past)
    n_blk = (past - first) // (2 * LANES)
    per = max([k for k in range(1, TK_DEC_REST // (2 * LANES) + 1) if n_blk % k == 0], default=1)
    rest = per * 2 * LANES
    n_rest = (past - first) // rest
    bidx = lambda b: (b, 0, 0)
    o1, carry, cmin = pl.pallas_call(
        functools.partial(_sb_dec_first_kernel, n_heads, ds),
        out_shape=[jax.ShapeDtypeStruct((db, ds, width), BF16),
                   jax.ShapeDtypeStruct((db, rows, 1), F32),
                   jax.ShapeDtypeStruct((db, 1, LANES), F32)],
        grid=(db,),
        in_specs=[pl.BlockSpec((None, ds, width), bidx)] * 3
        + [pl.BlockSpec((None, width, first), lambda b: (b, 0, past // first - 1))] * 2,
        out_specs=[pl.BlockSpec((None, ds, width), bidx), pl.BlockSpec((None, rows, 1), bidx),
                   pl.BlockSpec((None, 1, LANES), bidx)],
        compiler_params=_cparams(("arbitrary",)),
        name="sb_decode_first",
    )(q, kn, vn, kct, vct)
    if n_rest == 0:
        return o1
    need = (cmin[:, 0, 0] < UNDERFLOW_EXP).astype(jnp.int32)
    bidx2 = lambda b, s, need_ref: (b, 0, 0)
    tile = (None, width, rest)

    def cache_idx(b, s, need_ref):
        on = need_ref[b] != 0
        return (jnp.where(on, b, 0), 0, jnp.where(on, n_rest - 1 - s, 0))

    grid_spec = pltpu.PrefetchScalarGridSpec(
        num_scalar_prefetch=1,
        grid=(db, n_rest),
        in_specs=[pl.BlockSpec((None, ds, width), bidx2), pl.BlockSpec((None, rows, 1), bidx2),
                  pl.BlockSpec((None, ds, width), bidx2),
                  pl.BlockSpec(tile, cache_idx), pl.BlockSpec(tile, cache_idx)],
        out_specs=pl.BlockSpec((None, ds, width), bidx2),
        scratch_shapes=[pltpu.VMEM((rows, width), BF16), pltpu.VMEM((rows, width), F32),
                        pltpu.VMEM((rows, 1), F32)],
    )
    return pl.pallas_call(
        functools.partial(_sb_dec_rest_kernel, n_heads, ds),
        out_shape=jax.ShapeDtypeStruct((db, ds, width), BF16),
        grid_spec=grid_spec,
        compiler_params=_cparams(("arbitrary", "arbitrary")),
        name="sb_decode_rest",
    )(need, q, carry, o1, kct, vct)


def _post_kernel(n_act, final, f_chunk, *refs):
    acts = refs[:n_act]
    wo_ref, x_ref, gate1_ref, sh_ref, sc_ref, gate_ref, g_ref, wu_ref, wd_ref = refs[n_act:n_act + 9]
    fg_ref = refs[n_act + 9] if final else None
    o_ref = refs[-1]
    y = None
    off = 0
    for a_ref in acts:
        kdim = a_ref.shape[1]
        part = _dot(a_ref[...], wo_ref[off:off + kdim, :])
        y = part if y is None else y + part
        off += kdim
    x = x_ref[...] + gate1_ref[...] * y
    h = _rms_mod(x, g_ref[...], sc_ref[...], sh_ref[...]).astype(BF16)
    d_ff = wu_ref.shape[1]
    acc = None
    for c in range(d_ff // f_chunk):
        cs = slice(c * f_chunk, (c + 1) * f_chunk)
        a = jnp.maximum(_dot(h, wu_ref[:, cs]), 0.0)
        part = _dot((a * a).astype(BF16), wd_ref[cs, :])
        acc = part if acc is None else acc + part
    xo = x + gate_ref[...] * acc
    if final:
        y = xo * lax.rsqrt(jnp.mean(xo * xo, axis=-1, keepdims=True) + EPS)
        xo = y * fg_ref[...]
    o_ref[...] = xo


def _post_call(prompt, acts, wo, x2, mod_arr, g, wu, wd, final_g, rows_per_batch):
    m, d = x2.shape
    tm = TM_ROWS if prompt else m
    tpb = rows_per_batch // tm if prompt else 1
    row = lambda i: (i, 0)
    const2 = lambda i: (0, 0)
    final = final_g is not None
    resident = lambda w: pl.BlockSpec(w.shape, const2, pipeline_mode=pl.Buffered(1))
    in_specs = [pl.BlockSpec((tm, a.shape[1]), row) for a in acts]
    in_specs += [resident(wo), pl.BlockSpec((tm, d), row)]
    in_specs += [_mod_spec(prompt, tm, d, c, tpb) for c in (2, 3, 4, 5)]
    in_specs += [pl.BlockSpec((1, d), const2), resident(wu), resident(wd)]
    args = list(acts) + [wo, x2] + [mod_arr] * 4 + [g, wu, wd]
    if final:
        in_specs.append(pl.BlockSpec((1, d), const2))
        args.append(final_g)
    return pl.pallas_call(
        functools.partial(_post_kernel, len(acts), final, 1024),
        out_shape=jax.ShapeDtypeStruct((m, d), F32),
        grid=(m // tm,), in_specs=in_specs, out_specs=pl.BlockSpec((tm, d), row),
        compiler_params=_cparams(("arbitrary",)),
        name="post",
    )(*args)


def _sb_in_prompt_kernel(x_ref, sh_ref, sc_ref, g_ref, wq_ref, wkt_ref, wvt_ref,
                         q_ref, ktb_ref, vtb_ref, kt_ref, vt_ref):
    h = _rms_mod(x_ref[...], g_ref[...], sc_ref[...], sh_ref[...]).astype(BF16)
    q_ref[...] = (_dot(h, wq_ref[...]) * (DH ** -0.5)).astype(q_ref.dtype)
    kt = _dot_nt(wkt_ref[...], h)
    kt_ref[...] = kt
    ktb_ref[...] = kt.astype(BF16)
    vt = _dot_nt(wvt_ref[...], h)
    vt_ref[...] = vt
    vtb_ref[...] = vt.astype(BF16)


def _sb_in_sample_kernel(x_ref, sh_ref, sc_ref, g_ref, wq_ref, wkt_ref, wvt_ref,
                         q_ref, kb_ref, vb_ref, k32_ref, v32_ref):
    h = _rms_mod(x_ref[...], g_ref[...], sc_ref[...], sh_ref[...]).astype(BF16)
    q_ref[...] = (_dot(h, wq_ref[...]) * (DH ** -0.5)).astype(q_ref.dtype)
    k = _dot_nt(h, wkt_ref[...])
    k32_ref[...] = k
    kb_ref[...] = k.astype(BF16)
    v = _dot_nt(h, wvt_ref[...])
    v32_ref[...] = v
    vb_ref[...] = v.astype(BF16)


def _sb_in_call(prompt, x2, mod_arr, g, wts, rows_per_batch):
    wq, wkt, wvt = wts
    m, d = x2.shape
    nb = m // rows_per_batch
    tm = TM_ROWS if prompt else m
    tpb = rows_per_batch // tm if prompt else 1
    row = lambda i: (i, 0)
    const2 = lambda i: (0, 0)
    in_specs = [pl.BlockSpec((tm, d), row),
                _mod_spec(prompt, tm, d, 0, tpb), _mod_spec(prompt, tm, d, 1, tpb),
                pl.BlockSpec((1, d), const2)] + [pl.BlockSpec(w.shape, const2) for w in wts]
    if prompt:
        tcol = lambda i: (i // tpb, 0, i % tpb)
        out_shape = ([jax.ShapeDtypeStruct((m, d), BF16)]
                     + [jax.ShapeDtypeStruct((nb, d, rows_per_batch), BF16)] * 2
                     + [jax.ShapeDtypeStruct((nb, d, rows_per_batch), F32)] * 2)
        out_specs = [pl.BlockSpec((tm, d), row)] + [pl.BlockSpec((None, d, tm), tcol)] * 4
        body = _sb_in_prompt_kernel
    else:
        out_shape = [jax.ShapeDtypeStruct((m, d), BF16)] * 3 + [jax.ShapeDtypeStruct((m, d), F32)] * 2
        out_specs = [pl.BlockSpec((tm, d), row)] * 5
        body = _sb_in_sample_kernel
    return pl.pallas_call(
        body, out_shape=out_shape, grid=(m // tm,), in_specs=in_specs, out_specs=out_specs,
        compiler_params=_cparams(("arbitrary",)),
        name="sb_in_prompt" if prompt else "sb_in_sample",
    )(x2, mod_arr, mod_arr, g, wq, wkt, wvt)


def _to_state(xt, n_heads):
    nb, _, l = xt.shape
    return xt.reshape(nb, n_heads, DH, l).transpose(0, 3, 1, 2)[None]


def _from_cache(c):
    nb, p, n_heads, dh = c.shape
    return c.transpose(0, 2, 3, 1).reshape(nb, n_heads * dh, p)


def _trunk(prompt, x, mods, past, wts):
    nb, rpb, d = x.shape
    m = nb * rpb
    a_w = H_FOX * DH
    x2 = x.reshape(m, d)
    norm_g, ab_wts, w_out_ab, sb_wts, w_out_sb, w_up, w_down, final_g = wts

    if prompt:
        outs = _ab_in_call(True, x2, mods[0], norm_g[0, 0][None], ab_wts, None, rpb, 0)
        q, ktb, vtb, kt32, vt32, lft, fcp, kaug, pooled, pstate = outs
        a_out = _fox_call(q.reshape(nb, rpb, a_w), ktb, vtb, kaug, fcp)
        st_fox = (_to_state(kt32, H_FOX), _to_state(vt32, H_FOX), lft.transpose(0, 2, 1)[None])
    else:
        cfk, cfv, cflogf, spool, csk, csv = past
        p0 = cfk.shape[1]
        hist = jnp.pad(spool, ((0, 0), (HIST_ROWS - spool.shape[1], 0), (0, 0)))
        outs = _ab_in_call(False, x2, mods[0], norm_g[0, 0][None], ab_wts, hist, rpb, p0)
        q, kb, vb, k32, v32, lft, pooled, pstate = outs
        lfn_t = lft.reshape(H_FOX, nb, rpb).transpose(1, 0, 2)
        a_out = _fox_dec_call(q.reshape(nb, rpb, a_w), kb.reshape(nb, rpb, a_w), vb.reshape(nb, rpb, a_w),
                              _from_cache(cfk), _from_cache(cfv), lfn_t, cflogf.transpose(0, 2, 1))
        st_fox = (k32.reshape(1, nb, rpb, H_FOX, DH), v32.reshape(1, nb, rpb, H_FOX, DH),
                  lfn_t.transpose(0, 2, 1)[None])
    x2 = _post_call(prompt, [a_out.reshape(m, a_w), pooled], w_out_ab, x2, mods[0], norm_g[0, 1][None],
                    w_up[0], w_down[0], None, rpb)

    n_heads = d // DH
    outs = _sb_in_call(prompt, x2, mods[1], norm_g[1, 0][None], sb_wts, rpb)
    if prompt:
        q1, ktb1, vtb1, skt32, svt32 = outs
        s_out = _sb_call(q1.reshape(nb, rpb, d), ktb1, vtb1)
        st_sb = (_to_state(skt32, n_heads), _to_state(svt32, n_heads))
    else:
        q1, kb1, vb1, sk32, sv32 = outs
        s_out = _sb_dec_call(q1.reshape(nb, rpb, d), kb1.reshape(nb, rpb, d), vb1.reshape(nb, rpb, d),
                             _from_cache(csk), _from_cache(csv))
        st_sb = (sk32.reshape(1, nb, rpb, n_heads, DH), sv32.reshape(1, nb, rpb, n_heads, DH))
    y = _post_call(prompt, [s_out.reshape(m, d)], w_out_sb, x2, mods[1], norm_g[1, 1][None],
                   w_up[1], w_down[1], final_g[None], rpb)

    states = st_fox + (pstate[:, 1:, :][None],) + st_sb
    return y.reshape(nb, rpb, d), states


def kernel(x_prompt, x_sample, c_prompt, c_sample, cache_fox_k, cache_fox_v, cache_fox_logf, state_pool, cache_sb_k, cache_sb_v, w_ada, b_ada, norm_g, w_in_ab, b_forget, w_pool, pool_scale, w_out_ab, w_in_sb, w_out_sb, w_up, w_down, final_g):
    b, l, d = x_prompt.shape
    db, ds, _ = x_sample.shape
    a_w = H_FOX * DH

    c_all = jnp.concatenate([c_prompt, c_sample, jnp.zeros((ADA_ROWS - b - db, d), F32)], axis=0)
    mod = _ada_call(c_all, w_ada, b_ada)
    mods_p = [mod[i, :b][:, None, :] for i in range(mod.shape[0])]
    mods_s = [jnp.repeat(mod[i, b:b + db], ds, axis=0) for i in range(mod.shape[0])]

    w_in = w_in_ab[0]
    gate_pad = ((0, GATE_ROWS - H_FOX), (0, 0))
    ab_wts = (w_in[:, :a_w].astype(BF16),
              w_in[:, a_w:2 * a_w].T.astype(BF16),
              w_in[:, 2 * a_w:3 * a_w].T.astype(BF16),
              w_in[:, 3 * a_w + H_FOX:].astype(BF16),
              jnp.pad(w_in[:, 3 * a_w:3 * a_w + H_FOX].T, gate_pad).astype(BF16),
              jnp.pad(b_forget[0][:, None], gate_pad),
              w_pool[0].astype(BF16), pool_scale[0][None])
    w_sb = w_in_sb[0]
    sb_wts = (w_sb[:, :d].astype(BF16), w_sb[:, d:2 * d].T.astype(BF16), w_sb[:, 2 * d:].T.astype(BF16))
    wts = (norm_g, ab_wts, w_out_ab[0].astype(BF16), sb_wts, w_out_sb[0].astype(BF16),
           [w.astype(BF16) for w in w_up], [w.astype(BF16) for w in w_down], final_g)

    y_p, st_p = _trunk(True, x_prompt, mods_p, None, wts)
    past = (cache_fox_k[0], cache_fox_v[0], cache_fox_logf[0], state_pool[0],
            cache_sb_k[0], cache_sb_v[0])
    y_s, st_s = _trunk(False, x_sample, mods_s, past, wts)
    fk, fv, fl, pool, sk, sv = st_p
    fk2, fv2, fl2, pool2, sk2, sv2 = st_s
    return (y_p, y_s, fk, fv, fl, pool, sk, sv, fk2, fv2, fl2, pool2, sk2, sv2)
```

```python
import functools

import jax
import jax.numpy as jnp
from jax import lax
from jax.experimental import pallas as pl
from jax.experimental.pallas import tpu as pltpu

F32 = jnp.float32
BF16 = jnp.bfloat16

DH = 64
H_FOX = 8
EPS = 1e-6
POOL_WINDOWS = (2, 4, 8, 16)
POOL_GC = 128
HIST_ROWS = 16
NEG_BIG = -1e30
LOG2E = 1.4426950408889634
LANES = 128
GATE_ROWS = 16
AUG_ROWS = 16
OFF_SLOT = 12
VMEM_LIMIT = 56 * 1024 * 1024

UNDERFLOW_EXP = 88.0
NORM_MARGIN = 1.01

T_ATT = 256
NQ_ATT = 4
T_SB = 256
NQ_SB = 4
FIRST_SB = 2
TK_DEC_FIRST = 256
TK_DEC_REST = 1280
TK_DEC_FOX = 1024
TM_ROWS = 512
ADA_ROWS = 32
ADA_TN = 1536


def _cparams(sem):
    return pltpu.CompilerParams(dimension_semantics=sem, vmem_limit_bytes=VMEM_LIMIT)


def _rms_mod(x, g, sc, sh):
    y = x * lax.rsqrt(jnp.mean(x * x, axis=-1, keepdims=True) + EPS)
    return (y * g) * (1.0 + sc) + sh


def _softplus(z):
    return jnp.maximum(z, 0.0) + jnp.log(1.0 + jnp.exp2(jnp.abs(z) * (-LOG2E)))


def _log_sigmoid(x):
    return jnp.minimum(x, 0.0) - jnp.log(1.0 + jnp.exp(-jnp.abs(x)))


def _split3(x):
    hi = x.astype(BF16)
    r = x - hi.astype(F32)
    mid = r.astype(BF16)
    lo = (r - mid.astype(F32)).astype(BF16)
    return hi, mid, lo


def _dot(a, b):
    return jnp.dot(a, b, preferred_element_type=F32)


def _dot_nt(a, b):
    return lax.dot_general(a, b, (((1,), (1,)), ((), ())), preferred_element_type=F32)


def _dot3(x, m):
    hi, mid, lo = _split3(x)
    return _dot(hi, m) + _dot(mid, m) + _dot(lo, m)


def _iota(shape, dim):
    return lax.broadcasted_iota(jnp.int32, shape, dim)


def _log2(n):
    assert n & (n - 1) == 0, n
    return n.bit_length() - 1


def _ada_kernel(c_ref, w_ref, b_ref, o_ref):
    c = c_ref[...]
    cond = (c / (1.0 + jnp.exp(-c))).astype(BF16)
    o_ref[...] = _dot(cond, w_ref[...].astype(BF16)) + b_ref[...]


def _ada_call(c_all, w_ada, b_ada):
    depth, d, n = w_ada.shape
    return pl.pallas_call(
        _ada_kernel,
        out_shape=jax.ShapeDtypeStruct((depth, ADA_ROWS, n), F32),
        grid=(depth, n // ADA_TN),
        in_specs=[
            pl.BlockSpec((ADA_ROWS, d), lambda i, j: (0, 0)),
            pl.BlockSpec((None, d, ADA_TN), lambda i, j: (i, 0, j)),
            pl.BlockSpec((None, 1, ADA_TN), lambda i, j: (i, 0, j)),
        ],
        out_specs=pl.BlockSpec((None, ADA_ROWS, ADA_TN), lambda i, j: (i, 0, j)),
        compiler_params=_cparams(("arbitrary", "arbitrary")),
        name="ada_mod",
    )(c_all, w_ada, b_ada.reshape(depth, 1, n))


def _mod_spec(prompt, tm, d, chunk, tiles_per_batch):
    if prompt:
        return pl.BlockSpec((None, 1, d), lambda i: (i // tiles_per_batch, 0, chunk))
    return pl.BlockSpec((tm, d), lambda i: (i, chunk))


def _pool_branch(u, ext_ref, pos, ng, rg, wp_ref, ps_ref, pooled_ref, pst_ref):
    tm, a_w = u.shape
    u3 = u.reshape(ng, rg, a_w)
    ext_ref[:, HIST_ROWS:HIST_ROWS + rg, :] = u3
    pst_ref[...] = u3[:, rg - HIST_ROWS:rg, :]
    for g, w in enumerate(POOL_WINDOWS):
        cs = slice(g * POOL_GC, (g + 1) * POOL_GC)
        acc = u3[:, :, cs]
        for kk in range(1, w):
            acc = acc + ext_ref[:, pl.ds(HIST_ROWS - kk, rg), cs]
        cnt = jnp.minimum(pos + 1, w).astype(F32)
        pooled = acc.reshape(tm, POOL_GC) / cnt - u[:, cs]
        pg = _dot(pooled.astype(BF16), wp_ref[g]) * ps_ref[:, cs]
        pooled_ref[:, cs] = pg.astype(pooled_ref.dtype)


def _ab_in_prompt_kernel(tm, tiles_per_batch, x_ref, sh_ref, sc_ref, g_ref, wq_ref, wkt_ref, wvt_ref,
                         wu_ref, wft_ref, bf_ref, wp_ref, ps_ref,
                         q_ref, ktb_ref, vtb_ref, kt_ref, vt_ref, lft_ref, fcp_ref, kaug_ref, pooled_ref, pst_ref,
                         ext_ref, carry_ref):
    h = _rms_mod(x_ref[...], g_ref[...], sc_ref[...], sh_ref[...]).astype(BF16)
    q_ref[...] = (_dot(h, wq_ref[...]) * (DH ** -0.5)).astype(q_ref.dtype)
    kt = _dot_nt(wkt_ref[...], h)
    kt_ref[...] = kt
    ktb_ref[...] = kt.astype(BF16)
    vt = _dot_nt(wvt_ref[...], h)
    vt_ref[...] = vt
    vtb_ref[...] = vt.astype(BF16)

    tile_in_batch = pl.program_id(0) % tiles_per_batch
    first = tile_in_batch == 0
    logf = _log_sigmoid(_dot_nt(wft_ref[...], h) + bf_ref[...])
    lft_ref[...] = logf[0:H_FOX, :]
    tri = (_iota((tm, tm), 0) <= _iota((tm, tm), 1)).astype(BF16)
    fcum = _dot3(logf, tri) + jnp.where(first, 0.0, carry_ref[...])
    carry_ref[...] = fcum[:, tm - 1:tm]
    parts = _split3(fcum)
    prow = _iota((4 * H_FOX, GATE_ROWS), 0)
    pick = jnp.logical_and((prow & 7) < 2, _iota((4 * H_FOX, GATE_ROWS), 1) == (prow >> 3) * 2 + (prow & 7))
    pick = jnp.where(pick, 1.0, 0.0).astype(BF16)
    fcp_ref[...] = (_dot(pick, parts[0]) + _dot(pick, parts[1])) + _dot(pick, parts[2])

    n_aug = (H_FOX // 2) * AUG_ROWS
    slot = _iota((n_aug, GATE_ROWS), 0) & (AUG_ROWS - 1)
    head0 = (_iota((n_aug, GATE_ROWS), 0) >> _log2(AUG_ROWS)) * 2
    col = _iota((n_aug, GATE_ROWS), 1)
    kaug = None
    for part, piece in enumerate(parts):
        sel = jnp.logical_or(jnp.logical_and(slot == 3 + part, col == head0),
                             jnp.logical_and(slot == 9 + part, col == head0 + 1))
        term = _dot(jnp.where(sel, -1.0, 0.0).astype(BF16), piece)
        kaug = term if kaug is None else kaug + term
    slot1 = _iota((n_aug, 1), 0) & (AUG_ROWS - 1)
    ones = jnp.logical_or(slot1 < 3, jnp.logical_and(slot1 >= 6, slot1 < 9))
    kaug_ref[...] = (kaug + jnp.where(ones, 1.0, 0.0)).astype(BF16)

    @pl.when(first)
    def _():
        ext_ref[:, 0:HIST_ROWS, :] = jnp.zeros((1, HIST_ROWS, ext_ref.shape[2]), F32)

    @pl.when(jnp.logical_not(first))
    def _():
        ext_ref[:, 0:HIST_ROWS, :] = ext_ref[:, tm:tm + HIST_ROWS, :]

    pos = tile_in_batch * tm + _iota((tm, 1), 0)
    _pool_branch(_dot(h, wu_ref[...]), ext_ref, pos, 1, tm, wp_ref, ps_ref, pooled_ref, pst_ref)


def _ab_in_sample_kernel(ng, rg, p0, x_ref, sh_ref, sc_ref, g_ref, wq_ref, wkt_ref, wvt_ref,
                         wu_ref, wft_ref, bf_ref, wp_ref, ps_ref, hist_ref,
                         q_ref, kb_ref, vb_ref, k32_ref, v32_ref, lft_ref, pooled_ref, pst_ref, ext_ref):
    tm = ng * rg
    h = _rms_mod(x_ref[...], g_ref[...], sc_ref[...], sh_ref[...]).astype(BF16)
    q_ref[...] = (_dot(h, wq_ref[...]) * (DH ** -0.5)).astype(q_ref.dtype)
    k = _dot_nt(h, wkt_ref[...])
    k32_ref[...] = k
    kb_ref[...] = k.astype(BF16)
    v = _dot_nt(h, wvt_ref[...])
    v32_ref[...] = v
    vb_ref[...] = v.astype(BF16)
    logf = _log_sigmoid(_dot_nt(wft_ref[...], h) + bf_ref[...])
    lft_ref[...] = logf[0:H_FOX, :]
    ext_ref[:, 0:HIST_ROWS, :] = hist_ref[...]
    pos = p0 + (_iota((tm, 1), 0) & (rg - 1))
    _pool_branch(_dot(h, wu_ref[...]), ext_ref, pos, ng, rg, wp_ref, ps_ref, pooled_ref, pst_ref)


def _ab_in_call(prompt, x2, mod_arr, g, wts, hist, rows_per_batch, p0):
    wq, wkt, wvt, wu, wft, bf, wp, ps = wts
    m, d = x2.shape
    a_w = H_FOX * DH
    nb = m // rows_per_batch
    tm = TM_ROWS if prompt else m
    tpb = rows_per_batch // tm if prompt else 1
    row = lambda i: (i, 0)
    const2 = lambda i: (0, 0)
    in_specs = [pl.BlockSpec((tm, d), row), _mod_spec(prompt, tm, d, 0, tpb), _mod_spec(prompt, tm, d, 1, tpb),
                pl.BlockSpec((1, d), const2)]
    in_specs += [pl.BlockSpec(w.shape, const2) for w in (wq, wkt, wvt, wu, wft, bf)]
    in_specs += [pl.BlockSpec(wp.shape, lambda i: (0, 0, 0)), pl.BlockSpec(ps.shape, const2)]
    args = [x2, mod_arr, mod_arr, g, wq, wkt, wvt, wu, wft, bf, wp, ps]
    if prompt:
        tcol = lambda i: (i // tpb, 0, i % tpb)
        out_shape = [jax.ShapeDtypeStruct((m, a_w), BF16),
                     jax.ShapeDtypeStruct((nb, a_w, rows_per_batch), BF16),
                     jax.ShapeDtypeStruct((nb, a_w, rows_per_batch), BF16),
                     jax.ShapeDtypeStruct((nb, a_w, rows_per_batch), F32),
                     jax.ShapeDtypeStruct((nb, a_w, rows_per_batch), F32),
                     jax.ShapeDtypeStruct((nb, H_FOX, rows_per_batch), F32),
                     jax.ShapeDtypeStruct((nb, 4 * H_FOX, rows_per_batch), F32),
                     jax.ShapeDtypeStruct((nb, (H_FOX // 2) * AUG_ROWS, rows_per_batch), BF16),
                     jax.ShapeDtypeStruct((m, a_w), BF16),
                     jax.ShapeDtypeStruct((nb, HIST_ROWS, a_w), F32)]
        out_specs = ([pl.BlockSpec((tm, a_w), row)] + [pl.BlockSpec((None, a_w, tm), tcol)] * 4
                     + [pl.BlockSpec((None, H_FOX, tm), tcol), pl.BlockSpec((None, 4 * H_FOX, tm), tcol)]
                     + [pl.BlockSpec((None, (H_FOX // 2) * AUG_ROWS, tm), tcol)]
                     + [pl.BlockSpec((tm, a_w), row),
                        pl.BlockSpec((1, HIST_ROWS, a_w), lambda i: (i // tpb, 0, 0))])
        scratch = [pltpu.VMEM((1, HIST_ROWS + tm, a_w), F32), pltpu.VMEM((GATE_ROWS, 1), F32)]
        body = functools.partial(_ab_in_prompt_kernel, tm, tpb)
    else:
        in_specs.append(pl.BlockSpec(hist.shape, lambda i: (0, 0, 0)))
        args.append(hist)
        out_shape = [jax.ShapeDtypeStruct((m, a_w), BF16), jax.ShapeDtypeStruct((m, a_w), BF16),
                     jax.ShapeDtypeStruct((m, a_w), BF16), jax.ShapeDtypeStruct((m, a_w), F32),
                     jax.ShapeDtypeStruct((m, a_w), F32), jax.ShapeDtypeStruct((H_FOX, m), F32),
                     jax.ShapeDtypeStruct((m, a_w), BF16), jax.ShapeDtypeStruct((nb, HIST_ROWS, a_w), F32)]
        out_specs = ([pl.BlockSpec((tm, a_w), row)] * 5 + [pl.BlockSpec((H_FOX, tm), const2)]
                     + [pl.BlockSpec((tm, a_w), row), pl.BlockSpec((nb, HIST_ROWS, a_w), lambda i: (0, 0, 0))])
        scratch = [pltpu.VMEM((nb, HIST_ROWS + rows_per_batch, a_w), F32)]
        body = functools.partial(_ab_in_sample_kernel, nb, rows_per_batch, p0)
    return pl.pallas_call(
        body, out_shape=out_shape, grid=(m // tm,), in_specs=in_specs, out_specs=out_specs,
        scratch_shapes=scratch,
        compiler_params=_cparams(("arbitrary",)),
        name="ab_in_prompt" if prompt else "ab_in_sample",
    )(*args)


def _head_split(q2):
    lane = _iota((1, LANES), 1)
    zero = jnp.zeros_like(q2)
    return jnp.where(lane < DH, q2, zero), jnp.where(lane >= DH, q2, zero)


def _pair_sel():
    return (_iota((8, LANES), 0) == (_iota((8, LANES), 1) >> _log2(DH))).astype(BF16)


def _fox_kernel(q_ref, kt_ref, vt_ref, ka_ref, fr_ref, fall_ref, o_ref,
                m_ref, acc_ref, kn_ref):
    t, nq = T_ATT, NQ_ATT
    qg = pl.program_id(2)
    lane = _iota((1, LANES), 1)
    lane2 = _iota((1, 2), 1)

    @pl.when(qg == 0)
    def _():
        chunk = 4 * t

        def body(i, mx):
            ks = pl.multiple_of(i * chunk, chunk)
            kf = kt_ref[:, pl.ds(ks, chunk)].astype(F32)
            n2 = _dot(_pair_sel(), (kf * kf).astype(BF16))
            return jnp.maximum(mx, jnp.max(n2, axis=1, keepdims=True))

        mx = lax.fori_loop(0, kt_ref.shape[1] // chunk, body, jnp.zeros((8, 1), F32))
        kn_ref[...] = jnp.sqrt(mx)

    row_a = _iota((AUG_ROWS, 1), 0)
    row_v = _iota((LANES, 1), 0)
    qcat, bound0 = [], []
    for s in range(nq):
        q2 = q_ref[s * t:(s + 1) * t, :]
        frows = fr_ref[:, s * t:(s + 1) * t]
        eye = (_iota((t, t), 0) == _iota((t, t), 1)).astype(BF16)
        per_head = []
        for j, qh in enumerate(_head_split(q2)):
            hi, mid, lo = (x.astype(F32) for x in _split3(frows[j:j + 1, :]))
            base = 6 * j
            qat = jnp.where(row_a == base, hi, jnp.where(row_a == base + 1, mid, jnp.where(
                row_a == base + 2, lo,
                jnp.where(jnp.logical_and(row_a >= base + 3, row_a < base + 6), 1.0, 0.0))))
            qat = jnp.where(row_a == OFF_SLOT, NEG_BIG, qat)
            qa = _dot_nt(eye, qat.astype(BF16))
            per_head.append(jnp.concatenate([qh, qa.astype(BF16)], axis=1))
        qcat.append(per_head)
        qf = q2.astype(F32)
        qn2 = _dot_nt((qf * qf).astype(BF16), _pair_sel())
        qn = jnp.sqrt(jnp.max(qn2, axis=0, keepdims=True))
        bound0.append([qn[:, j:j + 1] * kn_ref[j:j + 1, :] * NORM_MARGIN + frows[j:j + 1, 0:1] for j in range(2)])
    chains = [(s, j) for s in range(nq) for j in range(2)]
    rep = lambda m: jnp.concatenate([m] * (t // LANES), axis=1)

    def front(n, diag, chains=chains):
        kbs = [qg * nq + s - n for s in range(nq)]
        kss = [pl.multiple_of((kb if diag else jnp.maximum(kb, 0)) * t, t) for kb in kbs]
        causal = _iota((t, t), 1) <= _iota((t, t), 0)
        kcat = []
        for s in range(nq):
            ka = ka_ref[:, pl.ds(kss[s], t)]
            if not diag:
                off = jnp.where(kbs[s] >= 0, 0.0, 1.0).astype(BF16)
                ka = jnp.where(_iota((AUG_ROWS, 1), 0) == OFF_SLOT, off, ka)
            kcat.append(jnp.concatenate([kt_ref[:, pl.ds(kss[s], t)], ka], axis=0))
        sc = []
        for s, j in chains:
            x = _dot(qcat[s][j], kcat[s])
            sc.append(jnp.where(causal, x, NEG_BIG) if diag else x)
        vcat = []
        for s, j in chains:
            v2 = vt_ref[:, pl.ds(kss[s], t)]
            mine = (row_v < DH) if j == 0 else (row_v >= DH)
            vcat.append(jnp.where(mine, v2, jnp.ones_like(v2)))
        return sc, vcat

    (sc0, v0), (sc1, v1) = front(0, True), front(1, False)
    m01 = [jnp.broadcast_to(jnp.maximum(jnp.max(x0, axis=1, keepdims=True), jnp.max(x1, axis=1, keepdims=True)),
                            (t, LANES)) for x0, x1 in zip(sc0, sc1)]
    pv0 = [_dot_nt(jnp.exp(x - rep(m)).astype(BF16), vv) for x, m, vv in zip(sc0, m01, v0)]
    pv1 = [_dot_nt(jnp.exp(x - rep(m)).astype(BF16), vv) for x, m, vv in zip(sc1, m01, v1)]
    for i, (s, j) in enumerate(chains):
        acc_ref[2 * s + j] = pv0[i] + pv1[i]
        m_ref[2 * s + j] = m01[i]

    def step(n, chains):
        sc, vcat = front(n, False, chains)
        m_prev = [m_ref[2 * s + j] for s, j in chains]
        m_new = [jnp.maximum(mp, jnp.max(x, axis=1, keepdims=True)) for mp, x in zip(m_prev, sc)]
        pv = [_dot_nt(jnp.exp(x - rep(mn)).astype(BF16), vv) for x, mn, vv in zip(sc, m_new, vcat)]
        for i, (s, j) in enumerate(chains):
            idx = 2 * s + j
            acc_ref[idx] = jnp.exp(m_prev[i] - m_new[i]) * acc_ref[idx] + pv[i]
            m_ref[idx] = m_new[i]

    def more(n):
        go = [False, False]
        for s in range(nq):
            kb = qg * nq + s - n
            last = pl.multiple_of(jnp.maximum(kb, 0) * t + (t - LANES), LANES)
            ft = fall_ref[:, pl.ds(last, LANES)][:, LANES - 1:LANES]
            for j in range(2):
                top = jnp.max(bound0[s][j] - ft[j:j + 1, :]) - jnp.min(m_ref[2 * s + j])
                go[j] = jnp.logical_or(go[j], jnp.logical_and(kb >= 0, top > -UNDERFLOW_EXP))
        return go

    def body(chains, c):
        n = c[0]
        step(n, chains)
        return (n + 1, *more(n + 1))

    c = lax.while_loop(lambda c: jnp.logical_and(c[1], c[2]), functools.partial(body, chains),
                       (jnp.int32(2), *more(2)))
    c = lax.while_loop(lambda c: c[1], functools.partial(body, [(s, 0) for s in range(nq)]), c)
    lax.while_loop(lambda c: c[2], functools.partial(body, [(s, 1) for s in range(nq)]), c)
    for s in range(nq):
        a0, a1 = acc_ref[2 * s], acc_ref[2 * s + 1]
        o = jnp.where(lane < DH, a0 / pltpu.roll(a0, DH, 1), a1 / pltpu.roll(a1, DH, 1))
        o_ref[s * t:(s + 1) * t, :] = o.astype(o_ref.dtype)


def _fox_call(q, ktb, vtb, kaug, fcp):
    b, l, a_w = q.shape
    hp = a_w // LANES
    t, nq = T_ATT, NQ_ATT
    tq = t * nq
    return pl.pallas_call(
        _fox_kernel,
        out_shape=jax.ShapeDtypeStruct((b, l, a_w), BF16),
        grid=(b, hp, l // tq),
        in_specs=[
            pl.BlockSpec((None, tq, LANES), lambda bi, h, qi: (bi, qi, h)),
            pl.BlockSpec((None, LANES, l), lambda bi, h, qi: (bi, h, 0)),
            pl.BlockSpec((None, LANES, l), lambda bi, h, qi: (bi, h, 0)),
            pl.BlockSpec((None, AUG_ROWS, l), lambda bi, h, qi: (bi, h, 0)),
            pl.BlockSpec((None, 8, tq), lambda bi, h, qi: (bi, h, qi)),
            pl.BlockSpec((None, 8, l), lambda bi, h, qi: (bi, h, 0)),
        ],
        out_specs=pl.BlockSpec((None, tq, LANES), lambda bi, h, qi: (bi, qi, h)),
        scratch_shapes=[pltpu.VMEM((2 * nq, t, LANES), F32), pltpu.VMEM((2 * nq, t, LANES), F32),
                        pltpu.VMEM((8, 1), F32)],
        compiler_params=_cparams(("arbitrary", "arbitrary", "arbitrary")),
        name="fox_prompt",
    )(q, ktb, vtb, kaug, fcp, fcp)


def _sb_kernel(q_ref, kt_ref, vt_ref, o_ref, carry_ref, acc_ref):
    t, nq = T_SB, NQ_SB
    qg = pl.program_id(2)
    lane = _iota((1, LANES), 1)
    qh = [_head_split(q_ref[s * t:(s + 1) * t, :]) for s in range(nq)]
    valid = _iota((t, t), 1) < _iota((t, t), 0)

    def logits(s, n, diag):
        kb = qg * nq + s - n
        ks = pl.multiple_of((kb if diag else jnp.maximum(kb, 0)) * t, t)
        return kb, ks, [_dot(qh[s][j], kt_ref[:, pl.ds(ks, t)]) for j in range(2)]

    def suffix(diag, z):
        tri = (_iota((t, t), 0) > _iota((t, t), 1)).astype(BF16)
        sp = [_softplus(zz) for zz in z]
        spm = [jnp.where(valid, x, 0.0) for x in sp] if diag else sp
        later = [_dot(x.astype(BF16), tri) for x in spm]
        return [zz - x for zz, x in zip(z, sp)], spm, later

    def weigh(diag, kb, ks, ls, spm, later, carry):
        v2 = vt_ref[:, pl.ds(ks, t)]
        if not diag:
            v2 = jnp.where(kb >= 0, v2, jnp.zeros_like(v2))
        pv, new_carry = [], []
        for j in range(2):
            e = ls[j] - later[j]
            a = jnp.exp(e if carry is None else e - carry[j])
            if diag:
                a = jnp.where(valid, a, 0.0)
            pv.append(_dot_nt(a.astype(BF16), v2))
            rs = jnp.sum(spm[j], axis=1, keepdims=True)
            new_carry.append(rs if carry is None else carry[j] + rs)
        return pv, new_carry

    def run(units, carry, acc):
        st1, st2 = {}, {}
        for k in range(len(units) + 2):
            if k < len(units):
                st1[k] = logits(*units[k])
            if 0 <= k - 1 < len(units):
                st2[k - 1] = suffix(units[k - 1][2], st1[k - 1][2])
            if 0 <= k - 2 < len(units):
                s, _, diag = units[k - 2]
                kb, ks, _ = st1.pop(k - 2)
                pv, carry[s] = weigh(diag, kb, ks, *st2.pop(k - 2), carry[s])
                acc[s] = pv if acc[s] is None else [x + y for x, y in zip(acc[s], pv)]
        return carry, acc

    carry, acc = run([(s, n, n == 0) for n in range(FIRST_SB) for s in range(nq)], [None] * nq, [None] * nq)
    for s in range(nq):
        for j in range(2):
            acc_ref[2 * s + j] = acc[s][j]
            carry_ref[2 * s + j] = carry[s][j]

    def more(n):
        in_range = n <= qg * nq + nq - 1
        return jnp.logical_and(in_range, jnp.min(carry_ref[...]) < UNDERFLOW_EXP)

    def body(c):
        n, _ = c
        carry, acc = run([(s, n, False) for s in range(nq)],
                         [[carry_ref[2 * s], carry_ref[2 * s + 1]] for s in range(nq)], [None] * nq)
        for s in range(nq):
            for j in range(2):
                acc_ref[2 * s + j] = acc_ref[2 * s + j] + acc[s][j]
                carry_ref[2 * s + j] = carry[s][j]
        return n + 1, more(n + 1)

    lax.while_loop(lambda c: c[1], body, (jnp.int32(FIRST_SB), more(FIRST_SB)))
    for s in range(nq):
        o = jnp.where(lane < DH, acc_ref[2 * s], acc_ref[2 * s + 1])
        o_ref[s * t:(s + 1) * t, :] = o.astype(o_ref.dtype)


def _sb_call(q, ktb, vtb):
    b, l, d = q.shape
    hp = d // LANES
    t, nq = T_SB, NQ_SB
    tq = t * nq
    return pl.pallas_call(
        _sb_kernel,
        out_shape=jax.ShapeDtypeStruct((b, l, d), BF16),
        grid=(b, hp, l // tq),
        in_specs=[
            pl.BlockSpec((None, tq, LANES), lambda bi, h, qi: (bi, qi, h)),
            pl.BlockSpec((None, LANES, l), lambda bi, h, qi: (bi, h, 0)),
            pl.BlockSpec((None, LANES, l), lambda bi, h, qi: (bi, h, 0)),
        ],
        out_specs=pl.BlockSpec((None, tq, LANES), lambda bi, h, qi: (bi, qi, h)),
        scratch_shapes=[pltpu.VMEM((2 * nq, t, 1), F32), pltpu.VMEM((2 * nq, t, LANES), F32)],
        compiler_params=_cparams(("arbitrary", "arbitrary", "arbitrary")),
        name="sb_prompt",
    )(q, ktb, vtb)


def _rep_matrix(rows, n_heads, ds):
    return ((_iota((rows, ds), 0) >> _log2(n_heads)) == _iota((rows, ds), 1)).astype(BF16)


def _head_mask(rows, n_heads, width):
    return (_iota((rows, width), 0) & (n_heads - 1)) == (_iota((rows, width), 1) >> _log2(DH))


def _block_diag_q(q, n_heads):
    ds, width = q.shape
    rows = ds * n_heads
    qrep = _dot(_rep_matrix(rows, n_heads, ds), q)
    return jnp.where(_head_mask(rows, n_heads, width), qrep, 0.0).astype(BF16)


def _collect_heads(acc, n_heads, ds):
    rows, width = acc.shape
    om = jnp.where(_head_mask(rows, n_heads, width), acc, 0.0).astype(BF16)
    rep_t = ((_iota((ds, rows), 1) >> _log2(n_heads)) == _iota((ds, rows), 0)).astype(BF16)
    return _dot(rep_t, om)


def _suffix_sums(x, exact=True, blk=2 * LANES):
    r, n = x.shape
    parts = [x[:, b * blk:(b + 1) * blk] for b in range(n // blk)]
    tri = (_iota((blk, blk), 0) > _iota((blk, blk), 1)).astype(BF16)
    stacked = jnp.concatenate(parts, axis=0)
    w = _dot3(stacked, tri) if exact else _dot(stacked.astype(BF16), tri)
    out, tail = [], None
    for b in reversed(range(len(parts))):
        cur = w[b * r:(b + 1) * r]
        out.append(cur if tail is None else cur + tail)
        tot = jnp.sum(parts[b], axis=1, keepdims=True)
        tail = tot if tail is None else tail + tot
    return jnp.concatenate(out[::-1], axis=1), tail


def _fox_dec_kernel(n_heads, ds, q_ref, kn_ref, vn_ref, lfn_ref, kc_ref, vc_ref, lfc_ref,
                    o_ref, qbd_ref, m_ref, l_ref, acc_ref, cf_ref):
    rows = ds * n_heads
    step_i = pl.program_id(1)
    n_steps = pl.num_programs(1)

    def update(s, pv_fn):
        m_prev = m_ref[...]
        m_new = jnp.maximum(m_prev, jnp.max(s, axis=1, keepdims=True))
        alpha = jnp.exp(m_prev - m_new)
        p = jnp.exp(s - m_new)
        l_ref[...] = alpha * l_ref[...] + jnp.sum(p, axis=1, keepdims=True)
        acc_ref[...] = alpha * acc_ref[...] + pv_fn(p.astype(BF16))
        m_ref[...] = m_new

    @pl.when(step_i == 0)
    def _():
        qbd_ref[...] = _block_diag_q(q_ref[...], n_heads)
        m_ref[...] = jnp.full(m_ref.shape, NEG_BIG, F32)
        l_ref[...] = jnp.zeros(l_ref.shape, F32)
        acc_ref[...] = jnp.zeros(acc_ref.shape, F32)
        a = jnp.concatenate([lfn_ref[...]] * ds, axis=0)
        ri = _iota((rows, ds), 0) >> _log2(n_heads)
        cj = _iota((rows, ds), 1)
        am = jnp.where(cj <= ri, a, 0.0)
        tri = (_iota((ds, ds), 0) > _iota((ds, ds), 1)).astype(BF16)
        s = _dot_nt(qbd_ref[...], kn_ref[...]) + _dot3(am, tri)
        s = jnp.where(cj <= ri, s, NEG_BIG)
        update(s, lambda p: _dot(p, vn_ref[...]))
        cf_ref[...] = jnp.sum(am, axis=1, keepdims=True)

    @pl.when(step_i > 0)
    def _():
        reps = rows // (2 * n_heads)
        later, total = _suffix_sums(jnp.concatenate([lfc_ref[...]] * 2, axis=0))
        cf = cf_ref[...]
        s = _dot(qbd_ref[...], kc_ref[...].astype(BF16)) + (jnp.concatenate([later] * reps, axis=0) + cf)
        update(s, lambda p: _dot_nt(p, vc_ref[...].astype(BF16)))
        cf_ref[...] = cf + jnp.concatenate([total] * reps, axis=0)

    @pl.when(step_i == n_steps - 1)
    def _():
        o_ref[...] = _collect_heads(acc_ref[...] / l_ref[...], n_heads, ds).astype(o_ref.dtype)


def _fox_dec_call(q, kn, vn, kct, vct, lfn_t, lfc_t):
    db, ds, width = q.shape
    n_heads = width // DH
    rows = ds * n_heads
    tk = min(TK_DEC_FOX, kct.shape[2])
    nkb = kct.shape[2] // tk
    cache_blk = lambda s: jnp.minimum(nkb - s, nkb - 1)
    new_idx = lambda b, s: (b, 0, 0)
    cache_idx = lambda b, s: (b, 0, cache_blk(s))
    in_specs = ([pl.BlockSpec((None, ds, width), new_idx)] * 3
                + [pl.BlockSpec((None, n_heads, ds), new_idx)]
                + [pl.BlockSpec((None, width, tk), cache_idx)] * 2
                + [pl.BlockSpec((None, n_heads, tk), cache_idx)])
    scratch = [pltpu.VMEM((rows, width), BF16), pltpu.VMEM((rows, 1), F32), pltpu.VMEM((rows, 1), F32),
               pltpu.VMEM((rows, width), F32), pltpu.VMEM((rows, 1), F32)]
    return pl.pallas_call(
        functools.partial(_fox_dec_kernel, n_heads, ds),
        out_shape=jax.ShapeDtypeStruct((db, ds, width), BF16),
        grid=(db, nkb + 1),
        in_specs=in_specs,
        out_specs=pl.BlockSpec((None, ds, width), new_idx),
        scratch_shapes=scratch,
        compiler_params=_cparams(("arbitrary", "arbitrary")),
        name="fox_decode",
    )(q, kn, vn, lfn_t, kct, vct, lfc_t)


def _sb_dec_tile(qbd, kc_ref, vc_ref, carry):
    z = _dot(qbd, kc_ref[...].astype(BF16))
    sp = _softplus(z)
    later, total = _suffix_sums(sp, exact=False)
    a = jnp.exp((z - sp) - later - carry)
    return _dot_nt(a.astype(BF16), vc_ref[...].astype(BF16)), carry + total


def _sb_dec_first_kernel(n_heads, ds, q_ref, kn_ref, vn_ref, kc_ref, vc_ref,
                         o_ref, carry_out_ref, cmin_ref):
    rows = ds * n_heads
    qbd = _block_diag_q(q_ref[...], n_heads)
    ri = _iota((rows, ds), 0) >> _log2(n_heads)
    cj = _iota((rows, ds), 1)
    valid = cj < ri
    z = _dot_nt(qbd, kn_ref[...])
    sp = _softplus(z)
    spm = jnp.where(valid, sp, 0.0)
    tri = (_iota((ds, ds), 0) > _iota((ds, ds), 1)).astype(BF16)
    later = _dot(spm.astype(BF16), tri)
    a = jnp.where(valid, jnp.exp((z - sp) - later), 0.0)
    acc = _dot(a.astype(BF16), vn_ref[...])
    carry = jnp.sum(spm, axis=1, keepdims=True)
    part, carry = _sb_dec_tile(qbd, kc_ref, vc_ref, carry)
    o_ref[...] = _collect_heads(acc + part, n_heads, ds).astype(o_ref.dtype)
    carry_out_ref[...] = carry
    cmin_ref[...] = jnp.broadcast_to(jnp.min(carry, axis=0, keepdims=True), cmin_ref.shape)


def _sb_dec_rest_kernel(n_heads, ds, need_ref, q_ref, carry_in_ref, o1_ref, kc_ref, vc_ref,
                        o_ref, qbd_ref, acc_ref, carry_ref):
    step_i = pl.program_id(1)
    last = pl.num_programs(1) - 1
    needed = need_ref[pl.program_id(0)] != 0

    @pl.when(needed)
    def _():
        @pl.when(step_i == 0)
        def _():
            qbd_ref[...] = _block_diag_q(q_ref[...], n_heads)
            acc_ref[...] = jnp.zeros(acc_ref.shape, F32)
            carry_ref[...] = carry_in_ref[...]

        @pl.when(jnp.min(carry_ref[...]) < UNDERFLOW_EXP)
        def _():
            part, carry = _sb_dec_tile(qbd_ref[...], kc_ref, vc_ref, carry_ref[...])
            acc_ref[...] = acc_ref[...] + part
            carry_ref[...] = carry

        @pl.when(step_i == last)
        def _():
            o = o1_ref[...].astype(F32) + _collect_heads(acc_ref[...], n_heads, ds)
            o_ref[...] = o.astype(o_ref.dtype)

    @pl.when(jnp.logical_and(jnp.logical_not(needed), step_i == last))
    def _():
        o_ref[...] = o1_ref[...]


def _sb_dec_call(q, kn, vn, kct, vct):
    db, ds, width = q.shape
    n_heads = width // DH
    rows = ds * n_heads
    past = kct.shape[2]
    first = min(TK_DEC_FIRST, past)
    n_blk = (past - first) // (2 * LANES)
    per = max([k for k in range(1, TK_DEC_REST // (2 * LANES) + 1) if n_blk % k == 0], default=1)
    rest = per * 2 * LANES
    n_rest = (past - first) // rest
    bidx = lambda b: (b, 0, 0)
    o1, carry, cmin = pl.pallas_call(
        functools.partial(_sb_dec_first_kernel, n_heads, ds),
        out_shape=[jax.ShapeDtypeStruct((db, ds, width), BF16),
                   jax.ShapeDtypeStruct((db, rows, 1), F32),
                   jax.ShapeDtypeStruct((db, 1, LANES), F32)],
        grid=(db,),
        in_specs=[pl.BlockSpec((None, ds, width), bidx)] * 3
        + [pl.BlockSpec((None, width, first), lambda b: (b, 0, past // first - 1))] * 2,
        out_specs=[pl.BlockSpec((None, ds, width), bidx), pl.BlockSpec((None, rows, 1), bidx),
                   pl.BlockSpec((None, 1, LANES), bidx)],
        compiler_params=_cparams(("arbitrary",)),
        name="sb_decode_first",
    )(q, kn, vn, kct, vct)
    if n_rest == 0:
        return o1
    need = (cmin[:, 0, 0] < UNDERFLOW_EXP).astype(jnp.int32)
    bidx2 = lambda b, s, need_ref: (b, 0, 0)
    tile = (None, width, rest)

    def cache_idx(b, s, need_ref):
        on = need_ref[b] != 0
        return (jnp.where(on, b, 0), 0, jnp.where(on, n_rest - 1 - s, 0))

    grid_spec = pltpu.PrefetchScalarGridSpec(
        num_scalar_prefetch=1,
        grid=(db, n_rest),
        in_specs=[pl.BlockSpec((None, ds, width), bidx2), pl.BlockSpec((None, rows, 1), bidx2),
                  pl.BlockSpec((None, ds, width), bidx2),
                  pl.BlockSpec(tile, cache_idx), pl.BlockSpec(tile, cache_idx)],
        out_specs=pl.BlockSpec((None, ds, width), bidx2),
        scratch_shapes=[pltpu.VMEM((rows, width), BF16), pltpu.VMEM((rows, width), F32),
                        pltpu.VMEM((rows, 1), F32)],
    )
    return pl.pallas_call(
        functools.partial(_sb_dec_rest_kernel, n_heads, ds),
        out_shape=jax.ShapeDtypeStruct((db, ds, width), BF16),
        grid_spec=grid_spec,
        compiler_params=_cparams(("arbitrary", "arbitrary")),
        name="sb_decode_rest",
    )(need, q, carry, o1, kct, vct)


def _post_kernel(n_act, final, f_chunk, *refs):
    acts = refs[:n_act]
    wo_ref, x_ref, gate1_ref, sh_ref, sc_ref, gate_ref, g_ref, wu_ref, wd_ref = refs[n_act:n_act + 9]
    fg_ref = refs[n_act + 9] if final else None
    o_ref = refs[-1]
    y = None
    off = 0
    for a_ref in acts:
        kdim = a_ref.shape[1]
        part = _dot(a_ref[...], wo_ref[off:off + kdim, :])
        y = part if y is None else y + part
        off += kdim
    x = x_ref[...] + gate1_ref[...] * y
    h = _rms_mod(x, g_ref[...], sc_ref[...], sh_ref[...]).astype(BF16)
    d_ff = wu_ref.shape[1]
    acc = None
    for c in range(d_ff // f_chunk):
        cs = slice(c * f_chunk, (c + 1) * f_chunk)
        a = jnp.maximum(_dot(h, wu_ref[:, cs]), 0.0)
        part = _dot((a * a).astype(BF16), wd_ref[cs, :])
        acc = part if acc is None else acc + part
    xo = x + gate_ref[...] * acc
    if final:
        y = xo * lax.rsqrt(jnp.mean(xo * xo, axis=-1, keepdims=True) + EPS)
        xo = y * fg_ref[...]
    o_ref[...] = xo


def _post_call(prompt, acts, wo, x2, mod_arr, g, wu, wd, final_g, rows_per_batch):
    m, d = x2.shape
    tm = TM_ROWS if prompt else m
    tpb = rows_per_batch // tm if prompt else 1
    row = lambda i: (i, 0)
    const2 = lambda i: (0, 0)
    final = final_g is not None
    resident = lambda w: pl.BlockSpec(w.shape, const2, pipeline_mode=pl.Buffered(1))
    in_specs = [pl.BlockSpec((tm, a.shape[1]), row) for a in acts]
    in_specs += [resident(wo), pl.BlockSpec((tm, d), row)]
    in_specs += [_mod_spec(prompt, tm, d, c, tpb) for c in (2, 3, 4, 5)]
    in_specs += [pl.BlockSpec((1, d), const2), resident(wu), resident(wd)]
    args = list(acts) + [wo, x2] + [mod_arr] * 4 + [g, wu, wd]
    if final:
        in_specs.append(pl.BlockSpec((1, d), const2))
        args.append(final_g)
    return pl.pallas_call(
        functools.partial(_post_kernel, len(acts), final, 1024),
        out_shape=jax.ShapeDtypeStruct((m, d), F32),
        grid=(m // tm,), in_specs=in_specs, out_specs=pl.BlockSpec((tm, d), row),
        compiler_params=_cparams(("arbitrary",)),
        name="post",
    )(*args)


def _sb_in_prompt_kernel(x_ref, sh_ref, sc_ref, g_ref, wq_ref, wkt_ref, wvt_ref,
                         q_ref, ktb_ref, vtb_ref, kt_ref, vt_ref):
    h = _rms_mod(x_ref[...], g_ref[...], sc_ref[...], sh_ref[...]).astype(BF16)
    q_ref[...] = (_dot(h, wq_ref[...]) * (DH ** -0.5)).astype(q_ref.dtype)
    kt = _dot_nt(wkt_ref[...], h)
    kt_ref[...] = kt
    ktb_ref[...] = kt.astype(BF16)
    vt = _dot_nt(wvt_ref[...], h)
    vt_ref[...] = vt
    vtb_ref[...] = vt.astype(BF16)


def _sb_in_sample_kernel(x_ref, sh_ref, sc_ref, g_ref, wq_ref, wkt_ref, wvt_ref,
                         q_ref, kb_ref, vb_ref, k32_ref, v32_ref):
    h = _rms_mod(x_ref[...], g_ref[...], sc_ref[...], sh_ref[...]).astype(BF16)
    q_ref[...] = (_dot(h, wq_ref[...]) * (DH ** -0.5)).astype(q_ref.dtype)
    k = _dot_nt(h, wkt_ref[...])
    k32_ref[...] = k
    kb_ref[...] = k.astype(BF16)
    v = _dot_nt(h, wvt_ref[...])
    v32_ref[...] = v
    vb_ref[...] = v.astype(BF16)


def _sb_in_call(prompt, x2, mod_arr, g, wts, rows_per_batch):
    wq, wkt, wvt = wts
    m, d = x2.shape
    nb = m // rows_per_batch
    tm = TM_ROWS if prompt else m
    tpb = rows_per_batch // tm if prompt else 1
    row = lambda i: (i, 0)
    const2 = lambda i: (0, 0)
    in_specs = [pl.BlockSpec((tm, d), row),
                _mod_spec(prompt, tm, d, 0, tpb), _mod_spec(prompt, tm, d, 1, tpb),
                pl.BlockSpec((1, d), const2)] + [pl.BlockSpec(w.shape, const2) for w in wts]
    if prompt:
        tcol = lambda i: (i // tpb, 0, i % tpb)
        out_shape = ([jax.ShapeDtypeStruct((m, d), BF16)]
                     + [jax.ShapeDtypeStruct((nb, d, rows_per_batch), BF16)] * 2
                     + [jax.ShapeDtypeStruct((nb, d, rows_per_batch), F32)] * 2)
        out_specs = [pl.BlockSpec((tm, d), row)] + [pl.BlockSpec((None, d, tm), tcol)] * 4
        body = _sb_in_prompt_kernel
    else:
        out_shape = [jax.ShapeDtypeStruct((m, d), BF16)] * 3 + [jax.ShapeDtypeStruct((m, d), F32)] * 2
        out_specs = [pl.BlockSpec((tm, d), row)] * 5
        body = _sb_in_sample_kernel
    return pl.pallas_call(
        body, out_shape=out_shape, grid=(m // tm,), in_specs=in_specs, out_specs=out_specs,
        compiler_params=_cparams(("arbitrary",)),
        name="sb_in_prompt" if prompt else "sb_in_sample",
    )(x2, mod_arr, mod_arr, g, wq, wkt, wvt)


def _to_state(xt, n_heads):
    nb, _, l = xt.shape
    return xt.reshape(nb, n_heads, DH, l).transpose(0, 3, 1, 2)[None]


def _from_cache(c):
    nb, p, n_heads, dh = c.shape
    return c.transpose(0, 2, 3, 1).reshape(nb, n_heads * dh, p)


def _trunk(prompt, x, mods, past, wts):
    nb, rpb, d = x.shape
    m = nb * rpb
    a_w = H_FOX * DH
    x2 = x.reshape(m, d)
    norm_g, ab_wts, w_out_ab, sb_wts, w_out_sb, w_up, w_down, final_g = wts

    if prompt:
        outs = _ab_in_call(True, x2, mods[0], norm_g[0, 0][None], ab_wts, None, rpb, 0)
        q, ktb, vtb, kt32, vt32, lft, fcp, kaug, pooled, pstate = outs
        a_out = _fox_call(q.reshape(nb, rpb, a_w), ktb, vtb, kaug, fcp)
        st_fox = (_to_state(kt32, H_FOX), _to_state(vt32, H_FOX), lft.transpose(0, 2, 1)[None])
    else:
        cfk, cfv, cflogf, spool, csk, csv = past
        p0 = cfk.shape[1]
        hist = jnp.pad(spool, ((0, 0), (HIST_ROWS - spool.shape[1], 0), (0, 0)))
        outs = _ab_in_call(False, x2, mods[0], norm_g[0, 0][None], ab_wts, hist, rpb, p0)
        q, kb, vb, k32, v32, lft, pooled, pstate = outs
        lfn_t = lft.reshape(H_FOX, nb, rpb).transpose(1, 0, 2)
        a_out = _fox_dec_call(q.reshape(nb, rpb, a_w), kb.reshape(nb, rpb, a_w), vb.reshape(nb, rpb, a_w),
                              _from_cache(cfk), _from_cache(cfv), lfn_t, cflogf.transpose(0, 2, 1))
        st_fox = (k32.reshape(1, nb, rpb, H_FOX, DH), v32.reshape(1, nb, rpb, H_FOX, DH),
                  lfn_t.transpose(0, 2, 1)[None])
    x2 = _post_call(prompt, [a_out.reshape(m, a_w), pooled], w_out_ab, x2, mods[0], norm_g[0, 1][None],
                    w_up[0], w_down[0], None, rpb)

    n_heads = d // DH
    outs = _sb_in_call(prompt, x2, mods[1], norm_g[1, 0][None], sb_wts, rpb)
    if prompt:
        q1, ktb1, vtb1, skt32, svt32 = outs
        s_out = _sb_call(q1.reshape(nb, rpb, d), ktb1, vtb1)
        st_sb = (_to_state(skt32, n_heads), _to_state(svt32, n_heads))
    else:
        q1, kb1, vb1, sk32, sv32 = outs
        s_out = _sb_dec_call(q1.reshape(nb, rpb, d), kb1.reshape(nb, rpb, d), vb1.reshape(nb, rpb, d),
                             _from_cache(csk), _from_cache(csv))
        st_sb = (sk32.reshape(1, nb, rpb, n_heads, DH), sv32.reshape(1, nb, rpb, n_heads, DH))
    y = _post_call(prompt, [s_out.reshape(m, d)], w_out_sb, x2, mods[1], norm_g[1, 1][None],
                   w_up[1], w_down[1], final_g[None], rpb)

    states = st_fox + (pstate[:, 1:, :][None],) + st_sb
    return y.reshape(nb, rpb, d), states


def kernel(x_prompt, x_sample, c_prompt, c_sample, cache_fox_k, cache_fox_v, cache_fox_logf, state_pool, cache_sb_k, cache_sb_v, w_ada, b_ada, norm_g, w_in_ab, b_forget, w_pool, pool_scale, w_out_ab, w_in_sb, w_out_sb, w_up, w_down, final_g):
    b, l, d = x_prompt.shape
    db, ds, _ = x_sample.shape
    a_w = H_FOX * DH

    c_all = jnp.concatenate([c_prompt, c_sample, jnp.zeros((ADA_ROWS - b - db, d), F32)], axis=0)
    mod = _ada_call(c_all, w_ada, b_ada)
    mods_p = [mod[i, :b][:, None, :] for i in range(mod.shape[0])]
    mods_s = [jnp.repeat(mod[i, b:b + db], ds, axis=0) for i in range(mod.shape[0])]

    w_in = w_in_ab[0]
    gate_pad = ((0, GATE_ROWS - H_FOX), (0, 0))
    ab_wts = (w_in[:, :a_w].astype(BF16),
              w_in[:, a_w:2 * a_w].T.astype(BF16),
              w_in[:, 2 * a_w:3 * a_w].T.astype(BF16),
              w_in[:, 3 * a_w + H_FOX:].astype(BF16),
              jnp.pad(w_in[:, 3 * a_w:3 * a_w + H_FOX].T, gate_pad).astype(BF16),
              jnp.pad(b_forget[0][:, None], gate_pad),
              w_pool[0].astype(BF16), pool_scale[0][None])
    w_sb = w_in_sb[0]
    sb_wts = (w_sb[:, :d].astype(BF16), w_sb[:, d:2 * d].T.astype(BF16), w_sb[:, 2 * d:].T.astype(BF16))
    wts = (norm_g, ab_wts, w_out_ab[0].astype(BF16), sb_wts, w_out_sb[0].astype(BF16),
           [w.astype(BF16) for w in w_up], [w.astype(BF16) for w in w_down], final_g)

    y_p, st_p = _trunk(True, x_prompt, mods_p, None, wts)
    past = (cache_fox_k[0], cache_fox_v[0], cache_fox_logf[0], state_pool[0],
            cache_sb_k[0], cache_sb_v[0])
    y_s, st_s = _trunk(False, x_sample, mods_s, past, wts)
    fk, fv, fl, pool, sk, sv = st_p
    fk2, fv2, fl2, pool2, sk2, sv2 = st_s
    return (y_p, y_s, fk, fv, fl, pool, sk, sv, fk2, fv2, fl2, pool2, sk2, sv2)
```

```python
import functools

import jax
import jax.numpy as jnp
from jax import lax
from jax.experimental import pallas as pl
from jax.experimental.pallas import tpu as pltpu

F32 = jnp.float32
BF16 = jnp.bfloat16

DH = 64
H_FOX = 8
EPS = 1e-6
POOL_WINDOWS = (2, 4, 8, 16)
POOL_GC = 128
HIST_ROWS = 16
NEG_BIG = -1e30
LOG2E = 1.4426950408889634
LANES = 128
GATE_ROWS = 16
AUG_ROWS = 16
OFF_SLOT = 12
VMEM_LIMIT = 56 * 1024 * 1024

UNDERFLOW_EXP = 88.0
NORM_MARGIN = 1.01

T_ATT = 256
NQ_ATT = 4
T_SB = 256
NQ_SB = 4
FIRST_SB = 2
TK_DEC_FIRST = 256
TK_DEC_REST = 1280
TK_DEC_FOX = 2048
TM_ROWS = 512
ADA_ROWS = 32
ADA_TN = 1536


def _cparams(sem):
    return pltpu.CompilerParams(dimension_semantics=sem, vmem_limit_bytes=VMEM_LIMIT)


def _rms_mod(x, g, sc, sh):
    y = x * lax.rsqrt(jnp.mean(x * x, axis=-1, keepdims=True) + EPS)
    return (y * g) * (1.0 + sc) + sh


def _softplus(z):
    return jnp.maximum(z, 0.0) + jnp.log(1.0 + jnp.exp2(jnp.abs(z) * (-LOG2E)))


def _log_sigmoid(x):
    return jnp.minimum(x, 0.0) - jnp.log(1.0 + jnp.exp(-jnp.abs(x)))


def _split3(x):
    hi = x.astype(BF16)
    r = x - hi.astype(F32)
    mid = r.astype(BF16)
    lo = (r - mid.astype(F32)).astype(BF16)
    return hi, mid, lo


def _dot(a, b):
    return jnp.dot(a, b, preferred_element_type=F32)


def _dot_nt(a, b):
    return lax.dot_general(a, b, (((1,), (1,)), ((), ())), preferred_element_type=F32)


def _dot3(x, m):
    hi, mid, lo = _split3(x)
    return _dot(hi, m) + _dot(mid, m) + _dot(lo, m)


def _iota(shape, dim):
    return lax.broadcasted_iota(jnp.int32, shape, dim)


def _log2(n):
    assert n & (n - 1) == 0, n
    return n.bit_length() - 1


def _ada_kernel(c_ref, w_ref, b_ref, o_ref):
    c = c_ref[...]
    cond = (c / (1.0 + jnp.exp(-c))).astype(BF16)
    o_ref[...] = _dot(cond, w_ref[...].astype(BF16)) + b_ref[...]


def _ada_call(c_all, w_ada, b_ada):
    depth, d, n = w_ada.shape
    return pl.pallas_call(
        _ada_kernel,
        out_shape=jax.ShapeDtypeStruct((depth, ADA_ROWS, n), F32),
        grid=(depth, n // ADA_TN),
        in_specs=[
            pl.BlockSpec((ADA_ROWS, d), lambda i, j: (0, 0)),
            pl.BlockSpec((None, d, ADA_TN), lambda i, j: (i, 0, j)),
            pl.BlockSpec((None, 1, ADA_TN), lambda i, j: (i, 0, j)),
        ],
        out_specs=pl.BlockSpec((None, ADA_ROWS, ADA_TN), lambda i, j: (i, 0, j)),
        compiler_params=_cparams(("arbitrary", "arbitrary")),
        name="ada_mod",
    )(c_all, w_ada, b_ada.reshape(depth, 1, n))


def _mod_spec(prompt, tm, d, chunk, tiles_per_batch):
    if prompt:
        return pl.BlockSpec((None, 1, d), lambda i: (i // tiles_per_batch, 0, chunk))
    return pl.BlockSpec((tm, d), lambda i: (i, chunk))


def _pool_branch(u, ext_ref, pos, ng, rg, wp_ref, ps_ref, pooled_ref, pst_ref):
    tm, a_w = u.shape
    u3 = u.reshape(ng, rg, a_w)
    ext_ref[:, HIST_ROWS:HIST_ROWS + rg, :] = u3
    pst_ref[...] = u3[:, rg - HIST_ROWS:rg, :]
    for g, w in enumerate(POOL_WINDOWS):
        cs = slice(g * POOL_GC, (g + 1) * POOL_GC)
        acc = u3[:, :, cs]
        for kk in range(1, w):
            acc = acc + ext_ref[:, pl.ds(HIST_ROWS - kk, rg), cs]
        cnt = jnp.minimum(pos + 1, w).astype(F32)
        pooled = acc.reshape(tm, POOL_GC) / cnt - u[:, cs]
        pg = _dot(pooled.astype(BF16), wp_ref[g]) * ps_ref[:, cs]
        pooled_ref[:, cs] = pg.astype(pooled_ref.dtype)


def _ab_in_prompt_kernel(tm, tiles_per_batch, x_ref, sh_ref, sc_ref, g_ref, wq_ref, wkt_ref, wvt_ref,
                         wu_ref, wft_ref, bf_ref, wp_ref, ps_ref,
                         q_ref, ktb_ref, vtb_ref, kt_ref, vt_ref, lft_ref, fcp_ref, kaug_ref, pooled_ref, pst_ref,
                         ext_ref, carry_ref):
    h = _rms_mod(x_ref[...], g_ref[...], sc_ref[...], sh_ref[...]).astype(BF16)
    q_ref[...] = (_dot(h, wq_ref[...]) * (DH ** -0.5)).astype(q_ref.dtype)
    kt = _dot_nt(wkt_ref[...], h)
    kt_ref[...] = kt
    ktb_ref[...] = kt.astype(BF16)
    vt = _dot_nt(wvt_ref[...], h)
    vt_ref[...] = vt
    vtb_ref[...] = vt.astype(BF16)

    tile_in_batch = pl.program_id(0) % tiles_per_batch
    first = tile_in_batch == 0
    logf = _log_sigmoid(_dot_nt(wft_ref[...], h) + bf_ref[...])
    lft_ref[...] = logf[0:H_FOX, :]
    tri = (_iota((tm, tm), 0) <= _iota((tm, tm), 1)).astype(BF16)
    fcum = _dot3(logf, tri) + jnp.where(first, 0.0, carry_ref[...])
    carry_ref[...] = fcum[:, tm - 1:tm]
    parts = _split3(fcum)
    prow = _iota((4 * H_FOX, GATE_ROWS), 0)
    pick = jnp.logical_and((prow & 7) < 2, _iota((4 * H_FOX, GATE_ROWS), 1) == (prow >> 3) * 2 + (prow & 7))
    pick = jnp.where(pick, 1.0, 0.0).astype(BF16)
    fcp_ref[...] = (_dot(pick, parts[0]) + _dot(pick, parts[1])) + _dot(pick, parts[2])

    n_aug = (H_FOX // 2) * AUG_ROWS
    slot = _iota((n_aug, GATE_ROWS), 0) & (AUG_ROWS - 1)
    head0 = (_iota((n_aug, GATE_ROWS), 0) >> _log2(AUG_ROWS)) * 2
    col = _iota((n_aug, GATE_ROWS), 1)
    kaug = None
    for part, piece in enumerate(parts):
        sel = jnp.logical_or(jnp.logical_and(slot == 3 + part, col == head0),
                             jnp.logical_and(slot == 9 + part, col == head0 + 1))
        term = _dot(jnp.where(sel, -1.0, 0.0).astype(BF16), piece)
        kaug = term if kaug is None else kaug + term
    slot1 = _iota((n_aug, 1), 0) & (AUG_ROWS - 1)
    ones = jnp.logical_or(slot1 < 3, jnp.logical_and(slot1 >= 6, slot1 < 9))
    kaug_ref[...] = (kaug + jnp.where(ones, 1.0, 0.0)).astype(BF16)

    @pl.when(first)
    def _():
        ext_ref[:, 0:HIST_ROWS, :] = jnp.zeros((1, HIST_ROWS, ext_ref.shape[2]), F32)

    @pl.when(jnp.logical_not(first))
    def _():
        ext_ref[:, 0:HIST_ROWS, :] = ext_ref[:, tm:tm + HIST_ROWS, :]

    pos = tile_in_batch * tm + _iota((tm, 1), 0)
    _pool_branch(_dot(h, wu_ref[...]), ext_ref, pos, 1, tm, wp_ref, ps_ref, pooled_ref, pst_ref)


def _ab_in_sample_kernel(ng, rg, p0, x_ref, sh_ref, sc_ref, g_ref, wq_ref, wkt_ref, wvt_ref,
                         wu_ref, wft_ref, bf_ref, wp_ref, ps_ref, hist_ref,
                         q_ref, kb_ref, vb_ref, k32_ref, v32_ref, lft_ref, pooled_ref, pst_ref, ext_ref):
    tm = ng * rg
    h = _rms_mod(x_ref[...], g_ref[...], sc_ref[...], sh_ref[...]).astype(BF16)
    q_ref[...] = (_dot(h, wq_ref[...]) * (DH ** -0.5)).astype(q_ref.dtype)
    k = _dot_nt(h, wkt_ref[...])
    k32_ref[...] = k
    kb_ref[...] = k.astype(BF16)
    v = _dot_nt(h, wvt_ref[...])
    v32_ref[...] = v
    vb_ref[...] = v.astype(BF16)
    logf = _log_sigmoid(_dot_nt(wft_ref[...], h) + bf_ref[...])
    lft_ref[...] = logf[0:H_FOX, :]
    ext_ref[:, 0:HIST_ROWS, :] = hist_ref[...]
    pos = p0 + (_iota((tm, 1), 0) & (rg - 1))
    _pool_branch(_dot(h, wu_ref[...]), ext_ref, pos, ng, rg, wp_ref, ps_ref, pooled_ref, pst_ref)


def _ab_in_call(prompt, x2, mod_arr, g, wts, hist, rows_per_batch, p0):
    wq, wkt, wvt, wu, wft, bf, wp, ps = wts
    m, d = x2.shape
    a_w = H_FOX * DH
    nb = m // rows_per_batch
    tm = TM_ROWS if prompt else m
    tpb = rows_per_batch // tm if prompt else 1
    row = lambda i: (i, 0)
    const2 = lambda i: (0, 0)
    in_specs = [pl.BlockSpec((tm, d), row), _mod_spec(prompt, tm, d, 0, tpb), _mod_spec(prompt, tm, d, 1, tpb),
                pl.BlockSpec((1, d), const2)]
    in_specs += [pl.BlockSpec(w.shape, const2) for w in (wq, wkt, wvt, wu, wft, bf)]
    in_specs += [pl.BlockSpec(wp.shape, lambda i: (0, 0, 0)), pl.BlockSpec(ps.shape, const2)]
    args = [x2, mod_arr, mod_arr, g, wq, wkt, wvt, wu, wft, bf, wp, ps]
    if prompt:
        tcol = lambda i: (i // tpb, 0, i % tpb)
        out_shape = [jax.ShapeDtypeStruct((m, a_w), BF16),
                     jax.ShapeDtypeStruct((nb, a_w, rows_per_batch), BF16),
                     jax.ShapeDtypeStruct((nb, a_w, rows_per_batch), BF16),
                     jax.ShapeDtypeStruct((nb, a_w, rows_per_batch), F32),
                     jax.ShapeDtypeStruct((nb, a_w, rows_per_batch), F32),
                     jax.ShapeDtypeStruct((nb, H_FOX, rows_per_batch), F32),
                     jax.ShapeDtypeStruct((nb, 4 * H_FOX, rows_per_batch), F32),
                     jax.ShapeDtypeStruct((nb, (H_FOX // 2) * AUG_ROWS, rows_per_batch), BF16),
                     jax.ShapeDtypeStruct((m, a_w), BF16),
                     jax.ShapeDtypeStruct((nb, HIST_ROWS, a_w), F32)]
        out_specs = ([pl.BlockSpec((tm, a_w), row)] + [pl.BlockSpec((None, a_w, tm), tcol)] * 4
                     + [pl.BlockSpec((None, H_FOX, tm), tcol), pl.BlockSpec((None, 4 * H_FOX, tm), tcol)]
                     + [pl.BlockSpec((None, (H_FOX // 2) * AUG_ROWS, tm), tcol)]
                     + [pl.BlockSpec((tm, a_w), row),
                        pl.BlockSpec((1, HIST_ROWS, a_w), lambda i: (i // tpb, 0, 0))])
        scratch = [pltpu.VMEM((1, HIST_ROWS + tm, a_w), F32), pltpu.VMEM((GATE_ROWS, 1), F32)]
        body = functools.partial(_ab_in_prompt_kernel, tm, tpb)
    else:
        in_specs.append(pl.BlockSpec(hist.shape, lambda i: (0, 0, 0)))
        args.append(hist)
        out_shape = [jax.ShapeDtypeStruct((m, a_w), BF16), jax.ShapeDtypeStruct((m, a_w), BF16),
                     jax.ShapeDtypeStruct((m, a_w), BF16), jax.ShapeDtypeStruct((m, a_w), F32),
                     jax.ShapeDtypeStruct((m, a_w), F32), jax.ShapeDtypeStruct((H_FOX, m), F32),
                     jax.ShapeDtypeStruct((m, a_w), BF16), jax.ShapeDtypeStruct((nb, HIST_ROWS, a_w), F32)]
        out_specs = ([pl.BlockSpec((tm, a_w), row)] * 5 + [pl.BlockSpec((H_FOX, tm), const2)]
                     + [pl.BlockSpec((tm, a_w), row), pl.BlockSpec((nb, HIST_ROWS, a_w), lambda i: (0, 0, 0))])
        scratch = [pltpu.VMEM((nb, HIST_ROWS + rows_per_batch, a_w), F32)]
        body = functools.partial(_ab_in_sample_kernel, nb, rows_per_batch, p0)
    return pl.pallas_call(
        body, out_shape=out_shape, grid=(m // tm,), in_specs=in_specs, out_specs=out_specs,
        scratch_shapes=scratch,
        compiler_params=_cparams(("arbitrary",)),
        name="ab_in_prompt" if prompt else "ab_in_sample",
    )(*args)


def _head_split(q2):
    lane = _iota((1, LANES), 1)
    zero = jnp.zeros_like(q2)
    return jnp.where(lane < DH, q2, zero), jnp.where(lane >= DH, q2, zero)


def _pair_sel():
    return (_iota((8, LANES), 0) == (_iota((8, LANES), 1) >> _log2(DH))).astype(BF16)


def _fox_kernel(q_ref, kt_ref, vt_ref, ka_ref, fr_ref, fall_ref, o_ref,
                m_ref, acc_ref, kn_ref):
    t, nq = T_ATT, NQ_ATT
    qg = pl.program_id(2)
    lane = _iota((1, LANES), 1)
    lane2 = _iota((1, 2), 1)

    @pl.when(qg == 0)
    def _():
        chunk = 4 * t

        def body(i, mx):
            ks = pl.multiple_of(i * chunk, chunk)
            kf = kt_ref[:, pl.ds(ks, chunk)].astype(F32)
            n2 = _dot(_pair_sel(), (kf * kf).astype(BF16))
            return jnp.maximum(mx, jnp.max(n2, axis=1, keepdims=True))

        mx = lax.fori_loop(0, kt_ref.shape[1] // chunk, body, jnp.zeros((8, 1), F32))
        kn_ref[...] = jnp.sqrt(mx)

    row_a = _iota((AUG_ROWS, 1), 0)
    row_v = _iota((LANES, 1), 0)
    qcat, bound0 = [], []
    for s in range(nq):
        q2 = q_ref[s * t:(s + 1) * t, :]
        frows = fr_ref[:, s * t:(s + 1) * t]
        eye = (_iota((t, t), 0) == _iota((t, t), 1)).astype(BF16)
        per_head = []
        for j, qh in enumerate(_head_split(q2)):
            hi, mid, lo = (x.astype(F32) for x in _split3(frows[j:j + 1, :]))
            base = 6 * j
            qat = jnp.where(row_a == base, hi, jnp.where(row_a == base + 1, mid, jnp.where(
                row_a == base + 2, lo,
                jnp.where(jnp.logical_and(row_a >= base + 3, row_a < base + 6), 1.0, 0.0))))
            qat = jnp.where(row_a == OFF_SLOT, NEG_BIG, qat)
            qa = _dot_nt(eye, qat.astype(BF16))
            per_head.append(jnp.concatenate([qh, qa.astype(BF16)], axis=1))
        qcat.append(per_head)
        qf = q2.astype(F32)
        qn2 = _dot_nt((qf * qf).astype(BF16), _pair_sel())
        qn = jnp.sqrt(jnp.max(qn2, axis=0, keepdims=True))
        bound0.append([qn[:, j:j + 1] * kn_ref[j:j + 1, :] * NORM_MARGIN + frows[j:j + 1, 0:1] for j in range(2)])
    chains = [(s, j) for s in range(nq) for j in range(2)]
    rep = lambda m: jnp.concatenate([m] * (t // LANES), axis=1)

    def front(n, diag, chains=chains):
        kbs = [qg * nq + s - n for s in range(nq)]
        kss = [pl.multiple_of((kb if diag else jnp.maximum(kb, 0)) * t, t) for kb in kbs]
        causal = _iota((t, t), 1) <= _iota((t, t), 0)
        kcat = []
        for s in range(nq):
            ka = ka_ref[:, pl.ds(kss[s], t)]
            if not diag:
                off = jnp.where(kbs[s] >= 0, 0.0, 1.0).astype(BF16)
                ka = jnp.where(_iota((AUG_ROWS, 1), 0) == OFF_SLOT, off, ka)
            kcat.append(jnp.concatenate([kt_ref[:, pl.ds(kss[s], t)], ka], axis=0))
        sc = []
        for s, j in chains:
            x = _dot(qcat[s][j], kcat[s])
            sc.append(jnp.where(causal, x, NEG_BIG) if diag else x)
        vcat = []
        for s, j in chains:
            v2 = vt_ref[:, pl.ds(kss[s], t)]
            mine = (row_v < DH) if j == 0 else (row_v >= DH)
            vcat.append(jnp.where(mine, v2, jnp.ones_like(v2)))
        return sc, vcat

    (sc0, v0), (sc1, v1) = front(0, True), front(1, False)
    m01 = [jnp.broadcast_to(jnp.maximum(jnp.max(x0, axis=1, keepdims=True), jnp.max(x1, axis=1, keepdims=True)),
                            (t, LANES)) for x0, x1 in zip(sc0, sc1)]
    pv0 = [_dot_nt(jnp.exp(x - rep(m)).astype(BF16), vv) for x, m, vv in zip(sc0, m01, v0)]
    pv1 = [_dot_nt(jnp.exp(x - rep(m)).astype(BF16), vv) for x, m, vv in zip(sc1, m01, v1)]
    for i, (s, j) in enumerate(chains):
        acc_ref[2 * s + j] = pv0[i] + pv1[i]
        m_ref[2 * s + j] = m01[i]

    def step(n, chains):
        sc, vcat = front(n, False, chains)
        m_prev = [m_ref[2 * s + j] for s, j in chains]
        m_new = [jnp.maximum(mp, jnp.max(x, axis=1, keepdims=True)) for mp, x in zip(m_prev, sc)]
        pv = [_dot_nt(jnp.exp(x - rep(mn)).astype(BF16), vv) for x, mn, vv in zip(sc, m_new, vcat)]
        for i, (s, j) in enumerate(chains):
            idx = 2 * s + j
            acc_ref[idx] = jnp.exp(m_prev[i] - m_new[i]) * acc_ref[idx] + pv[i]
            m_ref[idx] = m_new[i]

    def more(n):
        go = [False, False]
        for s in range(nq):
            kb = qg * nq + s - n
            last = pl.multiple_of(jnp.maximum(kb, 0) * t + (t - LANES), LANES)
            ft = fall_ref[:, pl.ds(last, LANES)][:, LANES - 1:LANES]
            for j in range(2):
                top = jnp.max(bound0[s][j] - ft[j:j + 1, :]) - jnp.min(m_ref[2 * s + j])
                go[j] = jnp.logical_or(go[j], jnp.logical_and(kb >= 0, top > -UNDERFLOW_EXP))
        return go

    def body(chains, c):
        n = c[0]
        step(n, chains)
        return (n + 1, *more(n + 1))

    c = lax.while_loop(lambda c: jnp.logical_and(c[1], c[2]), functools.partial(body, chains),
                       (jnp.int32(2), *more(2)))
    c = lax.while_loop(lambda c: c[1], functools.partial(body, [(s, 0) for s in range(nq)]), c)
    lax.while_loop(lambda c: c[2], functools.partial(body, [(s, 1) for s in range(nq)]), c)
    for s in range(nq):
        a0, a1 = acc_ref[2 * s], acc_ref[2 * s + 1]
        o = jnp.where(lane < DH, a0 / pltpu.roll(a0, DH, 1), a1 / pltpu.roll(a1, DH, 1))
        o_ref[s * t:(s + 1) * t, :] = o.astype(o_ref.dtype)


def _fox_call(q, ktb, vtb, kaug, fcp):
    b, l, a_w = q.shape
    hp = a_w // LANES
    t, nq = T_ATT, NQ_ATT
    tq = t * nq
    return pl.pallas_call(
        _fox_kernel,
        out_shape=jax.ShapeDtypeStruct((b, l, a_w), BF16),
        grid=(b, hp, l // tq),
        in_specs=[
            pl.BlockSpec((None, tq, LANES), lambda bi, h, qi: (bi, qi, h)),
            pl.BlockSpec((None, LANES, l), lambda bi, h, qi: (bi, h, 0)),
            pl.BlockSpec((None, LANES, l), lambda bi, h, qi: (bi, h, 0)),
            pl.BlockSpec((None, AUG_ROWS, l), lambda bi, h, qi: (bi, h, 0)),
            pl.BlockSpec((None, 8, tq), lambda bi, h, qi: (bi, h, qi)),
            pl.BlockSpec((None, 8, l), lambda bi, h, qi: (bi, h, 0)),
        ],
        out_specs=pl.BlockSpec((None, tq, LANES), lambda bi, h, qi: (bi, qi, h)),
        scratch_shapes=[pltpu.VMEM((2 * nq, t, LANES), F32), pltpu.VMEM((2 * nq, t, LANES), F32),
                        pltpu.VMEM((8, 1), F32)],
        compiler_params=_cparams(("arbitrary", "arbitrary", "arbitrary")),
        name="fox_prompt",
    )(q, ktb, vtb, kaug, fcp, fcp)


def _sb_kernel(q_ref, kt_ref, vt_ref, o_ref, carry_ref, acc_ref):
    t, nq = T_SB, NQ_SB
    qg = pl.program_id(2)
    lane = _iota((1, LANES), 1)
    qh = [_head_split(q_ref[s * t:(s + 1) * t, :]) for s in range(nq)]
    valid = _iota((t, t), 1) < _iota((t, t), 0)

    def logits(s, n, diag):
        kb = qg * nq + s - n
        ks = pl.multiple_of((kb if diag else jnp.maximum(kb, 0)) * t, t)
        return kb, ks, [_dot(qh[s][j], kt_ref[:, pl.ds(ks, t)]) for j in range(2)]

    def suffix(diag, z):
        tri = (_iota((t, t), 0) > _iota((t, t), 1)).astype(BF16)
        sp = [_softplus(zz) for zz in z]
        spm = [jnp.where(valid, x, 0.0) for x in sp] if diag else sp
        later = [_dot(x.astype(BF16), tri) for x in spm]
        return [zz - x for zz, x in zip(z, sp)], spm, later

    def weigh(diag, kb, ks, ls, spm, later, carry):
        v2 = vt_ref[:, pl.ds(ks, t)]
        if not diag:
            v2 = jnp.where(kb >= 0, v2, jnp.zeros_like(v2))
        pv, new_carry = [], []
        for j in range(2):
            e = ls[j] - later[j]
            a = jnp.exp(e if carry is None else e - carry[j])
            if diag:
                a = jnp.where(valid, a, 0.0)
            pv.append(_dot_nt(a.astype(BF16), v2))
            rs = jnp.sum(spm[j], axis=1, keepdims=True)
            new_carry.append(rs if carry is None else carry[j] + rs)
        return pv, new_carry

    def run(units, carry, acc):
        st1, st2 = {}, {}
        for k in range(len(units) + 2):
            if k < len(units):
                st1[k] = logits(*units[k])
            if 0 <= k - 1 < len(units):
                st2[k - 1] = suffix(units[k - 1][2], st1[k - 1][2])
            if 0 <= k - 2 < len(units):
                s, _, diag = units[k - 2]
                kb, ks, _ = st1.pop(k - 2)
                pv, carry[s] = weigh(diag, kb, ks, *st2.pop(k - 2), carry[s])
                acc[s] = pv if acc[s] is None else [x + y for x, y in zip(acc[s], pv)]
        return carry, acc

    carry, acc = run([(s, n, n == 0) for n in range(FIRST_SB) for s in range(nq)], [None] * nq, [None] * nq)
    for s in range(nq):
        for j in range(2):
            acc_ref[2 * s + j] = acc[s][j]
            carry_ref[2 * s + j] = carry[s][j]

    def more(n):
        in_range = n <= qg * nq + nq - 1
        return jnp.logical_and(in_range, jnp.min(carry_ref[...]) < UNDERFLOW_EXP)

    def body(c):
        n, _ = c
        carry, acc = run([(s, n, False) for s in range(nq)],
                         [[carry_ref[2 * s], carry_ref[2 * s + 1]] for s in range(nq)], [None] * nq)
        for s in range(nq):
            for j in range(2):
                acc_ref[2 * s + j] = acc_ref[2 * s + j] + acc[s][j]
                carry_ref[2 * s + j] = carry[s][j]
        return n + 1, more(n + 1)

    lax.while_loop(lambda c: c[1], body, (jnp.int32(FIRST_SB), more(FIRST_SB)))
    for s in range(nq):
        o = jnp.where(lane < DH, acc_ref[2 * s], acc_ref[2 * s + 1])
        o_ref[s * t:(s + 1) * t, :] = o.astype(o_ref.dtype)


def _sb_call(q, ktb, vtb):
    b, l, d = q.shape
    hp = d // LANES
    t, nq = T_SB, NQ_SB
    tq = t * nq
    return pl.pallas_call(
        _sb_kernel,
        out_shape=jax.ShapeDtypeStruct((b, l, d), BF16),
        grid=(b, hp, l // tq),
        in_specs=[
            pl.BlockSpec((None, tq, LANES), lambda bi, h, qi: (bi, qi, h)),
            pl.BlockSpec((None, LANES, l), lambda bi, h, qi: (bi, h, 0)),
            pl.BlockSpec((None, LANES, l), lambda bi, h, qi: (bi, h, 0)),
        ],
        out_specs=pl.BlockSpec((None, tq, LANES), lambda bi, h, qi: (bi, qi, h)),
        scratch_shapes=[pltpu.VMEM((2 * nq, t, 1), F32), pltpu.VMEM((2 * nq, t, LANES), F32)],
        compiler_params=_cparams(("arbitrary", "arbitrary", "arbitrary")),
        name="sb_prompt",
    )(q, ktb, vtb)


def _rep_matrix(rows, n_heads, ds):
    return ((_iota((rows, ds), 0) >> _log2(n_heads)) == _iota((rows, ds), 1)).astype(BF16)


def _head_mask(rows, n_heads, width):
    return (_iota((rows, width), 0) & (n_heads - 1)) == (_iota((rows, width), 1) >> _log2(DH))


def _block_diag_q(q, n_heads):
    ds, width = q.shape
    rows = ds * n_heads
    qrep = _dot(_rep_matrix(rows, n_heads, ds), q)
    return jnp.where(_head_mask(rows, n_heads, width), qrep, 0.0).astype(BF16)


def _collect_heads(acc, n_heads, ds):
    rows, width = acc.shape
    om = jnp.where(_head_mask(rows, n_heads, width), acc, 0.0).astype(BF16)
    rep_t = ((_iota((ds, rows), 1) >> _log2(n_heads)) == _iota((ds, rows), 0)).astype(BF16)
    return _dot(rep_t, om)


def _suffix_sums(x, exact=True, blk=2 * LANES):
    r, n = x.shape
    parts = [x[:, b * blk:(b + 1) * blk] for b in range(n // blk)]
    tri = (_iota((blk, blk), 0) > _iota((blk, blk), 1)).astype(BF16)
    stacked = jnp.concatenate(parts, axis=0)
    w = _dot3(stacked, tri) if exact else _dot(stacked.astype(BF16), tri)
    out, tail = [], None
    for b in reversed(range(len(parts))):
        cur = w[b * r:(b + 1) * r]
        out.append(cur if tail is None else cur + tail)
        tot = jnp.sum(parts[b], axis=1, keepdims=True)
        tail = tot if tail is None else tail + tot
    return jnp.concatenate(out[::-1], axis=1), tail


def _fox_dec_kernel(n_heads, ds, q_ref, kn_ref, vn_ref, lfn_ref, kc_ref, vc_ref, lfc_ref,
                    o_ref, qbd_ref, m_ref, l_ref, acc_ref, cf_ref):
    rows = ds * n_heads
    step_i = pl.program_id(1)
    n_steps = pl.num_programs(1)

    def update(s, pv_fn):
        m_prev = m_ref[...]
        m_new = jnp.maximum(m_prev, jnp.max(s, axis=1, keepdims=True))
        alpha = jnp.exp(m_prev - m_new)
        p = jnp.exp(s - m_new)
        l_ref[...] = alpha * l_ref[...] + jnp.sum(p, axis=1, keepdims=True)
        acc_ref[...] = alpha * acc_ref[...] + pv_fn(p.astype(BF16))
        m_ref[...] = m_new

    @pl.when(step_i == 0)
    def _():
        qbd_ref[...] = _block_diag_q(q_ref[...], n_heads)
        m_ref[...] = jnp.full(m_ref.shape, NEG_BIG, F32)
        l_ref[...] = jnp.zeros(l_ref.shape, F32)
        acc_ref[...] = jnp.zeros(acc_ref.shape, F32)
        a = jnp.concatenate([lfn_ref[...]] * ds, axis=0)
        ri = _iota((rows, ds), 0) >> _log2(n_heads)
        cj = _iota((rows, ds), 1)
        am = jnp.where(cj <= ri, a, 0.0)
        tri = (_iota((ds, ds), 0) > _iota((ds, ds), 1)).astype(BF16)
        s = _dot_nt(qbd_ref[...], kn_ref[...]) + _dot3(am, tri)
        s = jnp.where(cj <= ri, s, NEG_BIG)
        update(s, lambda p: _dot(p, vn_ref[...]))
        cf_ref[...] = jnp.sum(am, axis=1, keepdims=True)

    @pl.when(step_i > 0)
    def _():
        reps = rows // (2 * n_heads)
        later, total = _suffix_sums(jnp.concatenate([lfc_ref[...]] * 2, axis=0))
        cf = cf_ref[...]
        s = _dot(qbd_ref[...], kc_ref[...].astype(BF16)) + (jnp.concatenate([later] * reps, axis=0) + cf)
        update(s, lambda p: _dot_nt(p, vc_ref[...].astype(BF16)))
        cf_ref[...] = cf + jnp.concatenate([total] * reps, axis=0)

    @pl.when(step_i == n_steps - 1)
    def _():
        o_ref[...] = _collect_heads(acc_ref[...] / l_ref[...], n_heads, ds).astype(o_ref.dtype)


def _fox_dec_call(q, kn, vn, kct, vct, lfn_t, lfc_t):
    db, ds, width = q.shape
    n_heads = width // DH
    rows = ds * n_heads
    tk = min(TK_DEC_FOX, kct.shape[2])
    nkb = kct.shape[2] // tk
    cache_blk = lambda s: jnp.minimum(nkb - s, nkb - 1)
    new_idx = lambda b, s: (b, 0, 0)
    cache_idx = lambda b, s: (b, 0, cache_blk(s))
    in_specs = ([pl.BlockSpec((None, ds, width), new_idx)] * 3
                + [pl.BlockSpec((None, n_heads, ds), new_idx)]
                + [pl.BlockSpec((None, width, tk), cache_idx)] * 2
                + [pl.BlockSpec((None, n_heads, tk), cache_idx)])
    scratch = [pltpu.VMEM((rows, width), BF16), pltpu.VMEM((rows, 1), F32), pltpu.VMEM((rows, 1), F32),
               pltpu.VMEM((rows, width), F32), pltpu.VMEM((rows, 1), F32)]
    return pl.pallas_call(
        functools.partial(_fox_dec_kernel, n_heads, ds),
        out_shape=jax.ShapeDtypeStruct((db, ds, width), BF16),
        grid=(db, nkb + 1),
        in_specs=in_specs,
        out_specs=pl.BlockSpec((None, ds, width), new_idx),
        scratch_shapes=scratch,
        compiler_params=_cparams(("arbitrary", "arbitrary")),
        name="fox_decode",
    )(q, kn, vn, lfn_t, kct, vct, lfc_t)


def _sb_dec_tile(qbd, kc_ref, vc_ref, carry):
    z = _dot(qbd, kc_ref[...].astype(BF16))
    sp = _softplus(z)
    later, total = _suffix_sums(sp, exact=False)
    a = jnp.exp((z - sp) - later - carry)
    return _dot_nt(a.astype(BF16), vc_ref[...].astype(BF16)), carry + total


def _sb_dec_first_kernel(n_heads, ds, q_ref, kn_ref, vn_ref, kc_ref, vc_ref,
                         o_ref, carry_out_ref, cmin_ref):
    rows = ds * n_heads
    qbd = _block_diag_q(q_ref[...], n_heads)
    ri = _iota((rows, ds), 0) >> _log2(n_heads)
    cj = _iota((rows, ds), 1)
    valid = cj < ri
    z = _dot_nt(qbd, kn_ref[...])
    sp = _softplus(z)
    spm = jnp.where(valid, sp, 0.0)
    tri = (_iota((ds, ds), 0) > _iota((ds, ds), 1)).astype(BF16)
    later = _dot(spm.astype(BF16), tri)
    a = jnp.where(valid, jnp.exp((z - sp) - later), 0.0)
    acc = _dot(a.astype(BF16), vn_ref[...])
    carry = jnp.sum(spm, axis=1, keepdims=True)
    part, carry = _sb_dec_tile(qbd, kc_ref, vc_ref, carry)
    o_ref[...] = _collect_heads(acc + part, n_heads, ds).astype(o_ref.dtype)
    carry_out_ref[...] = carry
    cmin_ref[...] = jnp.broadcast_to(jnp.min(carry, axis=0, keepdims=True), cmin_ref.shape)


def _sb_dec_rest_kernel(n_heads, ds, need_ref, q_ref, carry_in_ref, o1_ref, kc_ref, vc_ref,
                        o_ref, qbd_ref, acc_ref, carry_ref):
    step_i = pl.program_id(1)
    last = pl.num_programs(1) - 1
    needed = need_ref[pl.program_id(0)] != 0

    @pl.when(needed)
    def _():
        @pl.when(step_i == 0)
        def _():
            qbd_ref[...] = _block_diag_q(q_ref[...], n_heads)
            acc_ref[...] = jnp.zeros(acc_ref.shape, F32)
            carry_ref[...] = carry_in_ref[...]

        @pl.when(jnp.min(carry_ref[...]) < UNDERFLOW_EXP)
        def _():
            part, carry = _sb_dec_tile(qbd_ref[...], kc_ref, vc_ref, carry_ref[...])
            acc_ref[...] = acc_ref[...] + part
            carry_ref[...] = carry

        @pl.when(step_i == last)
        def _():
            o = o1_ref[...].astype(F32) + _collect_heads(acc_ref[...], n_heads, ds)
            o_ref[...] = o.astype(o_ref.dtype)

    @pl.when(jnp.logical_and(jnp.logical_not(needed), step_i == last))
    def _():
        o_ref[...] = o1_ref[...]


def _sb_dec_call(q, kn, vn, kct, vct):
    db, ds, width = q.shape
    n_heads = width // DH
    rows = ds * n_heads
    past = kct.shape[2]
    first = min(TK_DEC_FIRST, past)
    n_blk = (past - first) // (2 * LANES)
    per = max([k for k in range(1, TK_DEC_REST // (2 * LANES) + 1) if n_blk % k == 0], default=1)
    rest = per * 2 * LANES
    n_rest = (past - first) // rest
    bidx = lambda b: (b, 0, 0)
    o1, carry, cmin = pl.pallas_call(
        functools.partial(_sb_dec_first_kernel, n_heads, ds),
        out_shape=[jax.ShapeDtypeStruct((db, ds, width), BF16),
                   jax.ShapeDtypeStruct((db, rows, 1), F32),
                   jax.ShapeDtypeStruct((db, 1, LANES), F32)],
        grid=(db,),
        in_specs=[pl.BlockSpec((None, ds, width), bidx)] * 3
        + [pl.BlockSpec((None, width, first), lambda b: (b, 0, past // first - 1))] * 2,
        out_specs=[pl.BlockSpec((None, ds, width), bidx), pl.BlockSpec((None, rows, 1), bidx),
                   pl.BlockSpec((None, 1, LANES), bidx)],
        compiler_params=_cparams(("arbitrary",)),
        name="sb_decode_first",
    )(q, kn, vn, kct, vct)
    if n_rest == 0:
        return o1
    need = (cmin[:, 0, 0] < UNDERFLOW_EXP).astype(jnp.int32)
    bidx2 = lambda b, s, need_ref: (b, 0, 0)
    tile = (None, width, rest)

    def cache_idx(b, s, need_ref):
        on = need_ref[b] != 0
        return (jnp.where(on, b, 0), 0, jnp.where(on, n_rest - 1 - s, 0))

    grid_spec = pltpu.PrefetchScalarGridSpec(
        num_scalar_prefetch=1,
        grid=(db, n_rest),
        in_specs=[pl.BlockSpec((None, ds, width), bidx2), pl.BlockSpec((None, rows, 1), bidx2),
                  pl.BlockSpec((None, ds, width), bidx2),
                  pl.BlockSpec(tile, cache_idx), pl.BlockSpec(tile, cache_idx)],
        out_specs=pl.BlockSpec((None, ds, width), bidx2),
        scratch_shapes=[pltpu.VMEM((rows, width), BF16), pltpu.VMEM((rows, width), F32),
                        pltpu.VMEM((rows, 1), F32)],
    )
    return pl.pallas_call(
        functools.partial(_sb_dec_rest_kernel, n_heads, ds),
        out_shape=jax.ShapeDtypeStruct((db, ds, width), BF16),
        grid_spec=grid_spec,
        compiler_params=_cparams(("arbitrary", "arbitrary")),
        name="sb_decode_rest",
    )(need, q, carry, o1, kct, vct)


def _post_kernel(n_act, final, f_chunk, *refs):
    acts = refs[:n_act]
    wo_ref, x_ref, gate1_ref, sh_ref, sc_ref, gate_ref, g_ref, wu_ref, wd_ref = refs[n_act:n_act + 9]
    fg_ref = refs[n_act + 9] if final else None
    o_ref = refs[-1]
    y = None
    off = 0
    for a_ref in acts:
        kdim = a_ref.shape[1]
        part = _dot(a_ref[...], wo_ref[off:off + kdim, :])
        y = part if y is None else y + part
        off += kdim
    x = x_ref[...] + gate1_ref[...] * y
    h = _rms_mod(x, g_ref[...], sc_ref[...], sh_ref[...]).astype(BF16)
    d_ff = wu_ref.shape[1]
    acc = None
    for c in range(d_ff // f_chunk):
        cs = slice(c * f_chunk, (c + 1) * f_chunk)
        a = jnp.maximum(_dot(h, wu_ref[:, cs]), 0.0)
        part = _dot((a * a).astype(BF16), wd_ref[cs, :])
        acc = part if acc is None else acc + part
    xo = x + gate_ref[...] * acc
    if final:
        y = xo * lax.rsqrt(jnp.mean(xo * xo, axis=-1, keepdims=True) + EPS)
        xo = y * fg_ref[...]
    o_ref[...] = xo


def _post_call(prompt, acts, wo, x2, mod_arr, g, layer, wu, wd, final_g, rows_per_batch):
    m, d = x2.shape
    tm = TM_ROWS if prompt else m
    tpb = rows_per_batch // tm if prompt else 1
    row = lambda i: (i, 0)
    const2 = lambda i: (0, 0)
    final = final_g is not None
    resident = lambda w: pl.BlockSpec(w.shape, const2, pipeline_mode=pl.Buffered(1))
    in_specs = [pl.BlockSpec((tm, a.shape[1]), row) for a in acts]
    in_specs += [resident(wo), pl.BlockSpec((tm, d), row)]
    in_specs += [_mod_spec(prompt, tm, d, c, tpb) for c in (2, 3, 4, 5)]
    stacked = lambda w: pl.BlockSpec((None,) + w.shape[1:], lambda i: (layer, 0, 0), pipeline_mode=pl.Buffered(1))
    in_specs += [pl.BlockSpec((1, d), const2), stacked(wu), stacked(wd)]
    args = list(acts) + [wo, x2] + [mod_arr] * 4 + [g, wu, wd]
    if final:
        in_specs.append(pl.BlockSpec((1, d), const2))
        args.append(final_g)
    return pl.pallas_call(
        functools.partial(_post_kernel, len(acts), final, 1024),
        out_shape=jax.ShapeDtypeStruct((m, d), F32),
        grid=(m // tm,), in_specs=in_specs, out_specs=pl.BlockSpec((tm, d), row),
        compiler_params=_cparams(("arbitrary",)),
        name="post",
    )(*args)


def _sb_in_prompt_kernel(x_ref, sh_ref, sc_ref, g_ref, wq_ref, wkt_ref, wvt_ref,
                         q_ref, ktb_ref, vtb_ref, kt_ref, vt_ref):
    h = _rms_mod(x_ref[...], g_ref[...], sc_ref[...], sh_ref[...]).astype(BF16)
    q_ref[...] = (_dot(h, wq_ref[...]) * (DH ** -0.5)).astype(q_ref.dtype)
    kt = _dot_nt(wkt_ref[...], h)
    kt_ref[...] = kt
    ktb_ref[...] = kt.astype(BF16)
    vt = _dot_nt(wvt_ref[...], h)
    vt_ref[...] = vt
    vtb_ref[...] = vt.astype(BF16)


def _sb_in_sample_kernel(x_ref, sh_ref, sc_ref, g_ref, wq_ref, wkt_ref, wvt_ref,
                         q_ref, kb_ref, vb_ref, k32_ref, v32_ref):
    h = _rms_mod(x_ref[...], g_ref[...], sc_ref[...], sh_ref[...]).astype(BF16)
    q_ref[...] = (_dot(h, wq_ref[...]) * (DH ** -0.5)).astype(q_ref.dtype)
    k = _dot_nt(h, wkt_ref[...])
    k32_ref[...] = k
    kb_ref[...] = k.astype(BF16)
    v = _dot_nt(h, wvt_ref[...])
    v32_ref[...] = v
    vb_ref[...] = v.astype(BF16)


def _sb_in_call(prompt, x2, mod_arr, g, wts, rows_per_batch):
    wq, wkt, wvt = wts
    m, d = x2.shape
    nb = m // rows_per_batch
    tm = TM_ROWS if prompt else m
    tpb = rows_per_batch // tm if prompt else 1
    row = lambda i: (i, 0)
    const2 = lambda i: (0, 0)
    in_specs = [pl.BlockSpec((tm, d), row),
                _mod_spec(prompt, tm, d, 0, tpb), _mod_spec(prompt, tm, d, 1, tpb),
                pl.BlockSpec((1, d), const2)] + [pl.BlockSpec(w.shape, const2) for w in wts]
    if prompt:
        tcol = lambda i: (i // tpb, 0, i % tpb)
        out_shape = ([jax.ShapeDtypeStruct((m, d), BF16)]
                     + [jax.ShapeDtypeStruct((nb, d, rows_per_batch), BF16)] * 2
                     + [jax.ShapeDtypeStruct((nb, d, rows_per_batch), F32)] * 2)
        out_specs = [pl.BlockSpec((tm, d), row)] + [pl.BlockSpec((None, d, tm), tcol)] * 4
        body = _sb_in_prompt_kernel
    else:
        out_shape = [jax.ShapeDtypeStruct((m, d), BF16)] * 3 + [jax.ShapeDtypeStruct((m, d), F32)] * 2
        out_specs = [pl.BlockSpec((tm, d), row)] * 5
        body = _sb_in_sample_kernel
    return pl.pallas_call(
        body, out_shape=out_shape, grid=(m // tm,), in_specs=in_specs, out_specs=out_specs,
        compiler_params=_cparams(("arbitrary",)),
        name="sb_in_prompt" if prompt else "sb_in_sample",
    )(x2, mod_arr, mod_arr, g, wq, wkt, wvt)


def _to_state(xt, n_heads):
    nb, _, l = xt.shape
    return xt.reshape(nb, n_heads, DH, l).transpose(0, 3, 1, 2)[None]


def _from_cache(c):
    nb, p, n_heads, dh = c.shape
    return c.transpose(0, 2, 3, 1).reshape(nb, n_heads * dh, p)


def _trunk(prompt, x, mods, past, wts):
    nb, rpb, d = x.shape
    m = nb * rpb
    a_w = H_FOX * DH
    x2 = x.reshape(m, d)
    norm_g, ab_wts, w_out_ab, sb_wts, w_out_sb, w_up, w_down, final_g = wts

    if prompt:
        outs = _ab_in_call(True, x2, mods[0], norm_g[0, 0][None], ab_wts, None, rpb, 0)
        q, ktb, vtb, kt32, vt32, lft, fcp, kaug, pooled, pstate = outs
        a_out = _fox_call(q.reshape(nb, rpb, a_w), ktb, vtb, kaug, fcp)
        st_fox = (_to_state(kt32, H_FOX), _to_state(vt32, H_FOX), lft.transpose(0, 2, 1)[None])
    else:
        cfk, cfv, cflogf, spool, csk, csv = past
        p0 = cfk.shape[1]
        hist = jnp.pad(spool, ((0, 0), (HIST_ROWS - spool.shape[1], 0), (0, 0)))
        outs = _ab_in_call(False, x2, mods[0], norm_g[0, 0][None], ab_wts, hist, rpb, p0)
        q, kb, vb, k32, v32, lft, pooled, pstate = outs
        lfn_t = lft.reshape(H_FOX, nb, rpb).transpose(1, 0, 2)
        a_out = _fox_dec_call(q.reshape(nb, rpb, a_w), kb.reshape(nb, rpb, a_w), vb.reshape(nb, rpb, a_w),
                              _from_cache(cfk), _from_cache(cfv), lfn_t, cflogf.transpose(0, 2, 1))
        st_fox = (k32.reshape(1, nb, rpb, H_FOX, DH), v32.reshape(1, nb, rpb, H_FOX, DH),
                  lfn_t.transpose(0, 2, 1)[None])
    x2 = _post_call(prompt, [a_out.reshape(m, a_w), pooled], w_out_ab, x2, mods[0], norm_g[0, 1][None],
                    0, w_up, w_down, None, rpb)

    n_heads = d // DH
    outs = _sb_in_call(prompt, x2, mods[1], norm_g[1, 0][None], sb_wts, rpb)
    if prompt:
        q1, ktb1, vtb1, skt32, svt32 = outs
        s_out = _sb_call(q1.reshape(nb, rpb, d), ktb1, vtb1)
        st_sb = (_to_state(skt32, n_heads), _to_state(svt32, n_heads))
    else:
        q1, kb1, vb1, sk32, sv32 = outs
        s_out = _sb_dec_call(q1.reshape(nb, rpb, d), kb1.reshape(nb, rpb, d), vb1.reshape(nb, rpb, d),
                             _from_cache(csk), _from_cache(csv))
        st_sb = (sk32.reshape(1, nb, rpb, n_heads, DH), sv32.reshape(1, nb, rpb, n_heads, DH))
    y = _post_call(prompt, [s_out.reshape(m, d)], w_out_sb, x2, mods[1], norm_g[1, 1][None],
                   1, w_up, w_down, final_g[None], rpb)

    states = st_fox + (pstate[:, 1:, :][None],) + st_sb
    return y.reshape(nb, rpb, d), states


def kernel(x_prompt, x_sample, c_prompt, c_sample, cache_fox_k, cache_fox_v, cache_fox_logf, state_pool, cache_sb_k, cache_sb_v, w_ada, b_ada, norm_g, w_in_ab, b_forget, w_pool, pool_scale, w_out_ab, w_in_sb, w_out_sb, w_up, w_down, final_g):
    b, l, d = x_prompt.shape
    db, ds, _ = x_sample.shape
    a_w = H_FOX * DH

    c_all = jnp.concatenate([c_prompt, c_sample, jnp.zeros((ADA_ROWS - b - db, d), F32)], axis=0)
    mod = _ada_call(c_all, w_ada, b_ada)
    mods_p = [mod[i, :b][:, None, :] for i in range(mod.shape[0])]
    mods_s = [jnp.repeat(mod[i, b:b + db], ds, axis=0) for i in range(mod.shape[0])]

    w_in = w_in_ab[0]
    gate_pad = ((0, GATE_ROWS - H_FOX), (0, 0))
    ab_wts = (w_in[:, :a_w].astype(BF16),
              w_in[:, a_w:2 * a_w].T.astype(BF16),
              w_in[:, 2 * a_w:3 * a_w].T.astype(BF16),
              w_in[:, 3 * a_w + H_FOX:].astype(BF16),
              jnp.pad(w_in[:, 3 * a_w:3 * a_w + H_FOX].T, gate_pad).astype(BF16),
              jnp.pad(b_forget[0][:, None], gate_pad),
              w_pool[0].astype(BF16), pool_scale[0][None])
    w_sb = w_in_sb[0]
    sb_wts = (w_sb[:, :d].astype(BF16), w_sb[:, d:2 * d].T.astype(BF16), w_sb[:, 2 * d:].T.astype(BF16))
    wts = (norm_g, ab_wts, w_out_ab[0].astype(BF16), sb_wts, w_out_sb[0].astype(BF16),
           w_up.astype(BF16), w_down.astype(BF16), final_g)

    y_p, st_p = _trunk(True, x_prompt, mods_p, None, wts)
    past = (cache_fox_k[0], cache_fox_v[0], cache_fox_logf[0], state_pool[0],
            cache_sb_k[0], cache_sb_v[0])
    y_s, st_s = _trunk(False, x_sample, mods_s, past, wts)
    fk, fv, fl, pool, sk, sv = st_p
    fk2, fv2, fl2, pool2, sk2, sv2 = st_s
    return (y_p, y_s, fk, fv, fl, pool, sk, sv, fk2, fv2, fl2, pool2, sk2, sv2)
```

```python
import functools

import jax
import jax.numpy as jnp
from jax import lax
from jax.experimental import pallas as pl
from jax.experimental.pallas import tpu as pltpu

F32 = jnp.float32
BF16 = jnp.bfloat16

DH = 64
H_FOX = 8
EPS = 1e-6
POOL_WINDOWS = (2, 4, 8, 16)
POOL_GC = 128
HIST_ROWS = 16
NEG_BIG = -1e30
LOG2E = 1.4426950408889634
LANES = 128
GATE_ROWS = 16
AUG_ROWS = 16
OFF_SLOT = 12
VMEM_LIMIT = 56 * 1024 * 1024

UNDERFLOW_EXP = 88.0
NORM_MARGIN = 1.01

T_ATT = 256
NQ_ATT = 4
T_SB = 256
NQ_SB = 4
FIRST_SB = 2
TK_DEC_FIRST = 256
TK_DEC_REST = 1280
TK_DEC_FOX = 2048
TM_ROWS = 512
ADA_ROWS = 32
ADA_TN = 1536


def _cparams(sem):
    return pltpu.CompilerParams(dimension_semantics=sem, vmem_limit_bytes=VMEM_LIMIT)


def _rms_mod(x, g, sc, sh):
    y = x * lax.rsqrt(jnp.mean(x * x, axis=-1, keepdims=True) + EPS)
    return (y * g) * (1.0 + sc) + sh


def _softplus(z):
    return jnp.maximum(z, 0.0) + jnp.log(1.0 + jnp.exp2(jnp.abs(z) * (-LOG2E)))


def _log_sigmoid(x):
    return jnp.minimum(x, 0.0) - jnp.log(1.0 + jnp.exp(-jnp.abs(x)))


def _split3(x):
    hi = x.astype(BF16)
    r = x - hi.astype(F32)
    mid = r.astype(BF16)
    lo = (r - mid.astype(F32)).astype(BF16)
    return hi, mid, lo


def _dot(a, b):
    return jnp.dot(a, b, preferred_element_type=F32)


def _dot_nt(a, b):
    return lax.dot_general(a, b, (((1,), (1,)), ((), ())), preferred_element_type=F32)


def _dot3(x, m):
    hi, mid, lo = _split3(x)
    return _dot(hi, m) + _dot(mid, m) + _dot(lo, m)


def _iota(shape, dim):
    return lax.broadcasted_iota(jnp.int32, shape, dim)


def _log2(n):
    assert n & (n - 1) == 0, n
    return n.bit_length() - 1


def _ada_kernel(c_ref, w_ref, b_ref, o_ref):
    c = c_ref[...]
    cond = (c / (1.0 + jnp.exp(-c))).astype(BF16)
    o_ref[...] = _dot(cond, w_ref[...].astype(BF16)) + b_ref[...]


def _ada_call(c_all, w_ada, b_ada):
    depth, d, n = w_ada.shape
    return pl.pallas_call(
        _ada_kernel,
        out_shape=jax.ShapeDtypeStruct((depth, ADA_ROWS, n), F32),
        grid=(depth, n // ADA_TN),
        in_specs=[
            pl.BlockSpec((ADA_ROWS, d), lambda i, j: (0, 0)),
            pl.BlockSpec((None, d, ADA_TN), lambda i, j: (i, 0, j)),
            pl.BlockSpec((None, 1, ADA_TN), lambda i, j: (i, 0, j)),
        ],
        out_specs=pl.BlockSpec((None, ADA_ROWS, ADA_TN), lambda i, j: (i, 0, j)),
        compiler_params=_cparams(("arbitrary", "arbitrary")),
        name="ada_mod",
    )(c_all, w_ada, b_ada.reshape(depth, 1, n))


def _mod_spec(prompt, tm, d, chunk, tiles_per_batch):
    if prompt:
        return pl.BlockSpec((None, 1, d), lambda i: (i // tiles_per_batch, 0, chunk))
    return pl.BlockSpec((tm, d), lambda i: (i, chunk))


def _pool_branch(u, ext_ref, pos, ng, rg, wp_ref, ps_ref, pooled_ref, pst_ref):
    tm, a_w = u.shape
    u3 = u.reshape(ng, rg, a_w)
    ext_ref[:, HIST_ROWS:HIST_ROWS + rg, :] = u3
    pst_ref[...] = u3[:, rg - HIST_ROWS:rg, :]
    for g, w in enumerate(POOL_WINDOWS):
        cs = slice(g * POOL_GC, (g + 1) * POOL_GC)
        acc = u3[:, :, cs]
        for kk in range(1, w):
            acc = acc + ext_ref[:, pl.ds(HIST_ROWS - kk, rg), cs]
        cnt = jnp.minimum(pos + 1, w).astype(F32)
        pooled = acc.reshape(tm, POOL_GC) / cnt - u[:, cs]
        pg = _dot(pooled.astype(BF16), wp_ref[g]) * ps_ref[:, cs]
        pooled_ref[:, cs] = pg.astype(pooled_ref.dtype)


def _ab_in_prompt_kernel(tm, tiles_per_batch, x_ref, sh_ref, sc_ref, g_ref, wq_ref, wkt_ref, wvt_ref,
                         wu_ref, wft_ref, bf_ref, wp_ref, ps_ref,
                         q_ref, ktb_ref, vtb_ref, kt_ref, vt_ref, lft_ref, fcp_ref, kaug_ref, pooled_ref, pst_ref,
                         ext_ref, carry_ref):
    h = _rms_mod(x_ref[...], g_ref[...], sc_ref[...], sh_ref[...]).astype(BF16)
    q_ref[...] = (_dot(h, wq_ref[...]) * (DH ** -0.5)).astype(q_ref.dtype)
    kt = _dot_nt(wkt_ref[...], h)
    kt_ref[...] = kt
    ktb_ref[...] = kt.astype(BF16)
    vt = _dot_nt(wvt_ref[...], h)
    vt_ref[...] = vt
    vtb_ref[...] = vt.astype(BF16)

    tile_in_batch = pl.program_id(0) % tiles_per_batch
    first = tile_in_batch == 0
    logf = _log_sigmoid(_dot_nt(wft_ref[...], h) + bf_ref[...])
    lft_ref[...] = logf[0:H_FOX, :]
    tri = (_iota((tm, tm), 0) <= _iota((tm, tm), 1)).astype(BF16)
    fcum = _dot3(logf, tri) + jnp.where(first, 0.0, carry_ref[...])
    carry_ref[...] = fcum[:, tm - 1:tm]
    parts = _split3(fcum)
    prow = _iota((4 * H_FOX, GATE_ROWS), 0)
    pick = jnp.logical_and((prow & 7) < 2, _iota((4 * H_FOX, GATE_ROWS), 1) == (prow >> 3) * 2 + (prow & 7))
    pick = jnp.where(pick, 1.0, 0.0).astype(BF16)
    fcp_ref[...] = (_dot(pick, parts[0]) + _dot(pick, parts[1])) + _dot(pick, parts[2])

    n_aug = (H_FOX // 2) * AUG_ROWS
    slot = _iota((n_aug, GATE_ROWS), 0) & (AUG_ROWS - 1)
    head0 = (_iota((n_aug, GATE_ROWS), 0) >> _log2(AUG_ROWS)) * 2
    col = _iota((n_aug, GATE_ROWS), 1)
    kaug = None
    for part, piece in enumerate(parts):
        sel = jnp.logical_or(jnp.logical_and(slot == 3 + part, col == head0),
                             jnp.logical_and(slot == 9 + part, col == head0 + 1))
        term = _dot(jnp.where(sel, -1.0, 0.0).astype(BF16), piece)
        kaug = term if kaug is None else kaug + term
    slot1 = _iota((n_aug, 1), 0) & (AUG_ROWS - 1)
    ones = jnp.logical_or(slot1 < 3, jnp.logical_and(slot1 >= 6, slot1 < 9))
    kaug_ref[...] = (kaug + jnp.where(ones, 1.0, 0.0)).astype(BF16)

    @pl.when(first)
    def _():
        ext_ref[:, 0:HIST_ROWS, :] = jnp.zeros((1, HIST_ROWS, ext_ref.shape[2]), F32)

    @pl.when(jnp.logical_not(first))
    def _():
        ext_ref[:, 0:HIST_ROWS, :] = ext_ref[:, tm:tm + HIST_ROWS, :]

    pos = tile_in_batch * tm + _iota((tm, 1), 0)
    _pool_branch(_dot(h, wu_ref[...]), ext_ref, pos, 1, tm, wp_ref, ps_ref, pooled_ref, pst_ref)


def _ab_in_sample_kernel(ng, rg, p0, x_ref, sh_ref, sc_ref, g_ref, wq_ref, wkt_ref, wvt_ref,
                         wu_ref, wft_ref, bf_ref, wp_ref, ps_ref, hist_ref,
                         q_ref, kb_ref, vb_ref, k32_ref, v32_ref, lft_ref, pooled_ref, pst_ref, ext_ref):
    tm = ng * rg
    h = _rms_mod(x_ref[...], g_ref[...], sc_ref[...], sh_ref[...]).astype(BF16)
    q_ref[...] = (_dot(h, wq_ref[...]) * (DH ** -0.5)).astype(q_ref.dtype)
    k = _dot_nt(h, wkt_ref[...])
    k32_ref[...] = k
    kb_ref[...] = k.astype(BF16)
    v = _dot_nt(h, wvt_ref[...])
    v32_ref[...] = v
    vb_ref[...] = v.astype(BF16)
    logf = _log_sigmoid(_dot_nt(wft_ref[...], h) + bf_ref[...])
    lft_ref[...] = logf[0:H_FOX, :]
    ext_ref[:, 0:HIST_ROWS, :] = hist_ref[...]
    pos = p0 + (_iota((tm, 1), 0) & (rg - 1))
    _pool_branch(_dot(h, wu_ref[...]), ext_ref, pos, ng, rg, wp_ref, ps_ref, pooled_ref, pst_ref)


def _ab_in_call(prompt, x2, mod_arr, g, wts, hist, rows_per_batch, p0):
    wq, wkt, wvt, wu, wft, bf, wp, ps = wts
    m, d = x2.shape
    a_w = H_FOX * DH
    nb = m // rows_per_batch
    tm = TM_ROWS if prompt else m
    tpb = rows_per_batch // tm if prompt else 1
    row = lambda i: (i, 0)
    const2 = lambda i: (0, 0)
    in_specs = [pl.BlockSpec((tm, d), row), _mod_spec(prompt, tm, d, 0, tpb), _mod_spec(prompt, tm, d, 1, tpb),
                pl.BlockSpec((1, d), const2)]
    in_specs += [pl.BlockSpec(w.shape, const2) for w in (wq, wkt, wvt, wu, wft, bf)]
    in_specs += [pl.BlockSpec(wp.shape, lambda i: (0, 0, 0)), pl.BlockSpec(ps.shape, const2)]
    args = [x2, mod_arr, mod_arr, g, wq, wkt, wvt, wu, wft, bf, wp, ps]
    if prompt:
        tcol = lambda i: (i // tpb, 0, i % tpb)
        out_shape = [jax.ShapeDtypeStruct((m, a_w), BF16),
                     jax.ShapeDtypeStruct((nb, a_w, rows_per_batch), BF16),
                     jax.ShapeDtypeStruct((nb, a_w, rows_per_batch), BF16),
                     jax.ShapeDtypeStruct((nb, a_w, rows_per_batch), F32),
                     jax.ShapeDtypeStruct((nb, a_w, rows_per_batch), F32),
                     jax.ShapeDtypeStruct((nb, H_FOX, rows_per_batch), F32),
                     jax.ShapeDtypeStruct((nb, 4 * H_FOX, rows_per_batch), F32),
                     jax.ShapeDtypeStruct((nb, (H_FOX // 2) * AUG_ROWS, rows_per_batch), BF16),
                     jax.ShapeDtypeStruct((m, a_w), BF16),
                     jax.ShapeDtypeStruct((nb, HIST_ROWS, a_w), F32)]
        out_specs = ([pl.BlockSpec((tm, a_w), row)] + [pl.BlockSpec((None, a_w, tm), tcol)] * 4
                     + [pl.BlockSpec((None, H_FOX, tm), tcol), pl.BlockSpec((None, 4 * H_FOX, tm), tcol)]
                     + [pl.BlockSpec((None, (H_FOX // 2) * AUG_ROWS, tm), tcol)]
                     + [pl.BlockSpec((tm, a_w), row),
                        pl.BlockSpec((1, HIST_ROWS, a_w), lambda i: (i // tpb, 0, 0))])
        scratch = [pltpu.VMEM((1, HIST_ROWS + tm, a_w), F32), pltpu.VMEM((GATE_ROWS, 1), F32)]
        body = functools.partial(_ab_in_prompt_kernel, tm, tpb)
    else:
        in_specs.append(pl.BlockSpec(hist.shape, lambda i: (0, 0, 0)))
        args.append(hist)
        out_shape = [jax.ShapeDtypeStruct((m, a_w), BF16), jax.ShapeDtypeStruct((m, a_w), BF16),
                     jax.ShapeDtypeStruct((m, a_w), BF16), jax.ShapeDtypeStruct((m, a_w), F32),
                     jax.ShapeDtypeStruct((m, a_w), F32), jax.ShapeDtypeStruct((H_FOX, m), F32),
                     jax.ShapeDtypeStruct((m, a_w), BF16), jax.ShapeDtypeStruct((nb, HIST_ROWS, a_w), F32)]
        out_specs = ([pl.BlockSpec((tm, a_w), row)] * 5 + [pl.BlockSpec((H_FOX, tm), const2)]
                     + [pl.BlockSpec((tm, a_w), row), pl.BlockSpec((nb, HIST_ROWS, a_w), lambda i: (0, 0, 0))])
        scratch = [pltpu.VMEM((nb, HIST_ROWS + rows_per_batch, a_w), F32)]
        body = functools.partial(_ab_in_sample_kernel, nb, rows_per_batch, p0)
    return pl.pallas_call(
        body, out_shape=out_shape, grid=(m // tm,), in_specs=in_specs, out_specs=out_specs,
        scratch_shapes=scratch,
        compiler_params=_cparams(("arbitrary",)),
        name="ab_in_prompt" if prompt else "ab_in_sample",
    )(*args)


def _head_split(q2):
    lane = _iota((1, LANES), 1)
    zero = jnp.zeros_like(q2)
    return jnp.where(lane < DH, q2, zero), jnp.where(lane >= DH, q2, zero)


def _pair_sel():
    return (_iota((8, LANES), 0) == (_iota((8, LANES), 1) >> _log2(DH))).astype(BF16)


def _fox_kernel(q_ref, kt_ref, vt_ref, ka_ref, fr_ref, fall_ref, o_ref,
                m_ref, acc_ref, kn_ref):
    t, nq = T_ATT, NQ_ATT
    qg = pl.program_id(2)
    lane = _iota((1, LANES), 1)

    @pl.when(qg == 0)
    def _():
        chunk = 4 * t

        def body(i, mx):
            ks = pl.multiple_of(i * chunk, chunk)
            kf = kt_ref[:, pl.ds(ks, chunk)].astype(F32)
            n2 = _dot(_pair_sel(), (kf * kf).astype(BF16))
            return jnp.maximum(mx, jnp.max(n2, axis=1, keepdims=True))

        mx = lax.fori_loop(0, kt_ref.shape[1] // chunk, body, jnp.zeros((8, 1), F32))
        kn_ref[...] = jnp.sqrt(mx)

    row_a = _iota((AUG_ROWS, 1), 0)
    row_v = _iota((LANES, 1), 0)
    qcat, bound0 = [], []
    for s in range(nq):
        q2 = q_ref[s * t:(s + 1) * t, :]
        frows = fr_ref[:, s * t:(s + 1) * t]
        eye = (_iota((t, t), 0) == _iota((t, t), 1)).astype(BF16)
        per_head = []
        for j, qh in enumerate(_head_split(q2)):
            hi, mid, lo = (x.astype(F32) for x in _split3(frows[j:j + 1, :]))
            base = 6 * j
            qat = jnp.where(row_a == base, hi, jnp.where(row_a == base + 1, mid, jnp.where(
                row_a == base + 2, lo,
                jnp.where(jnp.logical_and(row_a >= base + 3, row_a < base + 6), 1.0, 0.0))))
            qat = jnp.where(row_a == OFF_SLOT, NEG_BIG, qat)
            qa = _dot_nt(eye, qat.astype(BF16))
            per_head.append(jnp.concatenate([qh, qa.astype(BF16)], axis=1))
        qcat.append(per_head)
        qf = q2.astype(F32)
        qn2 = _dot_nt((qf * qf).astype(BF16), _pair_sel())
        qn = jnp.sqrt(jnp.max(qn2, axis=0, keepdims=True))
        bound0.append([qn[:, j:j + 1] * kn_ref[j:j + 1, :] * NORM_MARGIN + frows[j:j + 1, 0:1] for j in range(2)])
    chains = [(s, j) for s in range(nq) for j in range(2)]
    rep = lambda m: jnp.concatenate([m] * (t // LANES), axis=1)

    def front(n, diag):
        kbs = [qg * nq + s - n for s in range(nq)]
        kss = [pl.multiple_of((kb if diag else jnp.maximum(kb, 0)) * t, t) for kb in kbs]
        causal = _iota((t, t), 1) <= _iota((t, t), 0)
        kcat = []
        for s in range(nq):
            ka = ka_ref[:, pl.ds(kss[s], t)]
            if not diag:
                off = jnp.where(kbs[s] >= 0, 0.0, 1.0).astype(BF16)
                ka = jnp.where(_iota((AUG_ROWS, 1), 0) == OFF_SLOT, off, ka)
            kcat.append(jnp.concatenate([kt_ref[:, pl.ds(kss[s], t)], ka], axis=0))
        sc = []
        for s, j in chains:
            x = _dot(qcat[s][j], kcat[s])
            sc.append(jnp.where(causal, x, NEG_BIG) if diag else x)
        vcat = []
        for s, j in chains:
            v2 = vt_ref[:, pl.ds(kss[s], t)]
            mine = (row_v < DH) if j == 0 else (row_v >= DH)
            vcat.append(jnp.where(mine, v2, jnp.ones_like(v2)))
        return sc, vcat

    (sc0, v0), (sc1, v1) = front(0, True), front(1, False)
    m01 = [jnp.broadcast_to(jnp.maximum(jnp.max(x0, axis=1, keepdims=True), jnp.max(x1, axis=1, keepdims=True)),
                            (t, LANES)) for x0, x1 in zip(sc0, sc1)]
    pv0 = [_dot_nt(jnp.exp(x - rep(m)).astype(BF16), vv) for x, m, vv in zip(sc0, m01, v0)]
    pv1 = [_dot_nt(jnp.exp(x - rep(m)).astype(BF16), vv) for x, m, vv in zip(sc1, m01, v1)]
    for i, (s, j) in enumerate(chains):
        acc_ref[2 * s + j] = pv0[i] + pv1[i]
        m_ref[2 * s + j] = m01[i]

    def front_wide(n, chains):
        kcat, vts = [], []
        for s in range(nq):
            kb = qg * nq + s - n
            start = pl.multiple_of(jnp.maximum(kb - 1, 0) * t, t)
            off = jnp.where(start + _iota((1, 2 * t), 1) >= (kb + 1) * t, 1.0, 0.0).astype(BF16)
            ka = jnp.where(_iota((AUG_ROWS, 1), 0) == OFF_SLOT, off, ka_ref[:, pl.ds(start, 2 * t)])
            kcat.append(jnp.concatenate([kt_ref[:, pl.ds(start, 2 * t)], ka], axis=0))
            vts.append(vt_ref[:, pl.ds(start, 2 * t)])
        sc = [_dot(qcat[s][j], kcat[s]) for s, j in chains]
        vcat = [jnp.where((row_v < DH) if j == 0 else (row_v >= DH), vts[s], jnp.ones_like(vts[s]))
                for s, j in chains]
        return sc, vcat

    def step(n, chains):
        sc, vcat = front_wide(n, chains)
        m_prev = [m_ref[2 * s + j] for s, j in chains]
        m_new = [jnp.maximum(mp, jnp.max(x, axis=1, keepdims=True)) for mp, x in zip(m_prev, sc)]
        pv = [_dot_nt(jnp.exp(x - jnp.concatenate([rep(mn)] * 2, axis=1)).astype(BF16), vv)
              for x, mn, vv in zip(sc, m_new, vcat)]
        for i, (s, j) in enumerate(chains):
            idx = 2 * s + j
            acc_ref[idx] = jnp.exp(m_prev[i] - m_new[i]) * acc_ref[idx] + pv[i]
            m_ref[idx] = m_new[i]

    def more(n):
        go = [False, False]
        for s in range(nq):
            kb = qg * nq + s - n
            last = pl.multiple_of(jnp.maximum(kb, 0) * t + (t - LANES), LANES)
            ft = fall_ref[:, pl.ds(last, LANES)][:, LANES - 1:LANES]
            for j in range(2):
                top = jnp.max(bound0[s][j] - ft[j:j + 1, :]) - jnp.min(m_ref[2 * s + j])
                go[j] = jnp.logical_or(go[j], jnp.logical_and(kb >= 0, top > -UNDERFLOW_EXP))
        return go

    def body(chains, c):
        n = c[0]
        step(n, chains)
        return (n + 2, *more(n + 2))

    c = lax.while_loop(lambda c: jnp.logical_and(c[1], c[2]), functools.partial(body, chains),
                       (jnp.int32(2), *more(2)))
    c = lax.while_loop(lambda c: c[1], functools.partial(body, [(s, 0) for s in range(nq)]), c)
    lax.while_loop(lambda c: c[2], functools.partial(body, [(s, 1) for s in range(nq)]), c)
    for s in range(nq):
        a0, a1 = acc_ref[2 * s], acc_ref[2 * s + 1]
        o = jnp.where(lane < DH, a0 / pltpu.roll(a0, DH, 1), a1 / pltpu.roll(a1, DH, 1))
        o_ref[s * t:(s + 1) * t, :] = o.astype(o_ref.dtype)


def _fox_call(q, ktb, vtb, kaug, fcp):
    b, l, a_w = q.shape
    hp = a_w // LANES
    t, nq = T_ATT, NQ_ATT
    tq = t * nq
    return pl.pallas_call(
        _fox_kernel,
        out_shape=jax.ShapeDtypeStruct((b, l, a_w), BF16),
        grid=(b, hp, l // tq),
        in_specs=[
            pl.BlockSpec((None, tq, LANES), lambda bi, h, qi: (bi, qi, h)),
            pl.BlockSpec((None, LANES, l), lambda bi, h, qi: (bi, h, 0)),
            pl.BlockSpec((None, LANES, l), lambda bi, h, qi: (bi, h, 0)),
            pl.BlockSpec((None, AUG_ROWS, l), lambda bi, h, qi: (bi, h, 0)),
            pl.BlockSpec((None, 8, tq), lambda bi, h, qi: (bi, h, qi)),
            pl.BlockSpec((None, 8, l), lambda bi, h, qi: (bi, h, 0)),
        ],
        out_specs=pl.BlockSpec((None, tq, LANES), lambda bi, h, qi: (bi, qi, h)),
        scratch_shapes=[pltpu.VMEM((2 * nq, t, LANES), F32), pltpu.VMEM((2 * nq, t, LANES), F32),
                        pltpu.VMEM((8, 1), F32)],
        compiler_params=_cparams(("arbitrary", "arbitrary", "arbitrary")),
        name="fox_prompt",
    )(q, ktb, vtb, kaug, fcp, fcp)


def _sb_kernel(q_ref, kt_ref, vt_ref, o_ref, carry_ref, acc_ref):
    t, nq = T_SB, NQ_SB
    qg = pl.program_id(2)
    lane = _iota((1, LANES), 1)
    qh = [_head_split(q_ref[s * t:(s + 1) * t, :]) for s in range(nq)]
    valid = _iota((t, t), 1) < _iota((t, t), 0)

    def logits(s, n, diag):
        kb = qg * nq + s - n
        ks = pl.multiple_of((kb if diag else jnp.maximum(kb, 0)) * t, t)
        return kb, ks, [_dot(qh[s][j], kt_ref[:, pl.ds(ks, t)]) for j in range(2)]

    def suffix(diag, z):
        tri = (_iota((t, t), 0) > _iota((t, t), 1)).astype(BF16)
        sp = [_softplus(zz) for zz in z]
        spm = [jnp.where(valid, x, 0.0) for x in sp] if diag else sp
        later = [_dot(x.astype(BF16), tri) for x in spm]
        return [zz - x for zz, x in zip(z, sp)], spm, later

    def weigh(diag, kb, ks, ls, spm, later, carry):
        v2 = vt_ref[:, pl.ds(ks, t)]
        if not diag:
            v2 = jnp.where(kb >= 0, v2, jnp.zeros_like(v2))
        pv, new_carry = [], []
        for j in range(2):
            e = ls[j] - later[j]
            a = jnp.exp(e if carry is None else e - carry[j])
            if diag:
                a = jnp.where(valid, a, 0.0)
            pv.append(_dot_nt(a.astype(BF16), v2))
            rs = jnp.sum(spm[j], axis=1, keepdims=True)
            new_carry.append(rs if carry is None else carry[j] + rs)
        return pv, new_carry

    def run(units, carry, acc):
        st1, st2 = {}, {}
        for k in range(len(units) + 2):
            if k < len(units):
                st1[k] = logits(*units[k])
            if 0 <= k - 1 < len(units):
                st2[k - 1] = suffix(units[k - 1][2], st1[k - 1][2])
            if 0 <= k - 2 < len(units):
                s, _, diag = units[k - 2]
                kb, ks, _ = st1.pop(k - 2)
                pv, carry[s] = weigh(diag, kb, ks, *st2.pop(k - 2), carry[s])
                acc[s] = pv if acc[s] is None else [x + y for x, y in zip(acc[s], pv)]
        return carry, acc

    carry, acc = run([(s, n, n == 0) for n in range(FIRST_SB) for s in range(nq)], [None] * nq, [None] * nq)
    for s in range(nq):
        for j in range(2):
            acc_ref[2 * s + j] = acc[s][j]
            carry_ref[2 * s + j] = carry[s][j]

    def more(n):
        in_range = n <= qg * nq + nq - 1
        return jnp.logical_and(in_range, jnp.min(carry_ref[...]) < UNDERFLOW_EXP)

    def body(c):
        n, _ = c
        carry, acc = run([(s, n, False) for s in range(nq)],
                         [[carry_ref[2 * s], carry_ref[2 * s + 1]] for s in range(nq)], [None] * nq)
        for s in range(nq):
            for j in range(2):
                acc_ref[2 * s + j] = acc_ref[2 * s + j] + acc[s][j]
                carry_ref[2 * s + j] = carry[s][j]
        return n + 1, more(n + 1)

    lax.while_loop(lambda c: c[1], body, (jnp.int32(FIRST_SB), more(FIRST_SB)))
    for s in range(nq):
        o = jnp.where(lane < DH, acc_ref[2 * s], acc_ref[2 * s + 1])
        o_ref[s * t:(s + 1) * t, :] = o.astype(o_ref.dtype)


def _sb_call(q, ktb, vtb):
    b, l, d = q.shape
    hp = d // LANES
    t, nq = T_SB, NQ_SB
    tq = t * nq
    return pl.pallas_call(
        _sb_kernel,
        out_shape=jax.ShapeDtypeStruct((b, l, d), BF16),
        grid=(b, hp, l // tq),
        in_specs=[
            pl.BlockSpec((None, tq, LANES), lambda bi, h, qi: (bi, qi, h)),
            pl.BlockSpec((None, LANES, l), lambda bi, h, qi: (bi, h, 0)),
            pl.BlockSpec((None, LANES, l), lambda bi, h, qi: (bi, h, 0)),
        ],
        out_specs=pl.BlockSpec((None, tq, LANES), lambda bi, h, qi: (bi, qi, h)),
        scratch_shapes=[pltpu.VMEM((2 * nq, t, 1), F32), pltpu.VMEM((2 * nq, t, LANES), F32)],
        compiler_params=_cparams(("arbitrary", "arbitrary", "arbitrary")),
        name="sb_prompt",
    )(q, ktb, vtb)


def _rep_matrix(rows, n_heads, ds):
    return ((_iota((rows, ds), 0) >> _log2(n_heads)) == _iota((rows, ds), 1)).astype(BF16)


def _head_mask(rows, n_heads, width):
    return (_iota((rows, width), 0) & (n_heads - 1)) == (_iota((rows, width), 1) >> _log2(DH))


def _block_diag_q(q, n_heads):
    ds, width = q.shape
    rows = ds * n_heads
    qrep = _dot(_rep_matrix(rows, n_heads, ds), q)
    return jnp.where(_head_mask(rows, n_heads, width), qrep, 0.0).astype(BF16)


def _collect_heads(acc, n_heads, ds):
    rows, width = acc.shape
    om = jnp.where(_head_mask(rows, n_heads, width), acc, 0.0).astype(BF16)
    rep_t = ((_iota((ds, rows), 1) >> _log2(n_heads)) == _iota((ds, rows), 0)).astype(BF16)
    return _dot(rep_t, om)


def _suffix_sums(x, exact=True, blk=2 * LANES):
    r, n = x.shape
    parts = [x[:, b * blk:(b + 1) * blk] for b in range(n // blk)]
    tri = (_iota((blk, blk), 0) > _iota((blk, blk), 1)).astype(BF16)
    stacked = jnp.concatenate(parts, axis=0)
    w = _dot3(stacked, tri) if exact else _dot(stacked.astype(BF16), tri)
    out, tail = [], None
    for b in reversed(range(len(parts))):
        cur = w[b * r:(b + 1) * r]
        out.append(cur if tail is None else cur + tail)
        tot = jnp.sum(parts[b], axis=1, keepdims=True)
        tail = tot if tail is None else tail + tot
    return jnp.concatenate(out[::-1], axis=1), tail


def _fox_dec_kernel(n_heads, ds, q_ref, kn_ref, vn_ref, lfn_ref, kc_ref, vc_ref, lfc_ref,
                    o_ref, qbd_ref, m_ref, l_ref, acc_ref, cf_ref):
    rows = ds * n_heads
    step_i = pl.program_id(1)
    n_steps = pl.num_programs(1)

    def update(s, pv_fn):
        m_prev = m_ref[...]
        m_new = jnp.maximum(m_prev, jnp.max(s, axis=1, keepdims=True))
        alpha = jnp.exp(m_prev - m_new)
        p = jnp.exp(s - m_new)
        l_ref[...] = alpha * l_ref[...] + jnp.sum(p, axis=1, keepdims=True)
        acc_ref[...] = alpha * acc_ref[...] + pv_fn(p.astype(BF16))
        m_ref[...] = m_new

    @pl.when(step_i == 0)
    def _():
        qbd_ref[...] = _block_diag_q(q_ref[...], n_heads)
        m_ref[...] = jnp.full(m_ref.shape, NEG_BIG, F32)
        l_ref[...] = jnp.zeros(l_ref.shape, F32)
        acc_ref[...] = jnp.zeros(acc_ref.shape, F32)
        a = jnp.concatenate([lfn_ref[...]] * ds, axis=0)
        ri = _iota((rows, ds), 0) >> _log2(n_heads)
        cj = _iota((rows, ds), 1)
        am = jnp.where(cj <= ri, a, 0.0)
        tri = (_iota((ds, ds), 0) > _iota((ds, ds), 1)).astype(BF16)
        s = _dot_nt(qbd_ref[...], kn_ref[...]) + _dot3(am, tri)
        s = jnp.where(cj <= ri, s, NEG_BIG)
        update(s, lambda p: _dot(p, vn_ref[...]))
        cf_ref[...] = jnp.sum(am, axis=1, keepdims=True)

    @pl.when(step_i > 0)
    def _():
        reps = rows // (2 * n_heads)
        later, total = _suffix_sums(jnp.concatenate([lfc_ref[...]] * 2, axis=0))
        cf = cf_ref[...]
        s = _dot(qbd_ref[...], kc_ref[...].astype(BF16)) + (jnp.concatenate([later] * reps, axis=0) + cf)
        update(s, lambda p: _dot_nt(p, vc_ref[...].astype(BF16)))
        cf_ref[...] = cf + jnp.concatenate([total] * reps, axis=0)

    @pl.when(step_i == n_steps - 1)
    def _():
        o_ref[...] = _collect_heads(acc_ref[...] / l_ref[...], n_heads, ds).astype(o_ref.dtype)


def _fox_dec_call(q, kn, vn, kct, vct, lfn_t, lfc_t):
    db, ds, width = q.shape
    n_heads = width // DH
    rows = ds * n_heads
    tk = min(TK_DEC_FOX, kct.shape[2])
    nkb = kct.shape[2] // tk
    cache_blk = lambda s: jnp.minimum(nkb - s, nkb - 1)
    new_idx = lambda b, s: (b, 0, 0)
    cache_idx = lambda b, s: (b, 0, cache_blk(s))
    in_specs = ([pl.BlockSpec((None, ds, width), new_idx)] * 3
                + [pl.BlockSpec((None, n_heads, ds), new_idx)]
                + [pl.BlockSpec((None, width, tk), cache_idx)] * 2
                + [pl.BlockSpec((None, n_heads, tk), cache_idx)])
    scratch = [pltpu.VMEM((rows, width), BF16), pltpu.VMEM((rows, 1), F32), pltpu.VMEM((rows, 1), F32),
               pltpu.VMEM((rows, width), F32), pltpu.VMEM((rows, 1), F32)]
    return pl.pallas_call(
        functools.partial(_fox_dec_kernel, n_heads, ds),
        out_shape=jax.ShapeDtypeStruct((db, ds, width), BF16),
        grid=(db, nkb + 1),
        in_specs=in_specs,
        out_specs=pl.BlockSpec((None, ds, width), new_idx),
        scratch_shapes=scratch,
        compiler_params=_cparams(("arbitrary", "arbitrary")),
        name="fox_decode",
    )(q, kn, vn, lfn_t, kct, vct, lfc_t)


def _sb_dec_tile(qbd, kc_ref, vc_ref, carry):
    z = _dot(qbd, kc_ref[...].astype(BF16))
    sp = _softplus(z)
    later, total = _suffix_sums(sp, exact=False)
    a = jnp.exp((z - sp) - later - carry)
    return _dot_nt(a.astype(BF16), vc_ref[...].astype(BF16)), carry + total


def _sb_dec_first_kernel(n_heads, ds, q_ref, kn_ref, vn_ref, kc_ref, vc_ref,
                         o_ref, carry_out_ref, cmin_ref):
    rows = ds * n_heads
    qbd = _block_diag_q(q_ref[...], n_heads)
    ri = _iota((rows, ds), 0) >> _log2(n_heads)
    cj = _iota((rows, ds), 1)
    valid = cj < ri
    z = _dot_nt(qbd, kn_ref[...])
    sp = _softplus(z)
    spm = jnp.where(valid, sp, 0.0)
    tri = (_iota((ds, ds), 0) > _iota((ds, ds), 1)).astype(BF16)
    later = _dot(spm.astype(BF16), tri)
    a = jnp.where(valid, jnp.exp((z - sp) - later), 0.0)
    acc = _dot(a.astype(BF16), vn_ref[...])
    carry = jnp.sum(spm, axis=1, keepdims=True)
    part, carry = _sb_dec_tile(qbd, kc_ref, vc_ref, carry)
    o_ref[...] = _collect_heads(acc + part, n_heads, ds).astype(o_ref.dtype)
    carry_out_ref[...] = carry
    cmin_ref[...] = jnp.broadcast_to(jnp.min(carry, axis=0, keepdims=True), cmin_ref.shape)


def _sb_dec_rest_kernel(n_heads, ds, need_ref, q_ref, carry_in_ref, o1_ref, kc_ref, vc_ref,
                        o_ref, qbd_ref, acc_ref, carry_ref):
    step_i = pl.program_id(1)
    last = pl.num_programs(1) - 1
    needed = need_ref[pl.program_id(0)] != 0

    @pl.when(needed)
    def _():
        @pl.when(step_i == 0)
        def _():
            qbd_ref[...] = _block_diag_q(q_ref[...], n_heads)
            acc_ref[...] = jnp.zeros(acc_ref.shape, F32)
            carry_ref[...] = carry_in_ref[...]

        @pl.when(jnp.min(carry_ref[...]) < UNDERFLOW_EXP)
        def _():
            part, carry = _sb_dec_tile(qbd_ref[...], kc_ref, vc_ref, carry_ref[...])
            acc_ref[...] = acc_ref[...] + part
            carry_ref[...] = carry

        @pl.when(step_i == last)
        def _():
            o = o1_ref[...].astype(F32) + _collect_heads(acc_ref[...], n_heads, ds)
            o_ref[...] = o.astype(o_ref.dtype)

    @pl.when(jnp.logical_and(jnp.logical_not(needed), step_i == last))
    def _():
        o_ref[...] = o1_ref[...]


def _sb_dec_call(q, kn, vn, kct, vct):
    db, ds, width = q.shape
    n_heads = width // DH
    rows = ds * n_heads
    past = kct.shape[2]
    first = min(TK_DEC_FIRST, past)
    n_blk = (past - first) // (2 * LANES)
    per = max([k for k in range(1, TK_DEC_REST // (2 * LANES) + 1) if n_blk % k == 0], default=1)
    rest = per * 2 * LANES
    n_rest = (past - first) // rest
    bidx = lambda b: (b, 0, 0)
    o1, carry, cmin = pl.pallas_call(
        functools.partial(_sb_dec_first_kernel, n_heads, ds),
        out_shape=[jax.ShapeDtypeStruct((db, ds, width), BF16),
                   jax.ShapeDtypeStruct((db, rows, 1), F32),
                   jax.ShapeDtypeStruct((db, 1, LANES), F32)],
        grid=(db,),
        in_specs=[pl.BlockSpec((None, ds, width), bidx)] * 3
        + [pl.BlockSpec((None, width, first), lambda b: (b, 0, past // first - 1))] * 2,
        out_specs=[pl.BlockSpec((None, ds, width), bidx), pl.BlockSpec((None, rows, 1), bidx),
                   pl.BlockSpec((None, 1, LANES), bidx)],
        compiler_params=_cparams(("arbitrary",)),
        name="sb_decode_first",
    )(q, kn, vn, kct, vct)
    if n_rest == 0:
        return o1
    need = (cmin[:, 0, 0] < UNDERFLOW_EXP).astype(jnp.int32)
    bidx2 = lambda b, s, need_ref: (b, 0, 0)
    tile = (None, width, rest)

    def cache_idx(b, s, need_ref):
        on = need_ref[b] != 0
        return (jnp.where(on, b, 0), 0, jnp.where(on, n_rest - 1 - s, 0))

    grid_spec = pltpu.PrefetchScalarGridSpec(
        num_scalar_prefetch=1,
        grid=(db, n_rest),
        in_specs=[pl.BlockSpec((None, ds, width), bidx2), pl.BlockSpec((None, rows, 1), bidx2),
                  pl.BlockSpec((None, ds, width), bidx2),
                  pl.BlockSpec(tile, cache_idx), pl.BlockSpec(tile, cache_idx)],
        out_specs=pl.BlockSpec((None, ds, width), bidx2),
        scratch_shapes=[pltpu.VMEM((rows, width), BF16), pltpu.VMEM((rows, width), F32),
                        pltpu.VMEM((rows, 1), F32)],
    )
    return pl.pallas_call(
        functools.partial(_sb_dec_rest_kernel, n_heads, ds),
        out_shape=jax.ShapeDtypeStruct((db, ds, width), BF16),
        grid_spec=grid_spec,
        compiler_params=_cparams(("arbitrary", "arbitrary")),
        name="sb_decode_rest",
    )(need, q, carry, o1, kct, vct)


def _post_kernel(n_act, final, f_chunk, *refs):
    acts = refs[:n_act]
    wo_ref, x_ref, gate1_ref, sh_ref, sc_ref, gate_ref, g_ref, wu_ref, wd_ref = refs[n_act:n_act + 9]
    fg_ref = refs[n_act + 9] if final else None
    o_ref = refs[-1]
    y = None
    off = 0
    for a_ref in acts:
        kdim = a_ref.shape[1]
        part = _dot(a_ref[...], wo_ref[off:off + kdim, :])
        y = part if y is None else y + part
        off += kdim
    x = x_ref[...] + gate1_ref[...] * y
    h = _rms_mod(x, g_ref[...], sc_ref[...], sh_ref[...]).astype(BF16)
    d_ff = wu_ref.shape[1]
    acc = None
    for c in range(d_ff // f_chunk):
        cs = slice(c * f_chunk, (c + 1) * f_chunk)
        a = jnp.maximum(_dot(h, wu_ref[:, cs]), 0.0)
        part = _dot((a * a).astype(BF16), wd_ref[cs, :])
        acc = part if acc is None else acc + part
    xo = x + gate_ref[...] * acc
    if final:
        y = xo * lax.rsqrt(jnp.mean(xo * xo, axis=-1, keepdims=True) + EPS)
        xo = y * fg_ref[...]
    o_ref[...] = xo


def _post_call(prompt, acts, wo, x2, mod_arr, g, layer, wu, wd, final_g, rows_per_batch):
    m, d = x2.shape
    tm = TM_ROWS if prompt else m
    tpb = rows_per_batch // tm if prompt else 1
    row = lambda i: (i, 0)
    const2 = lambda i: (0, 0)
    final = final_g is not None
    resident = lambda w: pl.BlockSpec(w.shape, const2, pipeline_mode=pl.Buffered(1))
    in_specs = [pl.BlockSpec((tm, a.shape[1]), row) for a in acts]
    in_specs += [resident(wo), pl.BlockSpec((tm, d), row)]
    in_specs += [_mod_spec(prompt, tm, d, c, tpb) for c in (2, 3, 4, 5)]
    stacked = lambda w: pl.BlockSpec((None,) + w.shape[1:], lambda i: (layer, 0, 0), pipeline_mode=pl.Buffered(1))
    in_specs += [pl.BlockSpec((1, d), const2), stacked(wu), stacked(wd)]
    args = list(acts) + [wo, x2] + [mod_arr] * 4 + [g, wu, wd]
    if final:
        in_specs.append(pl.BlockSpec((1, d), const2))
        args.append(final_g)
    return pl.pallas_call(
        functools.partial(_post_kernel, len(acts), final, 1024),
        out_shape=jax.ShapeDtypeStruct((m, d), F32),
        grid=(m // tm,), in_specs=in_specs, out_specs=pl.BlockSpec((tm, d), row),
        compiler_params=_cparams(("arbitrary",)),
        name="post",
    )(*args)


def _sb_in_prompt_kernel(x_ref, sh_ref, sc_ref, g_ref, wq_ref, wkt_ref, wvt_ref,
                         q_ref, ktb_ref, vtb_ref, kt_ref, vt_ref):
    h = _rms_mod(x_ref[...], g_ref[...], sc_ref[...], sh_ref[...]).astype(BF16)
    q_ref[...] = (_dot(h, wq_ref[...]) * (DH ** -0.5)).astype(q_ref.dtype)
    kt = _dot_nt(wkt_ref[...], h)
    kt_ref[...] = kt
    ktb_ref[...] = kt.astype(BF16)
    vt = _dot_nt(wvt_ref[...], h)
    vt_ref[...] = vt
    vtb_ref[...] = vt.astype(BF16)


def _sb_in_sample_kernel(x_ref, sh_ref, sc_ref, g_ref, wq_ref, wkt_ref, wvt_ref,
                         q_ref, kb_ref, vb_ref, k32_ref, v32_ref):
    h = _rms_mod(x_ref[...], g_ref[...], sc_ref[...], sh_ref[...]).astype(BF16)
    q_ref[...] = (_dot(h, wq_ref[...]) * (DH ** -0.5)).astype(q_ref.dtype)
    k = _dot_nt(h, wkt_ref[...])
    k32_ref[...] = k
    kb_ref[...] = k.astype(BF16)
    v = _dot_nt(h, wvt_ref[...])
    v32_ref[...] = v
    vb_ref[...] = v.astype(BF16)


def _sb_in_call(prompt, x2, mod_arr, g, wts, rows_per_batch):
    wq, wkt, wvt = wts
    m, d = x2.shape
    nb = m // rows_per_batch
    tm = TM_ROWS if prompt else m
    tpb = rows_per_batch // tm if prompt else 1
    row = lambda i: (i, 0)
    const2 = lambda i: (0, 0)
    in_specs = [pl.BlockSpec((tm, d), row),
                _mod_spec(prompt, tm, d, 0, tpb), _mod_spec(prompt, tm, d, 1, tpb),
                pl.BlockSpec((1, d), const2)] + [pl.BlockSpec(w.shape, const2) for w in wts]
    if prompt:
        tcol = lambda i: (i // tpb, 0, i % tpb)
        out_shape = ([jax.ShapeDtypeStruct((m, d), BF16)]
                     + [jax.ShapeDtypeStruct((nb, d, rows_per_batch), BF16)] * 2
                     + [jax.ShapeDtypeStruct((nb, d, rows_per_batch), F32)] * 2)
        out_specs = [pl.BlockSpec((tm, d), row)] + [pl.BlockSpec((None, d, tm), tcol)] * 4
        body = _sb_in_prompt_kernel
    else:
        out_shape = [jax.ShapeDtypeStruct((m, d), BF16)] * 3 + [jax.ShapeDtypeStruct((m, d), F32)] * 2
        out_specs = [pl.BlockSpec((tm, d), row)] * 5
        body = _sb_in_sample_kernel
    return pl.pallas_call(
        body, out_shape=out_shape, grid=(m // tm,), in_specs=in_specs, out_specs=out_specs,
        compiler_params=_cparams(("arbitrary",)),
        name="sb_in_prompt" if prompt else "sb_in_sample",
    )(x2, mod_arr, mod_arr, g, wq, wkt, wvt)


def _to_state(xt, n_heads):
    nb, _, l = xt.shape
    return xt.reshape(nb, n_heads, DH, l).transpose(0, 3, 1, 2)[None]


def _from_cache(c):
    nb, p, n_heads, dh = c.shape
    return c.transpose(0, 2, 3, 1).reshape(nb, n_heads * dh, p)


def _trunk(prompt, x, mods, past, wts):
    nb, rpb, d = x.shape
    m = nb * rpb
    a_w = H_FOX * DH
    x2 = x.reshape(m, d)
    norm_g, ab_wts, w_out_ab, sb_wts, w_out_sb, w_up, w_down, final_g = wts

    if prompt:
        outs = _ab_in_call(True, x2, mods[0], norm_g[0, 0][None], ab_wts, None, rpb, 0)
        q, ktb, vtb, kt32, vt32, lft, fcp, kaug, pooled, pstate = outs
        a_out = _fox_call(q.reshape(nb, rpb, a_w), ktb, vtb, kaug, fcp)
        st_fox = (_to_state(kt32, H_FOX), _to_state(vt32, H_FOX), lft.transpose(0, 2, 1)[None])
    else:
        cfk, cfv, cflogf, spool, csk, csv = past
        p0 = cfk.shape[1]
        hist = jnp.pad(spool, ((0, 0), (HIST_ROWS - spool.shape[1], 0), (0, 0)))
        outs = _ab_in_call(False, x2, mods[0], norm_g[0, 0][None], ab_wts, hist, rpb, p0)
        q, kb, vb, k32, v32, lft, pooled, pstate = outs
        lfn_t = lft.reshape(H_FOX, nb, rpb).transpose(1, 0, 2)
        a_out = _fox_dec_call(q.reshape(nb, rpb, a_w), kb.reshape(nb, rpb, a_w), vb.reshape(nb, rpb, a_w),
                              _from_cache(cfk), _from_cache(cfv), lfn_t, cflogf.transpose(0, 2, 1))
        st_fox = (k32.reshape(1, nb, rpb, H_FOX, DH), v32.reshape(1, nb, rpb, H_FOX, DH),
                  lfn_t.transpose(0, 2, 1)[None])
    x2 = _post_call(prompt, [a_out.reshape(m, a_w), pooled], w_out_ab, x2, mods[0], norm_g[0, 1][None],
                    0, w_up, w_down, None, rpb)

    n_heads = d // DH
    outs = _sb_in_call(prompt, x2, mods[1], norm_g[1, 0][None], sb_wts, rpb)
    if prompt:
        q1, ktb1, vtb1, skt32, svt32 = outs
        s_out = _sb_call(q1.reshape(nb, rpb, d), ktb1, vtb1)
        st_sb = (_to_state(skt32, n_heads), _to_state(svt32, n_heads))
    else:
        q1, kb1, vb1, sk32, sv32 = outs
        s_out = _sb_dec_call(q1.reshape(nb, rpb, d), kb1.reshape(nb, rpb, d), vb1.reshape(nb, rpb, d),
                             _from_cache(csk), _from_cache(csv))
        st_sb = (sk32.reshape(1, nb, rpb, n_heads, DH), sv32.reshape(1, nb, rpb, n_heads, DH))
    y = _post_call(prompt, [s_out.reshape(m, d)], w_out_sb, x2, mods[1], norm_g[1, 1][None],
                   1, w_up, w_down, final_g[None], rpb)

    states = st_fox + (pstate[:, 1:, :][None],) + st_sb
    return y.reshape(nb, rpb, d), states


def kernel(x_prompt, x_sample, c_prompt, c_sample, cache_fox_k, cache_fox_v, cache_fox_logf, state_pool, cache_sb_k, cache_sb_v, w_ada, b_ada, norm_g, w_in_ab, b_forget, w_pool, pool_scale, w_out_ab, w_in_sb, w_out_sb, w_up, w_down, final_g):
    b, _, d = x_prompt.shape
    db, ds, _ = x_sample.shape
    a_w = H_FOX * DH

    c_all = jnp.concatenate([c_prompt, c_sample, jnp.zeros((ADA_ROWS - b - db, d), F32)], axis=0)
    mod = _ada_call(c_all, w_ada, b_ada)
    mods_p = [mod[i, :b][:, None, :] for i in range(mod.shape[0])]
    mods_s = [jnp.repeat(mod[i, b:b + db], ds, axis=0) for i in range(mod.shape[0])]

    w_in = w_in_ab[0]
    gate_pad = ((0, GATE_ROWS - H_FOX), (0, 0))
    ab_wts = (w_in[:, :a_w].astype(BF16),
              w_in[:, a_w:2 * a_w].T.astype(BF16),
              w_in[:, 2 * a_w:3 * a_w].T.astype(BF16),
              w_in[:, 3 * a_w + H_FOX:].astype(BF16),
              jnp.pad(w_in[:, 3 * a_w:3 * a_w + H_FOX].T, gate_pad).astype(BF16),
              jnp.pad(b_forget[0][:, None], gate_pad),
              w_pool[0].astype(BF16), pool_scale[0][None])
    w_sb = w_in_sb[0]
    sb_wts = (w_sb[:, :d].astype(BF16), w_sb[:, d:2 * d].T.astype(BF16), w_sb[:, 2 * d:].T.astype(BF16))
    wts = (norm_g, ab_wts, w_out_ab[0].astype(BF16), sb_wts, w_out_sb[0].astype(BF16),
           w_up.astype(BF16), w_down.astype(BF16), final_g)

    y_p, st_p = _trunk(True, x_prompt, mods_p, None, wts)
    past = (cache_fox_k[0], cache_fox_v[0], cache_fox_logf[0], state_pool[0],
            cache_sb_k[0], cache_sb_v[0])
    y_s, st_s = _trunk(False, x_sample, mods_s, past, wts)
    fk, fv, fl, pool, sk, sv = st_p
    fk2, fv2, fl2, pool2, sk2, sv2 = st_s
    return (y_p, y_s, fk, fv, fl, pool, sk, sv, fk2, fv2, fl2, pool2, sk2, sv2)
```

```python
import functools

import jax
import jax.numpy as jnp
from jax import lax
from jax.experimental import pallas as pl
from jax.experimental.pallas import tpu as pltpu

F32 = jnp.float32
BF16 = jnp.bfloat16

DH = 64
H_FOX = 8
EPS = 1e-6
POOL_WINDOWS = (2, 4, 8, 16)
POOL_GC = 128
HIST_ROWS = 16
NEG_BIG = -1e30
LOG2E = 1.4426950408889634
LANES = 128
GATE_ROWS = 16
AUG_ROWS = 16
OFF_SLOT = 12
VMEM_LIMIT = 56 * 1024 * 1024

UNDERFLOW_EXP = 88.0
NORM_MARGIN = 1.01

T_ATT = 256
NQ_ATT = 4
T_SB = 256
NQ_SB = 4
FIRST_SB = 2
TK_DEC_FIRST = 256
TK_DEC_REST = 1280
TK_DEC_FOX = 4096
TM_ROWS = 512
ADA_ROWS = 32
ADA_TN = 1536


def _cparams(sem):
    return pltpu.CompilerParams(dimension_semantics=sem, vmem_limit_bytes=VMEM_LIMIT)


def _rms_mod(x, g, sc, sh):
    y = x * lax.rsqrt(jnp.mean(x * x, axis=-1, keepdims=True) + EPS)
    return (y * g) * (1.0 + sc) + sh


def _softplus(z):
    return jnp.maximum(z, 0.0) + jnp.log(1.0 + jnp.exp2(jnp.abs(z) * (-LOG2E)))


def _log_sigmoid(x):
    return jnp.minimum(x, 0.0) - jnp.log(1.0 + jnp.exp(-jnp.abs(x)))


def _split3(x):
    hi = x.astype(BF16)
    r = x - hi.astype(F32)
    mid = r.astype(BF16)
    lo = (r - mid.astype(F32)).astype(BF16)
    return hi, mid, lo


def _dot(a, b):
    return jnp.dot(a, b, preferred_element_type=F32)


def _dot_nt(a, b):
    return lax.dot_general(a, b, (((1,), (1,)), ((), ())), preferred_element_type=F32)


def _dot3(x, m):
    hi, mid, lo = _split3(x)
    return _dot(hi, m) + _dot(mid, m) + _dot(lo, m)


def _iota(shape, dim):
    return lax.broadcasted_iota(jnp.int32, shape, dim)


def _log2(n):
    assert n & (n - 1) == 0, n
    return n.bit_length() - 1


def _ada_kernel(c_ref, w_ref, b_ref, o_ref):
    c = c_ref[...]
    cond = (c / (1.0 + jnp.exp(-c))).astype(BF16)
    o_ref[...] = _dot(cond, w_ref[...].astype(BF16)) + b_ref[...]


def _ada_call(c_all, w_ada, b_ada):
    depth, d, n = w_ada.shape
    return pl.pallas_call(
        _ada_kernel,
        out_shape=jax.ShapeDtypeStruct((depth, ADA_ROWS, n), F32),
        grid=(depth, n // ADA_TN),
        in_specs=[
            pl.BlockSpec((ADA_ROWS, d), lambda i, j: (0, 0)),
            pl.BlockSpec((None, d, ADA_TN), lambda i, j: (i, 0, j)),
            pl.BlockSpec((None, 1, ADA_TN), lambda i, j: (i, 0, j)),
        ],
        out_specs=pl.BlockSpec((None, ADA_ROWS, ADA_TN), lambda i, j: (i, 0, j)),
        compiler_params=_cparams(("arbitrary", "arbitrary")),
        name="ada_mod",
    )(c_all, w_ada, b_ada.reshape(depth, 1, n))


def _mod_spec(prompt, tm, d, chunk, tiles_per_batch):
    if prompt:
        return pl.BlockSpec((None, 1, d), lambda i: (i // tiles_per_batch, 0, chunk))
    return pl.BlockSpec((tm, d), lambda i: (i, chunk))


def _pool_branch(u, ext_ref, pos, ng, rg, wp_ref, ps_ref, pooled_ref, pst_ref):
    tm, a_w = u.shape
    u3 = u.reshape(ng, rg, a_w)
    ext_ref[:, HIST_ROWS:HIST_ROWS + rg, :] = u3
    pst_ref[...] = u3[:, rg - HIST_ROWS:rg, :]
    for g, w in enumerate(POOL_WINDOWS):
        cs = slice(g * POOL_GC, (g + 1) * POOL_GC)
        acc = u3[:, :, cs]
        for kk in range(1, w):
            acc = acc + ext_ref[:, pl.ds(HIST_ROWS - kk, rg), cs]
        cnt = jnp.minimum(pos + 1, w).astype(F32)
        pooled = acc.reshape(tm, POOL_GC) / cnt - u[:, cs]
        pg = _dot(pooled.astype(BF16), wp_ref[g]) * ps_ref[:, cs]
        pooled_ref[:, cs] = pg.astype(pooled_ref.dtype)


def _ab_in_prompt_kernel(tm, tiles_per_batch, x_ref, sh_ref, sc_ref, g_ref, wq_ref, wkt_ref, wvt_ref,
                         wu_ref, wft_ref, bf_ref, wp_ref, ps_ref,
                         q_ref, ktb_ref, vtb_ref, kt_ref, vt_ref, lft_ref, fcp_ref, kaug_ref, pooled_ref, pst_ref,
                         ext_ref, carry_ref):
    h = _rms_mod(x_ref[...], g_ref[...], sc_ref[...], sh_ref[...]).astype(BF16)
    q_ref[...] = (_dot(h, wq_ref[...]) * (DH ** -0.5)).astype(q_ref.dtype)
    kt = _dot_nt(wkt_ref[...], h)
    kt_ref[...] = kt
    ktb_ref[...] = kt.astype(BF16)
    vt = _dot_nt(wvt_ref[...], h)
    vt_ref[...] = vt
    vtb_ref[...] = vt.astype(BF16)

    tile_in_batch = pl.program_id(0) % tiles_per_batch
    first = tile_in_batch == 0
    logf = _log_sigmoid(_dot_nt(wft_ref[...], h) + bf_ref[...])
    lft_ref[...] = logf[0:H_FOX, :]
    tri = (_iota((tm, tm), 0) <= _iota((tm, tm), 1)).astype(BF16)
    fcum = _dot3(logf, tri) + jnp.where(first, 0.0, carry_ref[...])
    carry_ref[...] = fcum[:, tm - 1:tm]
    parts = _split3(fcum)
    prow = _iota((4 * H_FOX, GATE_ROWS), 0)
    pick = jnp.logical_and((prow & 7) < 2, _iota((4 * H_FOX, GATE_ROWS), 1) == (prow >> 3) * 2 + (prow & 7))
    pick = jnp.where(pick, 1.0, 0.0).astype(BF16)
    fcp_ref[...] = (_dot(pick, parts[0]) + _dot(pick, parts[1])) + _dot(pick, parts[2])

    n_aug = (H_FOX // 2) * AUG_ROWS
    slot = _iota((n_aug, GATE_ROWS), 0) & (AUG_ROWS - 1)
    head0 = (_iota((n_aug, GATE_ROWS), 0) >> _log2(AUG_ROWS)) * 2
    col = _iota((n_aug, GATE_ROWS), 1)
    kaug = None
    for part, piece in enumerate(parts):
        sel = jnp.logical_or(jnp.logical_and(slot == 3 + part, col == head0),
                             jnp.logical_and(slot == 9 + part, col == head0 + 1))
        term = _dot(jnp.where(sel, -1.0, 0.0).astype(BF16), piece)
        kaug = term if kaug is None else kaug + term
    slot1 = _iota((n_aug, 1), 0) & (AUG_ROWS - 1)
    ones = jnp.logical_or(slot1 < 3, jnp.logical_and(slot1 >= 6, slot1 < 9))
    kaug_ref[...] = (kaug + jnp.where(ones, 1.0, 0.0)).astype(BF16)

    @pl.when(first)
    def _():
        ext_ref[:, 0:HIST_ROWS, :] = jnp.zeros((1, HIST_ROWS, ext_ref.shape[2]), F32)

    @pl.when(jnp.logical_not(first))
    def _():
        ext_ref[:, 0:HIST_ROWS, :] = ext_ref[:, tm:tm + HIST_ROWS, :]

    pos = tile_in_batch * tm + _iota((tm, 1), 0)
    _pool_branch(_dot(h, wu_ref[...]), ext_ref, pos, 1, tm, wp_ref, ps_ref, pooled_ref, pst_ref)


def _ab_in_sample_kernel(ng, rg, p0, x_ref, sh_ref, sc_ref, g_ref, wq_ref, wkt_ref, wvt_ref,
                         wu_ref, wft_ref, bf_ref, wp_ref, ps_ref, hist_ref,
                         q_ref, kb_ref, vb_ref, k32_ref, v32_ref, lft_ref, pooled_ref, pst_ref, ext_ref):
    tm = ng * rg
    h = _rms_mod(x_ref[...], g_ref[...], sc_ref[...], sh_ref[...]).astype(BF16)
    q_ref[...] = (_dot(h, wq_ref[...]) * (DH ** -0.5)).astype(q_ref.dtype)
    k = _dot_nt(h, wkt_ref[...])
    k32_ref[...] = k
    kb_ref[...] = k.astype(BF16)
    v = _dot_nt(h, wvt_ref[...])
    v32_ref[...] = v
    vb_ref[...] = v.astype(BF16)
    logf = _log_sigmoid(_dot_nt(wft_ref[...], h) + bf_ref[...])
    lft_ref[...] = logf[0:H_FOX, :]
    ext_ref[:, 0:HIST_ROWS, :] = hist_ref[...]
    pos = p0 + (_iota((tm, 1), 0) & (rg - 1))
    _pool_branch(_dot(h, wu_ref[...]), ext_ref, pos, ng, rg, wp_ref, ps_ref, pooled_ref, pst_ref)


def _ab_in_call(prompt, x2, mod_arr, g, wts, hist, rows_per_batch, p0):
    wq, wkt, wvt, wu, wft, bf, wp, ps = wts
    m, d = x2.shape
    a_w = H_FOX * DH
    nb = m // rows_per_batch
    tm = TM_ROWS if prompt else m
    tpb = rows_per_batch // tm if prompt else 1
    row = lambda i: (i, 0)
    const2 = lambda i: (0, 0)
    in_specs = [pl.BlockSpec((tm, d), row), _mod_spec(prompt, tm, d, 0, tpb), _mod_spec(prompt, tm, d, 1, tpb),
                pl.BlockSpec((1, d), const2)]
    in_specs += [pl.BlockSpec(w.shape, const2) for w in (wq, wkt, wvt, wu, wft, bf)]
    in_specs += [pl.BlockSpec(wp.shape, lambda i: (0, 0, 0)), pl.BlockSpec(ps.shape, const2)]
    args = [x2, mod_arr, mod_arr, g, wq, wkt, wvt, wu, wft, bf, wp, ps]
    if prompt:
        tcol = lambda i: (i // tpb, 0, i % tpb)
        out_shape = [jax.ShapeDtypeStruct((m, a_w), BF16),
                     jax.ShapeDtypeStruct((nb, a_w, rows_per_batch), BF16),
                     jax.ShapeDtypeStruct((nb, a_w, rows_per_batch), BF16),
                     jax.ShapeDtypeStruct((nb, a_w, rows_per_batch), F32),
                     jax.ShapeDtypeStruct((nb, a_w, rows_per_batch), F32),
                     jax.ShapeDtypeStruct((nb, H_FOX, rows_per_batch), F32),
                     jax.ShapeDtypeStruct((nb, 4 * H_FOX, rows_per_batch), F32),
                     jax.ShapeDtypeStruct((nb, (H_FOX // 2) * AUG_ROWS, rows_per_batch), BF16),
                     jax.ShapeDtypeStruct((m, a_w), BF16),
                     jax.ShapeDtypeStruct((nb, HIST_ROWS, a_w), F32)]
        out_specs = ([pl.BlockSpec((tm, a_w), row)] + [pl.BlockSpec((None, a_w, tm), tcol)] * 4
                     + [pl.BlockSpec((None, H_FOX, tm), tcol), pl.BlockSpec((None, 4 * H_FOX, tm), tcol)]
                     + [pl.BlockSpec((None, (H_FOX // 2) * AUG_ROWS, tm), tcol)]
                     + [pl.BlockSpec((tm, a_w), row),
                        pl.BlockSpec((1, HIST_ROWS, a_w), lambda i: (i // tpb, 0, 0))])
        scratch = [pltpu.VMEM((1, HIST_ROWS + tm, a_w), F32), pltpu.VMEM((GATE_ROWS, 1), F32)]
        body = functools.partial(_ab_in_prompt_kernel, tm, tpb)
    else:
        in_specs.append(pl.BlockSpec(hist.shape, lambda i: (0, 0, 0)))
        args.append(hist)
        out_shape = [jax.ShapeDtypeStruct((m, a_w), BF16), jax.ShapeDtypeStruct((m, a_w), BF16),
                     jax.ShapeDtypeStruct((m, a_w), BF16), jax.ShapeDtypeStruct((m, a_w), F32),
                     jax.ShapeDtypeStruct((m, a_w), F32), jax.ShapeDtypeStruct((H_FOX, m), F32),
                     jax.ShapeDtypeStruct((m, a_w), BF16), jax.ShapeDtypeStruct((nb, HIST_ROWS, a_w), F32)]
        out_specs = ([pl.BlockSpec((tm, a_w), row)] * 5 + [pl.BlockSpec((H_FOX, tm), const2)]
                     + [pl.BlockSpec((tm, a_w), row), pl.BlockSpec((nb, HIST_ROWS, a_w), lambda i: (0, 0, 0))])
        scratch = [pltpu.VMEM((nb, HIST_ROWS + rows_per_batch, a_w), F32)]
        body = functools.partial(_ab_in_sample_kernel, nb, rows_per_batch, p0)
    return pl.pallas_call(
        body, out_shape=out_shape, grid=(m // tm,), in_specs=in_specs, out_specs=out_specs,
        scratch_shapes=scratch,
        compiler_params=_cparams(("arbitrary",)),
        name="ab_in_prompt" if prompt else "ab_in_sample",
    )(*args)


def _head_split(q2):
    lane = _iota((1, LANES), 1)
    zero = jnp.zeros_like(q2)
    return jnp.where(lane < DH, q2, zero), jnp.where(lane >= DH, q2, zero)


def _pair_sel():
    return (_iota((8, LANES), 0) == (_iota((8, LANES), 1) >> _log2(DH))).astype(BF16)


def _fox_kernel(q_ref, kt_ref, vt_ref, ka_ref, fr_ref, fall_ref, o_ref,
                m_ref, acc_ref, kn_ref):
    t, nq = T_ATT, NQ_ATT
    qg = pl.program_id(2)
    lane = _iota((1, LANES), 1)

    @pl.when(qg == 0)
    def _():
        chunk = 4 * t

        def body(i, mx):
            ks = pl.multiple_of(i * chunk, chunk)
            kf = kt_ref[:, pl.ds(ks, chunk)].astype(F32)
            n2 = _dot(_pair_sel(), (kf * kf).astype(BF16))
            return jnp.maximum(mx, jnp.max(n2, axis=1, keepdims=True))

        mx = lax.fori_loop(0, kt_ref.shape[1] // chunk, body, jnp.zeros((8, 1), F32))
        kn_ref[...] = jnp.sqrt(mx)

    row_a = _iota((AUG_ROWS, 1), 0)
    row_v = _iota((LANES, 1), 0)
    qcat, bound0 = [], []
    for s in range(nq):
        q2 = q_ref[s * t:(s + 1) * t, :]
        frows = fr_ref[:, s * t:(s + 1) * t]
        eye = (_iota((t, t), 0) == _iota((t, t), 1)).astype(BF16)
        per_head = []
        for j, qh in enumerate(_head_split(q2)):
            hi, mid, lo = (x.astype(F32) for x in _split3(frows[j:j + 1, :]))
            base = 6 * j
            qat = jnp.where(row_a == base, hi, jnp.where(row_a == base + 1, mid, jnp.where(
                row_a == base + 2, lo,
                jnp.where(jnp.logical_and(row_a >= base + 3, row_a < base + 6), 1.0, 0.0))))
            qat = jnp.where(row_a == OFF_SLOT, NEG_BIG, qat)
            qa = _dot_nt(eye, qat.astype(BF16))
            per_head.append(jnp.concatenate([qh, qa.astype(BF16)], axis=1))
        qcat.append(per_head)
        qf = q2.astype(F32)
        qn2 = _dot_nt((qf * qf).astype(BF16), _pair_sel())
        qn = jnp.sqrt(jnp.max(qn2, axis=0, keepdims=True))
        bound0.append([qn[:, j:j + 1] * kn_ref[j:j + 1, :] * NORM_MARGIN + frows[j:j + 1, 0:1] for j in range(2)])
    chains = [(s, j) for s in range(nq) for j in range(2)]
    rep = lambda m: jnp.concatenate([m] * (t // LANES), axis=1)

    def front(n, diag):
        kbs = [qg * nq + s - n for s in range(nq)]
        kss = [pl.multiple_of((kb if diag else jnp.maximum(kb, 0)) * t, t) for kb in kbs]
        causal = _iota((t, t), 1) <= _iota((t, t), 0)
        kcat = []
        for s in range(nq):
            ka = ka_ref[:, pl.ds(kss[s], t)]
            if not diag:
                off = jnp.where(kbs[s] >= 0, 0.0, 1.0).astype(BF16)
                ka = jnp.where(_iota((AUG_ROWS, 1), 0) == OFF_SLOT, off, ka)
            kcat.append(jnp.concatenate([kt_ref[:, pl.ds(kss[s], t)], ka], axis=0))
        sc = []
        for s, j in chains:
            x = _dot(qcat[s][j], kcat[s])
            sc.append(jnp.where(causal, x, NEG_BIG) if diag else x)
        vcat = []
        for s, j in chains:
            v2 = vt_ref[:, pl.ds(kss[s], t)]
            mine = (row_v < DH) if j == 0 else (row_v >= DH)
            vcat.append(jnp.where(mine, v2, jnp.ones_like(v2)))
        return sc, vcat

    (sc0, v0), (sc1, v1) = front(0, True), front(1, False)
    m01 = [jnp.broadcast_to(jnp.maximum(jnp.max(x0, axis=1, keepdims=True), jnp.max(x1, axis=1, keepdims=True)),
                            (t, LANES)) for x0, x1 in zip(sc0, sc1)]
    pv0 = [_dot_nt(jnp.exp(x - rep(m)).astype(BF16), vv) for x, m, vv in zip(sc0, m01, v0)]
    pv1 = [_dot_nt(jnp.exp(x - rep(m)).astype(BF16), vv) for x, m, vv in zip(sc1, m01, v1)]
    for i, (s, j) in enumerate(chains):
        acc_ref[2 * s + j] = pv0[i] + pv1[i]
        m_ref[2 * s + j] = m01[i]

    def front_wide(n, chains):
        kcat, vts = [], []
        for s in range(nq):
            kb = qg * nq + s - n
            start = pl.multiple_of(jnp.maximum(kb - 1, 0) * t, t)
            off = jnp.where(start + _iota((1, 2 * t), 1) >= (kb + 1) * t, 1.0, 0.0).astype(BF16)
            ka = jnp.where(_iota((AUG_ROWS, 1), 0) == OFF_SLOT, off, ka_ref[:, pl.ds(start, 2 * t)])
            kcat.append(jnp.concatenate([kt_ref[:, pl.ds(start, 2 * t)], ka], axis=0))
            vts.append(vt_ref[:, pl.ds(start, 2 * t)])
        sc = [_dot(qcat[s][j], kcat[s]) for s, j in chains]
        vcat = [jnp.where((row_v < DH) if j == 0 else (row_v >= DH), vts[s], jnp.ones_like(vts[s]))
                for s, j in chains]
        return sc, vcat

    def step(n, chains):
        sc, vcat = front_wide(n, chains)
        m_prev = [m_ref[2 * s + j] for s, j in chains]
        m_new = [jnp.maximum(mp, jnp.max(x, axis=1, keepdims=True)) for mp, x in zip(m_prev, sc)]
        pv = [_dot_nt(jnp.exp(x - jnp.concatenate([rep(mn)] * 2, axis=1)).astype(BF16), vv)
              for x, mn, vv in zip(sc, m_new, vcat)]
        for i, (s, j) in enumerate(chains):
            idx = 2 * s + j
            acc_ref[idx] = jnp.exp(m_prev[i] - m_new[i]) * acc_ref[idx] + pv[i]
            m_ref[idx] = m_new[i]

    def more(n):
        go = [False, False]
        for s in range(nq):
            kb = qg * nq + s - n
            last = pl.multiple_of(jnp.maximum(kb, 0) * t + (t - LANES), LANES)
            ft = fall_ref[:, pl.ds(last, LANES)][:, LANES - 1:LANES]
            for j in range(2):
                top = jnp.max(bound0[s][j] - ft[j:j + 1, :]) - jnp.min(m_ref[2 * s + j])
                go[j] = jnp.logical_or(go[j], jnp.logical_and(kb >= 0, top > -UNDERFLOW_EXP))
        return go

    def body(chains, c):
        n = c[0]
        step(n, chains)
        return (n + 2, *more(n + 2))

    c = lax.while_loop(lambda c: jnp.logical_and(c[1], c[2]), functools.partial(body, chains),
                       (jnp.int32(2), *more(2)))
    c = lax.while_loop(lambda c: c[1], functools.partial(body, [(s, 0) for s in range(nq)]), c)
    lax.while_loop(lambda c: c[2], functools.partial(body, [(s, 1) for s in range(nq)]), c)
    for s in range(nq):
        a0, a1 = acc_ref[2 * s], acc_ref[2 * s + 1]
        o = jnp.where(lane < DH, a0 / pltpu.roll(a0, DH, 1), a1 / pltpu.roll(a1, DH, 1))
        o_ref[s * t:(s + 1) * t, :] = o.astype(o_ref.dtype)


def _fox_call(q, ktb, vtb, kaug, fcp):
    b, l, a_w = q.shape
    hp = a_w // LANES
    t, nq = T_ATT, NQ_ATT
    tq = t * nq
    return pl.pallas_call(
        _fox_kernel,
        out_shape=jax.ShapeDtypeStruct((b, l, a_w), BF16),
        grid=(b, hp, l // tq),
        in_specs=[
            pl.BlockSpec((None, tq, LANES), lambda bi, h, qi: (bi, qi, h)),
            pl.BlockSpec((None, LANES, l), lambda bi, h, qi: (bi, h, 0)),
            pl.BlockSpec((None, LANES, l), lambda bi, h, qi: (bi, h, 0)),
            pl.BlockSpec((None, AUG_ROWS, l), lambda bi, h, qi: (bi, h, 0)),
            pl.BlockSpec((None, 8, tq), lambda bi, h, qi: (bi, h, qi)),
            pl.BlockSpec((None, 8, l), lambda bi, h, qi: (bi, h, 0)),
        ],
        out_specs=pl.BlockSpec((None, tq, LANES), lambda bi, h, qi: (bi, qi, h)),
        scratch_shapes=[pltpu.VMEM((2 * nq, t, LANES), F32), pltpu.VMEM((2 * nq, t, LANES), F32),
                        pltpu.VMEM((8, 1), F32)],
        compiler_params=_cparams(("arbitrary", "arbitrary", "arbitrary")),
        name="fox_prompt",
    )(q, ktb, vtb, kaug, fcp, fcp)


def _sb_kernel(q_ref, kt_ref, vt_ref, o_ref, carry_ref, acc_ref):
    t, nq = T_SB, NQ_SB
    qg = pl.program_id(2)
    lane = _iota((1, LANES), 1)
    qh = [_head_split(q_ref[s * t:(s + 1) * t, :]) for s in range(nq)]
    valid = _iota((t, t), 1) < _iota((t, t), 0)

    def logits(s, n, diag):
        kb = qg * nq + s - n
        ks = pl.multiple_of((kb if diag else jnp.maximum(kb, 0)) * t, t)
        return kb, ks, [_dot(qh[s][j], kt_ref[:, pl.ds(ks, t)]) for j in range(2)]

    def suffix(diag, z):
        tri = (_iota((t, t), 0) > _iota((t, t), 1)).astype(BF16)
        sp = [_softplus(zz) for zz in z]
        spm = [jnp.where(valid, x, 0.0) for x in sp] if diag else sp
        later = [_dot(x.astype(BF16), tri) for x in spm]
        return [zz - x for zz, x in zip(z, sp)], spm, later

    def weigh(diag, kb, ks, ls, spm, later, carry):
        v2 = vt_ref[:, pl.ds(ks, t)]
        if not diag:
            v2 = jnp.where(kb >= 0, v2, jnp.zeros_like(v2))
        pv, new_carry = [], []
        for j in range(2):
            e = ls[j] - later[j]
            a = jnp.exp(e if carry is None else e - carry[j])
            if diag:
                a = jnp.where(valid, a, 0.0)
            pv.append(_dot_nt(a.astype(BF16), v2))
            rs = jnp.sum(spm[j], axis=1, keepdims=True)
            new_carry.append(rs if carry is None else carry[j] + rs)
        return pv, new_carry

    def run(units, carry, acc):
        st1, st2 = {}, {}
        for k in range(len(units) + 2):
            if k < len(units):
                st1[k] = logits(*units[k])
            if 0 <= k - 1 < len(units):
                st2[k - 1] = suffix(units[k - 1][2], st1[k - 1][2])
            if 0 <= k - 2 < len(units):
                s, _, diag = units[k - 2]
                kb, ks, _ = st1.pop(k - 2)
                pv, carry[s] = weigh(diag, kb, ks, *st2.pop(k - 2), carry[s])
                acc[s] = pv if acc[s] is None else [x + y for x, y in zip(acc[s], pv)]
        return carry, acc

    carry, acc = run([(s, n, n == 0) for n in range(FIRST_SB) for s in range(nq)], [None] * nq, [None] * nq)
    for s in range(nq):
        for j in range(2):
            acc_ref[2 * s + j] = acc[s][j]
            carry_ref[2 * s + j] = carry[s][j]

    def more(n):
        in_range = n <= qg * nq + nq - 1
        return jnp.logical_and(in_range, jnp.min(carry_ref[...]) < UNDERFLOW_EXP)

    def body(c):
        n, _ = c
        carry, acc = run([(s, n, False) for s in range(nq)],
                         [[carry_ref[2 * s], carry_ref[2 * s + 1]] for s in range(nq)], [None] * nq)
        for s in range(nq):
            for j in range(2):
                acc_ref[2 * s + j] = acc_ref[2 * s + j] + acc[s][j]
                carry_ref[2 * s + j] = carry[s][j]
        return n + 1, more(n + 1)

    lax.while_loop(lambda c: c[1], body, (jnp.int32(FIRST_SB), more(FIRST_SB)))
    for s in range(nq):
        o = jnp.where(lane < DH, acc_ref[2 * s], acc_ref[2 * s + 1])
        o_ref[s * t:(s + 1) * t, :] = o.astype(o_ref.dtype)


def _sb_call(q, ktb, vtb):
    b, l, d = q.shape
    hp = d // LANES
    t, nq = T_SB, NQ_SB
    tq = t * nq
    return pl.pallas_call(
        _sb_kernel,
        out_shape=jax.ShapeDtypeStruct((b, l, d), BF16),
        grid=(b, hp, l // tq),
        in_specs=[
            pl.BlockSpec((None, tq, LANES), lambda bi, h, qi: (bi, qi, h)),
            pl.BlockSpec((None, LANES, l), lambda bi, h, qi: (bi, h, 0)),
            pl.BlockSpec((None, LANES, l), lambda bi, h, qi: (bi, h, 0)),
        ],
        out_specs=pl.BlockSpec((None, tq, LANES), lambda bi, h, qi: (bi, qi, h)),
        scratch_shapes=[pltpu.VMEM((2 * nq, t, 1), F32), pltpu.VMEM((2 * nq, t, LANES), F32)],
        compiler_params=_cparams(("arbitrary", "arbitrary", "arbitrary")),
        name="sb_prompt",
    )(q, ktb, vtb)


def _rep_matrix(rows, n_heads, ds):
    return ((_iota((rows, ds), 0) >> _log2(n_heads)) == _iota((rows, ds), 1)).astype(BF16)


def _head_mask(rows, n_heads, width):
    return (_iota((rows, width), 0) & (n_heads - 1)) == (_iota((rows, width), 1) >> _log2(DH))


def _block_diag_q(q, n_heads):
    ds, width = q.shape
    rows = ds * n_heads
    qrep = _dot(_rep_matrix(rows, n_heads, ds), q)
    return jnp.where(_head_mask(rows, n_heads, width), qrep, 0.0).astype(BF16)


def _collect_heads(acc, n_heads, ds):
    rows, width = acc.shape
    om = jnp.where(_head_mask(rows, n_heads, width), acc, 0.0).astype(BF16)
    rep_t = ((_iota((ds, rows), 1) >> _log2(n_heads)) == _iota((ds, rows), 0)).astype(BF16)
    return _dot(rep_t, om)


def _suffix_sums(x, exact=True, blk=2 * LANES):
    r, n = x.shape
    parts = [x[:, b * blk:(b + 1) * blk] for b in range(n // blk)]
    tri = (_iota((blk, blk), 0) > _iota((blk, blk), 1)).astype(BF16)
    stacked = jnp.concatenate(parts, axis=0)
    w = _dot3(stacked, tri) if exact else _dot(stacked.astype(BF16), tri)
    out, tail = [], None
    for b in reversed(range(len(parts))):
        cur = w[b * r:(b + 1) * r]
        out.append(cur if tail is None else cur + tail)
        tot = jnp.sum(parts[b], axis=1, keepdims=True)
        tail = tot if tail is None else tail + tot
    return jnp.concatenate(out[::-1], axis=1), tail


def _fox_dec_kernel(n_heads, ds, q_ref, kn_ref, vn_ref, lfn_ref, kc_ref, vc_ref, lfc_ref,
                    o_ref, qbd_ref, m_ref, l_ref, acc_ref, cf_ref):
    rows = ds * n_heads
    step_i = pl.program_id(1)
    n_steps = pl.num_programs(1)

    def update(s, pv_fn):
        m_prev = m_ref[...]
        m_new = jnp.maximum(m_prev, jnp.max(s, axis=1, keepdims=True))
        alpha = jnp.exp(m_prev - m_new)
        p = jnp.exp(s - m_new)
        l_ref[...] = alpha * l_ref[...] + jnp.sum(p, axis=1, keepdims=True)
        acc_ref[...] = alpha * acc_ref[...] + pv_fn(p.astype(BF16))
        m_ref[...] = m_new

    @pl.when(step_i == 0)
    def _():
        qbd_ref[...] = _block_diag_q(q_ref[...], n_heads)
        m_ref[...] = jnp.full(m_ref.shape, NEG_BIG, F32)
        l_ref[...] = jnp.zeros(l_ref.shape, F32)
        acc_ref[...] = jnp.zeros(acc_ref.shape, F32)
        a = jnp.concatenate([lfn_ref[...]] * ds, axis=0)
        ri = _iota((rows, ds), 0) >> _log2(n_heads)
        cj = _iota((rows, ds), 1)
        am = jnp.where(cj <= ri, a, 0.0)
        tri = (_iota((ds, ds), 0) > _iota((ds, ds), 1)).astype(BF16)
        s = _dot_nt(qbd_ref[...], kn_ref[...]) + _dot3(am, tri)
        s = jnp.where(cj <= ri, s, NEG_BIG)
        update(s, lambda p: _dot(p, vn_ref[...]))
        cf_ref[...] = jnp.sum(am, axis=1, keepdims=True)

    @pl.when(step_i > 0)
    def _():
        reps = rows // (2 * n_heads)
        later, total = _suffix_sums(jnp.concatenate([lfc_ref[...]] * 2, axis=0))
        cf = cf_ref[...]
        s = _dot(qbd_ref[...], kc_ref[...].astype(BF16)) + (jnp.concatenate([later] * reps, axis=0) + cf)
        update(s, lambda p: _dot_nt(p, vc_ref[...].astype(BF16)))
        cf_ref[...] = cf + jnp.concatenate([total] * reps, axis=0)

    @pl.when(step_i == n_steps - 1)
    def _():
        o_ref[...] = _collect_heads(acc_ref[...] / l_ref[...], n_heads, ds).astype(o_ref.dtype)


def _fox_dec_call(q, kn, vn, kct, vct, lfn_t, lfc_t):
    db, ds, width = q.shape
    n_heads = width // DH
    rows = ds * n_heads
    tk = min(TK_DEC_FOX, kct.shape[2])
    nkb = kct.shape[2] // tk
    cache_blk = lambda s: jnp.minimum(nkb - s, nkb - 1)
    new_idx = lambda b, s: (b, 0, 0)
    cache_idx = lambda b, s: (b, 0, cache_blk(s))
    in_specs = ([pl.BlockSpec((None, ds, width), new_idx)] * 3
                + [pl.BlockSpec((None, n_heads, ds), new_idx)]
                + [pl.BlockSpec((None, width, tk), cache_idx)] * 2
                + [pl.BlockSpec((None, n_heads, tk), cache_idx)])
    scratch = [pltpu.VMEM((rows, width), BF16), pltpu.VMEM((rows, 1), F32), pltpu.VMEM((rows, 1), F32),
               pltpu.VMEM((rows, width), F32), pltpu.VMEM((rows, 1), F32)]
    return pl.pallas_call(
        functools.partial(_fox_dec_kernel, n_heads, ds),
        out_shape=jax.ShapeDtypeStruct((db, ds, width), BF16),
        grid=(db, nkb + 1),
        in_specs=in_specs,
        out_specs=pl.BlockSpec((None, ds, width), new_idx),
        scratch_shapes=scratch,
        compiler_params=_cparams(("arbitrary", "arbitrary")),
        name="fox_decode",
    )(q, kn, vn, lfn_t, kct, vct, lfc_t)


def _sb_dec_tile(qbd, kc_ref, vc_ref, carry):
    z = _dot(qbd, kc_ref[...].astype(BF16))
    sp = _softplus(z)
    later, total = _suffix_sums(sp, exact=False)
    a = jnp.exp((z - sp) - later - carry)
    return _dot_nt(a.astype(BF16), vc_ref[...].astype(BF16)), carry + total


def _sb_dec_first_kernel(n_heads, ds, q_ref, kn_ref, vn_ref, kc_ref, vc_ref,
                         o_ref, carry_out_ref, cmin_ref):
    rows = ds * n_heads
    qbd = _block_diag_q(q_ref[...], n_heads)
    ri = _iota((rows, ds), 0) >> _log2(n_heads)
    cj = _iota((rows, ds), 1)
    valid = cj < ri
    z = _dot_nt(qbd, kn_ref[...])
    sp = _softplus(z)
    spm = jnp.where(valid, sp, 0.0)
    tri = (_iota((ds, ds), 0) > _iota((ds, ds), 1)).astype(BF16)
    later = _dot(spm.astype(BF16), tri)
    a = jnp.where(valid, jnp.exp((z - sp) - later), 0.0)
    acc = _dot(a.astype(BF16), vn_ref[...])
    carry = jnp.sum(spm, axis=1, keepdims=True)
    part, carry = _sb_dec_tile(qbd, kc_ref, vc_ref, carry)
    o_ref[...] = _collect_heads(acc + part, n_heads, ds).astype(o_ref.dtype)
    carry_out_ref[...] = carry
    cmin_ref[...] = jnp.broadcast_to(jnp.min(carry, axis=0, keepdims=True), cmin_ref.shape)


def _sb_dec_rest_kernel(n_heads, ds, need_ref, q_ref, carry_in_ref, o1_ref, kc_ref, vc_ref,
                        o_ref, qbd_ref, acc_ref, carry_ref):
    step_i = pl.program_id(1)
    last = pl.num_programs(1) - 1
    needed = need_ref[pl.program_id(0)] != 0

    @pl.when(needed)
    def _():
        @pl.when(step_i == 0)
        def _():
            qbd_ref[...] = _block_diag_q(q_ref[...], n_heads)
            acc_ref[...] = jnp.zeros(acc_ref.shape, F32)
            carry_ref[...] = carry_in_ref[...]

        @pl.when(jnp.min(carry_ref[...]) < UNDERFLOW_EXP)
        def _():
            part, carry = _sb_dec_tile(qbd_ref[...], kc_ref, vc_ref, carry_ref[...])
            acc_ref[...] = acc_ref[...] + part
            carry_ref[...] = carry

        @pl.when(step_i == last)
        def _():
            o = o1_ref[...].astype(F32) + _collect_heads(acc_ref[...], n_heads, ds)
            o_ref[...] = o.astype(o_ref.dtype)

    @pl.when(jnp.logical_and(jnp.logical_not(needed), step_i == last))
    def _():
        o_ref[...] = o1_ref[...]


def _sb_dec_call(q, kn, vn, kct, vct):
    db, ds, width = q.shape
    n_heads = width // DH
    rows = ds * n_heads
    past = kct.shape[2]
    first = min(TK_DEC_FIRST, past)
    n_blk = (past - first) // (2 * LANES)
    per = max([k for k in range(1, TK_DEC_REST // (2 * LANES) + 1) if n_blk % k == 0], default=1)
    rest = per * 2 * LANES
    n_rest = (past - first) // rest
    bidx = lambda b: (b, 0, 0)
    o1, carry, cmin = pl.pallas_call(
        functools.partial(_sb_dec_first_kernel, n_heads, ds),
        out_shape=[jax.ShapeDtypeStruct((db, ds, width), BF16),
                   jax.ShapeDtypeStruct((db, rows, 1), F32),
                   jax.ShapeDtypeStruct((db, 1, LANES), F32)],
        grid=(db,),
        in_specs=[pl.BlockSpec((None, ds, width), bidx)] * 3
        + [pl.BlockSpec((None, width, first), lambda b: (b, 0, past // first - 1))] * 2,
        out_specs=[pl.BlockSpec((None, ds, width), bidx), pl.BlockSpec((None, rows, 1), bidx),
                   pl.BlockSpec((None, 1, LANES), bidx)],
        compiler_params=_cparams(("arbitrary",)),
        name="sb_decode_first",
    )(q, kn, vn, kct, vct)
    if n_rest == 0:
        return o1
    need = (cmin[:, 0, 0] < UNDERFLOW_EXP).astype(jnp.int32)
    bidx2 = lambda b, s, need_ref: (b, 0, 0)
    tile = (None, width, rest)

    def cache_idx(b, s, need_ref):
        on = need_ref[b] != 0
        return (jnp.where(on, b, 0), 0, jnp.where(on, n_rest - 1 - s, 0))

    grid_spec = pltpu.PrefetchScalarGridSpec(
        num_scalar_prefetch=1,
        grid=(db, n_rest),
        in_specs=[pl.BlockSpec((None, ds, width), bidx2), pl.BlockSpec((None, rows, 1), bidx2),
                  pl.BlockSpec((None, ds, width), bidx2),
                  pl.BlockSpec(tile, cache_idx), pl.BlockSpec(tile, cache_idx)],
        out_specs=pl.BlockSpec((None, ds, width), bidx2),
        scratch_shapes=[pltpu.VMEM((rows, width), BF16), pltpu.VMEM((rows, width), F32),
                        pltpu.VMEM((rows, 1), F32)],
    )
    return pl.pallas_call(
        functools.partial(_sb_dec_rest_kernel, n_heads, ds),
        out_shape=jax.ShapeDtypeStruct((db, ds, width), BF16),
        grid_spec=grid_spec,
        compiler_params=_cparams(("arbitrary", "arbitrary")),
        name="sb_decode_rest",
    )(need, q, carry, o1, kct, vct)


def _post_kernel(n_act, final, f_chunk, *refs):
    acts = refs[:n_act]
    wo_ref, x_ref, gate1_ref, sh_ref, sc_ref, gate_ref, g_ref, wu_ref, wd_ref = refs[n_act:n_act + 9]
    fg_ref = refs[n_act + 9] if final else None
    o_ref = refs[-1]
    y = None
    off = 0
    for a_ref in acts:
        kdim = a_ref.shape[1]
        part = _dot(a_ref[...], wo_ref[off:off + kdim, :])
        y = part if y is None else y + part
        off += kdim
    x = x_ref[...] + gate1_ref[...] * y
    h = _rms_mod(x, g_ref[...], sc_ref[...], sh_ref[...]).astype(BF16)
    d_ff = wu_ref.shape[1]
    acc = None
    for c in range(d_ff // f_chunk):
        cs = slice(c * f_chunk, (c + 1) * f_chunk)
        a = jnp.maximum(_dot(h, wu_ref[:, cs]), 0.0)
        part = _dot((a * a).astype(BF16), wd_ref[cs, :])
        acc = part if acc is None else acc + part
    xo = x + gate_ref[...] * acc
    if final:
        y = xo * lax.rsqrt(jnp.mean(xo * xo, axis=-1, keepdims=True) + EPS)
        xo = y * fg_ref[...]
    o_ref[...] = xo


def _post_call(prompt, acts, wo, x2, mod_arr, g, layer, wu, wd, final_g, rows_per_batch):
    m, d = x2.shape
    tm = TM_ROWS if prompt else m
    tpb = rows_per_batch // tm if prompt else 1
    row = lambda i: (i, 0)
    const2 = lambda i: (0, 0)
    final = final_g is not None
    resident = lambda w: pl.BlockSpec(w.shape, const2, pipeline_mode=pl.Buffered(1))
    in_specs = [pl.BlockSpec((tm, a.shape[1]), row) for a in acts]
    in_specs += [resident(wo), pl.BlockSpec((tm, d), row)]
    in_specs += [_mod_spec(prompt, tm, d, c, tpb) for c in (2, 3, 4, 5)]
    stacked = lambda w: pl.BlockSpec((None,) + w.shape[1:], lambda i: (layer, 0, 0), pipeline_mode=pl.Buffered(1))
    in_specs += [pl.BlockSpec((1, d), const2), stacked(wu), stacked(wd)]
    args = list(acts) + [wo, x2] + [mod_arr] * 4 + [g, wu, wd]
    if final:
        in_specs.append(pl.BlockSpec((1, d), const2))
        args.append(final_g)
    return pl.pallas_call(
        functools.partial(_post_kernel, len(acts), final, 1024),
        out_shape=jax.ShapeDtypeStruct((m, d), F32),
        grid=(m // tm,), in_specs=in_specs, out_specs=pl.BlockSpec((tm, d), row),
        compiler_params=_cparams(("arbitrary",)),
        name="post",
    )(*args)


def _sb_in_prompt_kernel(x_ref, sh_ref, sc_ref, g_ref, wq_ref, wkt_ref, wvt_ref,
                         q_ref, ktb_ref, vtb_ref, kt_ref, vt_ref):
    h = _rms_mod(x_ref[...], g_ref[...], sc_ref[...], sh_ref[...]).astype(BF16)
    q_ref[...] = (_dot(h, wq_ref[...]) * (DH ** -0.5)).astype(q_ref.dtype)
    kt = _dot_nt(wkt_ref[...], h)
    kt_ref[...] = kt
    ktb_ref[...] = kt.astype(BF16)
    vt = _dot_nt(wvt_ref[...], h)
    vt_ref[...] = vt
    vtb_ref[...] = vt.astype(BF16)


def _sb_in_sample_kernel(x_ref, sh_ref, sc_ref, g_ref, wq_ref, wkt_ref, wvt_ref,
                         q_ref, kb_ref, vb_ref, k32_ref, v32_ref):
    h = _rms_mod(x_ref[...], g_ref[...], sc_ref[...], sh_ref[...]).astype(BF16)
    q_ref[...] = (_dot(h, wq_ref[...]) * (DH ** -0.5)).astype(q_ref.dtype)
    k = _dot_nt(h, wkt_ref[...])
    k32_ref[...] = k
    kb_ref[...] = k.astype(BF16)
    v = _dot_nt(h, wvt_ref[...])
    v32_ref[...] = v
    vb_ref[...] = v.astype(BF16)


def _sb_in_call(prompt, x2, mod_arr, g, wts, rows_per_batch):
    wq, wkt, wvt = wts
    m, d = x2.shape
    nb = m // rows_per_batch
    tm = TM_ROWS if prompt else m
    tpb = rows_per_batch // tm if prompt else 1
    row = lambda i: (i, 0)
    const2 = lambda i: (0, 0)
    in_specs = [pl.BlockSpec((tm, d), row),
                _mod_spec(prompt, tm, d, 0, tpb), _mod_spec(prompt, tm, d, 1, tpb),
                pl.BlockSpec((1, d), const2)] + [pl.BlockSpec(w.shape, const2) for w in wts]
    if prompt:
        tcol = lambda i: (i // tpb, 0, i % tpb)
        out_shape = ([jax.ShapeDtypeStruct((m, d), BF16)]
                     + [jax.ShapeDtypeStruct((nb, d, rows_per_batch), BF16)] * 2
                     + [jax.ShapeDtypeStruct((nb, d, rows_per_batch), F32)] * 2)
        out_specs = [pl.BlockSpec((tm, d), row)] + [pl.BlockSpec((None, d, tm), tcol)] * 4
        body = _sb_in_prompt_kernel
    else:
        out_shape = [jax.ShapeDtypeStruct((m, d), BF16)] * 3 + [jax.ShapeDtypeStruct((m, d), F32)] * 2
        out_specs = [pl.BlockSpec((tm, d), row)] * 5
        body = _sb_in_sample_kernel
    return pl.pallas_call(
        body, out_shape=out_shape, grid=(m // tm,), in_specs=in_specs, out_specs=out_specs,
        compiler_params=_cparams(("arbitrary",)),
        name="sb_in_prompt" if prompt else "sb_in_sample",
    )(x2, mod_arr, mod_arr, g, wq, wkt, wvt)


def _to_state(xt, n_heads):
    nb, _, l = xt.shape
    return xt.reshape(nb, n_heads, DH, l).transpose(0, 3, 1, 2)[None]


def _from_cache(c):
    nb, p, n_heads, dh = c.shape
    return c.transpose(0, 2, 3, 1).reshape(nb, n_heads * dh, p)


def _trunk(prompt, x, mods, past, wts):
    nb, rpb, d = x.shape
    m = nb * rpb
    a_w = H_FOX * DH
    x2 = x.reshape(m, d)
    norm_g, ab_wts, w_out_ab, sb_wts, w_out_sb, w_up, w_down, final_g = wts

    if prompt:
        outs = _ab_in_call(True, x2, mods[0], norm_g[0, 0][None], ab_wts, None, rpb, 0)
        q, ktb, vtb, kt32, vt32, lft, fcp, kaug, pooled, pstate = outs
        a_out = _fox_call(q.reshape(nb, rpb, a_w), ktb, vtb, kaug, fcp)
        st_fox = (_to_state(kt32, H_FOX), _to_state(vt32, H_FOX), lft.transpose(0, 2, 1)[None])
    else:
        cfk, cfv, cflogf, spool, csk, csv = past
        p0 = cfk.shape[1]
        hist = jnp.pad(spool, ((0, 0), (HIST_ROWS - spool.shape[1], 0), (0, 0)))
        outs = _ab_in_call(False, x2, mods[0], norm_g[0, 0][None], ab_wts, hist, rpb, p0)
        q, kb, vb, k32, v32, lft, pooled, pstate = outs
        lfn_t = lft.reshape(H_FOX, nb, rpb).transpose(1, 0, 2)
        a_out = _fox_dec_call(q.reshape(nb, rpb, a_w), kb.reshape(nb, rpb, a_w), vb.reshape(nb, rpb, a_w),
                              _from_cache(cfk), _from_cache(cfv), lfn_t, cflogf.transpose(0, 2, 1))
        st_fox = (k32.reshape(1, nb, rpb, H_FOX, DH), v32.reshape(1, nb, rpb, H_FOX, DH),
                  lfn_t.transpose(0, 2, 1)[None])
    x2 = _post_call(prompt, [a_out.reshape(m, a_w), pooled], w_out_ab, x2, mods[0], norm_g[0, 1][None],
                    0, w_up, w_down, None, rpb)

    n_heads = d // DH
    outs = _sb_in_call(prompt, x2, mods[1], norm_g[1, 0][None], sb_wts, rpb)
    if prompt:
        q1, ktb1, vtb1, skt32, svt32 = outs
        s_out = _sb_call(q1.reshape(nb, rpb, d), ktb1, vtb1)
        st_sb = (_to_state(skt32, n_heads), _to_state(svt32, n_heads))
    else:
        q1, kb1, vb1, sk32, sv32 = outs
        s_out = _sb_dec_call(q1.reshape(nb, rpb, d), kb1.reshape(nb, rpb, d), vb1.reshape(nb, rpb, d),
                             _from_cache(csk), _from_cache(csv))
        st_sb = (sk32.reshape(1, nb, rpb, n_heads, DH), sv32.reshape(1, nb, rpb, n_heads, DH))
    y = _post_call(prompt, [s_out.reshape(m, d)], w_out_sb, x2, mods[1], norm_g[1, 1][None],
                   1, w_up, w_down, final_g[None], rpb)

    states = st_fox + (pstate[:, 1:, :][None],) + st_sb
    return y.reshape(nb, rpb, d), states


def kernel(x_prompt, x_sample, c_prompt, c_sample, cache_fox_k, cache_fox_v, cache_fox_logf, state_pool, cache_sb_k, cache_sb_v, w_ada, b_ada, norm_g, w_in_ab, b_forget, w_pool, pool_scale, w_out_ab, w_in_sb, w_out_sb, w_up, w_down, final_g):
    b, _, d = x_prompt.shape
    db, ds, _ = x_sample.shape
    a_w = H_FOX * DH

    c_all = jnp.concatenate([c_prompt, c_sample, jnp.zeros((ADA_ROWS - b - db, d), F32)], axis=0)
    mod = _ada_call(c_all, w_ada, b_ada)
    mods_p = [mod[i, :b][:, None, :] for i in range(mod.shape[0])]
    mods_s = [jnp.repeat(mod[i, b:b + db], ds, axis=0) for i in range(mod.shape[0])]

    w_in = w_in_ab[0]
    gate_pad = ((0, GATE_ROWS - H_FOX), (0, 0))
    ab_wts = (w_in[:, :a_w].astype(BF16),
              w_in[:, a_w:2 * a_w].T.astype(BF16),
              w_in[:, 2 * a_w:3 * a_w].T.astype(BF16),
              w_in[:, 3 * a_w + H_FOX:].astype(BF16),
              jnp.pad(w_in[:, 3 * a_w:3 * a_w + H_FOX].T, gate_pad).astype(BF16),
              jnp.pad(b_forget[0][:, None], gate_pad),
              w_pool[0].astype(BF16), pool_scale[0][None])
    w_sb = w_in_sb[0]
    sb_wts = (w_sb[:, :d].astype(BF16), w_sb[:, d:2 * d].T.astype(BF16), w_sb[:, 2 * d:].T.astype(BF16))
    wts = (norm_g, ab_wts, w_out_ab[0].astype(BF16), sb_wts, w_out_sb[0].astype(BF16),
           w_up.astype(BF16), w_down.astype(BF16), final_g)

    y_p, st_p = _trunk(True, x_prompt, mods_p, None, wts)
    past = (cache_fox_k[0], cache_fox_v[0], cache_fox_logf[0], state_pool[0],
            cache_sb_k[0], cache_sb_v[0])
    y_s, st_s = _trunk(False, x_sample, mods_s, past, wts)
    fk, fv, fl, pool, sk, sv = st_p
    fk2, fv2, fl2, pool2, sk2, sv2 = st_s
    return (y_p, y_s, fk, fv, fl, pool, sk, sv, fk2, fv2, fl2, pool2, sk2, sv2)
```

```python
import functools

import jax
import jax.numpy as jnp
from jax import lax
from jax.experimental import pallas as pl
from jax.experimental.pallas import tpu as pltpu

F32 = jnp.float32
BF16 = jnp.bfloat16

DH = 64
H_FOX = 8
EPS = 1e-6
POOL_WINDOWS = (2, 4, 8, 16)
POOL_GC = 128
HIST_ROWS = 16
NEG_BIG = -1e30
LOG2E = 1.4426950408889634
LANES = 128
GATE_ROWS = 16
AUG_ROWS = 16
OFF_SLOT = 12
VMEM_LIMIT = 56 * 1024 * 1024

UNDERFLOW_EXP = 88.0
NORM_MARGIN = 1.01

T_ATT = 256
NQ_ATT = 4
T_SB = 256
NQ_SB = 4
FIRST_SB = 2
TK_DEC_FIRST = 256
TK_DEC_REST = 1280
TK_DEC_FOX = 4096
TM_ROWS = 512
ADA_ROWS = 32
ADA_TN = 1536


def _cparams(sem):
    return pltpu.CompilerParams(dimension_semantics=sem, vmem_limit_bytes=VMEM_LIMIT)


def _rms_mod(x, g, sc, sh):
    y = x * lax.rsqrt(jnp.mean(x * x, axis=-1, keepdims=True) + EPS)
    return (y * g) * (1.0 + sc) + sh


def _softplus(z):
    return jnp.maximum(z, 0.0) + jnp.log(1.0 + jnp.exp2(jnp.abs(z) * (-LOG2E)))


def _log_sigmoid(x):
    return jnp.minimum(x, 0.0) - jnp.log(1.0 + jnp.exp(-jnp.abs(x)))


def _split3(x):
    hi = x.astype(BF16)
    r = x - hi.astype(F32)
    mid = r.astype(BF16)
    lo = (r - mid.astype(F32)).astype(BF16)
    return hi, mid, lo


def _dot(a, b):
    return jnp.dot(a, b, preferred_element_type=F32)


def _dot_nt(a, b):
    return lax.dot_general(a, b, (((1,), (1,)), ((), ())), preferred_element_type=F32)


def _dot3(x, m):
    hi, mid, lo = _split3(x)
    return _dot(hi, m) + _dot(mid, m) + _dot(lo, m)


def _iota(shape, dim):
    return lax.broadcasted_iota(jnp.int32, shape, dim)


def _log2(n):
    assert n & (n - 1) == 0, n
    return n.bit_length() - 1


def _ada_kernel(c_ref, w_ref, b_ref, o_ref):
    c = c_ref[...]
    cond = (c / (1.0 + jnp.exp(-c))).astype(BF16)
    o_ref[...] = _dot(cond, w_ref[...].astype(BF16)) + b_ref[...]


def _ada_call(c_all, w_ada, b_ada):
    depth, d, n = w_ada.shape
    return pl.pallas_call(
        _ada_kernel,
        out_shape=jax.ShapeDtypeStruct((depth, ADA_ROWS, n), F32),
        grid=(depth, n // ADA_TN),
        in_specs=[
            pl.BlockSpec((ADA_ROWS, d), lambda i, j: (0, 0)),
            pl.BlockSpec((None, d, ADA_TN), lambda i, j: (i, 0, j)),
            pl.BlockSpec((None, 1, ADA_TN), lambda i, j: (i, 0, j)),
        ],
        out_specs=pl.BlockSpec((None, ADA_ROWS, ADA_TN), lambda i, j: (i, 0, j)),
        compiler_params=_cparams(("arbitrary", "arbitrary")),
        name="ada_mod",
    )(c_all, w_ada, b_ada.reshape(depth, 1, n))


def _mod_spec(prompt, tm, d, chunk, tiles_per_batch):
    if prompt:
        return pl.BlockSpec((None, 1, d), lambda i: (i // tiles_per_batch, 0, chunk))
    return pl.BlockSpec((tm, d), lambda i: (i, chunk))


def _pool_branch(u, ext_ref, pos, ng, rg, wp_ref, ps_ref, pooled_ref, pst_ref):
    tm, a_w = u.shape
    u3 = u.reshape(ng, rg, a_w)
    ext_ref[:, HIST_ROWS:HIST_ROWS + rg, :] = u3
    pst_ref[...] = u3[:, rg - HIST_ROWS:rg, :]
    for g, w in enumerate(POOL_WINDOWS):
        cs = slice(g * POOL_GC, (g + 1) * POOL_GC)
        acc = u3[:, :, cs]
        for kk in range(1, w):
            acc = acc + ext_ref[:, pl.ds(HIST_ROWS - kk, rg), cs]
        cnt = jnp.minimum(pos + 1, w).astype(F32)
        pooled = acc.reshape(tm, POOL_GC) / cnt - u[:, cs]
        pg = _dot(pooled.astype(BF16), wp_ref[g]) * ps_ref[:, cs]
        pooled_ref[:, cs] = pg.astype(pooled_ref.dtype)


def _ab_in_prompt_kernel(tm, tiles_per_batch, x_ref, sh_ref, sc_ref, g_ref, wq_ref, wkt_ref, wvt_ref,
                         wu_ref, wft_ref, bf_ref, wp_ref, ps_ref,
                         q_ref, ktb_ref, vtb_ref, kt_ref, vt_ref, lft_ref, fcp_ref, kaug_ref, pooled_ref, pst_ref,
                         ext_ref, carry_ref):
    h = _rms_mod(x_ref[...], g_ref[...], sc_ref[...], sh_ref[...]).astype(BF16)
    q_ref[...] = (_dot(h, wq_ref[...]) * (DH ** -0.5)).astype(q_ref.dtype)
    kt = _dot_nt(wkt_ref[...], h)
    kt_ref[...] = kt
    ktb_ref[...] = kt.astype(BF16)
    vt = _dot_nt(wvt_ref[...], h)
    vt_ref[...] = vt
    vtb_ref[...] = vt.astype(BF16)

    tile_in_batch = pl.program_id(0) % tiles_per_batch
    first = tile_in_batch == 0
    logf = _log_sigmoid(_dot_nt(wft_ref[...], h) + bf_ref[...])
    lft_ref[...] = logf[0:H_FOX, :]
    tri = (_iota((tm, tm), 0) <= _iota((tm, tm), 1)).astype(BF16)
    fcum = _dot3(logf, tri) + jnp.where(first, 0.0, carry_ref[...])
    carry_ref[...] = fcum[:, tm - 1:tm]
    parts = _split3(fcum)
    prow = _iota((4 * H_FOX, GATE_ROWS), 0)
    pick = jnp.logical_and((prow & 7) < 2, _iota((4 * H_FOX, GATE_ROWS), 1) == (prow >> 3) * 2 + (prow & 7))
    pick = jnp.where(pick, 1.0, 0.0).astype(BF16)
    fcp_ref[...] = (_dot(pick, parts[0]) + _dot(pick, parts[1])) + _dot(pick, parts[2])

    n_aug = (H_FOX // 2) * AUG_ROWS
    slot = _iota((n_aug, GATE_ROWS), 0) & (AUG_ROWS - 1)
    head0 = (_iota((n_aug, GATE_ROWS), 0) >> _log2(AUG_ROWS)) * 2
    col = _iota((n_aug, GATE_ROWS), 1)
    kaug = None
    for part, piece in enumerate(parts):
        sel = jnp.logical_or(jnp.logical_and(slot == 3 + part, col == head0),
                             jnp.logical_and(slot == 9 + part, col == head0 + 1))
        term = _dot(jnp.where(sel, -1.0, 0.0).astype(BF16), piece)
        kaug = term if kaug is None else kaug + term
    slot1 = _iota((n_aug, 1), 0) & (AUG_ROWS - 1)
    ones = jnp.logical_or(slot1 < 3, jnp.logical_and(slot1 >= 6, slot1 < 9))
    kaug_ref[...] = (kaug + jnp.where(ones, 1.0, 0.0)).astype(BF16)

    @pl.when(first)
    def _():
        ext_ref[:, 0:HIST_ROWS, :] = jnp.zeros((1, HIST_ROWS, ext_ref.shape[2]), F32)

    @pl.when(jnp.logical_not(first))
    def _():
        ext_ref[:, 0:HIST_ROWS, :] = ext_ref[:, tm:tm + HIST_ROWS, :]

    pos = tile_in_batch * tm + _iota((tm, 1), 0)
    _pool_branch(_dot(h, wu_ref[...]), ext_ref, pos, 1, tm, wp_ref, ps_ref, pooled_ref, pst_ref)


def _ab_in_sample_kernel(ng, rg, p0, x_ref, sh_ref, sc_ref, g_ref, wq_ref, wkt_ref, wvt_ref,
                         wu_ref, wft_ref, bf_ref, wp_ref, ps_ref, hist_ref,
                         q_ref, kb_ref, vb_ref, k32_ref, v32_ref, lft_ref, pooled_ref, pst_ref, ext_ref):
    tm = ng * rg
    h = _rms_mod(x_ref[...], g_ref[...], sc_ref[...], sh_ref[...]).astype(BF16)
    q_ref[...] = (_dot(h, wq_ref[...]) * (DH ** -0.5)).astype(q_ref.dtype)
    k = _dot_nt(h, wkt_ref[...])
    k32_ref[...] = k
    kb_ref[...] = k.astype(BF16)
    v = _dot_nt(h, wvt_ref[...])
    v32_ref[...] = v
    vb_ref[...] = v.astype(BF16)
    logf = _log_sigmoid(_dot_nt(wft_ref[...], h) + bf_ref[...])
    lft_ref[...] = logf[0:H_FOX, :]
    ext_ref[:, 0:HIST_ROWS, :] = hist_ref[...]
    pos = p0 + (_iota((tm, 1), 0) & (rg - 1))
    _pool_branch(_dot(h, wu_ref[...]), ext_ref, pos, ng, rg, wp_ref, ps_ref, pooled_ref, pst_ref)


def _ab_in_call(prompt, x2, mod_arr, g, wts, hist, rows_per_batch, p0):
    wq, wkt, wvt, wu, wft, bf, wp, ps = wts
    m, d = x2.shape
    a_w = H_FOX * DH
    nb = m // rows_per_batch
    tm = TM_ROWS if prompt else m
    tpb = rows_per_batch // tm if prompt else 1
    row = lambda i: (i, 0)
    const2 = lambda i: (0, 0)
    in_specs = [pl.BlockSpec((tm, d), row), _mod_spec(prompt, tm, d, 0, tpb), _mod_spec(prompt, tm, d, 1, tpb),
                pl.BlockSpec((1, d), const2)]
    in_specs += [pl.BlockSpec(w.shape, const2) for w in (wq, wkt, wvt, wu, wft, bf)]
    in_specs += [pl.BlockSpec(wp.shape, lambda i: (0, 0, 0)), pl.BlockSpec(ps.shape, const2)]
    args = [x2, mod_arr, mod_arr, g, wq, wkt, wvt, wu, wft, bf, wp, ps]
    if prompt:
        tcol = lambda i: (i // tpb, 0, i % tpb)
        out_shape = [jax.ShapeDtypeStruct((m, a_w), BF16),
                     jax.ShapeDtypeStruct((nb, a_w, rows_per_batch), BF16),
                     jax.ShapeDtypeStruct((nb, a_w, rows_per_batch), BF16),
                     jax.ShapeDtypeStruct((nb, a_w, rows_per_batch), F32),
                     jax.ShapeDtypeStruct((nb, a_w, rows_per_batch), F32),
                     jax.ShapeDtypeStruct((nb, H_FOX, rows_per_batch), F32),
                     jax.ShapeDtypeStruct((nb, 4 * H_FOX, rows_per_batch), F32),
                     jax.ShapeDtypeStruct((nb, (H_FOX // 2) * AUG_ROWS, rows_per_batch), BF16),
                     jax.ShapeDtypeStruct((m, a_w), BF16),
                     jax.ShapeDtypeStruct((nb, HIST_ROWS, a_w), F32)]
        out_specs = ([pl.BlockSpec((tm, a_w), row)] + [pl.BlockSpec((None, a_w, tm), tcol)] * 4
                     + [pl.BlockSpec((None, H_FOX, tm), tcol), pl.BlockSpec((None, 4 * H_FOX, tm), tcol)]
                     + [pl.BlockSpec((None, (H_FOX // 2) * AUG_ROWS, tm), tcol)]
                     + [pl.BlockSpec((tm, a_w), row),
                        pl.BlockSpec((1, HIST_ROWS, a_w), lambda i: (i // tpb, 0, 0))])
        scratch = [pltpu.VMEM((1, HIST_ROWS + tm, a_w), F32), pltpu.VMEM((GATE_ROWS, 1), F32)]
        body = functools.partial(_ab_in_prompt_kernel, tm, tpb)
    else:
        in_specs.append(pl.BlockSpec(hist.shape, lambda i: (0, 0, 0)))
        args.append(hist)
        out_shape = [jax.ShapeDtypeStruct((m, a_w), BF16), jax.ShapeDtypeStruct((m, a_w), BF16),
                     jax.ShapeDtypeStruct((m, a_w), BF16), jax.ShapeDtypeStruct((m, a_w), F32),
                     jax.ShapeDtypeStruct((m, a_w), F32), jax.ShapeDtypeStruct((H_FOX, m), F32),
                     jax.ShapeDtypeStruct((m, a_w), BF16), jax.ShapeDtypeStruct((nb, HIST_ROWS, a_w), F32)]
        out_specs = ([pl.BlockSpec((tm, a_w), row)] * 5 + [pl.BlockSpec((H_FOX, tm), const2)]
                     + [pl.BlockSpec((tm, a_w), row), pl.BlockSpec((nb, HIST_ROWS, a_w), lambda i: (0, 0, 0))])
        scratch = [pltpu.VMEM((nb, HIST_ROWS + rows_per_batch, a_w), F32)]
        body = functools.partial(_ab_in_sample_kernel, nb, rows_per_batch, p0)
    return pl.pallas_call(
        body, out_shape=out_shape, grid=(m // tm,), in_specs=in_specs, out_specs=out_specs,
        scratch_shapes=scratch,
        compiler_params=_cparams(("arbitrary",)),
        name="ab_in_prompt" if prompt else "ab_in_sample",
    )(*args)


def _head_split(q2):
    lane = _iota((1, LANES), 1)
    zero = jnp.zeros_like(q2)
    return jnp.where(lane < DH, q2, zero), jnp.where(lane >= DH, q2, zero)


def _pair_sel():
    return (_iota((8, LANES), 0) == (_iota((8, LANES), 1) >> _log2(DH))).astype(BF16)


def _fox_kernel(q_ref, kt_ref, vt_ref, ka_ref, fr_ref, fall_ref, o_ref,
                m_ref, acc_ref, kn_ref):
    t, nq = T_ATT, NQ_ATT
    qg = pl.program_id(2)
    lane = _iota((1, LANES), 1)

    @pl.when(qg == 0)
    def _():
        chunk = 4 * t

        def body(i, mx):
            ks = pl.multiple_of(i * chunk, chunk)
            kf = kt_ref[:, pl.ds(ks, chunk)].astype(F32)
            n2 = _dot(_pair_sel(), (kf * kf).astype(BF16))
            return jnp.maximum(mx, jnp.max(n2, axis=1, keepdims=True))

        mx = lax.fori_loop(0, kt_ref.shape[1] // chunk, body, jnp.zeros((8, 1), F32))
        kn_ref[...] = jnp.sqrt(mx)

    row_a = _iota((AUG_ROWS, 1), 0)
    row_v = _iota((LANES, 1), 0)
    qcat, bound0 = [], []
    for s in range(nq):
        q2 = q_ref[s * t:(s + 1) * t, :]
        frows = fr_ref[:, s * t:(s + 1) * t]
        eye = (_iota((t, t), 0) == _iota((t, t), 1)).astype(BF16)
        per_head = []
        for j, qh in enumerate(_head_split(q2)):
            hi, mid, lo = (x.astype(F32) for x in _split3(frows[j:j + 1, :]))
            base = 6 * j
            qat = jnp.where(row_a == base, hi, jnp.where(row_a == base + 1, mid, jnp.where(
                row_a == base + 2, lo,
                jnp.where(jnp.logical_and(row_a >= base + 3, row_a < base + 6), 1.0, 0.0))))
            qat = jnp.where(row_a == OFF_SLOT, NEG_BIG, qat)
            qa = _dot_nt(eye, qat.astype(BF16))
            per_head.append(jnp.concatenate([qh, qa.astype(BF16)], axis=1))
        qcat.append(per_head)
        qf = q2.astype(F32)
        qn2 = _dot_nt((qf * qf).astype(BF16), _pair_sel())
        qn = jnp.sqrt(jnp.max(qn2, axis=0, keepdims=True))
        bound0.append([qn[:, j:j + 1] * kn_ref[j:j + 1, :] * NORM_MARGIN + frows[j:j + 1, 0:1] for j in range(2)])
    chains = [(s, j) for s in range(nq) for j in range(2)]
    rep = lambda m: jnp.concatenate([m] * (t // LANES), axis=1)

    def front(n, diag):
        kbs = [qg * nq + s - n for s in range(nq)]
        kss = [pl.multiple_of((kb if diag else jnp.maximum(kb, 0)) * t, t) for kb in kbs]
        causal = _iota((t, t), 1) <= _iota((t, t), 0)
        kcat = []
        for s in range(nq):
            ka = ka_ref[:, pl.ds(kss[s], t)]
            if not diag:
                off = jnp.where(kbs[s] >= 0, 0.0, 1.0).astype(BF16)
                ka = jnp.where(_iota((AUG_ROWS, 1), 0) == OFF_SLOT, off, ka)
            kcat.append(jnp.concatenate([kt_ref[:, pl.ds(kss[s], t)], ka], axis=0))
        sc = []
        for s, j in chains:
            x = _dot(qcat[s][j], kcat[s])
            sc.append(jnp.where(causal, x, NEG_BIG) if diag else x)
        vcat = []
        for s, j in chains:
            v2 = vt_ref[:, pl.ds(kss[s], t)]
            mine = (row_v < DH) if j == 0 else (row_v >= DH)
            vcat.append(jnp.where(mine, v2, jnp.ones_like(v2)))
        return sc, vcat

    (sc0, v0), (sc1, v1) = front(0, True), front(1, False)
    m01 = [jnp.broadcast_to(jnp.maximum(jnp.max(x0, axis=1, keepdims=True), jnp.max(x1, axis=1, keepdims=True)),
                            (t, LANES)) for x0, x1 in zip(sc0, sc1)]
    pv0 = [_dot_nt(jnp.exp(x - rep(m)).astype(BF16), vv) for x, m, vv in zip(sc0, m01, v0)]
    pv1 = [_dot_nt(jnp.exp(x - rep(m)).astype(BF16), vv) for x, m, vv in zip(sc1, m01, v1)]
    for i, (s, j) in enumerate(chains):
        acc_ref[2 * s + j] = pv0[i] + pv1[i]
        m_ref[2 * s + j] = m01[i]

    def front_wide(n, chains):
        kcat, vts = [], []
        for s in range(nq):
            kb = qg * nq + s - n
            start = pl.multiple_of(jnp.maximum(kb - 1, 0) * t, t)
            off = jnp.where(start + _iota((1, 2 * t), 1) >= (kb + 1) * t, 1.0, 0.0).astype(BF16)
            ka = jnp.where(_iota((AUG_ROWS, 1), 0) == OFF_SLOT, off, ka_ref[:, pl.ds(start, 2 * t)])
            kcat.append(jnp.concatenate([kt_ref[:, pl.ds(start, 2 * t)], ka], axis=0))
            vts.append(vt_ref[:, pl.ds(start, 2 * t)])
        sc = [_dot(qcat[s][j], kcat[s]) for s, j in chains]
        vcat = [jnp.where((row_v < DH) if j == 0 else (row_v >= DH), vts[s], jnp.ones_like(vts[s]))
                for s, j in chains]
        return sc, vcat

    def step(n, chains):
        sc, vcat = front_wide(n, chains)
        m_prev = [m_ref[2 * s + j] for s, j in chains]
        m_new = [jnp.maximum(mp, jnp.max(x, axis=1, keepdims=True)) for mp, x in zip(m_prev, sc)]
        pv = [_dot_nt(jnp.exp(x - jnp.concatenate([rep(mn)] * 2, axis=1)).astype(BF16), vv)
              for x, mn, vv in zip(sc, m_new, vcat)]
        for i, (s, j) in enumerate(chains):
            idx = 2 * s + j
            acc_ref[idx] = jnp.exp(m_prev[i] - m_new[i]) * acc_ref[idx] + pv[i]
            m_ref[idx] = m_new[i]

    def more(n):
        go = [False, False]
        for s in range(nq):
            kb = qg * nq + s - n
            last = pl.multiple_of(jnp.maximum(kb, 0) * t + (t - LANES), LANES)
            ft = fall_ref[:, pl.ds(last, LANES)][:, LANES - 1:LANES]
            for j in range(2):
                top = jnp.max(bound0[s][j] - ft[j:j + 1, :]) - jnp.min(m_ref[2 * s + j])
                go[j] = jnp.logical_or(go[j], jnp.logical_and(kb >= 0, top > -UNDERFLOW_EXP))
        return go

    def body(chains, c):
        n = c[0]
        step(n, chains)
        return (n + 2, *more(n + 2))

    c = lax.while_loop(lambda c: jnp.logical_and(c[1], c[2]), functools.partial(body, chains),
                       (jnp.int32(2), *more(2)))
    c = lax.while_loop(lambda c: c[1], functools.partial(body, [(s, 0) for s in range(nq)]), c)
    lax.while_loop(lambda c: c[2], functools.partial(body, [(s, 1) for s in range(nq)]), c)
    for s in range(nq):
        a0, a1 = acc_ref[2 * s], acc_ref[2 * s + 1]
        o = jnp.where(lane < DH, a0 / pltpu.roll(a0, DH, 1), a1 / pltpu.roll(a1, DH, 1))
        o_ref[s * t:(s + 1) * t, :] = o.astype(o_ref.dtype)


def _fox_call(q, ktb, vtb, kaug, fcp):
    b, l, a_w = q.shape
    hp = a_w // LANES
    t, nq = T_ATT, NQ_ATT
    tq = t * nq
    return pl.pallas_call(
        _fox_kernel,
        out_shape=jax.ShapeDtypeStruct((b, l, a_w), BF16),
        grid=(b, hp, l // tq),
        in_specs=[
            pl.BlockSpec((None, tq, LANES), lambda bi, h, qi: (bi, qi, h)),
            pl.BlockSpec((None, LANES, l), lambda bi, h, qi: (bi, h, 0)),
            pl.BlockSpec((None, LANES, l), lambda bi, h, qi: (bi, h, 0)),
            pl.BlockSpec((None, AUG_ROWS, l), lambda bi, h, qi: (bi, h, 0)),
            pl.BlockSpec((None, 8, tq), lambda bi, h, qi: (bi, h, qi)),
            pl.BlockSpec((None, 8, l), lambda bi, h, qi: (bi, h, 0)),
        ],
        out_specs=pl.BlockSpec((None, tq, LANES), lambda bi, h, qi: (bi, qi, h)),
        scratch_shapes=[pltpu.VMEM((2 * nq, t, LANES), F32), pltpu.VMEM((2 * nq, t, LANES), F32),
                        pltpu.VMEM((8, 1), F32)],
        compiler_params=_cparams(("arbitrary", "arbitrary", "arbitrary")),
        name="fox_prompt",
    )(q, ktb, vtb, kaug, fcp, fcp)


def _sb_kernel(q_ref, kt_ref, vt_ref, o_ref, carry_ref, acc_ref):
    t, nq = T_SB, NQ_SB
    qg = pl.program_id(2)
    lane = _iota((1, LANES), 1)
    qh = [_head_split(q_ref[s * t:(s + 1) * t, :]) for s in range(nq)]
    valid = _iota((t, t), 1) < _iota((t, t), 0)

    def logits(s, n, diag):
        kb = qg * nq + s - n
        ks = pl.multiple_of((kb if diag else jnp.maximum(kb, 0)) * t, t)
        return kb, ks, [_dot(qh[s][j], kt_ref[:, pl.ds(ks, t)]) for j in range(2)]

    def suffix(diag, z):
        tri = (_iota((t, t), 0) > _iota((t, t), 1)).astype(BF16)
        sp = [_softplus(zz) for zz in z]
        spm = [jnp.where(valid, x, 0.0) for x in sp] if diag else sp
        later = [_dot(x.astype(BF16), tri) for x in spm]
        return [zz - x for zz, x in zip(z, sp)], spm, later

    def weigh(diag, kb, ks, ls, spm, later, carry):
        v2 = vt_ref[:, pl.ds(ks, t)]
        if not diag:
            v2 = jnp.where(kb >= 0, v2, jnp.zeros_like(v2))
        pv, new_carry = [], []
        for j in range(2):
            e = ls[j] - later[j]
            a = jnp.exp(e if carry is None else e - carry[j])
            if diag:
                a = jnp.where(valid, a, 0.0)
            pv.append(_dot_nt(a.astype(BF16), v2))
            rs = jnp.sum(spm[j], axis=1, keepdims=True)
            new_carry.append(rs if carry is None else carry[j] + rs)
        return pv, new_carry

    def run(units, carry, acc):
        st1, st2 = {}, {}
        for k in range(len(units) + 2):
            if k < len(units):
                st1[k] = logits(*units[k])
            if 0 <= k - 1 < len(units):
                st2[k - 1] = suffix(units[k - 1][2], st1[k - 1][2])
            if 0 <= k - 2 < len(units):
                s, _, diag = units[k - 2]
                kb, ks, _ = st1.pop(k - 2)
                pv, carry[s] = weigh(diag, kb, ks, *st2.pop(k - 2), carry[s])
                acc[s] = pv if acc[s] is None else [x + y for x, y in zip(acc[s], pv)]
        return carry, acc

    carry, acc = run([(s, n, n == 0) for n in range(FIRST_SB) for s in range(nq)], [None] * nq, [None] * nq)
    for s in range(nq):
        for j in range(2):
            acc_ref[2 * s + j] = acc[s][j]
            carry_ref[2 * s + j] = carry[s][j]

    def more(n):
        in_range = n <= qg * nq + nq - 1
        return jnp.logical_and(in_range, jnp.min(carry_ref[...]) < UNDERFLOW_EXP)

    def body(c):
        n, _ = c
        carry, acc = run([(s, n, False) for s in range(nq)],
                         [[carry_ref[2 * s], carry_ref[2 * s + 1]] for s in range(nq)], [None] * nq)
        for s in range(nq):
            for j in range(2):
                acc_ref[2 * s + j] = acc_ref[2 * s + j] + acc[s][j]
                carry_ref[2 * s + j] = carry[s][j]
        return n + 1, more(n + 1)

    lax.while_loop(lambda c: c[1], body, (jnp.int32(FIRST_SB), more(FIRST_SB)))
    for s in range(nq):
        o = jnp.where(lane < DH, acc_ref[2 * s], acc_ref[2 * s + 1])
        o_ref[s * t:(s + 1) * t, :] = o.astype(o_ref.dtype)


def _sb_call(q, ktb, vtb):
    b, l, d = q.shape
    hp = d // LANES
    t, nq = T_SB, NQ_SB
    tq = t * nq
    return pl.pallas_call(
        _sb_kernel,
        out_shape=jax.ShapeDtypeStruct((b, l, d), BF16),
        grid=(b, hp, l // tq),
        in_specs=[
            pl.BlockSpec((None, tq, LANES), lambda bi, h, qi: (bi, qi, h)),
            pl.BlockSpec((None, LANES, l), lambda bi, h, qi: (bi, h, 0)),
            pl.BlockSpec((None, LANES, l), lambda bi, h, qi: (bi, h, 0)),
        ],
        out_specs=pl.BlockSpec((None, tq, LANES), lambda bi, h, qi: (bi, qi, h)),
        scratch_shapes=[pltpu.VMEM((2 * nq, t, 1), F32), pltpu.VMEM((2 * nq, t, LANES), F32)],
        compiler_params=_cparams(("arbitrary", "arbitrary", "arbitrary")),
        name="sb_prompt",
    )(q, ktb, vtb)


def _rep_matrix(rows, n_heads, ds):
    return ((_iota((rows, ds), 0) >> _log2(n_heads)) == _iota((rows, ds), 1)).astype(BF16)


def _head_mask(rows, n_heads, width):
    return (_iota((rows, width), 0) & (n_heads - 1)) == (_iota((rows, width), 1) >> _log2(DH))


def _block_diag_q(q, n_heads):
    ds, width = q.shape
    rows = ds * n_heads
    qrep = _dot(_rep_matrix(rows, n_heads, ds), q)
    return jnp.where(_head_mask(rows, n_heads, width), qrep, 0.0).astype(BF16)


def _collect_heads(acc, n_heads, ds):
    rows, width = acc.shape
    om = jnp.where(_head_mask(rows, n_heads, width), acc, 0.0).astype(BF16)
    rep_t = ((_iota((ds, rows), 1) >> _log2(n_heads)) == _iota((ds, rows), 0)).astype(BF16)
    return _dot(rep_t, om)


def _suffix_sums(x, exact=True, blk=2 * LANES):
    r, n = x.shape
    parts = [x[:, b * blk:(b + 1) * blk] for b in range(n // blk)]
    tri = (_iota((blk, blk), 0) > _iota((blk, blk), 1)).astype(BF16)
    stacked = jnp.concatenate(parts, axis=0)
    w = _dot3(stacked, tri) if exact else _dot(stacked.astype(BF16), tri)
    out, tail = [], None
    for b in reversed(range(len(parts))):
        cur = w[b * r:(b + 1) * r]
        out.append(cur if tail is None else cur + tail)
        tot = jnp.sum(parts[b], axis=1, keepdims=True)
        tail = tot if tail is None else tail + tot
    return jnp.concatenate(out[::-1], axis=1), tail


def _fox_dec_kernel(n_heads, ds, q_ref, kn_ref, vn_ref, lfn_ref, kc_ref, vc_ref, lfc_ref,
                    o_ref, qbd_ref, m_ref, l_ref, acc_ref, cf_ref):
    rows = ds * n_heads
    step_i = pl.program_id(1)
    n_steps = pl.num_programs(1)

    def update(s, pv_fn):
        m_prev = m_ref[...]
        m_new = jnp.maximum(m_prev, jnp.max(s, axis=1, keepdims=True))
        alpha = jnp.exp(m_prev - m_new)
        p = jnp.exp(s - m_new)
        l_ref[...] = alpha * l_ref[...] + jnp.sum(p, axis=1, keepdims=True)
        acc_ref[...] = alpha * acc_ref[...] + pv_fn(p.astype(BF16))
        m_ref[...] = m_new

    @pl.when(step_i == 0)
    def _():
        qbd_ref[...] = _block_diag_q(q_ref[...], n_heads)
        m_ref[...] = jnp.full(m_ref.shape, NEG_BIG, F32)
        l_ref[...] = jnp.zeros(l_ref.shape, F32)
        acc_ref[...] = jnp.zeros(acc_ref.shape, F32)
        a = jnp.concatenate([lfn_ref[...]] * ds, axis=0)
        ri = _iota((rows, ds), 0) >> _log2(n_heads)
        cj = _iota((rows, ds), 1)
        am = jnp.where(cj <= ri, a, 0.0)
        tri = (_iota((ds, ds), 0) > _iota((ds, ds), 1)).astype(BF16)
        s = _dot_nt(qbd_ref[...], kn_ref[...]) + _dot3(am, tri)
        s = jnp.where(cj <= ri, s, NEG_BIG)
        update(s, lambda p: _dot(p, vn_ref[...]))
        cf_ref[...] = jnp.sum(am, axis=1, keepdims=True)

    @pl.when(step_i > 0)
    def _():
        reps = rows // (2 * n_heads)
        later, total = _suffix_sums(jnp.concatenate([lfc_ref[...]] * 2, axis=0))
        cf = cf_ref[...]
        s = _dot(qbd_ref[...], kc_ref[...].astype(BF16)) + (jnp.concatenate([later] * reps, axis=0) + cf)
        update(s, lambda p: _dot_nt(p, vc_ref[...].astype(BF16)))
        cf_ref[...] = cf + jnp.concatenate([total] * reps, axis=0)

    @pl.when(step_i == n_steps - 1)
    def _():
        o_ref[...] = _collect_heads(acc_ref[...] / l_ref[...], n_heads, ds).astype(o_ref.dtype)


def _fox_dec_call(q, kn, vn, kct, vct, lfn_t, lfc_t):
    db, ds, width = q.shape
    n_heads = width // DH
    rows = ds * n_heads
    tk = min(TK_DEC_FOX, kct.shape[2])
    nkb = kct.shape[2] // tk
    cache_blk = lambda s: jnp.minimum(nkb - s, nkb - 1)
    new_idx = lambda b, s: (b, 0, 0)
    cache_idx = lambda b, s: (b, 0, cache_blk(s))
    in_specs = ([pl.BlockSpec((None, ds, width), new_idx)] * 3
                + [pl.BlockSpec((None, n_heads, ds), new_idx)]
                + [pl.BlockSpec((None, width, tk), cache_idx)] * 2
                + [pl.BlockSpec((None, n_heads, tk), cache_idx)])
    scratch = [pltpu.VMEM((rows, width), BF16), pltpu.VMEM((rows, 1), F32), pltpu.VMEM((rows, 1), F32),
               pltpu.VMEM((rows, width), F32), pltpu.VMEM((rows, 1), F32)]
    return pl.pallas_call(
        functools.partial(_fox_dec_kernel, n_heads, ds),
        out_shape=jax.ShapeDtypeStruct((db, ds, width), BF16),
        grid=(db, nkb + 1),
        in_specs=in_specs,
        out_specs=pl.BlockSpec((None, ds, width), new_idx),
        scratch_shapes=scratch,
        compiler_params=_cparams(("arbitrary", "arbitrary")),
        name="fox_decode",
    )(q, kn, vn, lfn_t, kct, vct, lfc_t)


def _sb_dec_tile(qbd, kc_ref, vc_ref, carry):
    z = _dot(qbd, kc_ref[...].astype(BF16))
    sp = _softplus(z)
    later, total = _suffix_sums(sp, exact=False)
    a = jnp.exp((z - sp) - later - carry)
    return _dot_nt(a.astype(BF16), vc_ref[...].astype(BF16)), carry + total


def _sb_dec_first_kernel(n_heads, ds, q_ref, kn_ref, vn_ref, kc_ref, vc_ref,
                         o_ref, carry_out_ref, cmin_ref):
    rows = ds * n_heads
    qbd = _block_diag_q(q_ref[...], n_heads)
    ri = _iota((rows, ds), 0) >> _log2(n_heads)
    cj = _iota((rows, ds), 1)
    valid = cj < ri
    z = _dot_nt(qbd, kn_ref[...])
    sp = _softplus(z)
    spm = jnp.where(valid, sp, 0.0)
    tri = (_iota((ds, ds), 0) > _iota((ds, ds), 1)).astype(BF16)
    later = _dot(spm.astype(BF16), tri)
    a = jnp.where(valid, jnp.exp((z - sp) - later), 0.0)
    acc = _dot(a.astype(BF16), vn_ref[...])
    carry = jnp.sum(spm, axis=1, keepdims=True)
    part, carry = _sb_dec_tile(qbd, kc_ref, vc_ref, carry)
    o_ref[...] = _collect_heads(acc + part, n_heads, ds).astype(o_ref.dtype)
    carry_out_ref[...] = carry
    cmin_ref[...] = jnp.broadcast_to(jnp.min(carry, axis=0, keepdims=True), cmin_ref.shape)


def _sb_dec_rest_kernel(n_heads, ds, n_rest, need_ref, q_ref, carry_in_ref, o1_ref, *refs):
    kc_refs, vc_refs = refs[:n_rest], refs[n_rest:2 * n_rest]
    o_ref, qbd_ref, acc_ref, carry_ref = refs[2 * n_rest:]
    needed = need_ref[pl.program_id(0)] != 0

    @pl.when(needed)
    def _():
        qbd_ref[...] = _block_diag_q(q_ref[...], n_heads)
        acc_ref[...] = jnp.zeros(acc_ref.shape, F32)
        carry_ref[...] = carry_in_ref[...]
        for i in reversed(range(n_rest)):
            @pl.when(jnp.min(carry_ref[...]) < UNDERFLOW_EXP)
            def _(i=i):
                part, carry = _sb_dec_tile(qbd_ref[...], kc_refs[i], vc_refs[i], carry_ref[...])
                acc_ref[...] = acc_ref[...] + part
                carry_ref[...] = carry
        o = o1_ref[...].astype(F32) + _collect_heads(acc_ref[...], n_heads, ds)
        o_ref[...] = o.astype(o_ref.dtype)

    @pl.when(jnp.logical_not(needed))
    def _():
        o_ref[...] = o1_ref[...]


def _sb_dec_call(q, kn, vn, kct, vct):
    db, ds, width = q.shape
    n_heads = width // DH
    rows = ds * n_heads
    past = kct.shape[2]
    first = min(TK_DEC_FIRST, past)
    n_blk = (past - first) // (2 * LANES)
    per = max([k for k in range(1, TK_DEC_REST // (2 * LANES) + 1) if n_blk % k == 0], default=1)
    rest = per * 2 * LANES
    n_rest = (past - first) // rest
    bidx = lambda b: (b, 0, 0)
    o1, carry, cmin = pl.pallas_call(
        functools.partial(_sb_dec_first_kernel, n_heads, ds),
        out_shape=[jax.ShapeDtypeStruct((db, ds, width), BF16),
                   jax.ShapeDtypeStruct((db, rows, 1), F32),
                   jax.ShapeDtypeStruct((db, 1, LANES), F32)],
        grid=(db,),
        in_specs=[pl.BlockSpec((None, ds, width), bidx)] * 3
        + [pl.BlockSpec((None, width, first), lambda b: (b, 0, past // first - 1))] * 2,
        out_specs=[pl.BlockSpec((None, ds, width), bidx), pl.BlockSpec((None, rows, 1), bidx),
                   pl.BlockSpec((None, 1, LANES), bidx)],
        compiler_params=_cparams(("arbitrary",)),
        name="sb_decode_first",
    )(q, kn, vn, kct, vct)
    if n_rest == 0:
        return o1
    need = (cmin[:, 0, 0] < UNDERFLOW_EXP).astype(jnp.int32)
    bidx2 = lambda b, need_ref: (b, 0, 0)

    def tile_spec(i):
        def idx(b, need_ref):
            on = need_ref[b] != 0
            return (jnp.where(on, b, 0), 0, jnp.where(on, i, 0))
        return pl.BlockSpec((None, width, rest), idx, pipeline_mode=pl.Buffered(1))

    grid_spec = pltpu.PrefetchScalarGridSpec(
        num_scalar_prefetch=1,
        grid=(db,),
        in_specs=[pl.BlockSpec((None, ds, width), bidx2), pl.BlockSpec((None, rows, 1), bidx2),
                  pl.BlockSpec((None, ds, width), bidx2)] + [tile_spec(i) for i in range(n_rest)] * 2,
        out_specs=pl.BlockSpec((None, ds, width), bidx2),
        scratch_shapes=[pltpu.VMEM((rows, width), BF16), pltpu.VMEM((rows, width), F32),
                        pltpu.VMEM((rows, 1), F32)],
    )
    return pl.pallas_call(
        functools.partial(_sb_dec_rest_kernel, n_heads, ds, n_rest),
        out_shape=jax.ShapeDtypeStruct((db, ds, width), BF16),
        grid_spec=grid_spec,
        compiler_params=_cparams(("arbitrary",)),
        name="sb_decode_rest",
    )(need, q, carry, o1, *([kct] * n_rest), *([vct] * n_rest))


def _post_kernel(n_act, final, f_chunk, *refs):
    acts = refs[:n_act]
    wo_ref, x_ref, gate1_ref, sh_ref, sc_ref, gate_ref, g_ref, wu_ref, wd_ref = refs[n_act:n_act + 9]
    fg_ref = refs[n_act + 9] if final else None
    o_ref = refs[-1]
    y = None
    off = 0
    for a_ref in acts:
        kdim = a_ref.shape[1]
        part = _dot(a_ref[...], wo_ref[off:off + kdim, :])
        y = part if y is None else y + part
        off += kdim
    x = x_ref[...] + gate1_ref[...] * y
    h = _rms_mod(x, g_ref[...], sc_ref[...], sh_ref[...]).astype(BF16)
    d_ff = wu_ref.shape[1]
    acc = None
    for c in range(d_ff // f_chunk):
        cs = slice(c * f_chunk, (c + 1) * f_chunk)
        a = jnp.maximum(_dot(h, wu_ref[:, cs]), 0.0)
        part = _dot((a * a).astype(BF16), wd_ref[cs, :])
        acc = part if acc is None else acc + part
    xo = x + gate_ref[...] * acc
    if final:
        y = xo * lax.rsqrt(jnp.mean(xo * xo, axis=-1, keepdims=True) + EPS)
        xo = y * fg_ref[...]
    o_ref[...] = xo


def _post_call(prompt, acts, wo, x2, mod_arr, g, layer, wu, wd, final_g, rows_per_batch):
    m, d = x2.shape
    tm = TM_ROWS if prompt else m
    tpb = rows_per_batch // tm if prompt else 1
    row = lambda i: (i, 0)
    const2 = lambda i: (0, 0)
    final = final_g is not None
    resident = lambda w: pl.BlockSpec(w.shape, const2, pipeline_mode=pl.Buffered(1))
    in_specs = [pl.BlockSpec((tm, a.shape[1]), row) for a in acts]
    in_specs += [resident(wo), pl.BlockSpec((tm, d), row)]
    in_specs += [_mod_spec(prompt, tm, d, c, tpb) for c in (2, 3, 4, 5)]
    stacked = lambda w: pl.BlockSpec((None,) + w.shape[1:], lambda i: (layer, 0, 0), pipeline_mode=pl.Buffered(1))
    in_specs += [pl.BlockSpec((1, d), const2), stacked(wu), stacked(wd)]
    args = list(acts) + [wo, x2] + [mod_arr] * 4 + [g, wu, wd]
    if final:
        in_specs.append(pl.BlockSpec((1, d), const2))
        args.append(final_g)
    return pl.pallas_call(
        functools.partial(_post_kernel, len(acts), final, 1024),
        out_shape=jax.ShapeDtypeStruct((m, d), F32),
        grid=(m // tm,), in_specs=in_specs, out_specs=pl.BlockSpec((tm, d), row),
        compiler_params=_cparams(("arbitrary",)),
        name="post",
    )(*args)


def _sb_in_prompt_kernel(x_ref, sh_ref, sc_ref, g_ref, wq_ref, wkt_ref, wvt_ref,
                         q_ref, ktb_ref, vtb_ref, kt_ref, vt_ref):
    h = _rms_mod(x_ref[...], g_ref[...], sc_ref[...], sh_ref[...]).astype(BF16)
    q_ref[...] = (_dot(h, wq_ref[...]) * (DH ** -0.5)).astype(q_ref.dtype)
    kt = _dot_nt(wkt_ref[...], h)
    kt_ref[...] = kt
    ktb_ref[...] = kt.astype(BF16)
    vt = _dot_nt(wvt_ref[...], h)
    vt_ref[...] = vt
    vtb_ref[...] = vt.astype(BF16)


def _sb_in_sample_kernel(x_ref, sh_ref, sc_ref, g_ref, wq_ref, wkt_ref, wvt_ref,
                         q_ref, kb_ref, vb_ref, k32_ref, v32_ref):
    h = _rms_mod(x_ref[...], g_ref[...], sc_ref[...], sh_ref[...]).astype(BF16)
    q_ref[...] = (_dot(h, wq_ref[...]) * (DH ** -0.5)).astype(q_ref.dtype)
    k = _dot_nt(h, wkt_ref[...])
    k32_ref[...] = k
    kb_ref[...] = k.astype(BF16)
    v = _dot_nt(h, wvt_ref[...])
    v32_ref[...] = v
    vb_ref[...] = v.astype(BF16)


def _sb_in_call(prompt, x2, mod_arr, g, wts, rows_per_batch):
    wq, wkt, wvt = wts
    m, d = x2.shape
    nb = m // rows_per_batch
    tm = TM_ROWS if prompt else m
    tpb = rows_per_batch // tm if prompt else 1
    row = lambda i: (i, 0)
    const2 = lambda i: (0, 0)
    in_specs = [pl.BlockSpec((tm, d), row),
                _mod_spec(prompt, tm, d, 0, tpb), _mod_spec(prompt, tm, d, 1, tpb),
                pl.BlockSpec((1, d), const2)] + [pl.BlockSpec(w.shape, const2) for w in wts]
    if prompt:
        tcol = lambda i: (i // tpb, 0, i % tpb)
        out_shape = ([jax.ShapeDtypeStruct((m, d), BF16)]
                     + [jax.ShapeDtypeStruct((nb, d, rows_per_batch), BF16)] * 2
                     + [jax.ShapeDtypeStruct((nb, d, rows_per_batch), F32)] * 2)
        out_specs = [pl.BlockSpec((tm, d), row)] + [pl.BlockSpec((None, d, tm), tcol)] * 4
        body = _sb_in_prompt_kernel
    else:
        out_shape = [jax.ShapeDtypeStruct((m, d), BF16)] * 3 + [jax.ShapeDtypeStruct((m, d), F32)] * 2
        out_specs = [pl.BlockSpec((tm, d), row)] * 5
        body = _sb_in_sample_kernel
    return pl.pallas_call(
        body, out_shape=out_shape, grid=(m // tm,), in_specs=in_specs, out_specs=out_specs,
        compiler_params=_cparams(("arbitrary",)),
        name="sb_in_prompt" if prompt else "sb_in_sample",
    )(x2, mod_arr, mod_arr, g, wq, wkt, wvt)


def _to_state(xt, n_heads):
    nb, _, l = xt.shape
    return xt.reshape(nb, n_heads, DH, l).transpose(0, 3, 1, 2)[None]


def _from_cache(c):
    nb, p, n_heads, dh = c.shape
    return c.transpose(0, 2, 3, 1).reshape(nb, n_heads * dh, p)


def _trunk(prompt, x, mods, past, wts):
    nb, rpb, d = x.shape
    m = nb * rpb
    a_w = H_FOX * DH
    x2 = x.reshape(m, d)
    norm_g, ab_wts, w_out_ab, sb_wts, w_out_sb, w_up, w_down, final_g = wts

    if prompt:
        outs = _ab_in_call(True, x2, mods[0], norm_g[0, 0][None], ab_wts, None, rpb, 0)
        q, ktb, vtb, kt32, vt32, lft, fcp, kaug, pooled, pstate = outs
        a_out = _fox_call(q.reshape(nb, rpb, a_w), ktb, vtb, kaug, fcp)
        st_fox = (_to_state(kt32, H_FOX), _to_state(vt32, H_FOX), lft.transpose(0, 2, 1)[None])
    else:
        cfk, cfv, cflogf, spool, csk, csv = past
        p0 = cfk.shape[1]
        hist = jnp.pad(spool, ((0, 0), (HIST_ROWS - spool.shape[1], 0), (0, 0)))
        outs = _ab_in_call(False, x2, mods[0], norm_g[0, 0][None], ab_wts, hist, rpb, p0)
        q, kb, vb, k32, v32, lft, pooled, pstate = outs
        lfn_t = lft.reshape(H_FOX, nb, rpb).transpose(1, 0, 2)
        a_out = _fox_dec_call(q.reshape(nb, rpb, a_w), kb.reshape(nb, rpb, a_w), vb.reshape(nb, rpb, a_w),
                              _from_cache(cfk), _from_cache(cfv), lfn_t, cflogf.transpose(0, 2, 1))
        st_fox = (k32.reshape(1, nb, rpb, H_FOX, DH), v32.reshape(1, nb, rpb, H_FOX, DH),
                  lfn_t.transpose(0, 2, 1)[None])
    x2 = _post_call(prompt, [a_out.reshape(m, a_w), pooled], w_out_ab, x2, mods[0], norm_g[0, 1][None],
                    0, w_up, w_down, None, rpb)

    n_heads = d // DH
    outs = _sb_in_call(prompt, x2, mods[1], norm_g[1, 0][None], sb_wts, rpb)
    if prompt:
        q1, ktb1, vtb1, skt32, svt32 = outs
        s_out = _sb_call(q1.reshape(nb, rpb, d), ktb1, vtb1)
        st_sb = (_to_state(skt32, n_heads), _to_state(svt32, n_heads))
    else:
        q1, kb1, vb1, sk32, sv32 = outs
        s_out = _sb_dec_call(q1.reshape(nb, rpb, d), kb1.reshape(nb, rpb, d), vb1.reshape(nb, rpb, d),
                             _from_cache(csk), _from_cache(csv))
        st_sb = (sk32.reshape(1, nb, rpb, n_heads, DH), sv32.reshape(1, nb, rpb, n_heads, DH))
    y = _post_call(prompt, [s_out.reshape(m, d)], w_out_sb, x2, mods[1], norm_g[1, 1][None],
                   1, w_up, w_down, final_g[None], rpb)

    states = st_fox + (pstate[:, 1:, :][None],) + st_sb
    return y.reshape(nb, rpb, d), states


def kernel(x_prompt, x_sample, c_prompt, c_sample, cache_fox_k, cache_fox_v, cache_fox_logf, state_pool, cache_sb_k, cache_sb_v, w_ada, b_ada, norm_g, w_in_ab, b_forget, w_pool, pool_scale, w_out_ab, w_in_sb, w_out_sb, w_up, w_down, final_g):
    b, _, d = x_prompt.shape
    db, ds, _ = x_sample.shape
    a_w = H_FOX * DH

    c_all = jnp.concatenate([c_prompt, c_sample, jnp.zeros((ADA_ROWS - b - db, d), F32)], axis=0)
    mod = _ada_call(c_all, w_ada, b_ada)
    mods_p = [mod[i, :b][:, None, :] for i in range(mod.shape[0])]
    mods_s = [jnp.repeat(mod[i, b:b + db], ds, axis=0) for i in range(mod.shape[0])]

    w_in = w_in_ab[0]
    gate_pad = ((0, GATE_ROWS - H_FOX), (0, 0))
    ab_wts = (w_in[:, :a_w].astype(BF16),
              w_in[:, a_w:2 * a_w].T.astype(BF16),
              w_in[:, 2 * a_w:3 * a_w].T.astype(BF16),
              w_in[:, 3 * a_w + H_FOX:].astype(BF16),
              jnp.pad(w_in[:, 3 * a_w:3 * a_w + H_FOX].T, gate_pad).astype(BF16),
              jnp.pad(b_forget[0][:, None], gate_pad),
              w_pool[0].astype(BF16), pool_scale[0][None])
    w_sb = w_in_sb[0]
    sb_wts = (w_sb[:, :d].astype(BF16), w_sb[:, d:2 * d].T.astype(BF16), w_sb[:, 2 * d:].T.astype(BF16))
    wts = (norm_g, ab_wts, w_out_ab[0].astype(BF16), sb_wts, w_out_sb[0].astype(BF16),
           w_up.astype(BF16), w_down.astype(BF16), final_g)

    y_p, st_p = _trunk(True, x_prompt, mods_p, None, wts)
    past = (cache_fox_k[0], cache_fox_v[0], cache_fox_logf[0], state_pool[0],
            cache_sb_k[0], cache_sb_v[0])
    y_s, st_s = _trunk(False, x_sample, mods_s, past, wts)
    fk, fv, fl, pool, sk, sv = st_p
    fk2, fv2, fl2, pool2, sk2, sv2 = st_s
    return (y_p, y_s, fk, fv, fl, pool, sk, sv, fk2, fv2, fl2, pool2, sk2, sv2)
```

```python
import functools

import jax
import jax.numpy as jnp
from jax import lax
from jax.experimental import pallas as pl
from jax.experimental.pallas import tpu as pltpu

F32 = jnp.float32
BF16 = jnp.bfloat16

DH = 64
H_FOX = 8
EPS = 1e-6
POOL_WINDOWS = (2, 4, 8, 16)
POOL_GC = 128
HIST_ROWS = 16
NEG_BIG = -1e30
LOG2E = 1.4426950408889634
LANES = 128
GATE_ROWS = 16
AUG_ROWS = 16
OFF_SLOT = 12
VMEM_LIMIT = 56 * 1024 * 1024

UNDERFLOW_EXP = 88.0
NORM_MARGIN = 1.01

T_ATT = 256
NQ_ATT = 4
T_SB = 256
NQ_SB = 4
FIRST_SB = 2
TK_DEC_FIRST = 256
TK_DEC_REST = 1280
TK_DEC_FOX = 4096
TM_ROWS = 512
ADA_ROWS = 32
ADA_TN = 1536


def _cparams(sem):
    return pltpu.CompilerParams(dimension_semantics=sem, vmem_limit_bytes=VMEM_LIMIT)


def _rms_mod(x, g, sc, sh):
    y = x * lax.rsqrt(jnp.mean(x * x, axis=-1, keepdims=True) + EPS)
    return (y * g) * (1.0 + sc) + sh


def _softplus(z):
    return jnp.maximum(z, 0.0) + jnp.log(1.0 + jnp.exp2(jnp.abs(z) * (-LOG2E)))


def _log_sigmoid(x):
    return jnp.minimum(x, 0.0) - jnp.log(1.0 + jnp.exp(-jnp.abs(x)))


def _split3(x):
    hi = x.astype(BF16)
    r = x - hi.astype(F32)
    mid = r.astype(BF16)
    lo = (r - mid.astype(F32)).astype(BF16)
    return hi, mid, lo


def _dot(a, b):
    return jnp.dot(a, b, preferred_element_type=F32)


def _dot_nt(a, b):
    return lax.dot_general(a, b, (((1,), (1,)), ((), ())), preferred_element_type=F32)


def _dot3(x, m):
    hi, mid, lo = _split3(x)
    return _dot(hi, m) + _dot(mid, m) + _dot(lo, m)


def _iota(shape, dim):
    return lax.broadcasted_iota(jnp.int32, shape, dim)


def _log2(n):
    assert n & (n - 1) == 0, n
    return n.bit_length() - 1


def _ada_kernel(c_ref, w_ref, b_ref, o_ref):
    c = c_ref[...]
    cond = (c / (1.0 + jnp.exp(-c))).astype(BF16)
    o_ref[...] = _dot(cond, w_ref[...].astype(BF16)) + b_ref[...]


def _ada_call(c_all, w_ada, b_ada):
    depth, d, n = w_ada.shape
    return pl.pallas_call(
        _ada_kernel,
        out_shape=jax.ShapeDtypeStruct((depth, ADA_ROWS, n), F32),
        grid=(depth, n // ADA_TN),
        in_specs=[
            pl.BlockSpec((ADA_ROWS, d), lambda i, j: (0, 0)),
            pl.BlockSpec((None, d, ADA_TN), lambda i, j: (i, 0, j)),
            pl.BlockSpec((None, 1, ADA_TN), lambda i, j: (i, 0, j)),
        ],
        out_specs=pl.BlockSpec((None, ADA_ROWS, ADA_TN), lambda i, j: (i, 0, j)),
        compiler_params=_cparams(("arbitrary", "arbitrary")),
        name="ada_mod",
    )(c_all, w_ada, b_ada.reshape(depth, 1, n))


def _mod_spec(prompt, tm, d, chunk, tiles_per_batch):
    if prompt:
        return pl.BlockSpec((None, 1, d), lambda i: (i // tiles_per_batch, 0, chunk))
    return pl.BlockSpec((tm, d), lambda i: (i, chunk))


def _pool_branch(u, ext_ref, pos, ng, rg, wp_ref, ps_ref, pooled_ref, pst_ref):
    tm, a_w = u.shape
    u3 = u.reshape(ng, rg, a_w)
    ext_ref[:, HIST_ROWS:HIST_ROWS + rg, :] = u3
    pst_ref[...] = u3[:, rg - HIST_ROWS:rg, :]
    for g, w in enumerate(POOL_WINDOWS):
        cs = slice(g * POOL_GC, (g + 1) * POOL_GC)
        acc = u3[:, :, cs]
        for kk in range(1, w):
            acc = acc + ext_ref[:, pl.ds(HIST_ROWS - kk, rg), cs]
        cnt = jnp.minimum(pos + 1, w).astype(F32)
        pooled = acc.reshape(tm, POOL_GC) / cnt - u[:, cs]
        pg = _dot(pooled.astype(BF16), wp_ref[g]) * ps_ref[:, cs]
        pooled_ref[:, cs] = pg.astype(pooled_ref.dtype)


def _ab_in_prompt_kernel(tm, tiles_per_batch, x_ref, sh_ref, sc_ref, g_ref, wq_ref, wkt_ref, wvt_ref,
                         wu_ref, wft_ref, bf_ref, wp_ref, ps_ref,
                         q_ref, ktb_ref, vtb_ref, kt_ref, vt_ref, lft_ref, fcp_ref, kaug_ref, pooled_ref, pst_ref,
                         ext_ref, carry_ref):
    h = _rms_mod(x_ref[...], g_ref[...], sc_ref[...], sh_ref[...]).astype(BF16)
    q_ref[...] = (_dot(h, wq_ref[...]) * (DH ** -0.5)).astype(q_ref.dtype)
    kt = _dot_nt(wkt_ref[...], h)
    kt_ref[...] = kt
    ktb_ref[...] = kt.astype(BF16)
    vt = _dot_nt(wvt_ref[...], h)
    vt_ref[...] = vt
    vtb_ref[...] = vt.astype(BF16)

    tile_in_batch = pl.program_id(0) % tiles_per_batch
    first = tile_in_batch == 0
    logf = _log_sigmoid(_dot_nt(wft_ref[...], h) + bf_ref[...])
    lft_ref[...] = logf[0:H_FOX, :]
    tri = (_iota((tm, tm), 0) <= _iota((tm, tm), 1)).astype(BF16)
    fcum = _dot3(logf, tri) + jnp.where(first, 0.0, carry_ref[...])
    carry_ref[...] = fcum[:, tm - 1:tm]
    parts = _split3(fcum)
    prow = _iota((4 * H_FOX, GATE_ROWS), 0)
    pick = jnp.logical_and((prow & 7) < 2, _iota((4 * H_FOX, GATE_ROWS), 1) == (prow >> 3) * 2 + (prow & 7))
    pick = jnp.where(pick, 1.0, 0.0).astype(BF16)
    fcp_ref[...] = (_dot(pick, parts[0]) + _dot(pick, parts[1])) + _dot(pick, parts[2])

    n_aug = (H_FOX // 2) * AUG_ROWS
    slot = _iota((n_aug, GATE_ROWS), 0) & (AUG_ROWS - 1)
    head0 = (_iota((n_aug, GATE_ROWS), 0) >> _log2(AUG_ROWS)) * 2
    col = _iota((n_aug, GATE_ROWS), 1)
    kaug = None
    for part, piece in enumerate(parts):
        sel = jnp.logical_or(jnp.logical_and(slot == 3 + part, col == head0),
                             jnp.logical_and(slot == 9 + part, col == head0 + 1))
        term = _dot(jnp.where(sel, -1.0, 0.0).astype(BF16), piece)
        kaug = term if kaug is None else kaug + term
    slot1 = _iota((n_aug, 1), 0) & (AUG_ROWS - 1)
    ones = jnp.logical_or(slot1 < 3, jnp.logical_and(slot1 >= 6, slot1 < 9))
    kaug_ref[...] = (kaug + jnp.where(ones, 1.0, 0.0)).astype(BF16)

    @pl.when(first)
    def _():
        ext_ref[:, 0:HIST_ROWS, :] = jnp.zeros((1, HIST_ROWS, ext_ref.shape[2]), F32)

    @pl.when(jnp.logical_not(first))
    def _():
        ext_ref[:, 0:HIST_ROWS, :] = ext_ref[:, tm:tm + HIST_ROWS, :]

    pos = tile_in_batch * tm + _iota((tm, 1), 0)
    _pool_branch(_dot(h, wu_ref[...]), ext_ref, pos, 1, tm, wp_ref, ps_ref, pooled_ref, pst_ref)


def _ab_in_sample_kernel(ng, rg, p0, x_ref, sh_ref, sc_ref, g_ref, wq_ref, wkt_ref, wvt_ref,
                         wu_ref, wft_ref, bf_ref, wp_ref, ps_ref, hist_ref,
                         q_ref, kb_ref, vb_ref, k32_ref, v32_ref, lft_ref, pooled_ref, pst_ref, ext_ref):
    tm = ng * rg
    h = _rms_mod(x_ref[...], g_ref[...], sc_ref[...], sh_ref[...]).astype(BF16)
    q_ref[...] = (_dot(h, wq_ref[...]) * (DH ** -0.5)).astype(q_ref.dtype)
    k = _dot_nt(h, wkt_ref[...])
    k32_ref[...] = k
    kb_ref[...] = k.astype(BF16)
    v = _dot_nt(h, wvt_ref[...])
    v32_ref[...] = v
    vb_ref[...] = v.astype(BF16)
    logf = _log_sigmoid(_dot_nt(wft_ref[...], h) + bf_ref[...])
    lft_ref[...] = logf[0:H_FOX, :]
    ext_ref[:, 0:HIST_ROWS, :] = hist_ref[...]
    pos = p0 + (_iota((tm, 1), 0) & (rg - 1))
    _pool_branch(_dot(h, wu_ref[...]), ext_ref, pos, ng, rg, wp_ref, ps_ref, pooled_ref, pst_ref)


def _ab_in_call(prompt, x2, mod_arr, g, wts, hist, rows_per_batch, p0):
    wq, wkt, wvt, wu, wft, bf, wp, ps = wts
    m, d = x2.shape
    a_w = H_FOX * DH
    nb = m // rows_per_batch
    tm = TM_ROWS if prompt else m
    assert rows_per_batch % tm == 0 if prompt else (
        rows_per_batch >= HIST_ROWS and rows_per_batch & (rows_per_batch - 1) == 0), (m, rows_per_batch)
    tpb = rows_per_batch // tm if prompt else 1
    row = lambda i: (i, 0)
    const2 = lambda i: (0, 0)
    in_specs = [pl.BlockSpec((tm, d), row), _mod_spec(prompt, tm, d, 0, tpb), _mod_spec(prompt, tm, d, 1, tpb),
                pl.BlockSpec((1, d), const2)]
    in_specs += [pl.BlockSpec(w.shape, const2) for w in (wq, wkt, wvt, wu, wft, bf)]
    in_specs += [pl.BlockSpec(wp.shape, lambda i: (0, 0, 0)), pl.BlockSpec(ps.shape, const2)]
    args = [x2, mod_arr, mod_arr, g, wq, wkt, wvt, wu, wft, bf, wp, ps]
    if prompt:
        tcol = lambda i: (i // tpb, 0, i % tpb)
        out_shape = [jax.ShapeDtypeStruct((m, a_w), BF16),
                     jax.ShapeDtypeStruct((nb, a_w, rows_per_batch), BF16),
                     jax.ShapeDtypeStruct((nb, a_w, rows_per_batch), BF16),
                     jax.ShapeDtypeStruct((nb, a_w, rows_per_batch), F32),
                     jax.ShapeDtypeStruct((nb, a_w, rows_per_batch), F32),
                     jax.ShapeDtypeStruct((nb, H_FOX, rows_per_batch), F32),
                     jax.ShapeDtypeStruct((nb, 4 * H_FOX, rows_per_batch), F32),
                     jax.ShapeDtypeStruct((nb, (H_FOX // 2) * AUG_ROWS, rows_per_batch), BF16),
                     jax.ShapeDtypeStruct((m, a_w), BF16),
                     jax.ShapeDtypeStruct((nb, HIST_ROWS, a_w), F32)]
        out_specs = ([pl.BlockSpec((tm, a_w), row)] + [pl.BlockSpec((None, a_w, tm), tcol)] * 4
                     + [pl.BlockSpec((None, H_FOX, tm), tcol), pl.BlockSpec((None, 4 * H_FOX, tm), tcol)]
                     + [pl.BlockSpec((None, (H_FOX // 2) * AUG_ROWS, tm), tcol)]
                     + [pl.BlockSpec((tm, a_w), row),
                        pl.BlockSpec((1, HIST_ROWS, a_w), lambda i: (i // tpb, 0, 0))])
        scratch = [pltpu.VMEM((1, HIST_ROWS + tm, a_w), F32), pltpu.VMEM((GATE_ROWS, 1), F32)]
        body = functools.partial(_ab_in_prompt_kernel, tm, tpb)
    else:
        in_specs.append(pl.BlockSpec(hist.shape, lambda i: (0, 0, 0)))
        args.append(hist)
        out_shape = [jax.ShapeDtypeStruct((m, a_w), BF16), jax.ShapeDtypeStruct((m, a_w), BF16),
                     jax.ShapeDtypeStruct((m, a_w), BF16), jax.ShapeDtypeStruct((m, a_w), F32),
                     jax.ShapeDtypeStruct((m, a_w), F32), jax.ShapeDtypeStruct((H_FOX, m), F32),
                     jax.ShapeDtypeStruct((m, a_w), BF16), jax.ShapeDtypeStruct((nb, HIST_ROWS, a_w), F32)]
        out_specs = ([pl.BlockSpec((tm, a_w), row)] * 5 + [pl.BlockSpec((H_FOX, tm), const2)]
                     + [pl.BlockSpec((tm, a_w), row), pl.BlockSpec((nb, HIST_ROWS, a_w), lambda i: (0, 0, 0))])
        scratch = [pltpu.VMEM((nb, HIST_ROWS + rows_per_batch, a_w), F32)]
        body = functools.partial(_ab_in_sample_kernel, nb, rows_per_batch, p0)
    return pl.pallas_call(
        body, out_shape=out_shape, grid=(m // tm,), in_specs=in_specs, out_specs=out_specs,
        scratch_shapes=scratch,
        compiler_params=_cparams(("arbitrary",)),
        name="ab_in_prompt" if prompt else "ab_in_sample",
    )(*args)


def _head_split(q2):
    lane = _iota((1, LANES), 1)
    zero = jnp.zeros_like(q2)
    return jnp.where(lane < DH, q2, zero), jnp.where(lane >= DH, q2, zero)


def _pair_sel():
    return (_iota((8, LANES), 0) == (_iota((8, LANES), 1) >> _log2(DH))).astype(BF16)


def _fox_kernel(q_ref, kt_ref, vt_ref, ka_ref, fr_ref, fall_ref, o_ref,
                m_ref, acc_ref, kn_ref):
    t, nq = T_ATT, NQ_ATT
    qg = pl.program_id(2)
    lane = _iota((1, LANES), 1)

    @pl.when(qg == 0)
    def _():
        chunk = 4 * t

        def body(i, mx):
            ks = pl.multiple_of(i * chunk, chunk)
            kf = kt_ref[:, pl.ds(ks, chunk)].astype(F32)
            n2 = _dot(_pair_sel(), (kf * kf).astype(BF16))
            return jnp.maximum(mx, jnp.max(n2, axis=1, keepdims=True))

        mx = lax.fori_loop(0, kt_ref.shape[1] // chunk, body, jnp.zeros((8, 1), F32))
        kn_ref[...] = jnp.sqrt(mx)

    row_a = _iota((AUG_ROWS, 1), 0)
    row_v = _iota((LANES, 1), 0)
    qcat, bound0 = [], []
    for s in range(nq):
        q2 = q_ref[s * t:(s + 1) * t, :]
        frows = fr_ref[:, s * t:(s + 1) * t]
        eye = (_iota((t, t), 0) == _iota((t, t), 1)).astype(BF16)
        per_head = []
        for j, qh in enumerate(_head_split(q2)):
            hi, mid, lo = (x.astype(F32) for x in _split3(frows[j:j + 1, :]))
            base = 6 * j
            qat = jnp.where(row_a == base, hi, jnp.where(row_a == base + 1, mid, jnp.where(
                row_a == base + 2, lo,
                jnp.where(jnp.logical_and(row_a >= base + 3, row_a < base + 6), 1.0, 0.0))))
            qat = jnp.where(row_a == OFF_SLOT, NEG_BIG, qat)
            qa = _dot_nt(eye, qat.astype(BF16))
            per_head.append(jnp.concatenate([qh, qa.astype(BF16)], axis=1))
        qcat.append(per_head)
        qf = q2.astype(F32)
        qn2 = _dot_nt((qf * qf).astype(BF16), _pair_sel())
        qn = jnp.sqrt(jnp.max(qn2, axis=0, keepdims=True))
        bound0.append([qn[:, j:j + 1] * kn_ref[j:j + 1, :] * NORM_MARGIN + frows[j:j + 1, 0:1] for j in range(2)])
    chains = [(s, j) for s in range(nq) for j in range(2)]
    rep = lambda m: jnp.concatenate([m] * (t // LANES), axis=1)

    def front(n, diag):
        kbs = [qg * nq + s - n for s in range(nq)]
        kss = [pl.multiple_of((kb if diag else jnp.maximum(kb, 0)) * t, t) for kb in kbs]
        causal = _iota((t, t), 1) <= _iota((t, t), 0)
        kcat = []
        for s in range(nq):
            ka = ka_ref[:, pl.ds(kss[s], t)]
            if not diag:
                off = jnp.where(kbs[s] >= 0, 0.0, 1.0).astype(BF16)
                ka = jnp.where(_iota((AUG_ROWS, 1), 0) == OFF_SLOT, off, ka)
            kcat.append(jnp.concatenate([kt_ref[:, pl.ds(kss[s], t)], ka], axis=0))
        sc = []
        for s, j in chains:
            x = _dot(qcat[s][j], kcat[s])
            sc.append(jnp.where(causal, x, NEG_BIG) if diag else x)
        vcat = []
        for s, j in chains:
            v2 = vt_ref[:, pl.ds(kss[s], t)]
            mine = (row_v < DH) if j == 0 else (row_v >= DH)
            vcat.append(jnp.where(mine, v2, jnp.ones_like(v2)))
        return sc, vcat

    (sc0, v0), (sc1, v1) = front(0, True), front(1, False)
    m01 = [jnp.broadcast_to(jnp.maximum(jnp.max(x0, axis=1, keepdims=True), jnp.max(x1, axis=1, keepdims=True)),
                            (t, LANES)) for x0, x1 in zip(sc0, sc1)]
    pv0 = [_dot_nt(jnp.exp(x - rep(m)).astype(BF16), vv) for x, m, vv in zip(sc0, m01, v0)]
    pv1 = [_dot_nt(jnp.exp(x - rep(m)).astype(BF16), vv) for x, m, vv in zip(sc1, m01, v1)]
    for i, (s, j) in enumerate(chains):
        acc_ref[2 * s + j] = pv0[i] + pv1[i]
        m_ref[2 * s + j] = m01[i]

    def front_wide(n, chains):
        kcat, vts = [], []
        for s in range(nq):
            kb = qg * nq + s - n
            start = pl.multiple_of(jnp.maximum(kb - 1, 0) * t, t)
            off = jnp.where(start + _iota((1, 2 * t), 1) >= (kb + 1) * t, 1.0, 0.0).astype(BF16)
            ka = jnp.where(_iota((AUG_ROWS, 1), 0) == OFF_SLOT, off, ka_ref[:, pl.ds(start, 2 * t)])
            kcat.append(jnp.concatenate([kt_ref[:, pl.ds(start, 2 * t)], ka], axis=0))
            vts.append(vt_ref[:, pl.ds(start, 2 * t)])
        sc = [_dot(qcat[s][j], kcat[s]) for s, j in chains]
        vcat = [jnp.where((row_v < DH) if j == 0 else (row_v >= DH), vts[s], jnp.ones_like(vts[s]))
                for s, j in chains]
        return sc, vcat

    def step(n, chains):
        sc, vcat = front_wide(n, chains)
        m_prev = [m_ref[2 * s + j] for s, j in chains]
        m_new = [jnp.maximum(mp, jnp.max(x, axis=1, keepdims=True)) for mp, x in zip(m_prev, sc)]
        pv = [_dot_nt(jnp.exp(x - jnp.concatenate([rep(mn)] * 2, axis=1)).astype(BF16), vv)
              for x, mn, vv in zip(sc, m_new, vcat)]
        for i, (s, j) in enumerate(chains):
            idx = 2 * s + j
            acc_ref[idx] = jnp.exp(m_prev[i] - m_new[i]) * acc_ref[idx] + pv[i]
            m_ref[idx] = m_new[i]

    def more(n):
        go = [False, False]
        for s in range(nq):
            kb = qg * nq + s - n
            last = pl.multiple_of(jnp.maximum(kb, 0) * t + (t - LANES), LANES)
            ft = fall_ref[:, pl.ds(last, LANES)][:, LANES - 1:LANES]
            for j in range(2):
                top = jnp.max(bound0[s][j] - ft[j:j + 1, :]) - jnp.min(m_ref[2 * s + j])
                go[j] = jnp.logical_or(go[j], jnp.logical_and(kb >= 0, top > -UNDERFLOW_EXP))
        return go

    def body(chains, c):
        n = c[0]
        step(n, chains)
        return (n + 2, *more(n + 2))

    c = lax.while_loop(lambda c: jnp.logical_and(c[1], c[2]), functools.partial(body, chains),
                       (jnp.int32(2), *more(2)))
    c = lax.while_loop(lambda c: c[1], functools.partial(body, [(s, 0) for s in range(nq)]), c)
    lax.while_loop(lambda c: c[2], functools.partial(body, [(s, 1) for s in range(nq)]), c)
    for s in range(nq):
        a0, a1 = acc_ref[2 * s], acc_ref[2 * s + 1]
        o = jnp.where(lane < DH, a0 / pltpu.roll(a0, DH, 1), a1 / pltpu.roll(a1, DH, 1))
        o_ref[s * t:(s + 1) * t, :] = o.astype(o_ref.dtype)


def _fox_call(q, ktb, vtb, kaug, fcp):
    b, l, a_w = q.shape
    hp = a_w // LANES
    t, nq = T_ATT, NQ_ATT
    tq = t * nq
    assert l % tq == 0 and l >= 2 * t and a_w % LANES == 0, (l, a_w)
    return pl.pallas_call(
        _fox_kernel,
        out_shape=jax.ShapeDtypeStruct((b, l, a_w), BF16),
        grid=(b, hp, l // tq),
        in_specs=[
            pl.BlockSpec((None, tq, LANES), lambda bi, h, qi: (bi, qi, h)),
            pl.BlockSpec((None, LANES, l), lambda bi, h, qi: (bi, h, 0)),
            pl.BlockSpec((None, LANES, l), lambda bi, h, qi: (bi, h, 0)),
            pl.BlockSpec((None, AUG_ROWS, l), lambda bi, h, qi: (bi, h, 0)),
            pl.BlockSpec((None, 8, tq), lambda bi, h, qi: (bi, h, qi)),
            pl.BlockSpec((None, 8, l), lambda bi, h, qi: (bi, h, 0)),
        ],
        out_specs=pl.BlockSpec((None, tq, LANES), lambda bi, h, qi: (bi, qi, h)),
        scratch_shapes=[pltpu.VMEM((2 * nq, t, LANES), F32), pltpu.VMEM((2 * nq, t, LANES), F32),
                        pltpu.VMEM((8, 1), F32)],
        compiler_params=_cparams(("arbitrary", "arbitrary", "arbitrary")),
        name="fox_prompt",
    )(q, ktb, vtb, kaug, fcp, fcp)


def _sb_kernel(q_ref, kt_ref, vt_ref, o_ref, carry_ref, acc_ref):
    t, nq = T_SB, NQ_SB
    qg = pl.program_id(2)
    lane = _iota((1, LANES), 1)
    qh = [_head_split(q_ref[s * t:(s + 1) * t, :]) for s in range(nq)]
    valid = _iota((t, t), 1) < _iota((t, t), 0)

    def logits(s, n, diag):
        kb = qg * nq + s - n
        ks = pl.multiple_of((kb if diag else jnp.maximum(kb, 0)) * t, t)
        return kb, ks, [_dot(qh[s][j], kt_ref[:, pl.ds(ks, t)]) for j in range(2)]

    def suffix(diag, z):
        tri = (_iota((t, t), 0) > _iota((t, t), 1)).astype(BF16)
        sp = [_softplus(zz) for zz in z]
        spm = [jnp.where(valid, x, 0.0) for x in sp] if diag else sp
        later = [_dot(x.astype(BF16), tri) for x in spm]
        return [zz - x for zz, x in zip(z, sp)], spm, later

    def weigh(diag, kb, ks, ls, spm, later, carry):
        v2 = vt_ref[:, pl.ds(ks, t)]
        if not diag:
            v2 = jnp.where(kb >= 0, v2, jnp.zeros_like(v2))
        pv, new_carry = [], []
        for j in range(2):
            e = ls[j] - later[j]
            a = jnp.exp(e if carry is None else e - carry[j])
            if diag:
                a = jnp.where(valid, a, 0.0)
            pv.append(_dot_nt(a.astype(BF16), v2))
            rs = jnp.sum(spm[j], axis=1, keepdims=True)
            new_carry.append(rs if carry is None else carry[j] + rs)
        return pv, new_carry

    def run(units, carry, acc):
        st1, st2 = {}, {}
        for k in range(len(units) + 2):
            if k < len(units):
                st1[k] = logits(*units[k])
            if 0 <= k - 1 < len(units):
                st2[k - 1] = suffix(units[k - 1][2], st1[k - 1][2])
            if 0 <= k - 2 < len(units):
                s, _, diag = units[k - 2]
                kb, ks, _ = st1.pop(k - 2)
                pv, carry[s] = weigh(diag, kb, ks, *st2.pop(k - 2), carry[s])
                acc[s] = pv if acc[s] is None else [x + y for x, y in zip(acc[s], pv)]
        return carry, acc

    carry, acc = run([(s, n, n == 0) for n in range(FIRST_SB) for s in range(nq)], [None] * nq, [None] * nq)
    for s in range(nq):
        for j in range(2):
            acc_ref[2 * s + j] = acc[s][j]
            carry_ref[2 * s + j] = carry[s][j]

    def more(n):
        in_range = n <= qg * nq + nq - 1
        return jnp.logical_and(in_range, jnp.min(carry_ref[...]) < UNDERFLOW_EXP)

    def body(c):
        n, _ = c
        carry, acc = run([(s, n, False) for s in range(nq)],
                         [[carry_ref[2 * s], carry_ref[2 * s + 1]] for s in range(nq)], [None] * nq)
        for s in range(nq):
            for j in range(2):
                acc_ref[2 * s + j] = acc_ref[2 * s + j] + acc[s][j]
                carry_ref[2 * s + j] = carry[s][j]
        return n + 1, more(n + 1)

    lax.while_loop(lambda c: c[1], body, (jnp.int32(FIRST_SB), more(FIRST_SB)))
    for s in range(nq):
        o = jnp.where(lane < DH, acc_ref[2 * s], acc_ref[2 * s + 1])
        o_ref[s * t:(s + 1) * t, :] = o.astype(o_ref.dtype)


def _sb_call(q, ktb, vtb):
    b, l, d = q.shape
    hp = d // LANES
    t, nq = T_SB, NQ_SB
    tq = t * nq
    assert l % tq == 0 and l >= FIRST_SB * t and d % LANES == 0, (l, d)
    return pl.pallas_call(
        _sb_kernel,
        out_shape=jax.ShapeDtypeStruct((b, l, d), BF16),
        grid=(b, hp, l // tq),
        in_specs=[
            pl.BlockSpec((None, tq, LANES), lambda bi, h, qi: (bi, qi, h)),
            pl.BlockSpec((None, LANES, l), lambda bi, h, qi: (bi, h, 0)),
            pl.BlockSpec((None, LANES, l), lambda bi, h, qi: (bi, h, 0)),
        ],
        out_specs=pl.BlockSpec((None, tq, LANES), lambda bi, h, qi: (bi, qi, h)),
        scratch_shapes=[pltpu.VMEM((2 * nq, t, 1), F32), pltpu.VMEM((2 * nq, t, LANES), F32)],
        compiler_params=_cparams(("arbitrary", "arbitrary", "arbitrary")),
        name="sb_prompt",
    )(q, ktb, vtb)


def _rep_matrix(rows, n_heads, ds):
    return ((_iota((rows, ds), 0) >> _log2(n_heads)) == _iota((rows, ds), 1)).astype(BF16)


def _head_mask(rows, n_heads, width):
    return (_iota((rows, width), 0) & (n_heads - 1)) == (_iota((rows, width), 1) >> _log2(DH))


def _block_diag_q(q, n_heads):
    ds, width = q.shape
    rows = ds * n_heads
    qrep = _dot(_rep_matrix(rows, n_heads, ds), q)
    return jnp.where(_head_mask(rows, n_heads, width), qrep, 0.0).astype(BF16)


def _collect_heads(acc, n_heads, ds):
    rows, width = acc.shape
    om = jnp.where(_head_mask(rows, n_heads, width), acc, 0.0).astype(BF16)
    rep_t = ((_iota((ds, rows), 1) >> _log2(n_heads)) == _iota((ds, rows), 0)).astype(BF16)
    return _dot(rep_t, om)


def _suffix_sums(x, exact=True, blk=2 * LANES):
    r, n = x.shape
    parts = [x[:, b * blk:(b + 1) * blk] for b in range(n // blk)]
    tri = (_iota((blk, blk), 0) > _iota((blk, blk), 1)).astype(BF16)
    stacked = jnp.concatenate(parts, axis=0)
    w = _dot3(stacked, tri) if exact else _dot(stacked.astype(BF16), tri)
    out, tail = [], None
    for b in reversed(range(len(parts))):
        cur = w[b * r:(b + 1) * r]
        out.append(cur if tail is None else cur + tail)
        tot = jnp.sum(parts[b], axis=1, keepdims=True)
        tail = tot if tail is None else tail + tot
    return jnp.concatenate(out[::-1], axis=1), tail


def _fox_dec_kernel(n_heads, ds, q_ref, kn_ref, vn_ref, lfn_ref, kc_ref, vc_ref, lfc_ref,
                    o_ref, qbd_ref, m_ref, l_ref, acc_ref, cf_ref):
    rows = ds * n_heads
    step_i = pl.program_id(1)
    n_steps = pl.num_programs(1)

    def update(s, pv_fn):
        m_prev = m_ref[...]
        m_new = jnp.maximum(m_prev, jnp.max(s, axis=1, keepdims=True))
        alpha = jnp.exp(m_prev - m_new)
        p = jnp.exp(s - m_new)
        l_ref[...] = alpha * l_ref[...] + jnp.sum(p, axis=1, keepdims=True)
        acc_ref[...] = alpha * acc_ref[...] + pv_fn(p.astype(BF16))
        m_ref[...] = m_new

    @pl.when(step_i == 0)
    def _():
        qbd_ref[...] = _block_diag_q(q_ref[...], n_heads)
        m_ref[...] = jnp.full(m_ref.shape, NEG_BIG, F32)
        l_ref[...] = jnp.zeros(l_ref.shape, F32)
        acc_ref[...] = jnp.zeros(acc_ref.shape, F32)
        a = jnp.concatenate([lfn_ref[...]] * ds, axis=0)
        ri = _iota((rows, ds), 0) >> _log2(n_heads)
        cj = _iota((rows, ds), 1)
        am = jnp.where(cj <= ri, a, 0.0)
        tri = (_iota((ds, ds), 0) > _iota((ds, ds), 1)).astype(BF16)
        s = _dot_nt(qbd_ref[...], kn_ref[...]) + _dot3(am, tri)
        s = jnp.where(cj <= ri, s, NEG_BIG)
        update(s, lambda p: _dot(p, vn_ref[...]))
        cf_ref[...] = jnp.sum(am, axis=1, keepdims=True)

    @pl.when(step_i > 0)
    def _():
        reps = rows // (2 * n_heads)
        later, total = _suffix_sums(jnp.concatenate([lfc_ref[...]] * 2, axis=0))
        cf = cf_ref[...]
        s = _dot(qbd_ref[...], kc_ref[...].astype(BF16)) + (jnp.concatenate([later] * reps, axis=0) + cf)
        update(s, lambda p: _dot_nt(p, vc_ref[...].astype(BF16)))
        cf_ref[...] = cf + jnp.concatenate([total] * reps, axis=0)

    @pl.when(step_i == n_steps - 1)
    def _():
        o_ref[...] = _collect_heads(acc_ref[...] / l_ref[...], n_heads, ds).astype(o_ref.dtype)


def _fox_dec_call(q, kn, vn, kct, vct, lfn_t, lfc_t):
    db, ds, width = q.shape
    n_heads = width // DH
    rows = ds * n_heads
    tk = min(TK_DEC_FOX, kct.shape[2])
    assert kct.shape[2] % tk == 0 and tk % (2 * LANES) == 0 and rows % (2 * n_heads) == 0, (kct.shape, rows)
    nkb = kct.shape[2] // tk
    cache_blk = lambda s: jnp.minimum(nkb - s, nkb - 1)
    new_idx = lambda b, s: (b, 0, 0)
    cache_idx = lambda b, s: (b, 0, cache_blk(s))
    in_specs = ([pl.BlockSpec((None, ds, width), new_idx)] * 3
                + [pl.BlockSpec((None, n_heads, ds), new_idx)]
                + [pl.BlockSpec((None, width, tk), cache_idx)] * 2
                + [pl.BlockSpec((None, n_heads, tk), cache_idx)])
    scratch = [pltpu.VMEM((rows, width), BF16), pltpu.VMEM((rows, 1), F32), pltpu.VMEM((rows, 1), F32),
               pltpu.VMEM((rows, width), F32), pltpu.VMEM((rows, 1), F32)]
    return pl.pallas_call(
        functools.partial(_fox_dec_kernel, n_heads, ds),
        out_shape=jax.ShapeDtypeStruct((db, ds, width), BF16),
        grid=(db, nkb + 1),
        in_specs=in_specs,
        out_specs=pl.BlockSpec((None, ds, width), new_idx),
        scratch_shapes=scratch,
        compiler_params=_cparams(("arbitrary", "arbitrary")),
        name="fox_decode",
    )(q, kn, vn, lfn_t, kct, vct, lfc_t)


def _sb_dec_tile(qbd, kc_ref, vc_ref, carry):
    z = _dot(qbd, kc_ref[...].astype(BF16))
    sp = _softplus(z)
    later, total = _suffix_sums(sp, exact=False)
    a = jnp.exp((z - sp) - later - carry)
    return _dot_nt(a.astype(BF16), vc_ref[...].astype(BF16)), carry + total


def _sb_dec_first_kernel(n_heads, ds, q_ref, kn_ref, vn_ref, kc_ref, vc_ref,
                         o_ref, carry_out_ref, cmin_ref):
    rows = ds * n_heads
    qbd = _block_diag_q(q_ref[...], n_heads)
    ri = _iota((rows, ds), 0) >> _log2(n_heads)
    cj = _iota((rows, ds), 1)
    valid = cj < ri
    z = _dot_nt(qbd, kn_ref[...])
    sp = _softplus(z)
    spm = jnp.where(valid, sp, 0.0)
    tri = (_iota((ds, ds), 0) > _iota((ds, ds), 1)).astype(BF16)
    later = _dot(spm.astype(BF16), tri)
    a = jnp.where(valid, jnp.exp((z - sp) - later), 0.0)
    acc = _dot(a.astype(BF16), vn_ref[...])
    carry = jnp.sum(spm, axis=1, keepdims=True)
    part, carry = _sb_dec_tile(qbd, kc_ref, vc_ref, carry)
    o_ref[...] = _collect_heads(acc + part, n_heads, ds).astype(o_ref.dtype)
    carry_out_ref[...] = carry
    cmin_ref[...] = jnp.broadcast_to(jnp.min(carry, axis=0, keepdims=True), cmin_ref.shape)


def _sb_dec_rest_kernel(n_heads, ds, n_rest, need_ref, q_ref, carry_in_ref, o1_ref, *refs):
    kc_refs, vc_refs = refs[:n_rest], refs[n_rest:2 * n_rest]
    o_ref, qbd_ref, acc_ref, carry_ref = refs[2 * n_rest:]
    needed = need_ref[pl.program_id(0)] != 0

    @pl.when(needed)
    def _():
        qbd_ref[...] = _block_diag_q(q_ref[...], n_heads)
        acc_ref[...] = jnp.zeros(acc_ref.shape, F32)
        carry_ref[...] = carry_in_ref[...]
        for i in reversed(range(n_rest)):
            @pl.when(jnp.min(carry_ref[...]) < UNDERFLOW_EXP)
            def _(i=i):
                part, carry = _sb_dec_tile(qbd_ref[...], kc_refs[i], vc_refs[i], carry_ref[...])
                acc_ref[...] = acc_ref[...] + part
                carry_ref[...] = carry
        o = o1_ref[...].astype(F32) + _collect_heads(acc_ref[...], n_heads, ds)
        o_ref[...] = o.astype(o_ref.dtype)

    @pl.when(jnp.logical_not(needed))
    def _():
        o_ref[...] = o1_ref[...]


def _sb_dec_call(q, kn, vn, kct, vct):
    db, ds, width = q.shape
    n_heads = width // DH
    rows = ds * n_heads
    past = kct.shape[2]
    first = min(TK_DEC_FIRST, past)
    assert past % first == 0 and first % (2 * LANES) == 0, past
    n_blk = (past - first) // (2 * LANES)
    per = max([k for k in range(1, TK_DEC_REST // (2 * LANES) + 1) if n_blk % k == 0], default=1)
    rest = per * 2 * LANES
    n_rest = (past - first) // rest
    bidx = lambda b: (b, 0, 0)
    o1, carry, cmin = pl.pallas_call(
        functools.partial(_sb_dec_first_kernel, n_heads, ds),
        out_shape=[jax.ShapeDtypeStruct((db, ds, width), BF16),
                   jax.ShapeDtypeStruct((db, rows, 1), F32),
                   jax.ShapeDtypeStruct((db, 1, LANES), F32)],
        grid=(db,),
        in_specs=[pl.BlockSpec((None, ds, width), bidx)] * 3
        + [pl.BlockSpec((None, width, first), lambda b: (b, 0, past // first - 1))] * 2,
        out_specs=[pl.BlockSpec((None, ds, width), bidx), pl.BlockSpec((None, rows, 1), bidx),
                   pl.BlockSpec((None, 1, LANES), bidx)],
        compiler_params=_cparams(("arbitrary",)),
        name="sb_decode_first",
    )(q, kn, vn, kct, vct)
    if n_rest == 0:
        return o1
    need = (cmin[:, 0, 0] < UNDERFLOW_EXP).astype(jnp.int32)
    bidx2 = lambda b, need_ref: (b, 0, 0)

    def tile_spec(i):
        def idx(b, need_ref):
            on = need_ref[b] != 0
            return (jnp.where(on, b, 0), 0, jnp.where(on, i, 0))
        return pl.BlockSpec((None, width, rest), idx, pipeline_mode=pl.Buffered(1))

    grid_spec = pltpu.PrefetchScalarGridSpec(
        num_scalar_prefetch=1,
        grid=(db,),
        in_specs=[pl.BlockSpec((None, ds, width), bidx2), pl.BlockSpec((None, rows, 1), bidx2),
                  pl.BlockSpec((None, ds, width), bidx2)] + [tile_spec(i) for i in range(n_rest)] * 2,
        out_specs=pl.BlockSpec((None, ds, width), bidx2),
        scratch_shapes=[pltpu.VMEM((rows, width), BF16), pltpu.VMEM((rows, width), F32),
                        pltpu.VMEM((rows, 1), F32)],
    )
    return pl.pallas_call(
        functools.partial(_sb_dec_rest_kernel, n_heads, ds, n_rest),
        out_shape=jax.ShapeDtypeStruct((db, ds, width), BF16),
        grid_spec=grid_spec,
        compiler_params=_cparams(("arbitrary",)),
        name="sb_decode_rest",
    )(need, q, carry, o1, *([kct] * n_rest), *([vct] * n_rest))


def _post_kernel(n_act, final, f_chunk, *refs):
    acts = refs[:n_act]
    wo_ref, x_ref, gate1_ref, sh_ref, sc_ref, gate_ref, g_ref, wu_ref, wd_ref = refs[n_act:n_act + 9]
    fg_ref = refs[n_act + 9] if final else None
    o_ref = refs[-1]
    y = None
    off = 0
    for a_ref in acts:
        kdim = a_ref.shape[1]
        part = _dot(a_ref[...], wo_ref[off:off + kdim, :])
        y = part if y is None else y + part
        off += kdim
    x = x_ref[...] + gate1_ref[...] * y
    h = _rms_mod(x, g_ref[...], sc_ref[...], sh_ref[...]).astype(BF16)
    d_ff = wu_ref.shape[1]
    acc = None
    for c in range(d_ff // f_chunk):
        cs = slice(c * f_chunk, (c + 1) * f_chunk)
        a = jnp.maximum(_dot(h, wu_ref[:, cs]), 0.0)
        part = _dot((a * a).astype(BF16), wd_ref[cs, :])
        acc = part if acc is None else acc + part
    xo = x + gate_ref[...] * acc
    if final:
        y = xo * lax.rsqrt(jnp.mean(xo * xo, axis=-1, keepdims=True) + EPS)
        xo = y * fg_ref[...]
    o_ref[...] = xo


def _post_call(prompt, acts, wo, x2, mod_arr, g, layer, wu, wd, final_g, rows_per_batch):
    m, d = x2.shape
    tm = TM_ROWS if prompt else m
    tpb = rows_per_batch // tm if prompt else 1
    row = lambda i: (i, 0)
    const2 = lambda i: (0, 0)
    final = final_g is not None
    resident = lambda w: pl.BlockSpec(w.shape, const2, pipeline_mode=pl.Buffered(1))
    in_specs = [pl.BlockSpec((tm, a.shape[1]), row) for a in acts]
    in_specs += [resident(wo), pl.BlockSpec((tm, d), row)]
    in_specs += [_mod_spec(prompt, tm, d, c, tpb) for c in (2, 3, 4, 5)]
    stacked = lambda w: pl.BlockSpec((None,) + w.shape[1:], lambda i: (layer, 0, 0), pipeline_mode=pl.Buffered(1))
    in_specs += [pl.BlockSpec((1, d), const2), stacked(wu), stacked(wd)]
    args = list(acts) + [wo, x2] + [mod_arr] * 4 + [g, wu, wd]
    if final:
        in_specs.append(pl.BlockSpec((1, d), const2))
        args.append(final_g)
    return pl.pallas_call(
        functools.partial(_post_kernel, len(acts), final, 1024),
        out_shape=jax.ShapeDtypeStruct((m, d), F32),
        grid=(m // tm,), in_specs=in_specs, out_specs=pl.BlockSpec((tm, d), row),
        compiler_params=_cparams(("arbitrary",)),
        name="post",
    )(*args)


def _sb_in_prompt_kernel(x_ref, sh_ref, sc_ref, g_ref, wq_ref, wkt_ref, wvt_ref,
                         q_ref, ktb_ref, vtb_ref, kt_ref, vt_ref):
    h = _rms_mod(x_ref[...], g_ref[...], sc_ref[...], sh_ref[...]).astype(BF16)
    q_ref[...] = (_dot(h, wq_ref[...]) * (DH ** -0.5)).astype(q_ref.dtype)
    kt = _dot_nt(wkt_ref[...], h)
    kt_ref[...] = kt
    ktb_ref[...] = kt.astype(BF16)
    vt = _dot_nt(wvt_ref[...], h)
    vt_ref[...] = vt
    vtb_ref[...] = vt.astype(BF16)


def _sb_in_sample_kernel(x_ref, sh_ref, sc_ref, g_ref, wq_ref, wkt_ref, wvt_ref,
                         q_ref, kb_ref, vb_ref, k32_ref, v32_ref):
    h = _rms_mod(x_ref[...], g_ref[...], sc_ref[...], sh_ref[...]).astype(BF16)
    q_ref[...] = (_dot(h, wq_ref[...]) * (DH ** -0.5)).astype(q_ref.dtype)
    k = _dot_nt(h, wkt_ref[...])
    k32_ref[...] = k
    kb_ref[...] = k.astype(BF16)
    v = _dot_nt(h, wvt_ref[...])
    v32_ref[...] = v
    vb_ref[...] = v.astype(BF16)


def _sb_in_call(prompt, x2, mod_arr, g, wts, rows_per_batch):
    wq, wkt, wvt = wts
    m, d = x2.shape
    nb = m // rows_per_batch
    tm = TM_ROWS if prompt else m
    tpb = rows_per_batch // tm if prompt else 1
    row = lambda i: (i, 0)
    const2 = lambda i: (0, 0)
    in_specs = [pl.BlockSpec((tm, d), row),
                _mod_spec(prompt, tm, d, 0, tpb), _mod_spec(prompt, tm, d, 1, tpb),
                pl.BlockSpec((1, d), const2)] + [pl.BlockSpec(w.shape, const2) for w in wts]
    if prompt:
        tcol = lambda i: (i // tpb, 0, i % tpb)
        out_shape = ([jax.ShapeDtypeStruct((m, d), BF16)]
                     + [jax.ShapeDtypeStruct((nb, d, rows_per_batch), BF16)] * 2
                     + [jax.ShapeDtypeStruct((nb, d, rows_per_batch), F32)] * 2)
        out_specs = [pl.BlockSpec((tm, d), row)] + [pl.BlockSpec((None, d, tm), tcol)] * 4
        body = _sb_in_prompt_kernel
    else:
        out_shape = [jax.ShapeDtypeStruct((m, d), BF16)] * 3 + [jax.ShapeDtypeStruct((m, d), F32)] * 2
        out_specs = [pl.BlockSpec((tm, d), row)] * 5
        body = _sb_in_sample_kernel
    return pl.pallas_call(
        body, out_shape=out_shape, grid=(m // tm,), in_specs=in_specs, out_specs=out_specs,
        compiler_params=_cparams(("arbitrary",)),
        name="sb_in_prompt" if prompt else "sb_in_sample",
    )(x2, mod_arr, mod_arr, g, wq, wkt, wvt)


def _to_state(xt, n_heads):
    nb, _, l = xt.shape
    return xt.reshape(nb, n_heads, DH, l).transpose(0, 3, 1, 2)[None]


def _from_cache(c):
    nb, p, n_heads, dh = c.shape
    return c.transpose(0, 2, 3, 1).reshape(nb, n_heads * dh, p)


def _trunk(prompt, x, mods, past, wts):
    nb, rpb, d = x.shape
    m = nb * rpb
    a_w = H_FOX * DH
    x2 = x.reshape(m, d)
    norm_g, ab_wts, w_out_ab, sb_wts, w_out_sb, w_up, w_down, final_g = wts

    if prompt:
        outs = _ab_in_call(True, x2, mods[0], norm_g[0, 0][None], ab_wts, None, rpb, 0)
        q, ktb, vtb, kt32, vt32, lft, fcp, kaug, pooled, pstate = outs
        a_out = _fox_call(q.reshape(nb, rpb, a_w), ktb, vtb, kaug, fcp)
        st_fox = (_to_state(kt32, H_FOX), _to_state(vt32, H_FOX), lft.transpose(0, 2, 1)[None])
    else:
        cfk, cfv, cflogf, spool, csk, csv = past
        p0 = cfk.shape[1]
        hist = jnp.pad(spool, ((0, 0), (HIST_ROWS - spool.shape[1], 0), (0, 0)))
        outs = _ab_in_call(False, x2, mods[0], norm_g[0, 0][None], ab_wts, hist, rpb, p0)
        q, kb, vb, k32, v32, lft, pooled, pstate = outs
        lfn_t = lft.reshape(H_FOX, nb, rpb).transpose(1, 0, 2)
        a_out = _fox_dec_call(q.reshape(nb, rpb, a_w), kb.reshape(nb, rpb, a_w), vb.reshape(nb, rpb, a_w),
                              _from_cache(cfk), _from_cache(cfv), lfn_t, cflogf.transpose(0, 2, 1))
        st_fox = (k32.reshape(1, nb, rpb, H_FOX, DH), v32.reshape(1, nb, rpb, H_FOX, DH),
                  lfn_t.transpose(0, 2, 1)[None])
    x2 = _post_call(prompt, [a_out.reshape(m, a_w), pooled], w_out_ab, x2, mods[0], norm_g[0, 1][None],
                    0, w_up, w_down, None, rpb)

    n_heads = d // DH
    outs = _sb_in_call(prompt, x2, mods[1], norm_g[1, 0][None], sb_wts, rpb)
    if prompt:
        q1, ktb1, vtb1, skt32, svt32 = outs
        s_out = _sb_call(q1.reshape(nb, rpb, d), ktb1, vtb1)
        st_sb = (_to_state(skt32, n_heads), _to_state(svt32, n_heads))
    else:
        q1, kb1, vb1, sk32, sv32 = outs
        s_out = _sb_dec_call(q1.reshape(nb, rpb, d), kb1.reshape(nb, rpb, d), vb1.reshape(nb, rpb, d),
                             _from_cache(csk), _from_cache(csv))
        st_sb = (sk32.reshape(1, nb, rpb, n_heads, DH), sv32.reshape(1, nb, rpb, n_heads, DH))
    y = _post_call(prompt, [s_out.reshape(m, d)], w_out_sb, x2, mods[1], norm_g[1, 1][None],
                   1, w_up, w_down, final_g[None], rpb)

    states = st_fox + (pstate[:, 1:, :][None],) + st_sb
    return y.reshape(nb, rpb, d), states


def kernel(x_prompt, x_sample, c_prompt, c_sample, cache_fox_k, cache_fox_v, cache_fox_logf, state_pool, cache_sb_k, cache_sb_v, w_ada, b_ada, norm_g, w_in_ab, b_forget, w_pool, pool_scale, w_out_ab, w_in_sb, w_out_sb, w_up, w_down, final_g):
    b, _, d = x_prompt.shape
    db, ds, _ = x_sample.shape
    a_w = H_FOX * DH

    c_all = jnp.concatenate([c_prompt, c_sample, jnp.zeros((ADA_ROWS - b - db, d), F32)], axis=0)
    mod = _ada_call(c_all, w_ada, b_ada)
    mods_p = [mod[i, :b][:, None, :] for i in range(mod.shape[0])]
    mods_s = [jnp.repeat(mod[i, b:b + db], ds, axis=0) for i in range(mod.shape[0])]

    w_in = w_in_ab[0]
    gate_pad = ((0, GATE_ROWS - H_FOX), (0, 0))
    ab_wts = (w_in[:, :a_w].astype(BF16),
              w_in[:, a_w:2 * a_w].T.astype(BF16),
              w_in[:, 2 * a_w:3 * a_w].T.astype(BF16),
              w_in[:, 3 * a_w + H_FOX:].astype(BF16),
              jnp.pad(w_in[:, 3 * a_w:3 * a_w + H_FOX].T, gate_pad).astype(BF16),
              jnp.pad(b_forget[0][:, None], gate_pad),
              w_pool[0].astype(BF16), pool_scale[0][None])
    w_sb = w_in_sb[0]
    sb_wts = (w_sb[:, :d].astype(BF16), w_sb[:, d:2 * d].T.astype(BF16), w_sb[:, 2 * d:].T.astype(BF16))
    wts = (norm_g, ab_wts, w_out_ab[0].astype(BF16), sb_wts, w_out_sb[0].astype(BF16),
           w_up.astype(BF16), w_down.astype(BF16), final_g)

    y_p, st_p = _trunk(True, x_prompt, mods_p, None, wts)
    past = (cache_fox_k[0], cache_fox_v[0], cache_fox_logf[0], state_pool[0],
            cache_sb_k[0], cache_sb_v[0])
    y_s, st_s = _trunk(False, x_sample, mods_s, past, wts)
    fk, fv, fl, pool, sk, sv = st_p
    fk2, fv2, fl2, pool2, sk2, sv2 = st_s
    return (y_p, y_s, fk, fv, fl, pool, sk, sv, fk2, fv2, fl2, pool2, sk2, sv2)
```

```python
import functools

import jax
import jax.numpy as jnp
from jax import lax
from jax.experimental import pallas as pl
from jax.experimental.pallas import tpu as pltpu

F32 = jnp.float32
BF16 = jnp.bfloat16

DH = 64
H_FOX = 8
EPS = 1e-6
POOL_WINDOWS = (2, 4, 8, 16)
POOL_GC = 128
HIST_ROWS = 16
NEG_BIG = -1e30
LOG2E = 1.4426950408889634
LANES = 128
GATE_ROWS = 16
AUG_ROWS = 16
OFF_SLOT = 12
VMEM_LIMIT = 56 * 1024 * 1024

UNDERFLOW_EXP = 88.0
NORM_MARGIN = 1.01

T_ATT = 256
NQ_ATT = 4
T_SB = 256
NQ_SB = 4
FIRST_SB = 2
TK_DEC_FIRST = 256
TK_DEC_REST = 1280
TK_DEC_FOX = 4096
TM_ROWS = 512
TM_ROWS_IN = 1024
ADA_ROWS = 32
ADA_TN = 1536


def _cparams(sem):
    return pltpu.CompilerParams(dimension_semantics=sem, vmem_limit_bytes=VMEM_LIMIT)


def _rms_mod(x, g, sc, sh):
    y = x * lax.rsqrt(jnp.mean(x * x, axis=-1, keepdims=True) + EPS)
    return (y * g) * (1.0 + sc) + sh


def _softplus(z):
    return jnp.maximum(z, 0.0) + jnp.log(1.0 + jnp.exp2(jnp.abs(z) * (-LOG2E)))


def _log_sigmoid(x):
    return jnp.minimum(x, 0.0) - jnp.log(1.0 + jnp.exp(-jnp.abs(x)))


def _split3(x):
    hi = x.astype(BF16)
    r = x - hi.astype(F32)
    mid = r.astype(BF16)
    lo = (r - mid.astype(F32)).astype(BF16)
    return hi, mid, lo


def _dot(a, b):
    return jnp.dot(a, b, preferred_element_type=F32)


def _dot_nt(a, b):
    return lax.dot_general(a, b, (((1,), (1,)), ((), ())), preferred_element_type=F32)


def _dot3(x, m):
    hi, mid, lo = _split3(x)
    return _dot(hi, m) + _dot(mid, m) + _dot(lo, m)


def _iota(shape, dim):
    return lax.broadcasted_iota(jnp.int32, shape, dim)


def _log2(n):
    assert n & (n - 1) == 0, n
    return n.bit_length() - 1


def _ada_kernel(c_ref, w_ref, b_ref, o_ref):
    c = c_ref[...]
    cond = (c / (1.0 + jnp.exp(-c))).astype(BF16)
    o_ref[...] = _dot(cond, w_ref[...].astype(BF16)) + b_ref[...]


def _ada_call(c_all, w_ada, b_ada):
    depth, d, n = w_ada.shape
    return pl.pallas_call(
        _ada_kernel,
        out_shape=jax.ShapeDtypeStruct((depth, ADA_ROWS, n), F32),
        grid=(depth, n // ADA_TN),
        in_specs=[
            pl.BlockSpec((ADA_ROWS, d), lambda i, j: (0, 0)),
            pl.BlockSpec((None, d, ADA_TN), lambda i, j: (i, 0, j)),
            pl.BlockSpec((None, 1, ADA_TN), lambda i, j: (i, 0, j)),
        ],
        out_specs=pl.BlockSpec((None, ADA_ROWS, ADA_TN), lambda i, j: (i, 0, j)),
        compiler_params=_cparams(("arbitrary", "arbitrary")),
        name="ada_mod",
    )(c_all, w_ada, b_ada.reshape(depth, 1, n))


def _mod_spec(prompt, tm, d, chunk, tiles_per_batch):
    if prompt:
        return pl.BlockSpec((None, 1, d), lambda i: (i // tiles_per_batch, 0, chunk))
    return pl.BlockSpec((tm, d), lambda i: (i, chunk))


def _pool_branch(u, ext_ref, pos, ng, rg, wp_ref, ps_ref, pooled_ref, pst_ref):
    tm, a_w = u.shape
    u3 = u.reshape(ng, rg, a_w)
    ext_ref[:, HIST_ROWS:HIST_ROWS + rg, :] = u3
    pst_ref[...] = u3[:, rg - HIST_ROWS:rg, :]
    for g, w in enumerate(POOL_WINDOWS):
        cs = slice(g * POOL_GC, (g + 1) * POOL_GC)
        acc = u3[:, :, cs]
        for kk in range(1, w):
            acc = acc + ext_ref[:, pl.ds(HIST_ROWS - kk, rg), cs]
        cnt = jnp.minimum(pos + 1, w).astype(F32)
        pooled = acc.reshape(tm, POOL_GC) / cnt - u[:, cs]
        pg = _dot(pooled.astype(BF16), wp_ref[g]) * ps_ref[:, cs]
        pooled_ref[:, cs] = pg.astype(pooled_ref.dtype)


def _ab_in_prompt_kernel(tm, tiles_per_batch, x_ref, sh_ref, sc_ref, g_ref, wq_ref, wkt_ref, wvt_ref,
                         wu_ref, wft_ref, bf_ref, wp_ref, ps_ref,
                         q_ref, ktb_ref, vtb_ref, kt_ref, vt_ref, lft_ref, fcp_ref, kaug_ref, pooled_ref, pst_ref,
                         ext_ref, carry_ref):
    h = _rms_mod(x_ref[...], g_ref[...], sc_ref[...], sh_ref[...]).astype(BF16)
    q_ref[...] = (_dot(h, wq_ref[...]) * (DH ** -0.5)).astype(q_ref.dtype)
    kt = _dot_nt(wkt_ref[...], h)
    kt_ref[...] = kt
    ktb_ref[...] = kt.astype(BF16)
    vt = _dot_nt(wvt_ref[...], h)
    vt_ref[...] = vt
    vtb_ref[...] = vt.astype(BF16)

    tile_in_batch = pl.program_id(0) % tiles_per_batch
    first = tile_in_batch == 0
    logf = _log_sigmoid(_dot_nt(wft_ref[...], h) + bf_ref[...])
    lft_ref[...] = logf[0:H_FOX, :]
    tri = (_iota((tm, tm), 0) <= _iota((tm, tm), 1)).astype(BF16)
    fcum = _dot3(logf, tri) + jnp.where(first, 0.0, carry_ref[...])
    carry_ref[...] = fcum[:, tm - 1:tm]
    parts = _split3(fcum)
    prow = _iota((4 * H_FOX, GATE_ROWS), 0)
    pick = jnp.logical_and((prow & 7) < 2, _iota((4 * H_FOX, GATE_ROWS), 1) == (prow >> 3) * 2 + (prow & 7))
    pick = jnp.where(pick, 1.0, 0.0).astype(BF16)
    fcp_ref[...] = (_dot(pick, parts[0]) + _dot(pick, parts[1])) + _dot(pick, parts[2])

    n_aug = (H_FOX // 2) * AUG_ROWS
    slot = _iota((n_aug, GATE_ROWS), 0) & (AUG_ROWS - 1)
    head0 = (_iota((n_aug, GATE_ROWS), 0) >> _log2(AUG_ROWS)) * 2
    col = _iota((n_aug, GATE_ROWS), 1)
    kaug = None
    for part, piece in enumerate(parts):
        sel = jnp.logical_or(jnp.logical_and(slot == 3 + part, col == head0),
                             jnp.logical_and(slot == 9 + part, col == head0 + 1))
        term = _dot(jnp.where(sel, -1.0, 0.0).astype(BF16), piece)
        kaug = term if kaug is None else kaug + term
    slot1 = _iota((n_aug, 1), 0) & (AUG_ROWS - 1)
    ones = jnp.logical_or(slot1 < 3, jnp.logical_and(slot1 >= 6, slot1 < 9))
    kaug_ref[...] = (kaug + jnp.where(ones, 1.0, 0.0)).astype(BF16)

    @pl.when(first)
    def _():
        ext_ref[:, 0:HIST_ROWS, :] = jnp.zeros((1, HIST_ROWS, ext_ref.shape[2]), F32)

    @pl.when(jnp.logical_not(first))
    def _():
        ext_ref[:, 0:HIST_ROWS, :] = ext_ref[:, tm:tm + HIST_ROWS, :]

    pos = tile_in_batch * tm + _iota((tm, 1), 0)
    _pool_branch(_dot(h, wu_ref[...]), ext_ref, pos, 1, tm, wp_ref, ps_ref, pooled_ref, pst_ref)


def _ab_in_sample_kernel(ng, rg, p0, x_ref, sh_ref, sc_ref, g_ref, wq_ref, wkt_ref, wvt_ref,
                         wu_ref, wft_ref, bf_ref, wp_ref, ps_ref, hist_ref,
                         q_ref, kb_ref, vb_ref, k32_ref, v32_ref, lft_ref, pooled_ref, pst_ref, ext_ref):
    tm = ng * rg
    h = _rms_mod(x_ref[...], g_ref[...], sc_ref[...], sh_ref[...]).astype(BF16)
    q_ref[...] = (_dot(h, wq_ref[...]) * (DH ** -0.5)).astype(q_ref.dtype)
    k = _dot_nt(h, wkt_ref[...])
    k32_ref[...] = k
    kb_ref[...] = k.astype(BF16)
    v = _dot_nt(h, wvt_ref[...])
    v32_ref[...] = v
    vb_ref[...] = v.astype(BF16)
    logf = _log_sigmoid(_dot_nt(wft_ref[...], h) + bf_ref[...])
    lft_ref[...] = logf[0:H_FOX, :]
    ext_ref[:, 0:HIST_ROWS, :] = hist_ref[...]
    pos = p0 + (_iota((tm, 1), 0) & (rg - 1))
    _pool_branch(_dot(h, wu_ref[...]), ext_ref, pos, ng, rg, wp_ref, ps_ref, pooled_ref, pst_ref)


def _ab_in_call(prompt, x2, mod_arr, g, wts, hist, rows_per_batch, p0):
    wq, wkt, wvt, wu, wft, bf, wp, ps = wts
    m, d = x2.shape
    a_w = H_FOX * DH
    nb = m // rows_per_batch
    tm = TM_ROWS_IN if prompt else m
    assert rows_per_batch % tm == 0 if prompt else (
        rows_per_batch >= HIST_ROWS and rows_per_batch & (rows_per_batch - 1) == 0), (m, rows_per_batch)
    tpb = rows_per_batch // tm if prompt else 1
    row = lambda i: (i, 0)
    const2 = lambda i: (0, 0)
    in_specs = [pl.BlockSpec((tm, d), row), _mod_spec(prompt, tm, d, 0, tpb), _mod_spec(prompt, tm, d, 1, tpb),
                pl.BlockSpec((1, d), const2)]
    in_specs += [pl.BlockSpec(w.shape, const2) for w in (wq, wkt, wvt, wu, wft, bf)]
    in_specs += [pl.BlockSpec(wp.shape, lambda i: (0, 0, 0)), pl.BlockSpec(ps.shape, const2)]
    args = [x2, mod_arr, mod_arr, g, wq, wkt, wvt, wu, wft, bf, wp, ps]
    if prompt:
        tcol = lambda i: (i // tpb, 0, i % tpb)
        out_shape = [jax.ShapeDtypeStruct((m, a_w), BF16),
                     jax.ShapeDtypeStruct((nb, a_w, rows_per_batch), BF16),
                     jax.ShapeDtypeStruct((nb, a_w, rows_per_batch), BF16),
                     jax.ShapeDtypeStruct((nb, a_w, rows_per_batch), F32),
                     jax.ShapeDtypeStruct((nb, a_w, rows_per_batch), F32),
                     jax.ShapeDtypeStruct((nb, H_FOX, rows_per_batch), F32),
                     jax.ShapeDtypeStruct((nb, 4 * H_FOX, rows_per_batch), F32),
                     jax.ShapeDtypeStruct((nb, (H_FOX // 2) * AUG_ROWS, rows_per_batch), BF16),
                     jax.ShapeDtypeStruct((m, a_w), BF16),
                     jax.ShapeDtypeStruct((nb, HIST_ROWS, a_w), F32)]
        out_specs = ([pl.BlockSpec((tm, a_w), row)] + [pl.BlockSpec((None, a_w, tm), tcol)] * 4
                     + [pl.BlockSpec((None, H_FOX, tm), tcol), pl.BlockSpec((None, 4 * H_FOX, tm), tcol)]
                     + [pl.BlockSpec((None, (H_FOX // 2) * AUG_ROWS, tm), tcol)]
                     + [pl.BlockSpec((tm, a_w), row),
                        pl.BlockSpec((1, HIST_ROWS, a_w), lambda i: (i // tpb, 0, 0))])
        scratch = [pltpu.VMEM((1, HIST_ROWS + tm, a_w), F32), pltpu.VMEM((GATE_ROWS, 1), F32)]
        body = functools.partial(_ab_in_prompt_kernel, tm, tpb)
    else:
        in_specs.append(pl.BlockSpec(hist.shape, lambda i: (0, 0, 0)))
        args.append(hist)
        out_shape = [jax.ShapeDtypeStruct((m, a_w), BF16), jax.ShapeDtypeStruct((m, a_w), BF16),
                     jax.ShapeDtypeStruct((m, a_w), BF16), jax.ShapeDtypeStruct((m, a_w), F32),
                     jax.ShapeDtypeStruct((m, a_w), F32), jax.ShapeDtypeStruct((H_FOX, m), F32),
                     jax.ShapeDtypeStruct((m, a_w), BF16), jax.ShapeDtypeStruct((nb, HIST_ROWS, a_w), F32)]
        out_specs = ([pl.BlockSpec((tm, a_w), row)] * 5 + [pl.BlockSpec((H_FOX, tm), const2)]
                     + [pl.BlockSpec((tm, a_w), row), pl.BlockSpec((nb, HIST_ROWS, a_w), lambda i: (0, 0, 0))])
        scratch = [pltpu.VMEM((nb, HIST_ROWS + rows_per_batch, a_w), F32)]
        body = functools.partial(_ab_in_sample_kernel, nb, rows_per_batch, p0)
    return pl.pallas_call(
        body, out_shape=out_shape, grid=(m // tm,), in_specs=in_specs, out_specs=out_specs,
        scratch_shapes=scratch,
        compiler_params=_cparams(("arbitrary",)),
        name="ab_in_prompt" if prompt else "ab_in_sample",
    )(*args)


def _head_split(q2):
    lane = _iota((1, LANES), 1)
    zero = jnp.zeros_like(q2)
    return jnp.where(lane < DH, q2, zero), jnp.where(lane >= DH, q2, zero)


def _pair_sel():
    return (_iota((8, LANES), 0) == (_iota((8, LANES), 1) >> _log2(DH))).astype(BF16)


def _fox_kernel(q_ref, kt_ref, vt_ref, ka_ref, fr_ref, fall_ref, o_ref,
                m_ref, acc_ref, kn_ref):
    t, nq = T_ATT, NQ_ATT
    qg = pl.program_id(2)
    lane = _iota((1, LANES), 1)

    @pl.when(qg == 0)
    def _():
        chunk = 4 * t

        def body(i, mx):
            ks = pl.multiple_of(i * chunk, chunk)
            kf = kt_ref[:, pl.ds(ks, chunk)].astype(F32)
            n2 = _dot(_pair_sel(), (kf * kf).astype(BF16))
            return jnp.maximum(mx, jnp.max(n2, axis=1, keepdims=True))

        mx = lax.fori_loop(0, kt_ref.shape[1] // chunk, body, jnp.zeros((8, 1), F32))
        kn_ref[...] = jnp.sqrt(mx)

    row_a = _iota((AUG_ROWS, 1), 0)
    row_v = _iota((LANES, 1), 0)
    qcat, bound0 = [], []
    for s in range(nq):
        q2 = q_ref[s * t:(s + 1) * t, :]
        frows = fr_ref[:, s * t:(s + 1) * t]
        eye = (_iota((t, t), 0) == _iota((t, t), 1)).astype(BF16)
        per_head = []
        for j, qh in enumerate(_head_split(q2)):
            hi, mid, lo = (x.astype(F32) for x in _split3(frows[j:j + 1, :]))
            base = 6 * j
            qat = jnp.where(row_a == base, hi, jnp.where(row_a == base + 1, mid, jnp.where(
                row_a == base + 2, lo,
                jnp.where(jnp.logical_and(row_a >= base + 3, row_a < base + 6), 1.0, 0.0))))
            qat = jnp.where(row_a == OFF_SLOT, NEG_BIG, qat)
            qa = _dot_nt(eye, qat.astype(BF16))
            per_head.append(jnp.concatenate([qh, qa.astype(BF16)], axis=1))
        qcat.append(per_head)
        qf = q2.astype(F32)
        qn2 = _dot_nt((qf * qf).astype(BF16), _pair_sel())
        qn = jnp.sqrt(jnp.max(qn2, axis=0, keepdims=True))
        bound0.append([qn[:, j:j + 1] * kn_ref[j:j + 1, :] * NORM_MARGIN + frows[j:j + 1, 0:1] for j in range(2)])
    chains = [(s, j) for s in range(nq) for j in range(2)]
    rep = lambda m: jnp.concatenate([m] * (t // LANES), axis=1)

    def front(n, diag):
        kbs = [qg * nq + s - n for s in range(nq)]
        kss = [pl.multiple_of((kb if diag else jnp.maximum(kb, 0)) * t, t) for kb in kbs]
        causal = _iota((t, t), 1) <= _iota((t, t), 0)
        kcat = []
        for s in range(nq):
            ka = ka_ref[:, pl.ds(kss[s], t)]
            if not diag:
                off = jnp.where(kbs[s] >= 0, 0.0, 1.0).astype(BF16)
                ka = jnp.where(_iota((AUG_ROWS, 1), 0) == OFF_SLOT, off, ka)
            kcat.append(jnp.concatenate([kt_ref[:, pl.ds(kss[s], t)], ka], axis=0))
        sc = []
        for s, j in chains:
            x = _dot(qcat[s][j], kcat[s])
            sc.append(jnp.where(causal, x, NEG_BIG) if diag else x)
        vcat = []
        for s, j in chains:
            v2 = vt_ref[:, pl.ds(kss[s], t)]
            mine = (row_v < DH) if j == 0 else (row_v >= DH)
            vcat.append(jnp.where(mine, v2, jnp.ones_like(v2)))
        return sc, vcat

    (sc0, v0), (sc1, v1) = front(0, True), front(1, False)
    m01 = [jnp.broadcast_to(jnp.maximum(jnp.max(x0, axis=1, keepdims=True), jnp.max(x1, axis=1, keepdims=True)),
                            (t, LANES)) for x0, x1 in zip(sc0, sc1)]
    pv0 = [_dot_nt(jnp.exp(x - rep(m)).astype(BF16), vv) for x, m, vv in zip(sc0, m01, v0)]
    pv1 = [_dot_nt(jnp.exp(x - rep(m)).astype(BF16), vv) for x, m, vv in zip(sc1, m01, v1)]
    for i, (s, j) in enumerate(chains):
        acc_ref[2 * s + j] = pv0[i] + pv1[i]
        m_ref[2 * s + j] = m01[i]

    def front_wide(n, chains):
        kcat, vts = [], []
        for s in range(nq):
            kb = qg * nq + s - n
            start = pl.multiple_of(jnp.maximum(kb - 1, 0) * t, t)
            off = jnp.where(start + _iota((1, 2 * t), 1) >= (kb + 1) * t, 1.0, 0.0).astype(BF16)
            ka = jnp.where(_iota((AUG_ROWS, 1), 0) == OFF_SLOT, off, ka_ref[:, pl.ds(start, 2 * t)])
            kcat.append(jnp.concatenate([kt_ref[:, pl.ds(start, 2 * t)], ka], axis=0))
            vts.append(vt_ref[:, pl.ds(start, 2 * t)])
        sc = [_dot(qcat[s][j], kcat[s]) for s, j in chains]
        vcat = [jnp.where((row_v < DH) if j == 0 else (row_v >= DH), vts[s], jnp.ones_like(vts[s]))
                for s, j in chains]
        return sc, vcat

    def step(n, chains):
        sc, vcat = front_wide(n, chains)
        m_prev = [m_ref[2 * s + j] for s, j in chains]
        m_new = [jnp.maximum(mp, jnp.max(x, axis=1, keepdims=True)) for mp, x in zip(m_prev, sc)]
        pv = [_dot_nt(jnp.exp(x - jnp.concatenate([rep(mn)] * 2, axis=1)).astype(BF16), vv)
              for x, mn, vv in zip(sc, m_new, vcat)]
        for i, (s, j) in enumerate(chains):
            idx = 2 * s + j
            acc_ref[idx] = jnp.exp(m_prev[i] - m_new[i]) * acc_ref[idx] + pv[i]
            m_ref[idx] = m_new[i]

    def more(n):
        go = [False, False]
        for s in range(nq):
            kb = qg * nq + s - n
            last = pl.multiple_of(jnp.maximum(kb, 0) * t + (t - LANES), LANES)
            ft = fall_ref[:, pl.ds(last, LANES)][:, LANES - 1:LANES]
            for j in range(2):
                top = jnp.max(bound0[s][j] - ft[j:j + 1, :]) - jnp.min(m_ref[2 * s + j])
                go[j] = jnp.logical_or(go[j], jnp.logical_and(kb >= 0, top > -UNDERFLOW_EXP))
        return go

    def body(chains, c):
        n = c[0]
        step(n, chains)
        return (n + 2, *more(n + 2))

    c = lax.while_loop(lambda c: jnp.logical_and(c[1], c[2]), functools.partial(body, chains),
                       (jnp.int32(2), *more(2)))
    c = lax.while_loop(lambda c: c[1], functools.partial(body, [(s, 0) for s in range(nq)]), c)
    lax.while_loop(lambda c: c[2], functools.partial(body, [(s, 1) for s in range(nq)]), c)
    for s in range(nq):
        a0, a1 = acc_ref[2 * s], acc_ref[2 * s + 1]
        o = jnp.where(lane < DH, a0 / pltpu.roll(a0, DH, 1), a1 / pltpu.roll(a1, DH, 1))
        o_ref[s * t:(s + 1) * t, :] = o.astype(o_ref.dtype)


def _fox_call(q, ktb, vtb, kaug, fcp):
    b, l, a_w = q.shape
    hp = a_w // LANES
    t, nq = T_ATT, NQ_ATT
    tq = t * nq
    assert l % tq == 0 and l >= 2 * t and a_w % LANES == 0, (l, a_w)
    return pl.pallas_call(
        _fox_kernel,
        out_shape=jax.ShapeDtypeStruct((b, l, a_w), BF16),
        grid=(b, hp, l // tq),
        in_specs=[
            pl.BlockSpec((None, tq, LANES), lambda bi, h, qi: (bi, qi, h)),
            pl.BlockSpec((None, LANES, l), lambda bi, h, qi: (bi, h, 0)),
            pl.BlockSpec((None, LANES, l), lambda bi, h, qi: (bi, h, 0)),
            pl.BlockSpec((None, AUG_ROWS, l), lambda bi, h, qi: (bi, h, 0)),
            pl.BlockSpec((None, 8, tq), lambda bi, h, qi: (bi, h, qi)),
            pl.BlockSpec((None, 8, l), lambda bi, h, qi: (bi, h, 0)),
        ],
        out_specs=pl.BlockSpec((None, tq, LANES), lambda bi, h, qi: (bi, qi, h)),
        scratch_shapes=[pltpu.VMEM((2 * nq, t, LANES), F32), pltpu.VMEM((2 * nq, t, LANES), F32),
                        pltpu.VMEM((8, 1), F32)],
        compiler_params=_cparams(("arbitrary", "arbitrary", "arbitrary")),
        name="fox_prompt",
    )(q, ktb, vtb, kaug, fcp, fcp)


def _sb_kernel(q_ref, kt_ref, vt_ref, o_ref, carry_ref, acc_ref):
    t, nq = T_SB, NQ_SB
    qg = pl.program_id(2)
    lane = _iota((1, LANES), 1)
    qh = [_head_split(q_ref[s * t:(s + 1) * t, :]) for s in range(nq)]
    valid = _iota((t, t), 1) < _iota((t, t), 0)

    def logits(s, n, diag):
        kb = qg * nq + s - n
        ks = pl.multiple_of((kb if diag else jnp.maximum(kb, 0)) * t, t)
        return kb, ks, [_dot(qh[s][j], kt_ref[:, pl.ds(ks, t)]) for j in range(2)]

    def suffix(diag, z):
        tri = (_iota((t, t), 0) > _iota((t, t), 1)).astype(BF16)
        sp = [_softplus(zz) for zz in z]
        spm = [jnp.where(valid, x, 0.0) for x in sp] if diag else sp
        later = [_dot(x.astype(BF16), tri) for x in spm]
        return [zz - x for zz, x in zip(z, sp)], spm, later

    def weigh(diag, kb, ks, ls, spm, later, carry):
        v2 = vt_ref[:, pl.ds(ks, t)]
        if not diag:
            v2 = jnp.where(kb >= 0, v2, jnp.zeros_like(v2))
        pv, new_carry = [], []
        for j in range(2):
            e = ls[j] - later[j]
            a = jnp.exp(e if carry is None else e - carry[j])
            if diag:
                a = jnp.where(valid, a, 0.0)
            pv.append(_dot_nt(a.astype(BF16), v2))
            rs = jnp.sum(spm[j], axis=1, keepdims=True)
            new_carry.append(rs if carry is None else carry[j] + rs)
        return pv, new_carry

    def run(units, carry, acc):
        st1, st2 = {}, {}
        for k in range(len(units) + 2):
            if k < len(units):
                st1[k] = logits(*units[k])
            if 0 <= k - 1 < len(units):
                st2[k - 1] = suffix(units[k - 1][2], st1[k - 1][2])
            if 0 <= k - 2 < len(units):
                s, _, diag = units[k - 2]
                kb, ks, _ = st1.pop(k - 2)
                pv, carry[s] = weigh(diag, kb, ks, *st2.pop(k - 2), carry[s])
                acc[s] = pv if acc[s] is None else [x + y for x, y in zip(acc[s], pv)]
        return carry, acc

    carry, acc = run([(s, n, n == 0) for n in range(FIRST_SB) for s in range(nq)], [None] * nq, [None] * nq)
    for s in range(nq):
        for j in range(2):
            acc_ref[2 * s + j] = acc[s][j]
            carry_ref[2 * s + j] = carry[s][j]

    def more(n):
        in_range = n <= qg * nq + nq - 1
        return jnp.logical_and(in_range, jnp.min(carry_ref[...]) < UNDERFLOW_EXP)

    def body(c):
        n, _ = c
        carry, acc = run([(s, n, False) for s in range(nq)],
                         [[carry_ref[2 * s], carry_ref[2 * s + 1]] for s in range(nq)], [None] * nq)
        for s in range(nq):
            for j in range(2):
                acc_ref[2 * s + j] = acc_ref[2 * s + j] + acc[s][j]
                carry_ref[2 * s + j] = carry[s][j]
        return n + 1, more(n + 1)

    lax.while_loop(lambda c: c[1], body, (jnp.int32(FIRST_SB), more(FIRST_SB)))
    for s in range(nq):
        o = jnp.where(lane < DH, acc_ref[2 * s], acc_ref[2 * s + 1])
        o_ref[s * t:(s + 1) * t, :] = o.astype(o_ref.dtype)


def _sb_call(q, ktb, vtb):
    b, l, d = q.shape
    hp = d // LANES
    t, nq = T_SB, NQ_SB
    tq = t * nq
    assert l % tq == 0 and l >= FIRST_SB * t and d % LANES == 0, (l, d)
    return pl.pallas_call(
        _sb_kernel,
        out_shape=jax.ShapeDtypeStruct((b, l, d), BF16),
        grid=(b, hp, l // tq),
        in_specs=[
            pl.BlockSpec((None, tq, LANES), lambda bi, h, qi: (bi, qi, h)),
            pl.BlockSpec((None, LANES, l), lambda bi, h, qi: (bi, h, 0)),
            pl.BlockSpec((None, LANES, l), lambda bi, h, qi: (bi, h, 0)),
        ],
        out_specs=pl.BlockSpec((None, tq, LANES), lambda bi, h, qi: (bi, qi, h)),
        scratch_shapes=[pltpu.VMEM((2 * nq, t, 1), F32), pltpu.VMEM((2 * nq, t, LANES), F32)],
        compiler_params=_cparams(("arbitrary", "arbitrary", "arbitrary")),
        name="sb_prompt",
    )(q, ktb, vtb)


def _rep_matrix(rows, n_heads, ds):
    return ((_iota((rows, ds), 0) >> _log2(n_heads)) == _iota((rows, ds), 1)).astype(BF16)


def _head_mask(rows, n_heads, width):
    return (_iota((rows, width), 0) & (n_heads - 1)) == (_iota((rows, width), 1) >> _log2(DH))


def _block_diag_q(q, n_heads):
    ds, width = q.shape
    rows = ds * n_heads
    qrep = _dot(_rep_matrix(rows, n_heads, ds), q)
    return jnp.where(_head_mask(rows, n_heads, width), qrep, 0.0).astype(BF16)


def _collect_heads(acc, n_heads, ds):
    rows, width = acc.shape
    om = jnp.where(_head_mask(rows, n_heads, width), acc, 0.0).astype(BF16)
    rep_t = ((_iota((ds, rows), 1) >> _log2(n_heads)) == _iota((ds, rows), 0)).astype(BF16)
    return _dot(rep_t, om)


def _suffix_sums(x, exact=True, blk=2 * LANES):
    r, n = x.shape
    parts = [x[:, b * blk:(b + 1) * blk] for b in range(n // blk)]
    tri = (_iota((blk, blk), 0) > _iota((blk, blk), 1)).astype(BF16)
    stacked = jnp.concatenate(parts, axis=0)
    w = _dot3(stacked, tri) if exact else _dot(stacked.astype(BF16), tri)
    out, tail = [], None
    for b in reversed(range(len(parts))):
        cur = w[b * r:(b + 1) * r]
        out.append(cur if tail is None else cur + tail)
        tot = jnp.sum(parts[b], axis=1, keepdims=True)
        tail = tot if tail is None else tail + tot
    return jnp.concatenate(out[::-1], axis=1), tail


def _fox_dec_kernel(n_heads, ds, q_ref, kn_ref, vn_ref, lfn_ref, kc_ref, vc_ref, lfc_ref,
                    o_ref, qbd_ref, m_ref, l_ref, acc_ref, cf_ref):
    rows = ds * n_heads
    step_i = pl.program_id(1)
    n_steps = pl.num_programs(1)

    def update(s, pv_fn):
        m_prev = m_ref[...]
        m_new = jnp.maximum(m_prev, jnp.max(s, axis=1, keepdims=True))
        alpha = jnp.exp(m_prev - m_new)
        p = jnp.exp(s - m_new)
        l_ref[...] = alpha * l_ref[...] + jnp.sum(p, axis=1, keepdims=True)
        acc_ref[...] = alpha * acc_ref[...] + pv_fn(p.astype(BF16))
        m_ref[...] = m_new

    @pl.when(step_i == 0)
    def _():
        qbd_ref[...] = _block_diag_q(q_ref[...], n_heads)
        m_ref[...] = jnp.full(m_ref.shape, NEG_BIG, F32)
        l_ref[...] = jnp.zeros(l_ref.shape, F32)
        acc_ref[...] = jnp.zeros(acc_ref.shape, F32)
        a = jnp.concatenate([lfn_ref[...]] * ds, axis=0)
        ri = _iota((rows, ds), 0) >> _log2(n_heads)
        cj = _iota((rows, ds), 1)
        am = jnp.where(cj <= ri, a, 0.0)
        tri = (_iota((ds, ds), 0) > _iota((ds, ds), 1)).astype(BF16)
        s = _dot_nt(qbd_ref[...], kn_ref[...]) + _dot3(am, tri)
        s = jnp.where(cj <= ri, s, NEG_BIG)
        update(s, lambda p: _dot(p, vn_ref[...]))
        cf_ref[...] = jnp.sum(am, axis=1, keepdims=True)

    @pl.when(step_i > 0)
    def _():
        reps = rows // (2 * n_heads)
        later, total = _suffix_sums(jnp.concatenate([lfc_ref[...]] * 2, axis=0))
        cf = cf_ref[...]
        s = _dot(qbd_ref[...], kc_ref[...].astype(BF16)) + (jnp.concatenate([later] * reps, axis=0) + cf)
        update(s, lambda p: _dot_nt(p, vc_ref[...].astype(BF16)))
        cf_ref[...] = cf + jnp.concatenate([total] * reps, axis=0)

    @pl.when(step_i == n_steps - 1)
    def _():
        o_ref[...] = _collect_heads(acc_ref[...] / l_ref[...], n_heads, ds).astype(o_ref.dtype)


def _fox_dec_call(q, kn, vn, kct, vct, lfn_t, lfc_t):
    db, ds, width = q.shape
    n_heads = width // DH
    rows = ds * n_heads
    tk = min(TK_DEC_FOX, kct.shape[2])
    assert kct.shape[2] % tk == 0 and tk % (2 * LANES) == 0 and rows % (2 * n_heads) == 0, (kct.shape, rows)
    nkb = kct.shape[2] // tk
    cache_blk = lambda s: jnp.minimum(nkb - s, nkb - 1)
    new_idx = lambda b, s: (b, 0, 0)
    cache_idx = lambda b, s: (b, 0, cache_blk(s))
    in_specs = ([pl.BlockSpec((None, ds, width), new_idx)] * 3
                + [pl.BlockSpec((None, n_heads, ds), new_idx)]
                + [pl.BlockSpec((None, width, tk), cache_idx)] * 2
                + [pl.BlockSpec((None, n_heads, tk), cache_idx)])
    scratch = [pltpu.VMEM((rows, width), BF16), pltpu.VMEM((rows, 1), F32), pltpu.VMEM((rows, 1), F32),
               pltpu.VMEM((rows, width), F32), pltpu.VMEM((rows, 1), F32)]
    return pl.pallas_call(
        functools.partial(_fox_dec_kernel, n_heads, ds),
        out_shape=jax.ShapeDtypeStruct((db, ds, width), BF16),
        grid=(db, nkb + 1),
        in_specs=in_specs,
        out_specs=pl.BlockSpec((None, ds, width), new_idx),
        scratch_shapes=scratch,
        compiler_params=_cparams(("arbitrary", "arbitrary")),
        name="fox_decode",
    )(q, kn, vn, lfn_t, kct, vct, lfc_t)


def _sb_dec_tile(qbd, kc_ref, vc_ref, carry):
    z = _dot(qbd, kc_ref[...].astype(BF16))
    sp = _softplus(z)
    later, total = _suffix_sums(sp, exact=False)
    a = jnp.exp((z - sp) - later - carry)
    return _dot_nt(a.astype(BF16), vc_ref[...].astype(BF16)), carry + total


def _sb_dec_first_kernel(n_heads, ds, q_ref, kn_ref, vn_ref, kc_ref, vc_ref,
                         o_ref, carry_out_ref, cmin_ref):
    rows = ds * n_heads
    qbd = _block_diag_q(q_ref[...], n_heads)
    ri = _iota((rows, ds), 0) >> _log2(n_heads)
    cj = _iota((rows, ds), 1)
    valid = cj < ri
    z = _dot_nt(qbd, kn_ref[...])
    sp = _softplus(z)
    spm = jnp.where(valid, sp, 0.0)
    tri = (_iota((ds, ds), 0) > _iota((ds, ds), 1)).astype(BF16)
    later = _dot(spm.astype(BF16), tri)
    a = jnp.where(valid, jnp.exp((z - sp) - later), 0.0)
    acc = _dot(a.astype(BF16), vn_ref[...])
    carry = jnp.sum(spm, axis=1, keepdims=True)
    part, carry = _sb_dec_tile(qbd, kc_ref, vc_ref, carry)
    o_ref[...] = _collect_heads(acc + part, n_heads, ds).astype(o_ref.dtype)
    carry_out_ref[...] = carry
    cmin_ref[...] = jnp.broadcast_to(jnp.min(carry, axis=0, keepdims=True), cmin_ref.shape)


def _sb_dec_rest_kernel(n_heads, ds, n_rest, need_ref, q_ref, carry_in_ref, o1_ref, *refs):
    kc_refs, vc_refs = refs[:n_rest], refs[n_rest:2 * n_rest]
    o_ref, qbd_ref, acc_ref, carry_ref = refs[2 * n_rest:]
    needed = need_ref[pl.program_id(0)] != 0

    @pl.when(needed)
    def _():
        qbd_ref[...] = _block_diag_q(q_ref[...], n_heads)
        acc_ref[...] = jnp.zeros(acc_ref.shape, F32)
        carry_ref[...] = carry_in_ref[...]
        for i in reversed(range(n_rest)):
            @pl.when(jnp.min(carry_ref[...]) < UNDERFLOW_EXP)
            def _(i=i):
                part, carry = _sb_dec_tile(qbd_ref[...], kc_refs[i], vc_refs[i], carry_ref[...])
                acc_ref[...] = acc_ref[...] + part
                carry_ref[...] = carry
        o = o1_ref[...].astype(F32) + _collect_heads(acc_ref[...], n_heads, ds)
        o_ref[...] = o.astype(o_ref.dtype)

    @pl.when(jnp.logical_not(needed))
    def _():
        o_ref[...] = o1_ref[...]


def _sb_dec_call(q, kn, vn, kct, vct):
    db, ds, width = q.shape
    n_heads = width // DH
    rows = ds * n_heads
    past = kct.shape[2]
    first = min(TK_DEC_FIRST, past)
    assert past % first == 0 and first % (2 * LANES) == 0, past
    n_blk = (past - first) // (2 * LANES)
    per = max([k for k in range(1, TK_DEC_REST // (2 * LANES) + 1) if n_blk % k == 0], default=1)
    rest = per * 2 * LANES
    n_rest = (past - first) // rest
    bidx = lambda b: (b, 0, 0)
    o1, carry, cmin = pl.pallas_call(
        functools.partial(_sb_dec_first_kernel, n_heads, ds),
        out_shape=[jax.ShapeDtypeStruct((db, ds, width), BF16),
                   jax.ShapeDtypeStruct((db, rows, 1), F32),
                   jax.ShapeDtypeStruct((db, 1, LANES), F32)],
        grid=(db,),
        in_specs=[pl.BlockSpec((None, ds, width), bidx)] * 3
        + [pl.BlockSpec((None, width, first), lambda b: (b, 0, past // first - 1))] * 2,
        out_specs=[pl.BlockSpec((None, ds, width), bidx), pl.BlockSpec((None, rows, 1), bidx),
                   pl.BlockSpec((None, 1, LANES), bidx)],
        compiler_params=_cparams(("arbitrary",)),
        name="sb_decode_first",
    )(q, kn, vn, kct, vct)
    if n_rest == 0:
        return o1
    need = (cmin[:, 0, 0] < UNDERFLOW_EXP).astype(jnp.int32)
    bidx2 = lambda b, need_ref: (b, 0, 0)

    def tile_spec(i):
        def idx(b, need_ref):
            on = need_ref[b] != 0
            return (jnp.where(on, b, 0), 0, jnp.where(on, i, 0))
        return pl.BlockSpec((None, width, rest), idx, pipeline_mode=pl.Buffered(1))

    grid_spec = pltpu.PrefetchScalarGridSpec(
        num_scalar_prefetch=1,
        grid=(db,),
        in_specs=[pl.BlockSpec((None, ds, width), bidx2), pl.BlockSpec((None, rows, 1), bidx2),
                  pl.BlockSpec((None, ds, width), bidx2)] + [tile_spec(i) for i in range(n_rest)] * 2,
        out_specs=pl.BlockSpec((None, ds, width), bidx2),
        scratch_shapes=[pltpu.VMEM((rows, width), BF16), pltpu.VMEM((rows, width), F32),
                        pltpu.VMEM((rows, 1), F32)],
    )
    return pl.pallas_call(
        functools.partial(_sb_dec_rest_kernel, n_heads, ds, n_rest),
        out_shape=jax.ShapeDtypeStruct((db, ds, width), BF16),
        grid_spec=grid_spec,
        compiler_params=_cparams(("arbitrary",)),
        name="sb_decode_rest",
    )(need, q, carry, o1, *([kct] * n_rest), *([vct] * n_rest))


def _post_kernel(n_act, final, f_chunk, *refs):
    acts = refs[:n_act]
    wo_ref, x_ref, gate1_ref, sh_ref, sc_ref, gate_ref, g_ref, wu_ref, wd_ref = refs[n_act:n_act + 9]
    fg_ref = refs[n_act + 9] if final else None
    o_ref = refs[-1]
    y = None
    off = 0
    for a_ref in acts:
        kdim = a_ref.shape[1]
        part = _dot(a_ref[...], wo_ref[off:off + kdim, :])
        y = part if y is None else y + part
        off += kdim
    x = x_ref[...] + gate1_ref[...] * y
    h = _rms_mod(x, g_ref[...], sc_ref[...], sh_ref[...]).astype(BF16)
    d_ff = wu_ref.shape[1]
    acc = None
    for c in range(d_ff // f_chunk):
        cs = slice(c * f_chunk, (c + 1) * f_chunk)
        a = jnp.maximum(_dot(h, wu_ref[:, cs]), 0.0)
        part = _dot((a * a).astype(BF16), wd_ref[cs, :])
        acc = part if acc is None else acc + part
    xo = x + gate_ref[...] * acc
    if final:
        y = xo * lax.rsqrt(jnp.mean(xo * xo, axis=-1, keepdims=True) + EPS)
        xo = y * fg_ref[...]
    o_ref[...] = xo


def _post_call(prompt, acts, wo, x2, mod_arr, g, layer, wu, wd, final_g, rows_per_batch):
    m, d = x2.shape
    tm = TM_ROWS if prompt else m
    tpb = rows_per_batch // tm if prompt else 1
    row = lambda i: (i, 0)
    const2 = lambda i: (0, 0)
    final = final_g is not None
    resident = lambda w: pl.BlockSpec(w.shape, const2, pipeline_mode=pl.Buffered(1))
    in_specs = [pl.BlockSpec((tm, a.shape[1]), row) for a in acts]
    in_specs += [resident(wo), pl.BlockSpec((tm, d), row)]
    in_specs += [_mod_spec(prompt, tm, d, c, tpb) for c in (2, 3, 4, 5)]
    stacked = lambda w: pl.BlockSpec((None,) + w.shape[1:], lambda i: (layer, 0, 0), pipeline_mode=pl.Buffered(1))
    in_specs += [pl.BlockSpec((1, d), const2), stacked(wu), stacked(wd)]
    args = list(acts) + [wo, x2] + [mod_arr] * 4 + [g, wu, wd]
    if final:
        in_specs.append(pl.BlockSpec((1, d), const2))
        args.append(final_g)
    return pl.pallas_call(
        functools.partial(_post_kernel, len(acts), final, 1024),
        out_shape=jax.ShapeDtypeStruct((m, d), F32),
        grid=(m // tm,), in_specs=in_specs, out_specs=pl.BlockSpec((tm, d), row),
        compiler_params=_cparams(("arbitrary",)),
        name="post",
    )(*args)


def _sb_in_prompt_kernel(x_ref, sh_ref, sc_ref, g_ref, wq_ref, wkt_ref, wvt_ref,
                         q_ref, ktb_ref, vtb_ref, kt_ref, vt_ref):
    h = _rms_mod(x_ref[...], g_ref[...], sc_ref[...], sh_ref[...]).astype(BF16)
    q_ref[...] = (_dot(h, wq_ref[...]) * (DH ** -0.5)).astype(q_ref.dtype)
    kt = _dot_nt(wkt_ref[...], h)
    kt_ref[...] = kt
    ktb_ref[...] = kt.astype(BF16)
    vt = _dot_nt(wvt_ref[...], h)
    vt_ref[...] = vt
    vtb_ref[...] = vt.astype(BF16)


def _sb_in_sample_kernel(x_ref, sh_ref, sc_ref, g_ref, wq_ref, wkt_ref, wvt_ref,
                         q_ref, kb_ref, vb_ref, k32_ref, v32_ref):
    h = _rms_mod(x_ref[...], g_ref[...], sc_ref[...], sh_ref[...]).astype(BF16)
    q_ref[...] = (_dot(h, wq_ref[...]) * (DH ** -0.5)).astype(q_ref.dtype)
    k = _dot_nt(h, wkt_ref[...])
    k32_ref[...] = k
    kb_ref[...] = k.astype(BF16)
    v = _dot_nt(h, wvt_ref[...])
    v32_ref[...] = v
    vb_ref[...] = v.astype(BF16)


def _sb_in_call(prompt, x2, mod_arr, g, wts, rows_per_batch):
    wq, wkt, wvt = wts
    m, d = x2.shape
    nb = m // rows_per_batch
    tm = TM_ROWS_IN if prompt else m
    tpb = rows_per_batch // tm if prompt else 1
    row = lambda i: (i, 0)
    const2 = lambda i: (0, 0)
    in_specs = [pl.BlockSpec((tm, d), row),
                _mod_spec(prompt, tm, d, 0, tpb), _mod_spec(prompt, tm, d, 1, tpb),
                pl.BlockSpec((1, d), const2)] + [pl.BlockSpec(w.shape, const2) for w in wts]
    if prompt:
        tcol = lambda i: (i // tpb, 0, i % tpb)
        out_shape = ([jax.ShapeDtypeStruct((m, d), BF16)]
                     + [jax.ShapeDtypeStruct((nb, d, rows_per_batch), BF16)] * 2
                     + [jax.ShapeDtypeStruct((nb, d, rows_per_batch), F32)] * 2)
        out_specs = [pl.BlockSpec((tm, d), row)] + [pl.BlockSpec((None, d, tm), tcol)] * 4
        body = _sb_in_prompt_kernel
    else:
        out_shape = [jax.ShapeDtypeStruct((m, d), BF16)] * 3 + [jax.ShapeDtypeStruct((m, d), F32)] * 2
        out_specs = [pl.BlockSpec((tm, d), row)] * 5
        body = _sb_in_sample_kernel
    return pl.pallas_call(
        body, out_shape=out_shape, grid=(m // tm,), in_specs=in_specs, out_specs=out_specs,
        compiler_params=_cparams(("arbitrary",)),
        name="sb_in_prompt" if prompt else "sb_in_sample",
    )(x2, mod_arr, mod_arr, g, wq, wkt, wvt)


def _to_state(xt, n_heads):
    nb, _, l = xt.shape
    return xt.reshape(nb, n_heads, DH, l).transpose(0, 3, 1, 2)[None]


def _from_cache(c):
    nb, p, n_heads, dh = c.shape
    return c.transpose(0, 2, 3, 1).reshape(nb, n_heads * dh, p)


def _trunk(prompt, x, mods, past, wts):
    nb, rpb, d = x.shape
    m = nb * rpb
    a_w = H_FOX * DH
    x2 = x.reshape(m, d)
    norm_g, ab_wts, w_out_ab, sb_wts, w_out_sb, w_up, w_down, final_g = wts

    if prompt:
        outs = _ab_in_call(True, x2, mods[0], norm_g[0, 0][None], ab_wts, None, rpb, 0)
        q, ktb, vtb, kt32, vt32, lft, fcp, kaug, pooled, pstate = outs
        a_out = _fox_call(q.reshape(nb, rpb, a_w), ktb, vtb, kaug, fcp)
        st_fox = (_to_state(kt32, H_FOX), _to_state(vt32, H_FOX), lft.transpose(0, 2, 1)[None])
    else:
        cfk, cfv, cflogf, spool, csk, csv = past
        p0 = cfk.shape[1]
        hist = jnp.pad(spool, ((0, 0), (HIST_ROWS - spool.shape[1], 0), (0, 0)))
        outs = _ab_in_call(False, x2, mods[0], norm_g[0, 0][None], ab_wts, hist, rpb, p0)
        q, kb, vb, k32, v32, lft, pooled, pstate = outs
        lfn_t = lft.reshape(H_FOX, nb, rpb).transpose(1, 0, 2)
        a_out = _fox_dec_call(q.reshape(nb, rpb, a_w), kb.reshape(nb, rpb, a_w), vb.reshape(nb, rpb, a_w),
                              _from_cache(cfk), _from_cache(cfv), lfn_t, cflogf.transpose(0, 2, 1))
        st_fox = (k32.reshape(1, nb, rpb, H_FOX, DH), v32.reshape(1, nb, rpb, H_FOX, DH),
                  lfn_t.transpose(0, 2, 1)[None])
    x2 = _post_call(prompt, [a_out.reshape(m, a_w), pooled], w_out_ab, x2, mods[0], norm_g[0, 1][None],
                    0, w_up, w_down, None, rpb)

    n_heads = d // DH
    outs = _sb_in_call(prompt, x2, mods[1], norm_g[1, 0][None], sb_wts, rpb)
    if prompt:
        q1, ktb1, vtb1, skt32, svt32 = outs
        s_out = _sb_call(q1.reshape(nb, rpb, d), ktb1, vtb1)
        st_sb = (_to_state(skt32, n_heads), _to_state(svt32, n_heads))
    else:
        q1, kb1, vb1, sk32, sv32 = outs
        s_out = _sb_dec_call(q1.reshape(nb, rpb, d), kb1.reshape(nb, rpb, d), vb1.reshape(nb, rpb, d),
                             _from_cache(csk), _from_cache(csv))
        st_sb = (sk32.reshape(1, nb, rpb, n_heads, DH), sv32.reshape(1, nb, rpb, n_heads, DH))
    y = _post_call(prompt, [s_out.reshape(m, d)], w_out_sb, x2, mods[1], norm_g[1, 1][None],
                   1, w_up, w_down, final_g[None], rpb)

    states = st_fox + (pstate[:, 1:, :][None],) + st_sb
    return y.reshape(nb, rpb, d), states


def kernel(x_prompt, x_sample, c_prompt, c_sample, cache_fox_k, cache_fox_v, cache_fox_logf, state_pool, cache_sb_k, cache_sb_v, w_ada, b_ada, norm_g, w_in_ab, b_forget, w_pool, pool_scale, w_out_ab, w_in_sb, w_out_sb, w_up, w_down, final_g):
    b, _, d = x_prompt.shape
    db, ds, _ = x_sample.shape
    a_w = H_FOX * DH

    c_all = jnp.concatenate([c_prompt, c_sample, jnp.zeros((ADA_ROWS - b - db, d), F32)], axis=0)
    mod = _ada_call(c_all, w_ada, b_ada)
    mods_p = [mod[i, :b][:, None, :] for i in range(mod.shape[0])]
    mods_s = [jnp.repeat(mod[i, b:b + db], ds, axis=0) for i in range(mod.shape[0])]

    w_in = w_in_ab[0]
    gate_pad = ((0, GATE_ROWS - H_FOX), (0, 0))
    ab_wts = (w_in[:, :a_w].astype(BF16),
              w_in[:, a_w:2 * a_w].T.astype(BF16),
              w_in[:, 2 * a_w:3 * a_w].T.astype(BF16),
              w_in[:, 3 * a_w + H_FOX:].astype(BF16),
              jnp.pad(w_in[:, 3 * a_w:3 * a_w + H_FOX].T, gate_pad).astype(BF16),
              jnp.pad(b_forget[0][:, None], gate_pad),
              w_pool[0].astype(BF16), pool_scale[0][None])
    w_sb = w_in_sb[0]
    sb_wts = (w_sb[:, :d].astype(BF16), w_sb[:, d:2 * d].T.astype(BF16), w_sb[:, 2 * d:].T.astype(BF16))
    wts = (norm_g, ab_wts, w_out_ab[0].astype(BF16), sb_wts, w_out_sb[0].astype(BF16),
           w_up.astype(BF16), w_down.astype(BF16), final_g)

    y_p, st_p = _trunk(True, x_prompt, mods_p, None, wts)
    past = (cache_fox_k[0], cache_fox_v[0], cache_fox_logf[0], state_pool[0],
            cache_sb_k[0], cache_sb_v[0])
    y_s, st_s = _trunk(False, x_sample, mods_s, past, wts)
    fk, fv, fl, pool, sk, sv = st_p
    fk2, fv2, fl2, pool2, sk2, sv2 = st_s
    return (y_p, y_s, fk, fv, fl, pool, sk, sv, fk2, fv2, fl2, pool2, sk2, sv2)
```
